```python
import math
import jax, jax.numpy as jnp
from jax import lax
import numpy as np

D_MODEL = 1024
BATCH = 8
SEQ = 2048
DEPTH = 1

HEAD_DIM = 64
ATTN_WIDTH = D_MODEL // 2
ATTN_HEADS = ATTN_WIDTH // HEAD_DIM
MOBA_BLOCK = 256
MOBA_TOPK = 3
Q_CHUNK = 32
POOL_WINDOWS = (2, 4, 8, 16)
POOL_GROUPS = len(POOL_WINDOWS)
POOL_WIDTH = D_MODEL // 2
POOL_GROUP_WIDTH = POOL_WIDTH // POOL_GROUPS
N_BRANCHES = 2
IN_WIDTH = 3 * ATTN_WIDTH + POOL_WIDTH + N_BRANCHES * D_MODEL
D_FF = 2816
RPB_BUCKETS = 32
RPB_MAX_DISTANCE = 128
RMS_EPS = 1e-6

kernel_name = "hybrid_moba_pool_macaron_block"


def rms_norm(x, g):
    x32 = x.astype(jnp.float32)
    y = x32 * lax.rsqrt(jnp.mean(x32 * x32, axis=-1, keepdims=True) + RMS_EPS)
    return (y * g.astype(jnp.float32)).astype(x.dtype)


def swiglu(x, w_gate, w_up, w_down):
    return (jax.nn.silu(x @ w_gate) * (x @ w_up)) @ w_down


def rpb_bucket(dist):
    n = jnp.maximum(dist, 0)
    max_exact = RPB_BUCKETS // 2
    nf = jnp.maximum(n, 1).astype(jnp.float32)
    large = max_exact + (jnp.log(nf / max_exact) / math.log(RPB_MAX_DISTANCE / max_exact)
                         * (RPB_BUCKETS - max_exact)).astype(jnp.int32)
    large = jnp.minimum(large, RPB_BUCKETS - 1)
    return jnp.where(n < max_exact, n, large)


def moba_attention(q, k, v, rpb_table):
    B, H, S, Dh = q.shape
    nb = -(-S // MOBA_BLOCK)
    s_pad = nb * MOBA_BLOCK
    pad = ((0, 0), (0, 0), (0, s_pad - S), (0, 0))
    k_pad = jnp.pad(k, pad)
    v_pad = jnp.pad(v, pad)
    k_blocks = k_pad.reshape(B, H, nb, MOBA_BLOCK, Dh)
    v_blocks = v_pad.reshape(B, H, nb, MOBA_BLOCK, Dh)
    k_mean = jnp.mean(k_blocks, axis=3)
    pos = jnp.arange(S, dtype=jnp.int32)
    q_blk = pos // MOBA_BLOCK
    gate = jnp.einsum('bhsd,bhnd->bhsn', q, k_mean).astype(jnp.float32)
    past = jnp.arange(nb, dtype=jnp.int32)[None, :] < q_blk[:, None]
    gate = jnp.where(past, gate, -jnp.inf)
    n_sel = min(MOBA_TOPK, nb)
    _, sel_idx = lax.top_k(gate, n_sel)
    table_hb = rpb_table.astype(jnp.float32).T
    scale = HEAD_DIM ** -0.5
    bi = jnp.arange(B)[:, None, None, None]
    hi = jnp.arange(H)[None, :, None, None]
    offs = jnp.arange(MOBA_BLOCK, dtype=jnp.int32)

    def chunk(c):
        t0 = c * Q_CHUNK
        ib = t0 // MOBA_BLOCK
        tq = t0 + jnp.arange(Q_CHUNK, dtype=jnp.int32)
        qc = lax.dynamic_slice_in_dim(q, t0, Q_CHUNK, axis=2)
        idx = lax.dynamic_slice_in_dim(sel_idx, t0, Q_CHUNK, axis=2)
        kg = k_blocks[bi, hi, idx]
        vg = v_blocks[bi, hi, idx]
        kpos_sel = idx[..., None] * MOBA_BLOCK + offs
        lg_sel = jnp.einsum('bhqd,bhqnkd->bhqnk', qc, kg).astype(jnp.float32) * scale
        lg_sel = lg_sel + table_hb[hi[..., None], rpb_bucket(tq[:, None, None] - kpos_sel)]
        valid = jnp.arange(n_sel, dtype=jnp.int32) < ib
        lg_sel = jnp.where(valid[:, None], lg_sel, -jnp.inf)
        ko = lax.dynamic_slice_in_dim(k_pad, ib * MOBA_BLOCK, MOBA_BLOCK, axis=2)
        vo = lax.dynamic_slice_in_dim(v_pad, ib * MOBA_BLOCK, MOBA_BLOCK, axis=2)
        dist = tq[:, None] - (ib * MOBA_BLOCK + offs)[None, :]
        lg_own = jnp.einsum('bhqd,bhkd->bhqk', qc, ko).astype(jnp.float32) * scale
        lg_own = lg_own + table_hb[:, rpb_bucket(dist)]
        lg_own = jnp.where(dist >= 0, lg_own, -jnp.inf)
        logits = jnp.concatenate([lg_sel.reshape(B, H, Q_CHUNK, n_sel * MOBA_BLOCK), lg_own], axis=-1)
        p = jax.nn.softmax(logits, axis=-1).astype(v.dtype)
        p_sel = p[..., :n_sel * MOBA_BLOCK].reshape(B, H, Q_CHUNK, n_sel, MOBA_BLOCK)
        p_own = p[..., n_sel * MOBA_BLOCK:]
        return (jnp.einsum('bhqnk,bhqnkd->bhqd', p_sel, vg)
                + jnp.einsum('bhqk,bhkd->bhqd', p_own, vo))

    outs = lax.map(chunk, jnp.arange(S // Q_CHUNK, dtype=jnp.int32))
    return outs.transpose(1, 2, 0, 3, 4).reshape(B, H, S, Dh)


def pool_mixer(z, w_group, ch_scale):
    B, S, _ = z.shape
    G = POOL_GROUP_WIDTH
    z32 = z.astype(jnp.float32)
    c_pad = jnp.concatenate([jnp.zeros((B, 1, POOL_WIDTH), jnp.float32),
                             jnp.cumsum(z32, axis=1)], axis=1)
    t1 = jnp.arange(1, S + 1, dtype=jnp.float32)
    outs = []
    for g, w in enumerate(POOL_WINDOWS):
        sl = slice(g * G, (g + 1) * G)
        cg = c_pad[:, :, sl]
        lag = jnp.concatenate([jnp.zeros((B, w - 1, G), jnp.float32), cg[:, :S - w + 1]], axis=1)
        mean = (cg[:, 1:] - lag) / jnp.minimum(t1, float(w))[None, :, None]
        outs.append(mean - z32[:, :, sl])
    pooled = jnp.stack(outs, axis=2).astype(z.dtype)
    mixed = jnp.einsum('bsgc,gcd->bsgd', pooled, w_group).reshape(B, S, POOL_WIDTH)
    return mixed * ch_scale


def setup_inputs(seed: int = 0) -> dict:
    key = jax.random.key(seed)
    ks = jax.random.split(key, 20)
    f32 = jnp.float32
    L = DEPTH

    def w(k, shape, fan_in):
        return jax.random.normal(k, shape, f32) * (fan_in ** -0.5)

    def gain(k, shape):
        return 1.0 + 0.02 * jax.random.normal(k, shape, f32)

    return {
        "x": jax.random.normal(ks[0], (BATCH, SEQ, D_MODEL), f32),
        "ffn1_norm": gain(ks[1], (L, D_MODEL)),
        "ffn1_w_gate": w(ks[2], (L, D_MODEL, D_FF), D_MODEL),
        "ffn1_w_up": w(ks[3], (L, D_MODEL, D_FF), D_MODEL),
        "ffn1_w_down": w(ks[4], (L, D_FF, D_MODEL), D_FF),
        "mix_norm": gain(ks[5], (L, D_MODEL)),
        "w_in": w(ks[6], (L, D_MODEL, IN_WIDTH), D_MODEL),
        "pool_w_group": w(ks[7], (L, POOL_GROUPS, POOL_GROUP_WIDTH, POOL_GROUP_WIDTH), POOL_GROUP_WIDTH),
        "pool_scale": gain(ks[8], (L, POOL_WIDTH)) + 0.08 * jax.random.normal(ks[9], (L, POOL_WIDTH), f32),
        "w_branch_attn": w(ks[10], (L, ATTN_WIDTH, D_MODEL), ATTN_WIDTH),
        "w_branch_pool": w(ks[11], (L, POOL_WIDTH, D_MODEL), POOL_WIDTH),
        "w_out": w(ks[12], (L, D_MODEL, D_MODEL), D_MODEL),
        "ffn2_norm": gain(ks[13], (L, D_MODEL)),
        "ffn2_w_gate": w(ks[14], (L, D_MODEL, D_FF), D_MODEL),
        "ffn2_w_up": w(ks[15], (L, D_MODEL, D_FF), D_MODEL),
        "ffn2_w_down": w(ks[16], (L, D_FF, D_MODEL), D_FF),
        "rpb_table": 0.5 * jax.random.normal(ks[17], (RPB_BUCKETS, ATTN_HEADS), f32),
        "final_norm": gain(ks[18], (D_MODEL,)),
    }


def reference(x, ffn1_norm, ffn1_w_gate, ffn1_w_up, ffn1_w_down, mix_norm, w_in,
              pool_w_group, pool_scale, w_branch_attn, w_branch_pool, w_out,
              ffn2_norm, ffn2_w_gate, ffn2_w_up, ffn2_w_down, rpb_table, final_norm):
    B, S, _ = x.shape
    h = x
    for l in range(DEPTH):
        h = h + 0.5 * swiglu(rms_norm(h, ffn1_norm[l]), ffn1_w_gate[l], ffn1_w_up[l], ffn1_w_down[l])
        u = rms_norm(h, mix_norm[l])
        proj = u @ w_in[l]
        q, k, v, z, g_attn, g_pool = jnp.split(
            proj, np.cumsum([ATTN_WIDTH, ATTN_WIDTH, ATTN_WIDTH, POOL_WIDTH, D_MODEL]).tolist(), axis=-1)
        to_heads = lambda t: t.reshape(B, S, ATTN_HEADS, HEAD_DIM).transpose(0, 2, 1, 3)
        a = moba_attention(to_heads(q), to_heads(k), to_heads(v), rpb_table)
        a = a.transpose(0, 2, 1, 3).reshape(B, S, ATTN_WIDTH)
        p = pool_mixer(z, pool_w_group[l], pool_scale[l])
        merged = (jax.nn.sigmoid(g_attn) * (a @ w_branch_attn[l])
                  + jax.nn.sigmoid(g_pool) * (p @ w_branch_pool[l]))
        h = h + merged @ w_out[l]
        h = h + 0.5 * swiglu(rms_norm(h, ffn2_norm[l]), ffn2_w_gate[l], ffn2_w_up[l], ffn2_w_down[l])
    return rms_norm(h, final_norm)
```

```python
import functools
import math

import jax
import jax.numpy as jnp
from jax import lax
from jax.experimental import pallas as pl
from jax.experimental.pallas import tpu as pltpu

D_MODEL = 1024
HEAD_DIM = 64
ATTN_WIDTH = 512
ATTN_HEADS = 8
MOBA_BLOCK = 256
MOBA_TOPK = 3
POOL_WINDOWS = (2, 4, 8, 16)
POOL_WIDTH = 512
POOL_GROUP_WIDTH = 128
POOL_HALO = 16
D_FF = 2816
RPB_BUCKETS = 32
RPB_MAX_DISTANCE = 128
RMS_EPS = 1e-6
MASK_VALUE = -1e30

V7X_VMEM_LIMIT_BYTES = 56 * 1024 * 1024

FFN_TOKENS = 512
MIX_TOKENS = 512

BF16 = jnp.bfloat16
F32 = jnp.float32


def _resident(shape):
    nd = len(shape)
    return pl.BlockSpec(shape, lambda *_: (0,) * nd, pipeline_mode=pl.Buffered(1))


def _dot(a, b):
    return jnp.dot(a, b, preferred_element_type=F32)


def _dot_nt(a, b):
    return lax.dot_general(a, b, (((1,), (1,)), ((), ())), preferred_element_type=F32)


def _rms(x, g):
    return x * lax.rsqrt(jnp.mean(x * x, axis=-1, keepdims=True) + RMS_EPS) * g


def _swiglu_half_step(x, g_ref, wg_ref, wu_ref, wd_ref):
    xn = _rms(x, g_ref[...]).astype(BF16)
    gate = _dot(xn, wg_ref[...])
    up = _dot(xn, wu_ref[...])
    act = (gate * jax.nn.sigmoid(gate) * up).astype(BF16)
    return x + 0.5 * _dot(act, wd_ref[...])


def _rpb_bucket(dist):
    n = jnp.maximum(dist, 0)
    max_exact = RPB_BUCKETS // 2
    nf = jnp.maximum(n, 1).astype(F32)
    large = max_exact + (jnp.log(nf / max_exact) / math.log(RPB_MAX_DISTANCE / max_exact)
                         * (RPB_BUCKETS - max_exact)).astype(jnp.int32)
    large = jnp.minimum(large, RPB_BUCKETS - 1)
    return jnp.where(n < max_exact, n, large)


def _rpb_bias_kernel(table_ref, diag_ref, sub_ref):
    h = pl.program_id(0)
    kl = lax.broadcasted_iota(jnp.int32, (MOBA_BLOCK, MOBA_BLOCK), 0)
    ql = lax.broadcasted_iota(jnp.int32, (MOBA_BLOCK, MOBA_BLOCK), 1)
    for out_ref, offset in ((diag_ref, 0), (sub_ref, MOBA_BLOCK)):
        dist = ql - kl + offset
        bucket = _rpb_bucket(dist)
        bias = jnp.zeros((MOBA_BLOCK, MOBA_BLOCK), F32)
        for b in range(RPB_BUCKETS):
            bias = jnp.where(bucket == b, table_ref[b, h], bias)
        if offset == 0:
            bias = jnp.where(dist >= 0, bias, MASK_VALUE)
        out_ref[0] = bias


def _rpb_bias(rpb_table):
    tile = jax.ShapeDtypeStruct((ATTN_HEADS, MOBA_BLOCK, MOBA_BLOCK), F32)
    spec = pl.BlockSpec((1, MOBA_BLOCK, MOBA_BLOCK), lambda h: (h, 0, 0))
    return pl.pallas_call(
        _rpb_bias_kernel,
        grid=(ATTN_HEADS,),
        in_specs=[pl.BlockSpec(memory_space=pltpu.SMEM)],
        out_specs=[spec, spec],
        out_shape=[tile, tile],
        name="rpb_bias",
    )(rpb_table)


def _ffn1_qkv_kernel(x_ref, g1_ref, wg_ref, wu_ref, wd_ref, gm_ref, wk_ref, wqv_t_ref,
                     h1_ref, un_ref, qt_ref, k_ref, vt_ref):
    h1 = _swiglu_half_step(x_ref[0], g1_ref, wg_ref, wu_ref, wd_ref)
    h1_ref[0] = h1
    un = _rms(h1, gm_ref[...]).astype(BF16)
    un_ref[0] = un
    k_ref[0] = _dot(un, wk_ref[...]).astype(BF16)
    qv_t = _dot_nt(wqv_t_ref[...], un)
    qt_ref[0] = (qv_t[:ATTN_WIDTH] * (HEAD_DIM ** -0.5)).astype(BF16)
    for c in range(FFN_TOKENS // MOBA_BLOCK):
        vt_ref[0, c] = qv_t[ATTN_WIDTH:, c * MOBA_BLOCK:(c + 1) * MOBA_BLOCK].astype(BF16)


def _ffn1_qkv(x, g1, wg, wu, wd, gm, wk, wqv_t):
    B, S, D = x.shape
    tm = FFN_TOKENS
    nblk = S // MOBA_BLOCK
    tok = lambda width, dt: (pl.BlockSpec((1, tm, width), lambda b, i: (b, i, 0)),
                             jax.ShapeDtypeStruct((B, S, width), dt))
    h1_spec, h1_shape = tok(D, F32)
    un_spec, un_shape = tok(D, BF16)
    k_spec, k_shape = tok(ATTN_WIDTH, BF16)
    return pl.pallas_call(
        _ffn1_qkv_kernel,
        grid=(B, S // tm),
        in_specs=[pl.BlockSpec((1, tm, D), lambda b, i: (b, i, 0)),
                  _resident(g1.shape), _resident(wg.shape), _resident(wu.shape), _resident(wd.shape),
                  _resident(gm.shape), _resident(wk.shape), _resident(wqv_t.shape)],
        out_specs=[h1_spec, un_spec,
                   pl.BlockSpec((1, ATTN_WIDTH, tm), lambda b, i: (b, 0, i)),
                   k_spec,
                   pl.BlockSpec((1, tm // MOBA_BLOCK, ATTN_WIDTH, MOBA_BLOCK), lambda b, i: (b, i, 0, 0))],
        out_shape=[h1_shape, un_shape,
                   jax.ShapeDtypeStruct((B, ATTN_WIDTH, S), BF16),
                   k_shape,
                   jax.ShapeDtypeStruct((B, nblk, ATTN_WIDTH, MOBA_BLOCK), BF16)],
        compiler_params=pltpu.CompilerParams(
            dimension_semantics=("arbitrary", "arbitrary"), vmem_limit_bytes=V7X_VMEM_LIMIT_BYTES),
        name="ffn1_qkv",
    )(x, g1, wg, wu, wd, gm, wk, wqv_t)


def _moba_attn_kernel(t31_ref, qt_ref, k_ref, vt_ref, diag_ref, sub_ref, a_ref,
                      kmhi_ref, kmlo_ref, rb_far_ref, rb_sub_ref, at_ref):
    qi = pl.program_id(1)
    nblk = k_ref.shape[1] // MOBA_BLOCK
    nq = MOBA_BLOCK

    @pl.when(qi == 0)
    def _():
        blk_row = lax.broadcasted_iota(jnp.int32, (nblk, ATTN_WIDTH), 0)
        km = jnp.zeros((nblk, ATTN_WIDTH), F32)
        for j in range(nblk):
            mean_j = jnp.mean(k_ref[0, j * MOBA_BLOCK:(j + 1) * MOBA_BLOCK, :].astype(F32), axis=0, keepdims=True)
            km = jnp.where(blk_row == j, mean_j, km)
        km = jnp.concatenate([km] * ATTN_HEADS, axis=0)
        row_head = lax.broadcasted_iota(jnp.int32, km.shape, 0) // nblk
        col_head = lax.broadcasted_iota(jnp.int32, km.shape, 1) // HEAD_DIM
        km = jnp.where(row_head == col_head, km, 0.0)
        hi = km.astype(BF16)
        kmhi_ref[...] = hi
        kmlo_ref[...] = (km - hi.astype(F32)).astype(BF16)

    qt = qt_ref[0]
    gates = _dot(kmhi_ref[...], qt) + _dot(kmlo_ref[...], qt)
    jrow = lax.broadcasted_iota(jnp.int32, (nblk, nq), 0)
    for h in range(ATTN_HEADS):
        g = gates[h * nblk:(h + 1) * nblk]
        cnt = jnp.zeros((nblk, nq), jnp.int32)
        for jp in range(nblk):
            gb = jnp.broadcast_to(g[jp:jp + 1], (nblk, nq))
            beats = (gb > g) | ((gb == g) & (jp < jrow))
            cnt = cnt + jnp.where(beats, jnp.where(jp < qi, 1, 0), 0)
        sel = (jrow < qi) & (cnt < MOBA_TOPK)
        far = jnp.where(sel, t31_ref[h], MASK_VALUE)
        sub = jnp.where(sel, 0.0, MASK_VALUE)
        for j in range(nblk):
            rb_far_ref[h * nblk + j] = far[j:j + 1]
            rb_sub_ref[h * nblk + j] = sub[j:j + 1]

    zeros_half = jnp.zeros((HEAD_DIM, nq), BF16)
    j_sub = jnp.maximum(qi - 1, 0)
    for h in range(ATTN_HEADS):
        pair, half = divmod(h, 2)
        qh = qt_ref[0, h * HEAD_DIM:(h + 1) * HEAD_DIM, :]
        qz = jnp.concatenate([qh, zeros_half] if half == 0 else [zeros_half, qh], axis=0)

        def block(j, bias, carry, h=h, pair=pair, qz=qz):
            m, l, o = carry
            start = pl.multiple_of(j * MOBA_BLOCK, MOBA_BLOCK)
            kblk = k_ref[0, pl.ds(start, MOBA_BLOCK), pair * 128:(pair + 1) * 128]
            s = _dot(kblk, qz) + bias
            m_new = jnp.maximum(m, jnp.max(s, axis=0, keepdims=True))
            alpha = jnp.exp(m - m_new)
            p = jnp.exp(s - m_new)
            l = alpha * l + jnp.sum(p, axis=0, keepdims=True)
            vblk = vt_ref[0, j, h * HEAD_DIM:(h + 1) * HEAD_DIM, :]
            o = alpha * o + _dot(vblk, p.astype(BF16))
            return m_new, l, o

        carry = (jnp.full((1, nq), MASK_VALUE, F32), jnp.zeros((1, nq), F32), jnp.zeros((HEAD_DIM, nq), F32))
        carry = block(qi, diag_ref[h], carry)
        carry = block(j_sub, sub_ref[h] + rb_sub_ref[h * nblk + j_sub], carry)
        carry = lax.fori_loop(0, j_sub, lambda j, c, h=h, block=block: block(j, rb_far_ref[h * nblk + j], c), carry)
        m, l, o = carry
        at_ref[h * HEAD_DIM:(h + 1) * HEAD_DIM, :] = o / l

    a_ref[0] = at_ref[...].T.astype(BF16)


def _moba_attn(qt, k, vt, diag, sub, t31):
    B, S, _ = k.shape
    nblk = S // MOBA_BLOCK
    tile = pl.BlockSpec((ATTN_HEADS, MOBA_BLOCK, MOBA_BLOCK), lambda b, i: (0, 0, 0), pipeline_mode=pl.Buffered(1))
    return pl.pallas_call(
        _moba_attn_kernel,
        grid=(B, nblk),
        in_specs=[pl.BlockSpec(memory_space=pltpu.SMEM),
                  pl.BlockSpec((1, ATTN_WIDTH, MOBA_BLOCK), lambda b, i: (b, 0, i)),
                  pl.BlockSpec((1, S, ATTN_WIDTH), lambda b, i: (b, 0, 0)),
                  pl.BlockSpec((1, nblk, ATTN_WIDTH, MOBA_BLOCK), lambda b, i: (b, 0, 0, 0)),
                  tile, tile],
        out_specs=pl.BlockSpec((1, MOBA_BLOCK, ATTN_WIDTH), lambda b, i: (b, i, 0)),
        out_shape=jax.ShapeDtypeStruct((B, S, ATTN_WIDTH), BF16),
        scratch_shapes=[pltpu.VMEM((ATTN_HEADS * nblk, ATTN_WIDTH), BF16),
                        pltpu.VMEM((ATTN_HEADS * nblk, ATTN_WIDTH), BF16),
                        pltpu.VMEM((ATTN_HEADS * nblk, 1, MOBA_BLOCK), F32),
                        pltpu.VMEM((ATTN_HEADS * nblk, 1, MOBA_BLOCK), F32),
                        pltpu.VMEM((ATTN_WIDTH, MOBA_BLOCK), F32)],
        compiler_params=pltpu.CompilerParams(
            dimension_semantics=("arbitrary", "arbitrary"), vmem_limit_bytes=V7X_VMEM_LIMIT_BYTES),
        name="moba_attn",
    )(t31, qt, k, vt, diag, sub)


def _mix_merge_kernel(h1_ref, un_ref, unprev_ref, a_ref, wzg_ref, wgrp_ref, scale_ref,
                      wa_ref, wp_ref, wout_ref, h2_ref, zext_ref):
    i = pl.program_id(1)
    tm = MIX_TOKENS
    zg = _dot(un_ref[0], wzg_ref[...])
    z = zg[:, :POOL_WIDTH]
    zprev = _dot(unprev_ref[0], wzg_ref[:, :POOL_WIDTH])
    zext_ref[:POOL_HALO] = jnp.where(i > 0, zprev, 0.0)
    zext_ref[POOL_HALO:] = z

    tpos = i * tm + lax.broadcasted_iota(jnp.int32, (tm, POOL_GROUP_WIDTH), 0)
    mixed = []
    for g, w in enumerate(POOL_WINDOWS):
        cols = slice(g * POOL_GROUP_WIDTH, (g + 1) * POOL_GROUP_WIDTH)
        wsum = zext_ref[POOL_HALO:, cols]
        for lag in range(1, w):
            wsum = wsum + zext_ref[POOL_HALO - lag:POOL_HALO - lag + tm, cols]
        mean = wsum / jnp.minimum(tpos + 1, w).astype(F32)
        pooled = (mean - zext_ref[POOL_HALO:, cols]).astype(BF16)
        mixed.append(_dot(pooled, wgrp_ref[g]))
    p = (jnp.concatenate(mixed, axis=1) * scale_ref[...]).astype(BF16)

    g_attn = zg[:, POOL_WIDTH:POOL_WIDTH + D_MODEL]
    g_pool = zg[:, POOL_WIDTH + D_MODEL:]
    merged = (jax.nn.sigmoid(g_attn) * _dot(a_ref[0], wa_ref[...])
              + jax.nn.sigmoid(g_pool) * _dot(p, wp_ref[...]))
    h2_ref[0] = h1_ref[0] + _dot(merged.astype(BF16), wout_ref[...])


def _mix_merge(h1, un, a, wzg, wgrp, scale, wa, wp, wout):
    B, S, D = h1.shape
    tm = MIX_TOKENS
    halo_per_tile = tm // POOL_HALO
    return pl.pallas_call(
        _mix_merge_kernel,
        grid=(B, S // tm),
        in_specs=[pl.BlockSpec((1, tm, D), lambda b, i: (b, i, 0)),
                  pl.BlockSpec((1, tm, D), lambda b, i: (b, i, 0)),
                  pl.BlockSpec((1, POOL_HALO, D), lambda b, i: (b, jnp.maximum(i * halo_per_tile - 1, 0), 0)),
                  pl.BlockSpec((1, tm, ATTN_WIDTH), lambda b, i: (b, i, 0)),
                  _resident(wzg.shape), _resident(wgrp.shape), _resident(scale.shape),
                  _resident(wa.shape), _resident(wp.shape), _resident(wout.shape)],
        out_specs=pl.BlockSpec((1, tm, D), lambda b, i: (b, i, 0)),
        out_shape=jax.ShapeDtypeStruct((B, S, D), F32),
        scratch_shapes=[pltpu.VMEM((POOL_HALO + tm, POOL_WIDTH), F32)],
        compiler_params=pltpu.CompilerParams(
            dimension_semantics=("arbitrary", "arbitrary"), vmem_limit_bytes=V7X_VMEM_LIMIT_BYTES),
        name="mix_merge",
    )(h1, un, un, a, wzg, wgrp, scale, wa, wp, wout)


def _ffn2_final_kernel(h_ref, g2_ref, wg_ref, wu_ref, wd_ref, gf_ref, out_ref):
    h3 = _swiglu_half_step(h_ref[0], g2_ref, wg_ref, wu_ref, wd_ref)
    out_ref[0] = _rms(h3, gf_ref[...])


def _ffn2_final(h, g2, wg, wu, wd, gf):
    B, S, D = h.shape
    tm = FFN_TOKENS
    return pl.pallas_call(
        _ffn2_final_kernel,
        grid=(B, S // tm),
        in_specs=[pl.BlockSpec((1, tm, D), lambda b, i: (b, i, 0)),
                  _resident(g2.shape), _resident(wg.shape), _resident(wu.shape), _resident(wd.shape),
                  _resident(gf.shape)],
        out_specs=pl.BlockSpec((1, tm, D), lambda b, i: (b, i, 0)),
        out_shape=jax.ShapeDtypeStruct((B, S, D), F32),
        compiler_params=pltpu.CompilerParams(
            dimension_semantics=("arbitrary", "arbitrary"), vmem_limit_bytes=V7X_VMEM_LIMIT_BYTES),
        name="ffn2_final",
    )(h, g2, wg, wu, wd, gf)


def kernel(x, ffn1_norm, ffn1_w_gate, ffn1_w_up, ffn1_w_down, mix_norm, w_in, pool_w_group, pool_scale,
           w_branch_attn, w_branch_pool, w_out, ffn2_norm, ffn2_w_gate, ffn2_w_up, ffn2_w_down,
           rpb_table, final_norm):
    B, S, D = x.shape
    assert (D, ffn1_w_gate.shape[0]) == (D_MODEL, 1)
    assert S % MOBA_BLOCK == 0 and S % FFN_TOKENS == 0 and S % MIX_TOKENS == 0
    bf = lambda w: w.astype(BF16)
    row = lambda v: v.reshape(1, -1)
    w_in = w_in[0]
    wq, wk, wv = (w_in[:, c * ATTN_WIDTH:(c + 1) * ATTN_WIDTH] for c in range(3))
    wqv_t = bf(jnp.concatenate([wq, wv], axis=1).T)
    wzg = bf(w_in[:, 3 * ATTN_WIDTH:])

    diag, sub = _rpb_bias(rpb_table)
    h1, un, qt, k, vt = _ffn1_qkv(x, row(ffn1_norm[0]), bf(ffn1_w_gate[0]), bf(ffn1_w_up[0]),
                                  bf(ffn1_w_down[0]), row(mix_norm[0]), bf(wk), wqv_t)
    a = _moba_attn(qt, k, vt, diag, sub, rpb_table[RPB_BUCKETS - 1])
    h2 = _mix_merge(h1, un, a, wzg, bf(pool_w_group[0]), row(pool_scale[0]),
                    bf(w_branch_attn[0]), bf(w_branch_pool[0]), bf(w_out[0]))
    return _ffn2_final(h2, row(ffn2_norm[0]), bf(ffn2_w_gate[0]), bf(ffn2_w_up[0]), bf(ffn2_w_down[0]),
                       row(final_norm))
```

```python
import functools
import math

import jax
import jax.numpy as jnp
from jax import lax
from jax.experimental import pallas as pl
from jax.experimental.pallas import tpu as pltpu

D_MODEL = 1024
HEAD_DIM = 64
ATTN_WIDTH = 512
ATTN_HEADS = 8
MOBA_BLOCK = 256
MOBA_TOPK = 3
POOL_WINDOWS = (2, 4, 8, 16)
POOL_WIDTH = 512
POOL_GROUP_WIDTH = 128
POOL_HALO = 16
D_FF = 2816
RPB_BUCKETS = 32
RPB_MAX_DISTANCE = 128
RMS_EPS = 1e-6
MASK_VALUE = -1e30
HEAD_SKEW = 1

V7X_VMEM_LIMIT_BYTES = 56 * 1024 * 1024

FFN_TOKENS = 512
MIX_TOKENS = 512

BF16 = jnp.bfloat16
F32 = jnp.float32


def _resident(shape):
    nd = len(shape)
    return pl.BlockSpec(shape, lambda *_: (0,) * nd, pipeline_mode=pl.Buffered(1))


def _dot(a, b):
    return jnp.dot(a, b, preferred_element_type=F32)


def _dot_nt(a, b):
    return lax.dot_general(a, b, (((1,), (1,)), ((), ())), preferred_element_type=F32)


def _rms(x, g):
    return x * lax.rsqrt(jnp.mean(x * x, axis=-1, keepdims=True) + RMS_EPS) * g


def _swiglu_half_step(x, g_ref, wg_ref, wu_ref, wd_ref):
    xn = _rms(x, g_ref[...]).astype(BF16)
    gate = _dot(xn, wg_ref[...])
    up = _dot(xn, wu_ref[...])
    act = (gate * jax.nn.sigmoid(gate) * up).astype(BF16)
    return x + 0.5 * _dot(act, wd_ref[...])


def _rpb_bucket(dist):
    n = jnp.maximum(dist, 0)
    max_exact = RPB_BUCKETS // 2
    nf = jnp.maximum(n, 1).astype(F32)
    large = max_exact + (jnp.log(nf / max_exact) / math.log(RPB_MAX_DISTANCE / max_exact)
                         * (RPB_BUCKETS - max_exact)).astype(jnp.int32)
    large = jnp.minimum(large, RPB_BUCKETS - 1)
    return jnp.where(n < max_exact, n, large)


def _rpb_bias_kernel(table_ref, diag_ref, sub_ref):
    h = pl.program_id(0)
    kl = lax.broadcasted_iota(jnp.int32, (MOBA_BLOCK, MOBA_BLOCK), 0)
    ql = lax.broadcasted_iota(jnp.int32, (MOBA_BLOCK, MOBA_BLOCK), 1)
    for out_ref, offset in ((diag_ref, 0), (sub_ref, MOBA_BLOCK)):
        dist = ql - kl + offset
        bucket = _rpb_bucket(dist)
        bias = jnp.zeros((MOBA_BLOCK, MOBA_BLOCK), F32)
        for b in range(RPB_BUCKETS):
            bias = jnp.where(bucket == b, table_ref[b, h], bias)
        if offset == 0:
            bias = jnp.where(dist >= 0, bias, MASK_VALUE)
        out_ref[0] = bias


def _rpb_bias(rpb_table):
    tile = jax.ShapeDtypeStruct((ATTN_HEADS, MOBA_BLOCK, MOBA_BLOCK), F32)
    spec = pl.BlockSpec((1, MOBA_BLOCK, MOBA_BLOCK), lambda h: (h, 0, 0))
    return pl.pallas_call(
        _rpb_bias_kernel,
        grid=(ATTN_HEADS,),
        in_specs=[pl.BlockSpec(memory_space=pltpu.SMEM)],
        out_specs=[spec, spec],
        out_shape=[tile, tile],
        name="rpb_bias",
    )(rpb_table)


def _ffn1_qkv_kernel(x_ref, g1_ref, wg_ref, wu_ref, wd_ref, gm_ref, wk_ref, wqv_t_ref,
                     h1_ref, un_ref, qt_ref, k_ref, vt_ref):
    h1 = _swiglu_half_step(x_ref[0], g1_ref, wg_ref, wu_ref, wd_ref)
    h1_ref[0] = h1
    un = _rms(h1, gm_ref[...]).astype(BF16)
    un_ref[0] = un
    k_ref[0] = _dot(un, wk_ref[...]).astype(BF16)
    qv_t = _dot_nt(wqv_t_ref[...], un)
    qt_ref[0] = (qv_t[:ATTN_WIDTH] * (HEAD_DIM ** -0.5)).astype(BF16)
    for c in range(FFN_TOKENS // MOBA_BLOCK):
        vt_ref[0, c] = qv_t[ATTN_WIDTH:, c * MOBA_BLOCK:(c + 1) * MOBA_BLOCK].astype(BF16)


def _ffn1_qkv(x, g1, wg, wu, wd, gm, wk, wqv_t):
    B, S, D = x.shape
    tm = FFN_TOKENS
    nblk = S // MOBA_BLOCK
    tok = lambda width, dt: (pl.BlockSpec((1, tm, width), lambda b, i: (b, i, 0)),
                             jax.ShapeDtypeStruct((B, S, width), dt))
    h1_spec, h1_shape = tok(D, F32)
    un_spec, un_shape = tok(D, BF16)
    k_spec, k_shape = tok(ATTN_WIDTH, BF16)
    return pl.pallas_call(
        _ffn1_qkv_kernel,
        grid=(B, S // tm),
        in_specs=[pl.BlockSpec((1, tm, D), lambda b, i: (b, i, 0)),
                  _resident(g1.shape), _resident(wg.shape), _resident(wu.shape), _resident(wd.shape),
                  _resident(gm.shape), _resident(wk.shape), _resident(wqv_t.shape)],
        out_specs=[h1_spec, un_spec,
                   pl.BlockSpec((1, ATTN_WIDTH, tm), lambda b, i: (b, 0, i)),
                   k_spec,
                   pl.BlockSpec((1, tm // MOBA_BLOCK, ATTN_WIDTH, MOBA_BLOCK), lambda b, i: (b, i, 0, 0))],
        out_shape=[h1_shape, un_shape,
                   jax.ShapeDtypeStruct((B, ATTN_WIDTH, S), BF16),
                   k_shape,
                   jax.ShapeDtypeStruct((B, nblk, ATTN_WIDTH, MOBA_BLOCK), BF16)],
        compiler_params=pltpu.CompilerParams(
            dimension_semantics=("arbitrary", "arbitrary"), vmem_limit_bytes=V7X_VMEM_LIMIT_BYTES),
        name="ffn1_qkv",
    )(x, g1, wg, wu, wd, gm, wk, wqv_t)


def _moba_attn_kernel(t31_ref, qt_ref, k_ref, vt_ref, diag_ref, sub_ref, a_ref,
                      kmhi_ref, kmlo_ref, rb_far_ref, rb_sub_ref, at_ref, qz_ref, m_ref, l_ref):
    qi = pl.program_id(1)
    nblk = k_ref.shape[1] // MOBA_BLOCK
    nq = MOBA_BLOCK

    @pl.when(qi == 0)
    def _():
        blk_row = lax.broadcasted_iota(jnp.int32, (nblk, ATTN_WIDTH), 0)
        km = jnp.zeros((nblk, ATTN_WIDTH), F32)
        for j in range(nblk):
            mean_j = jnp.mean(k_ref[0, j * MOBA_BLOCK:(j + 1) * MOBA_BLOCK, :].astype(F32), axis=0, keepdims=True)
            km = jnp.where(blk_row == j, mean_j, km)
        km = jnp.concatenate([km] * ATTN_HEADS, axis=0)
        row_head = lax.broadcasted_iota(jnp.int32, km.shape, 0) // nblk
        col_head = lax.broadcasted_iota(jnp.int32, km.shape, 1) // HEAD_DIM
        km = jnp.where(row_head == col_head, km, 0.0)
        hi = km.astype(BF16)
        kmhi_ref[...] = hi
        kmlo_ref[...] = (km - hi.astype(F32)).astype(BF16)

    qt = qt_ref[0]
    gates = _dot(kmhi_ref[...], qt) + _dot(kmlo_ref[...], qt)
    jrow = lax.broadcasted_iota(jnp.int32, (nblk, nq), 0)
    for h in range(ATTN_HEADS):
        g = gates[h * nblk:(h + 1) * nblk]
        cnt = jnp.zeros((nblk, nq), jnp.int32)
        for jp in range(nblk):
            gb = jnp.broadcast_to(g[jp:jp + 1], (nblk, nq))
            beats = (gb > g) | ((gb == g) & (jp < jrow))
            cnt = cnt + jnp.where(beats, jnp.where(jp < qi, 1, 0), 0)
        sel = (jrow < qi) & (cnt < MOBA_TOPK)
        far = jnp.where(sel, t31_ref[h], MASK_VALUE)
        sub = jnp.where(sel, 0.0, MASK_VALUE)
        for j in range(nblk):
            rb_far_ref[h * nblk + j] = far[j:j + 1]
            rb_sub_ref[h * nblk + j] = sub[j:j + 1]

    zeros_half = jnp.zeros((HEAD_DIM, nq), BF16)
    for h in range(ATTN_HEADS):
        qh = qt_ref[0, h * HEAD_DIM:(h + 1) * HEAD_DIM, :]
        qz_ref[h] = jnp.concatenate([qh, zeros_half] if h % 2 == 0 else [zeros_half, qh], axis=0)

    def key_block(j, kind):
        start = pl.multiple_of(j * MOBA_BLOCK, MOBA_BLOCK)
        scores = []
        for h in range(ATTN_HEADS):
            pair = h // 2
            kblk = k_ref[0, pl.ds(start, MOBA_BLOCK), pair * 128:(pair + 1) * 128]
            s = _dot(kblk, qz_ref[h])
            if kind == "diag":
                s = s + diag_ref[h]
            elif kind == "sub":
                s = s + sub_ref[h] + rb_sub_ref[h * nblk + j]
            else:
                s = s + rb_far_ref[h * nblk + j]
            scores.append(s)
        for h in range(ATTN_HEADS):
            rows = slice(h * HEAD_DIM, (h + 1) * HEAD_DIM)
            s = scores[h]
            m_blk = jnp.max(s, axis=0, keepdims=True)
            m_new = m_blk if kind == "diag" else jnp.maximum(m_ref[h:h + 1], m_blk)
            if h + HEAD_SKEW < ATTN_HEADS:
                m_new = jnp.maximum(m_new, jnp.minimum(scores[h + HEAD_SKEW][-1:, :], MASK_VALUE))
            p = jnp.exp(s - m_new)
            l_blk = jnp.sum(p, axis=0, keepdims=True)
            pv = _dot(vt_ref[0, j, rows, :], p.astype(BF16))
            if kind == "diag":
                l_ref[h:h + 1] = l_blk
                at_ref[rows] = pv
            else:
                alpha = jnp.exp(m_ref[h:h + 1] - m_new)
                l_ref[h:h + 1] = alpha * l_ref[h:h + 1] + l_blk
                at_ref[rows] = alpha * at_ref[rows] + pv
            m_ref[h:h + 1] = m_new

    j_sub = jnp.maximum(qi - 1, 0)
    key_block(qi, "diag")
    key_block(j_sub, "sub")

    def far_body(j, carry):
        key_block(j, "far")
        return carry

    lax.fori_loop(0, j_sub, far_body, 0)
    for h in range(ATTN_HEADS):
        rows = slice(h * HEAD_DIM, (h + 1) * HEAD_DIM)
        at_ref[rows] = at_ref[rows] / l_ref[h:h + 1]

    a_ref[0] = at_ref[...].T.astype(BF16)


def _moba_attn(qt, k, vt, diag, sub, t31):
    B, S, _ = k.shape
    nblk = S // MOBA_BLOCK
    tile = pl.BlockSpec((ATTN_HEADS, MOBA_BLOCK, MOBA_BLOCK), lambda b, i: (0, 0, 0), pipeline_mode=pl.Buffered(1))
    return pl.pallas_call(
        _moba_attn_kernel,
        grid=(B, nblk),
        in_specs=[pl.BlockSpec(memory_space=pltpu.SMEM),
                  pl.BlockSpec((1, ATTN_WIDTH, MOBA_BLOCK), lambda b, i: (b, 0, i)),
                  pl.BlockSpec((1, S, ATTN_WIDTH), lambda b, i: (b, 0, 0)),
                  pl.BlockSpec((1, nblk, ATTN_WIDTH, MOBA_BLOCK), lambda b, i: (b, 0, 0, 0)),
                  tile, tile],
        out_specs=pl.BlockSpec((1, MOBA_BLOCK, ATTN_WIDTH), lambda b, i: (b, i, 0)),
        out_shape=jax.ShapeDtypeStruct((B, S, ATTN_WIDTH), BF16),
        scratch_shapes=[pltpu.VMEM((ATTN_HEADS * nblk, ATTN_WIDTH), BF16),
                        pltpu.VMEM((ATTN_HEADS * nblk, ATTN_WIDTH), BF16),
                        pltpu.VMEM((ATTN_HEADS * nblk, 1, MOBA_BLOCK), F32),
                        pltpu.VMEM((ATTN_HEADS * nblk, 1, MOBA_BLOCK), F32),
                        pltpu.VMEM((ATTN_WIDTH, MOBA_BLOCK), F32),
                        pltpu.VMEM((ATTN_HEADS, 2 * HEAD_DIM, MOBA_BLOCK), BF16),
                        pltpu.VMEM((ATTN_HEADS, MOBA_BLOCK), F32),
                        pltpu.VMEM((ATTN_HEADS, MOBA_BLOCK), F32)],
        compiler_params=pltpu.CompilerParams(
            dimension_semantics=("arbitrary", "arbitrary"), vmem_limit_bytes=V7X_VMEM_LIMIT_BYTES),
        name="moba_attn",
    )(t31, qt, k, vt, diag, sub)


def _mix_merge_kernel(h1_ref, un_ref, unprev_ref, a_ref, wzg_ref, wgrp_ref, scale_ref,
                      wa_ref, wp_ref, wout_ref, h2_ref, zext_ref):
    i = pl.program_id(1)
    tm = MIX_TOKENS
    zg = _dot(un_ref[0], wzg_ref[...])
    z = zg[:, :POOL_WIDTH]
    zprev = _dot(unprev_ref[0], wzg_ref[:, :POOL_WIDTH])
    zext_ref[:POOL_HALO] = jnp.where(i > 0, zprev, 0.0)
    zext_ref[POOL_HALO:] = z

    tpos = i * tm + lax.broadcasted_iota(jnp.int32, (tm, POOL_GROUP_WIDTH), 0)
    mixed = []
    for g, w in enumerate(POOL_WINDOWS):
        cols = slice(g * POOL_GROUP_WIDTH, (g + 1) * POOL_GROUP_WIDTH)
        wsum = zext_ref[POOL_HALO:, cols]
        for lag in range(1, w):
            wsum = wsum + zext_ref[POOL_HALO - lag:POOL_HALO - lag + tm, cols]
        mean = wsum / jnp.minimum(tpos + 1, w).astype(F32)
        pooled = (mean - zext_ref[POOL_HALO:, cols]).astype(BF16)
        mixed.append(_dot(pooled, wgrp_ref[g]))
    p = (jnp.concatenate(mixed, axis=1) * scale_ref[...]).astype(BF16)

    g_attn = zg[:, POOL_WIDTH:POOL_WIDTH + D_MODEL]
    g_pool = zg[:, POOL_WIDTH + D_MODEL:]
    merged = (jax.nn.sigmoid(g_attn) * _dot(a_ref[0], wa_ref[...])
              + jax.nn.sigmoid(g_pool) * _dot(p, wp_ref[...]))
    h2_ref[0] = h1_ref[0] + _dot(merged.astype(BF16), wout_ref[...])


def _mix_merge(h1, un, a, wzg, wgrp, scale, wa, wp, wout):
    B, S, D = h1.shape
    tm = MIX_TOKENS
    halo_per_tile = tm // POOL_HALO
    return pl.pallas_call(
        _mix_merge_kernel,
        grid=(B, S // tm),
        in_specs=[pl.BlockSpec((1, tm, D), lambda b, i: (b, i, 0)),
                  pl.BlockSpec((1, tm, D), lambda b, i: (b, i, 0)),
                  pl.BlockSpec((1, POOL_HALO, D), lambda b, i: (b, jnp.maximum(i * halo_per_tile - 1, 0), 0)),
                  pl.BlockSpec((1, tm, ATTN_WIDTH), lambda b, i: (b, i, 0)),
                  _resident(wzg.shape), _resident(wgrp.shape), _resident(scale.shape),
                  _resident(wa.shape), _resident(wp.shape), _resident(wout.shape)],
        out_specs=pl.BlockSpec((1, tm, D), lambda b, i: (b, i, 0)),
        out_shape=jax.ShapeDtypeStruct((B, S, D), F32),
        scratch_shapes=[pltpu.VMEM((POOL_HALO + tm, POOL_WIDTH), F32)],
        compiler_params=pltpu.CompilerParams(
            dimension_semantics=("arbitrary", "arbitrary"), vmem_limit_bytes=V7X_VMEM_LIMIT_BYTES),
        name="mix_merge",
    )(h1, un, un, a, wzg, wgrp, scale, wa, wp, wout)


def _ffn2_final_kernel(h_ref, g2_ref, wg_ref, wu_ref, wd_ref, gf_ref, out_ref):
    h3 = _swiglu_half_step(h_ref[0], g2_ref, wg_ref, wu_ref, wd_ref)
    out_ref[0] = _rms(h3, gf_ref[...])


def _ffn2_final(h, g2, wg, wu, wd, gf):
    B, S, D = h.shape
    tm = FFN_TOKENS
    return pl.pallas_call(
        _ffn2_final_kernel,
        grid=(B, S // tm),
        in_specs=[pl.BlockSpec((1, tm, D), lambda b, i: (b, i, 0)),
                  _resident(g2.shape), _resident(wg.shape), _resident(wu.shape), _resident(wd.shape),
                  _resident(gf.shape)],
        out_specs=pl.BlockSpec((1, tm, D), lambda b, i: (b, i, 0)),
        out_shape=jax.ShapeDtypeStruct((B, S, D), F32),
        compiler_params=pltpu.CompilerParams(
            dimension_semantics=("arbitrary", "arbitrary"), vmem_limit_bytes=V7X_VMEM_LIMIT_BYTES),
        name="ffn2_final",
    )(h, g2, wg, wu, wd, gf)


def kernel(x, ffn1_norm, ffn1_w_gate, ffn1_w_up, ffn1_w_down, mix_norm, w_in, pool_w_group, pool_scale,
           w_branch_attn, w_branch_pool, w_out, ffn2_norm, ffn2_w_gate, ffn2_w_up, ffn2_w_down,
           rpb_table, final_norm):
    B, S, D = x.shape
    assert (D, ffn1_w_gate.shape[0]) == (D_MODEL, 1)
    assert S % MOBA_BLOCK == 0 and S % FFN_TOKENS == 0 and S % MIX_TOKENS == 0
    bf = lambda w: w.astype(BF16)
    row = lambda v: v.reshape(1, -1)
    w_in = w_in[0]
    wq, wk, wv = (w_in[:, c * ATTN_WIDTH:(c + 1) * ATTN_WIDTH] for c in range(3))
    wqv_t = bf(jnp.concatenate([wq, wv], axis=1).T)
    wzg = bf(w_in[:, 3 * ATTN_WIDTH:])

    diag, sub = _rpb_bias(rpb_table)
    h1, un, qt, k, vt = _ffn1_qkv(x, row(ffn1_norm[0]), bf(ffn1_w_gate[0]), bf(ffn1_w_up[0]),
                                  bf(ffn1_w_down[0]), row(mix_norm[0]), bf(wk), wqv_t)
    a = _moba_attn(qt, k, vt, diag, sub, rpb_table[RPB_BUCKETS - 1])
    h2 = _mix_merge(h1, un, a, wzg, bf(pool_w_group[0]), row(pool_scale[0]),
                    bf(w_branch_attn[0]), bf(w_branch_pool[0]), bf(w_out[0]))
    return _ffn2_final(h2, row(ffn2_norm[0]), bf(ffn2_w_gate[0]), bf(ffn2_w_up[0]), bf(ffn2_w_down[0]),
                       row(final_norm))
```

```python
import functools
import math

import jax
import jax.numpy as jnp
from jax import lax
from jax.experimental import pallas as pl
from jax.experimental.pallas import tpu as pltpu

D_MODEL = 1024
HEAD_DIM = 64
ATTN_WIDTH = 512
ATTN_HEADS = 8
MOBA_BLOCK = 256
MOBA_TOPK = 3
POOL_WINDOWS = (2, 4, 8, 16)
POOL_WIDTH = 512
POOL_GROUP_WIDTH = 128
POOL_HALO = 16
D_FF = 2816
RPB_BUCKETS = 32
RPB_MAX_DISTANCE = 128
RMS_EPS = 1e-6
MASK_VALUE = -1e30

V7X_VMEM_LIMIT_BYTES = 56 * 1024 * 1024

FFN_TOKENS = 512
MIX_TOKENS = 512

BF16 = jnp.bfloat16
F32 = jnp.float32


def _resident(shape):
    nd = len(shape)
    return pl.BlockSpec(shape, lambda *_: (0,) * nd, pipeline_mode=pl.Buffered(1))


def _dot(a, b):
    return jnp.dot(a, b, preferred_element_type=F32)


def _dot_nt(a, b):
    return lax.dot_general(a, b, (((1,), (1,)), ((), ())), preferred_element_type=F32)


def _rms(x, g):
    return x * lax.rsqrt(jnp.mean(x * x, axis=-1, keepdims=True) + RMS_EPS) * g


def _swiglu_half_step(x, g_ref, wg_ref, wu_ref, wd_ref):
    xn = _rms(x, g_ref[...]).astype(BF16)
    gate = _dot(xn, wg_ref[...])
    up = _dot(xn, wu_ref[...])
    act = (gate * jax.nn.sigmoid(gate) * up).astype(BF16)
    return x + 0.5 * _dot(act, wd_ref[...])


def _rpb_bucket(dist):
    n = jnp.maximum(dist, 0)
    max_exact = RPB_BUCKETS // 2
    nf = jnp.maximum(n, 1).astype(F32)
    large = max_exact + (jnp.log(nf / max_exact) / math.log(RPB_MAX_DISTANCE / max_exact)
                         * (RPB_BUCKETS - max_exact)).astype(jnp.int32)
    large = jnp.minimum(large, RPB_BUCKETS - 1)
    return jnp.where(n < max_exact, n, large)


def _rpb_bias_kernel(table_ref, diag_ref, sub_ref):
    h = pl.program_id(0)
    kl = lax.broadcasted_iota(jnp.int32, (MOBA_BLOCK, MOBA_BLOCK), 0)
    ql = lax.broadcasted_iota(jnp.int32, (MOBA_BLOCK, MOBA_BLOCK), 1)
    for out_ref, offset in ((diag_ref, 0), (sub_ref, MOBA_BLOCK)):
        dist = ql - kl + offset
        bucket = _rpb_bucket(dist)
        bias = jnp.zeros((MOBA_BLOCK, MOBA_BLOCK), F32)
        for b in range(RPB_BUCKETS):
            bias = jnp.where(bucket == b, table_ref[b, h], bias)
        if offset == 0:
            bias = jnp.where(dist >= 0, bias, MASK_VALUE)
        out_ref[0] = bias


def _rpb_bias(rpb_table):
    tile = jax.ShapeDtypeStruct((ATTN_HEADS, MOBA_BLOCK, MOBA_BLOCK), F32)
    spec = pl.BlockSpec((1, MOBA_BLOCK, MOBA_BLOCK), lambda h: (h, 0, 0))
    return pl.pallas_call(
        _rpb_bias_kernel,
        grid=(ATTN_HEADS,),
        in_specs=[pl.BlockSpec(memory_space=pltpu.SMEM)],
        out_specs=[spec, spec],
        out_shape=[tile, tile],
        name="rpb_bias",
    )(rpb_table)


def _ffn1_qkv_kernel(x_ref, g1_ref, wg_ref, wu_ref, wd_ref, gm_ref, wk_ref, wqv_t_ref,
                     h1_ref, un_ref, qt_ref, k_ref, vt_ref):
    h1 = _swiglu_half_step(x_ref[0], g1_ref, wg_ref, wu_ref, wd_ref)
    h1_ref[0] = h1
    un = _rms(h1, gm_ref[...]).astype(BF16)
    un_ref[0] = un
    k_ref[0] = _dot(un, wk_ref[...]).astype(BF16)
    qv_t = _dot_nt(wqv_t_ref[...], un)
    qt_ref[0] = (qv_t[:ATTN_WIDTH] * (HEAD_DIM ** -0.5)).astype(BF16)
    for c in range(FFN_TOKENS // MOBA_BLOCK):
        vt_ref[0, c] = qv_t[ATTN_WIDTH:, c * MOBA_BLOCK:(c + 1) * MOBA_BLOCK].astype(BF16)


def _ffn1_qkv(x, g1, wg, wu, wd, gm, wk, wqv_t):
    B, S, D = x.shape
    tm = FFN_TOKENS
    nblk = S // MOBA_BLOCK
    tok = lambda width, dt: (pl.BlockSpec((1, tm, width), lambda b, i: (b, i, 0)),
                             jax.ShapeDtypeStruct((B, S, width), dt))
    h1_spec, h1_shape = tok(D, F32)
    un_spec, un_shape = tok(D, BF16)
    k_spec, k_shape = tok(ATTN_WIDTH, BF16)
    return pl.pallas_call(
        _ffn1_qkv_kernel,
        grid=(B, S // tm),
        in_specs=[pl.BlockSpec((1, tm, D), lambda b, i: (b, i, 0)),
                  _resident(g1.shape), _resident(wg.shape), _resident(wu.shape), _resident(wd.shape),
                  _resident(gm.shape), _resident(wk.shape), _resident(wqv_t.shape)],
        out_specs=[h1_spec, un_spec,
                   pl.BlockSpec((1, ATTN_WIDTH, tm), lambda b, i: (b, 0, i)),
                   k_spec,
                   pl.BlockSpec((1, tm // MOBA_BLOCK, ATTN_WIDTH, MOBA_BLOCK), lambda b, i: (b, i, 0, 0))],
        out_shape=[h1_shape, un_shape,
                   jax.ShapeDtypeStruct((B, ATTN_WIDTH, S), BF16),
                   k_shape,
                   jax.ShapeDtypeStruct((B, nblk, ATTN_WIDTH, MOBA_BLOCK), BF16)],
        compiler_params=pltpu.CompilerParams(
            dimension_semantics=("arbitrary", "arbitrary"), vmem_limit_bytes=V7X_VMEM_LIMIT_BYTES),
        name="ffn1_qkv",
    )(x, g1, wg, wu, wd, gm, wk, wqv_t)


def _moba_attn_kernel(t31_ref, qt_ref, k_ref, vt_ref, diag_ref, sub_ref, a_ref,
                      kmhi_ref, kmlo_ref, rb_far_ref, rb_sub_ref, at_ref, qz_ref,
                      m_ref, l_ref, mblk_ref, alpha_ref, s_ref, p_ref):
    qi = pl.program_id(1)
    nblk = k_ref.shape[1] // MOBA_BLOCK
    nq = MOBA_BLOCK

    @pl.when(qi == 0)
    def _():
        blk_row = lax.broadcasted_iota(jnp.int32, (nblk, ATTN_WIDTH), 0)
        km = jnp.zeros((nblk, ATTN_WIDTH), F32)
        for j in range(nblk):
            mean_j = jnp.mean(k_ref[0, j * MOBA_BLOCK:(j + 1) * MOBA_BLOCK, :].astype(F32), axis=0, keepdims=True)
            km = jnp.where(blk_row == j, mean_j, km)
        km = jnp.concatenate([km] * ATTN_HEADS, axis=0)
        row_head = lax.broadcasted_iota(jnp.int32, km.shape, 0) // nblk
        col_head = lax.broadcasted_iota(jnp.int32, km.shape, 1) // HEAD_DIM
        km = jnp.where(row_head == col_head, km, 0.0)
        hi = km.astype(BF16)
        kmhi_ref[...] = hi
        kmlo_ref[...] = (km - hi.astype(F32)).astype(BF16)

    qt = qt_ref[0]
    gates = _dot(kmhi_ref[...], qt) + _dot(kmlo_ref[...], qt)
    jrow = lax.broadcasted_iota(jnp.int32, (nblk, nq), 0)
    for h in range(ATTN_HEADS):
        g = gates[h * nblk:(h + 1) * nblk]
        cnt = jnp.zeros((nblk, nq), jnp.int32)
        for jp in range(nblk):
            gb = jnp.broadcast_to(g[jp:jp + 1], (nblk, nq))
            beats = (gb > g) | ((gb == g) & (jp < jrow))
            cnt = cnt + jnp.where(beats, jnp.where(jp < qi, 1, 0), 0)
        sel = (jrow < qi) & (cnt < MOBA_TOPK)
        far = jnp.where(sel, t31_ref[h], MASK_VALUE)
        sub = jnp.where(sel, 0.0, MASK_VALUE)
        for j in range(nblk):
            rb_far_ref[h * nblk + j] = far[j:j + 1]
            rb_sub_ref[h * nblk + j] = sub[j:j + 1]

    zeros_half = jnp.zeros((HEAD_DIM, nq), BF16)
    for h in range(ATTN_HEADS):
        qh = qt_ref[0, h * HEAD_DIM:(h + 1) * HEAD_DIM, :]
        qz_ref[h] = jnp.concatenate([qh, zeros_half] if h % 2 == 0 else [zeros_half, qh], axis=0)

    def scores_stage(j, kind):
        start = pl.multiple_of(j * MOBA_BLOCK, MOBA_BLOCK)
        for h in range(ATTN_HEADS):
            pair = h // 2
            kblk = k_ref[0, pl.ds(start, MOBA_BLOCK), pair * 128:(pair + 1) * 128]
            s = _dot(kblk, qz_ref[h])
            if kind == "diag":
                s = s + diag_ref[h]
            elif kind == "sub":
                s = s + sub_ref[h]
            s_ref[h] = s
            mblk_ref[h] = jnp.max(s, axis=0, keepdims=True)

    def softmax_stage(row_bias):
        for h in range(ATTN_HEADS):
            rb = row_bias(h)
            m_old = m_ref[h]
            m_blk = mblk_ref[h] if rb is None else mblk_ref[h] + rb
            m_new = jnp.maximum(m_old, m_blk)
            alpha = jnp.exp(m_old - m_new)
            offset = m_new if rb is None else m_new - rb
            p = jnp.exp(s_ref[h] - offset)
            l_ref[h] = alpha * l_ref[h] + jnp.sum(p, axis=0, keepdims=True)
            p_ref[h] = p.astype(BF16)
            alpha_ref[h] = alpha
            m_ref[h] = m_new

    def values_stage(j):
        for h in range(ATTN_HEADS):
            rows = slice(h * HEAD_DIM, (h + 1) * HEAD_DIM)
            pv = _dot(vt_ref[0, j, rows, :], p_ref[h])
            at_ref[rows] = alpha_ref[h] * at_ref[rows] + pv

    m_ref[...] = jnp.full(m_ref.shape, MASK_VALUE, F32)
    l_ref[...] = jnp.zeros(l_ref.shape, F32)
    at_ref[...] = jnp.zeros(at_ref.shape, F32)
    j_sub = jnp.maximum(qi - 1, 0)
    n_far = j_sub

    scores_stage(qi, "diag")
    softmax_stage(lambda h: None)
    scores_stage(j_sub, "sub")
    values_stage(qi)
    softmax_stage(lambda h: rb_sub_ref[h * nblk + j_sub])
    scores_stage(0, "far")

    def far_body(t, carry):
        values_stage(jnp.where(t == 0, j_sub, t - 1))
        softmax_stage(lambda h: rb_far_ref[h * nblk + t])
        scores_stage(jnp.minimum(t + 1, jnp.maximum(n_far - 1, 0)), "far")
        return carry

    lax.fori_loop(0, n_far, far_body, 0)
    values_stage(jnp.where(n_far > 0, n_far - 1, j_sub))
    for h in range(ATTN_HEADS):
        rows = slice(h * HEAD_DIM, (h + 1) * HEAD_DIM)
        at_ref[rows] = at_ref[rows] / l_ref[h]

    a_ref[0] = at_ref[...].T.astype(BF16)


def _moba_attn(qt, k, vt, diag, sub, t31):
    B, S, _ = k.shape
    nblk = S // MOBA_BLOCK
    tile = pl.BlockSpec((ATTN_HEADS, MOBA_BLOCK, MOBA_BLOCK), lambda b, i: (0, 0, 0), pipeline_mode=pl.Buffered(1))
    return pl.pallas_call(
        _moba_attn_kernel,
        grid=(B, nblk),
        in_specs=[pl.BlockSpec(memory_space=pltpu.SMEM),
                  pl.BlockSpec((1, ATTN_WIDTH, MOBA_BLOCK), lambda b, i: (b, 0, i)),
                  pl.BlockSpec((1, S, ATTN_WIDTH), lambda b, i: (b, 0, 0)),
                  pl.BlockSpec((1, nblk, ATTN_WIDTH, MOBA_BLOCK), lambda b, i: (b, 0, 0, 0)),
                  tile, tile],
        out_specs=pl.BlockSpec((1, MOBA_BLOCK, ATTN_WIDTH), lambda b, i: (b, i, 0)),
        out_shape=jax.ShapeDtypeStruct((B, S, ATTN_WIDTH), BF16),
        scratch_shapes=[pltpu.VMEM((ATTN_HEADS * nblk, ATTN_WIDTH), BF16),
                        pltpu.VMEM((ATTN_HEADS * nblk, ATTN_WIDTH), BF16),
                        pltpu.VMEM((ATTN_HEADS * nblk, 1, MOBA_BLOCK), F32),
                        pltpu.VMEM((ATTN_HEADS * nblk, 1, MOBA_BLOCK), F32),
                        pltpu.VMEM((ATTN_WIDTH, MOBA_BLOCK), F32),
                        pltpu.VMEM((ATTN_HEADS, 2 * HEAD_DIM, MOBA_BLOCK), BF16),
                        *[pltpu.VMEM((ATTN_HEADS, 1, MOBA_BLOCK), F32)] * 4,
                        pltpu.VMEM((ATTN_HEADS, MOBA_BLOCK, MOBA_BLOCK), F32),
                        pltpu.VMEM((ATTN_HEADS, MOBA_BLOCK, MOBA_BLOCK), BF16)],
        compiler_params=pltpu.CompilerParams(
            dimension_semantics=("arbitrary", "arbitrary"), vmem_limit_bytes=V7X_VMEM_LIMIT_BYTES),
        name="moba_attn",
    )(t31, qt, k, vt, diag, sub)


def _mix_merge_kernel(h1_ref, un_ref, unprev_ref, a_ref, wzg_ref, wgrp_ref, scale_ref,
                      wa_ref, wp_ref, wout_ref, h2_ref, zext_ref):
    i = pl.program_id(1)
    tm = MIX_TOKENS
    zg = _dot(un_ref[0], wzg_ref[...])
    z = zg[:, :POOL_WIDTH]
    zprev = _dot(unprev_ref[0], wzg_ref[:, :POOL_WIDTH])
    zext_ref[:POOL_HALO] = jnp.where(i > 0, zprev, 0.0)
    zext_ref[POOL_HALO:] = z

    tpos = i * tm + lax.broadcasted_iota(jnp.int32, (tm, POOL_GROUP_WIDTH), 0)
    mixed = []
    for g, w in enumerate(POOL_WINDOWS):
        cols = slice(g * POOL_GROUP_WIDTH, (g + 1) * POOL_GROUP_WIDTH)
        wsum = zext_ref[POOL_HALO:, cols]
        for lag in range(1, w):
            wsum = wsum + zext_ref[POOL_HALO - lag:POOL_HALO - lag + tm, cols]
        mean = wsum / jnp.minimum(tpos + 1, w).astype(F32)
        pooled = (mean - zext_ref[POOL_HALO:, cols]).astype(BF16)
        mixed.append(_dot(pooled, wgrp_ref[g]))
    p = (jnp.concatenate(mixed, axis=1) * scale_ref[...]).astype(BF16)

    g_attn = zg[:, POOL_WIDTH:POOL_WIDTH + D_MODEL]
    g_pool = zg[:, POOL_WIDTH + D_MODEL:]
    merged = (jax.nn.sigmoid(g_attn) * _dot(a_ref[0], wa_ref[...])
              + jax.nn.sigmoid(g_pool) * _dot(p, wp_ref[...]))
    h2_ref[0] = h1_ref[0] + _dot(merged.astype(BF16), wout_ref[...])


def _mix_merge(h1, un, a, wzg, wgrp, scale, wa, wp, wout):
    B, S, D = h1.shape
    tm = MIX_TOKENS
    halo_per_tile = tm // POOL_HALO
    return pl.pallas_call(
        _mix_merge_kernel,
        grid=(B, S // tm),
        in_specs=[pl.BlockSpec((1, tm, D), lambda b, i: (b, i, 0)),
                  pl.BlockSpec((1, tm, D), lambda b, i: (b, i, 0)),
                  pl.BlockSpec((1, POOL_HALO, D), lambda b, i: (b, jnp.maximum(i * halo_per_tile - 1, 0), 0)),
                  pl.BlockSpec((1, tm, ATTN_WIDTH), lambda b, i: (b, i, 0)),
                  _resident(wzg.shape), _resident(wgrp.shape), _resident(scale.shape),
                  _resident(wa.shape), _resident(wp.shape), _resident(wout.shape)],
        out_specs=pl.BlockSpec((1, tm, D), lambda b, i: (b, i, 0)),
        out_shape=jax.ShapeDtypeStruct((B, S, D), F32),
        scratch_shapes=[pltpu.VMEM((POOL_HALO + tm, POOL_WIDTH), F32)],
        compiler_params=pltpu.CompilerParams(
            dimension_semantics=("arbitrary", "arbitrary"), vmem_limit_bytes=V7X_VMEM_LIMIT_BYTES),
        name="mix_merge",
    )(h1, un, un, a, wzg, wgrp, scale, wa, wp, wout)


def _ffn2_final_kernel(h_ref, g2_ref, wg_ref, wu_ref, wd_ref, gf_ref, out_ref):
    h3 = _swiglu_half_step(h_ref[0], g2_ref, wg_ref, wu_ref, wd_ref)
    out_ref[0] = _rms(h3, gf_ref[...])


def _ffn2_final(h, g2, wg, wu, wd, gf):
    B, S, D = h.shape
    tm = FFN_TOKENS
    return pl.pallas_call(
        _ffn2_final_kernel,
        grid=(B, S // tm),
        in_specs=[pl.BlockSpec((1, tm, D), lambda b, i: (b, i, 0)),
                  _resident(g2.shape), _resident(wg.shape), _resident(wu.shape), _resident(wd.shape),
                  _resident(gf.shape)],
        out_specs=pl.BlockSpec((1, tm, D), lambda b, i: (b, i, 0)),
        out_shape=jax.ShapeDtypeStruct((B, S, D), F32),
        compiler_params=pltpu.CompilerParams(
            dimension_semantics=("arbitrary", "arbitrary"), vmem_limit_bytes=V7X_VMEM_LIMIT_BYTES),
        name="ffn2_final",
    )(h, g2, wg, wu, wd, gf)


def kernel(x, ffn1_norm, ffn1_w_gate, ffn1_w_up, ffn1_w_down, mix_norm, w_in, pool_w_group, pool_scale,
           w_branch_attn, w_branch_pool, w_out, ffn2_norm, ffn2_w_gate, ffn2_w_up, ffn2_w_down,
           rpb_table, final_norm):
    B, S, D = x.shape
    assert (D, ffn1_w_gate.shape[0]) == (D_MODEL, 1)
    assert S % MOBA_BLOCK == 0 and S % FFN_TOKENS == 0 and S % MIX_TOKENS == 0
    bf = lambda w: w.astype(BF16)
    row = lambda v: v.reshape(1, -1)
    w_in = w_in[0]
    wq, wk, wv = (w_in[:, c * ATTN_WIDTH:(c + 1) * ATTN_WIDTH] for c in range(3))
    wqv_t = bf(jnp.concatenate([wq, wv], axis=1).T)
    wzg = bf(w_in[:, 3 * ATTN_WIDTH:])

    diag, sub = _rpb_bias(rpb_table)
    h1, un, qt, k, vt = _ffn1_qkv(x, row(ffn1_norm[0]), bf(ffn1_w_gate[0]), bf(ffn1_w_up[0]),
                                  bf(ffn1_w_down[0]), row(mix_norm[0]), bf(wk), wqv_t)
    a = _moba_attn(qt, k, vt, diag, sub, rpb_table[RPB_BUCKETS - 1])
    h2 = _mix_merge(h1, un, a, wzg, bf(pool_w_group[0]), row(pool_scale[0]),
                    bf(w_branch_attn[0]), bf(w_branch_pool[0]), bf(w_out[0]))
    return _ffn2_final(h2, row(ffn2_norm[0]), bf(ffn2_w_gate[0]), bf(ffn2_w_up[0]), bf(ffn2_w_down[0]),
                       row(final_norm))
```

```python
import math

import jax
import jax.numpy as jnp
from jax import lax
from jax.experimental import pallas as pl
from jax.experimental.pallas import tpu as pltpu

D_MODEL = 1024
HEAD_DIM = 64
ATTN_WIDTH = 512
ATTN_HEADS = 8
MOBA_BLOCK = 256
MOBA_TOPK = 3
POOL_WINDOWS = (2, 4, 8, 16)
POOL_WIDTH = 512
POOL_GROUP_WIDTH = 128
POOL_HALO = 16
D_FF = 2816
RPB_BUCKETS = 32
RPB_MAX_DISTANCE = 128
RMS_EPS = 1e-6
MASK_VALUE = -1e30

V7X_VMEM_LIMIT_BYTES = 56 * 1024 * 1024

FFN_TOKENS = 512
MIX_TOKENS = 512
N_LATER_WEIGHTS = 7

BF16 = jnp.bfloat16
F32 = jnp.float32


def _resident(shape):
    nd = len(shape)
    return pl.BlockSpec(shape, lambda *_: (0,) * nd, pipeline_mode=pl.Buffered(1))


def _dot(a, b):
    return jnp.dot(a, b, preferred_element_type=F32)


def _dot_nt(a, b):
    return lax.dot_general(a, b, (((1,), (1,)), ((), ())), preferred_element_type=F32)


def _rms(x, g):
    return x * lax.rsqrt(jnp.mean(x * x, axis=-1, keepdims=True) + RMS_EPS) * g


def _swiglu_half_step(x, g_ref, wg_ref, wu_ref, wd_ref):
    xn = _rms(x, g_ref[...]).astype(BF16)
    gate = _dot(xn, wg_ref[...])
    up = _dot(xn, wu_ref[...])
    act = (gate * jax.nn.sigmoid(gate) * up).astype(BF16)
    return x + 0.5 * _dot(act, wd_ref[...])


def _rpb_bucket(dist):
    n = jnp.maximum(dist, 0)
    max_exact = RPB_BUCKETS // 2
    nf = jnp.maximum(n, 1).astype(F32)
    large = max_exact + (jnp.log(nf / max_exact) / math.log(RPB_MAX_DISTANCE / max_exact)
                         * (RPB_BUCKETS - max_exact)).astype(jnp.int32)
    large = jnp.minimum(large, RPB_BUCKETS - 1)
    return jnp.where(n < max_exact, n, large)


def _rpb_bias_kernel(table_ref, diag_ref, sub_ref):
    h = pl.program_id(0)
    kl = lax.broadcasted_iota(jnp.int32, (MOBA_BLOCK, MOBA_BLOCK), 0)
    ql = lax.broadcasted_iota(jnp.int32, (MOBA_BLOCK, MOBA_BLOCK), 1)
    for out_ref, offset in ((diag_ref, 0), (sub_ref, MOBA_BLOCK)):
        dist = ql - kl + offset
        bucket = _rpb_bucket(dist)
        bias = jnp.zeros((MOBA_BLOCK, MOBA_BLOCK), F32)
        for b in range(RPB_BUCKETS):
            bias = jnp.where(bucket == b, table_ref[b, h], bias)
        if offset == 0:
            bias = jnp.where(dist >= 0, bias, MASK_VALUE)
        out_ref[0] = bias


def _rpb_bias(rpb_table):
    tile = jax.ShapeDtypeStruct((ATTN_HEADS, MOBA_BLOCK, MOBA_BLOCK), F32)
    spec = pl.BlockSpec((1, MOBA_BLOCK, MOBA_BLOCK), lambda h: (h, 0, 0))
    return pl.pallas_call(
        _rpb_bias_kernel,
        grid=(ATTN_HEADS,),
        in_specs=[pl.BlockSpec(memory_space=pltpu.SMEM)],
        out_specs=[spec, spec],
        out_shape=[tile, tile],
        name="rpb_bias",
    )(rpb_table)


def _ffn1_qkv_kernel(x_ref, g1_ref, wg_ref, wu_ref, wd_ref, gm_ref, wk_ref, wqv_t_ref, *refs):
    later_f32, (h1_ref, un_ref, qt_ref, k_ref, vt_ref), later_bf16 = (
        refs[:N_LATER_WEIGHTS], refs[N_LATER_WEIGHTS:N_LATER_WEIGHTS + 5], refs[N_LATER_WEIGHTS + 5:])
    for src_ref, dst_ref in zip(later_f32, later_bf16):
        dst_ref[...] = src_ref[...].astype(BF16)
    h1 = _swiglu_half_step(x_ref[0], g1_ref, wg_ref, wu_ref, wd_ref)
    h1_ref[0] = h1
    un = _rms(h1, gm_ref[...]).astype(BF16)
    un_ref[0] = un
    k_ref[0] = _dot(un, wk_ref[...]).astype(BF16)
    qv_t = _dot_nt(wqv_t_ref[...], un)
    qt_ref[0] = (qv_t[:ATTN_WIDTH] * (HEAD_DIM ** -0.5)).astype(BF16)
    for c in range(FFN_TOKENS // MOBA_BLOCK):
        vt_ref[0, c] = qv_t[ATTN_WIDTH:, c * MOBA_BLOCK:(c + 1) * MOBA_BLOCK].astype(BF16)


def _ffn1_qkv(x, g1, wg, wu, wd, gm, wk, wqv_t, later_weights):
    B, S, D = x.shape
    tm = FFN_TOKENS
    nblk = S // MOBA_BLOCK
    steps = B * (S // tm)
    assert len(later_weights) == N_LATER_WEIGHTS
    slabs = [w.reshape(steps, w.shape[0] // steps, w.shape[1]) for w in later_weights]
    slab_specs = [pl.BlockSpec((1,) + w.shape[1:], lambda b, i: (b * (S // tm) + i, 0, 0)) for w in slabs]
    tok = lambda width, dt: (pl.BlockSpec((1, tm, width), lambda b, i: (b, i, 0)),
                             jax.ShapeDtypeStruct((B, S, width), dt))
    h1_spec, h1_shape = tok(D, F32)
    un_spec, un_shape = tok(D, BF16)
    k_spec, k_shape = tok(ATTN_WIDTH, BF16)
    outs = pl.pallas_call(
        _ffn1_qkv_kernel,
        grid=(B, S // tm),
        in_specs=[pl.BlockSpec((1, tm, D), lambda b, i: (b, i, 0)),
                  _resident(g1.shape), _resident(wg.shape), _resident(wu.shape), _resident(wd.shape),
                  _resident(gm.shape), _resident(wk.shape), _resident(wqv_t.shape)] + slab_specs,
        out_specs=[h1_spec, un_spec,
                   pl.BlockSpec((1, ATTN_WIDTH, tm), lambda b, i: (b, 0, i)),
                   k_spec,
                   pl.BlockSpec((1, tm // MOBA_BLOCK, ATTN_WIDTH, MOBA_BLOCK), lambda b, i: (b, i, 0, 0))] + slab_specs,
        out_shape=[h1_shape, un_shape,
                   jax.ShapeDtypeStruct((B, ATTN_WIDTH, S), BF16),
                   k_shape,
                   jax.ShapeDtypeStruct((B, nblk, ATTN_WIDTH, MOBA_BLOCK), BF16)]
                  + [jax.ShapeDtypeStruct(w.shape, BF16) for w in slabs],
        compiler_params=pltpu.CompilerParams(
            dimension_semantics=("arbitrary", "arbitrary"), vmem_limit_bytes=V7X_VMEM_LIMIT_BYTES),
        name="ffn1_qkv",
    )(x, g1, wg, wu, wd, gm, wk, wqv_t, *slabs)
    return outs[:5], [o.reshape(w.shape) for o, w in zip(outs[5:], later_weights)]


def _moba_attn_kernel(t31_ref, qt_ref, k_ref, vt_ref, diag_ref, sub_ref, a_ref,
                      kmhi_ref, kmlo_ref, rb_far_ref, rb_sub_ref, at_ref, qz_ref,
                      m_ref, l_ref, mblk_ref, alpha_ref, s_ref, p_ref):
    qi = pl.program_id(1)
    nblk = k_ref.shape[1] // MOBA_BLOCK
    nq = MOBA_BLOCK

    @pl.when(qi == 0)
    def _():
        blk_row = lax.broadcasted_iota(jnp.int32, (nblk, ATTN_WIDTH), 0)
        km = jnp.zeros((nblk, ATTN_WIDTH), F32)
        for j in range(nblk):
            mean_j = jnp.mean(k_ref[0, j * MOBA_BLOCK:(j + 1) * MOBA_BLOCK, :].astype(F32), axis=0, keepdims=True)
            km = jnp.where(blk_row == j, mean_j, km)
        km = jnp.concatenate([km] * ATTN_HEADS, axis=0)
        row_head = lax.broadcasted_iota(jnp.int32, km.shape, 0) // nblk
        col_head = lax.broadcasted_iota(jnp.int32, km.shape, 1) // HEAD_DIM
        km = jnp.where(row_head == col_head, km, 0.0)
        hi = km.astype(BF16)
        kmhi_ref[...] = hi
        kmlo_ref[...] = (km - hi.astype(F32)).astype(BF16)

    qt = qt_ref[0]
    gates = _dot(kmhi_ref[...], qt) + _dot(kmlo_ref[...], qt)
    jrow = lax.broadcasted_iota(jnp.int32, (nblk, nq), 0)
    for h in range(ATTN_HEADS):
        g = gates[h * nblk:(h + 1) * nblk]
        cnt = jnp.zeros((nblk, nq), jnp.int32)
        for jp in range(nblk):
            gb = jnp.broadcast_to(g[jp:jp + 1], (nblk, nq))
            beats = (gb > g) | ((gb == g) & (jp < jrow))
            cnt = cnt + jnp.where(beats, jnp.where(jp < qi, 1, 0), 0)
        sel = (jrow < qi) & (cnt < MOBA_TOPK)
        far = jnp.where(sel, t31_ref[h], MASK_VALUE)
        sub = jnp.where(sel, 0.0, MASK_VALUE)
        for j in range(nblk):
            rb_far_ref[h * nblk + j] = far[j:j + 1]
            rb_sub_ref[h * nblk + j] = sub[j:j + 1]

    zeros_half = jnp.zeros((HEAD_DIM, nq), BF16)
    for h in range(ATTN_HEADS):
        qh = qt_ref[0, h * HEAD_DIM:(h + 1) * HEAD_DIM, :]
        qz_ref[h] = jnp.concatenate([qh, zeros_half] if h % 2 == 0 else [zeros_half, qh], axis=0)

    def scores_stage(j, kind):
        start = pl.multiple_of(j * MOBA_BLOCK, MOBA_BLOCK)
        for h in range(ATTN_HEADS):
            pair = h // 2
            kblk = k_ref[0, pl.ds(start, MOBA_BLOCK), pair * 128:(pair + 1) * 128]
            s = _dot(kblk, qz_ref[h])
            if kind == "diag":
                s = s + diag_ref[h]
            elif kind == "sub":
                s = s + sub_ref[h]
            s_ref[h] = s
            mblk_ref[h] = jnp.max(s, axis=0, keepdims=True)

    def softmax_stage(row_bias):
        for h in range(ATTN_HEADS):
            rb = row_bias(h)
            m_old = m_ref[h]
            m_blk = mblk_ref[h] if rb is None else mblk_ref[h] + rb
            m_new = jnp.maximum(m_old, m_blk)
            alpha = jnp.exp(m_old - m_new)
            offset = m_new if rb is None else m_new - rb
            p = jnp.exp(s_ref[h] - offset)
            l_ref[h] = alpha * l_ref[h] + jnp.sum(p, axis=0, keepdims=True)
            p_ref[h] = p.astype(BF16)
            alpha_ref[h] = alpha
            m_ref[h] = m_new

    def values_stage(j):
        for h in range(ATTN_HEADS):
            rows = slice(h * HEAD_DIM, (h + 1) * HEAD_DIM)
            pv = _dot(vt_ref[0, j, rows, :], p_ref[h])
            at_ref[rows] = alpha_ref[h] * at_ref[rows] + pv

    m_ref[...] = jnp.full(m_ref.shape, MASK_VALUE, F32)
    l_ref[...] = jnp.zeros(l_ref.shape, F32)
    at_ref[...] = jnp.zeros(at_ref.shape, F32)
    j_sub = jnp.maximum(qi - 1, 0)
    n_far = j_sub

    scores_stage(qi, "diag")
    softmax_stage(lambda h: None)
    scores_stage(j_sub, "sub")
    values_stage(qi)
    softmax_stage(lambda h: rb_sub_ref[h * nblk + j_sub])
    scores_stage(0, "far")

    def far_body(t, carry):
        values_stage(jnp.where(t == 0, j_sub, t - 1))
        softmax_stage(lambda h: rb_far_ref[h * nblk + t])
        scores_stage(jnp.minimum(t + 1, jnp.maximum(n_far - 1, 0)), "far")
        return carry

    lax.fori_loop(0, n_far, far_body, 0)
    values_stage(jnp.where(n_far > 0, n_far - 1, j_sub))
    for h in range(ATTN_HEADS):
        rows = slice(h * HEAD_DIM, (h + 1) * HEAD_DIM)
        at_ref[rows] = at_ref[rows] * (1.0 / l_ref[h])

    a_ref[0] = at_ref[...].T.astype(BF16)


def _moba_attn(qt, k, vt, diag, sub, t31):
    B, S, _ = k.shape
    nblk = S // MOBA_BLOCK
    tile = pl.BlockSpec((ATTN_HEADS, MOBA_BLOCK, MOBA_BLOCK), lambda b, i: (0, 0, 0), pipeline_mode=pl.Buffered(1))
    return pl.pallas_call(
        _moba_attn_kernel,
        grid=(B, nblk),
        in_specs=[pl.BlockSpec(memory_space=pltpu.SMEM),
                  pl.BlockSpec((1, ATTN_WIDTH, MOBA_BLOCK), lambda b, i: (b, 0, i)),
                  pl.BlockSpec((1, S, ATTN_WIDTH), lambda b, i: (b, 0, 0)),
                  pl.BlockSpec((1, nblk, ATTN_WIDTH, MOBA_BLOCK), lambda b, i: (b, 0, 0, 0)),
                  tile, tile],
        out_specs=pl.BlockSpec((1, MOBA_BLOCK, ATTN_WIDTH), lambda b, i: (b, i, 0)),
        out_shape=jax.ShapeDtypeStruct((B, S, ATTN_WIDTH), BF16),
        scratch_shapes=[pltpu.VMEM((ATTN_HEADS * nblk, ATTN_WIDTH), BF16),
                        pltpu.VMEM((ATTN_HEADS * nblk, ATTN_WIDTH), BF16),
                        pltpu.VMEM((ATTN_HEADS * nblk, 1, MOBA_BLOCK), F32),
                        pltpu.VMEM((ATTN_HEADS * nblk, 1, MOBA_BLOCK), F32),
                        pltpu.VMEM((ATTN_WIDTH, MOBA_BLOCK), F32),
                        pltpu.VMEM((ATTN_HEADS, 2 * HEAD_DIM, MOBA_BLOCK), BF16),
                        *[pltpu.VMEM((ATTN_HEADS, 1, MOBA_BLOCK), F32)] * 4,
                        pltpu.VMEM((ATTN_HEADS, MOBA_BLOCK, MOBA_BLOCK), F32),
                        pltpu.VMEM((ATTN_HEADS, MOBA_BLOCK, MOBA_BLOCK), BF16)],
        compiler_params=pltpu.CompilerParams(
            dimension_semantics=("arbitrary", "arbitrary"), vmem_limit_bytes=V7X_VMEM_LIMIT_BYTES),
        name="moba_attn",
    )(t31, qt, k, vt, diag, sub)


def _mix_merge_kernel(h1_ref, un_ref, unprev_ref, a_ref, win_ref, wgrp_ref, scale_ref,
                      wa_ref, wp_ref, wout_ref, h2_ref, zext_ref):
    i = pl.program_id(1)
    tm = MIX_TOKENS
    z0 = 3 * ATTN_WIDTH
    zg = _dot(un_ref[0], win_ref[:, z0:])
    z = zg[:, :POOL_WIDTH]
    zprev = _dot(unprev_ref[0], win_ref[:, z0:z0 + POOL_WIDTH])
    zext_ref[:POOL_HALO] = jnp.where(i > 0, zprev, 0.0)
    zext_ref[POOL_HALO:] = z

    tpos = i * tm + lax.broadcasted_iota(jnp.int32, (tm, POOL_GROUP_WIDTH), 0)
    mixed = []
    for g, w in enumerate(POOL_WINDOWS):
        cols = slice(g * POOL_GROUP_WIDTH, (g + 1) * POOL_GROUP_WIDTH)
        wsum = zext_ref[POOL_HALO:, cols]
        for lag in range(1, w):
            wsum = wsum + zext_ref[POOL_HALO - lag:POOL_HALO - lag + tm, cols]
        mean = wsum / jnp.minimum(tpos + 1, w).astype(F32)
        pooled = (mean - zext_ref[POOL_HALO:, cols]).astype(BF16)
        mixed.append(_dot(pooled, wgrp_ref[g]))
    p = (jnp.concatenate(mixed, axis=1) * scale_ref[...]).astype(BF16)

    g_attn = zg[:, POOL_WIDTH:POOL_WIDTH + D_MODEL]
    g_pool = zg[:, POOL_WIDTH + D_MODEL:]
    merged = (jax.nn.sigmoid(g_attn) * _dot(a_ref[0], wa_ref[...])
              + jax.nn.sigmoid(g_pool) * _dot(p, wp_ref[...]))
    h2_ref[0] = h1_ref[0] + _dot(merged.astype(BF16), wout_ref[...])


def _mix_merge(h1, un, a, win, wgrp, scale, wa, wp, wout):
    B, S, D = h1.shape
    tm = MIX_TOKENS
    halo_per_tile = tm // POOL_HALO
    return pl.pallas_call(
        _mix_merge_kernel,
        grid=(B, S // tm),
        in_specs=[pl.BlockSpec((1, tm, D), lambda b, i: (b, i, 0)),
                  pl.BlockSpec((1, tm, D), lambda b, i: (b, i, 0)),
                  pl.BlockSpec((1, POOL_HALO, D), lambda b, i: (b, jnp.maximum(i * halo_per_tile - 1, 0), 0)),
                  pl.BlockSpec((1, tm, ATTN_WIDTH), lambda b, i: (b, i, 0)),
                  _resident(win.shape), _resident(wgrp.shape), _resident(scale.shape),
                  _resident(wa.shape), _resident(wp.shape), _resident(wout.shape)],
        out_specs=pl.BlockSpec((1, tm, D), lambda b, i: (b, i, 0)),
        out_shape=jax.ShapeDtypeStruct((B, S, D), F32),
        scratch_shapes=[pltpu.VMEM((POOL_HALO + tm, POOL_WIDTH), F32)],
        compiler_params=pltpu.CompilerParams(
            dimension_semantics=("arbitrary", "arbitrary"), vmem_limit_bytes=V7X_VMEM_LIMIT_BYTES),
        name="mix_merge",
    )(h1, un, un, a, win, wgrp, scale, wa, wp, wout)


def _ffn2_final_kernel(h_ref, g2_ref, wg_ref, wu_ref, wd_ref, gf_ref, out_ref):
    h3 = _swiglu_half_step(h_ref[0], g2_ref, wg_ref, wu_ref, wd_ref)
    out_ref[0] = _rms(h3, gf_ref[...])


def _ffn2_final(h, g2, wg, wu, wd, gf):
    B, S, D = h.shape
    tm = FFN_TOKENS
    return pl.pallas_call(
        _ffn2_final_kernel,
        grid=(B, S // tm),
        in_specs=[pl.BlockSpec((1, tm, D), lambda b, i: (b, i, 0)),
                  _resident(g2.shape), _resident(wg.shape), _resident(wu.shape), _resident(wd.shape),
                  _resident(gf.shape)],
        out_specs=pl.BlockSpec((1, tm, D), lambda b, i: (b, i, 0)),
        out_shape=jax.ShapeDtypeStruct((B, S, D), F32),
        compiler_params=pltpu.CompilerParams(
            dimension_semantics=("arbitrary", "arbitrary"), vmem_limit_bytes=V7X_VMEM_LIMIT_BYTES),
        name="ffn2_final",
    )(h, g2, wg, wu, wd, gf)


def kernel(x, ffn1_norm, ffn1_w_gate, ffn1_w_up, ffn1_w_down, mix_norm, w_in, pool_w_group, pool_scale,
           w_branch_attn, w_branch_pool, w_out, ffn2_norm, ffn2_w_gate, ffn2_w_up, ffn2_w_down,
           rpb_table, final_norm):
    B, S, D = x.shape
    assert (D, ffn1_w_gate.shape[0]) == (D_MODEL, 1)
    assert S % MOBA_BLOCK == 0 and S % FFN_TOKENS == 0 and S % MIX_TOKENS == 0
    bf = lambda w: w.astype(BF16)
    row = lambda v: v.reshape(1, -1)
    w_in = w_in[0]
    wq, wk, wv = (w_in[:, c * ATTN_WIDTH:(c + 1) * ATTN_WIDTH] for c in range(3))
    wqv_t = bf(jnp.concatenate([wq, wv], axis=1).T)

    diag, sub = _rpb_bias(rpb_table)
    (h1, un, qt, k, vt), later = _ffn1_qkv(
        x, row(ffn1_norm[0]), bf(ffn1_w_gate[0]), bf(ffn1_w_up[0]), bf(ffn1_w_down[0]), row(mix_norm[0]), bf(wk),
        wqv_t, [w_in, w_branch_attn[0], w_branch_pool[0], w_out[0], ffn2_w_gate[0], ffn2_w_up[0], ffn2_w_down[0]])
    win_bf, wa_bf, wp_bf, wout_bf, wg2_bf, wu2_bf, wd2_bf = later
    a = _moba_attn(qt, k, vt, diag, sub, rpb_table[RPB_BUCKETS - 1])
    h2 = _mix_merge(h1, un, a, win_bf, bf(pool_w_group[0]), row(pool_scale[0]), wa_bf, wp_bf, wout_bf)
    return _ffn2_final(h2, row(ffn2_norm[0]), wg2_bf, wu2_bf, wd2_bf, row(final_norm))
```

```python
import math

import jax
import jax.numpy as jnp
from jax import lax
from jax.experimental import pallas as pl
from jax.experimental.pallas import tpu as pltpu

D_MODEL = 1024
HEAD_DIM = 64
ATTN_WIDTH = 512
ATTN_HEADS = 8
MOBA_BLOCK = 256
MOBA_TOPK = 3
POOL_WINDOWS = (2, 4, 8, 16)
POOL_WIDTH = 512
POOL_GROUP_WIDTH = 128
POOL_HALO = 16
D_FF = 2816
RPB_BUCKETS = 32
RPB_MAX_DISTANCE = 128
RMS_EPS = 1e-6
MASK_VALUE = -1e30

V7X_VMEM_LIMIT_BYTES = 56 * 1024 * 1024

FFN_TOKENS = 512
MIX_TOKENS = 512
N_LATER_WEIGHTS = 7
BF16_SUBLANES = 16

BF16 = jnp.bfloat16
F32 = jnp.float32


def _resident(shape):
    nd = len(shape)
    return pl.BlockSpec(shape, lambda *_: (0,) * nd, pipeline_mode=pl.Buffered(1))


def _dot(a, b):
    return jnp.dot(a, b, preferred_element_type=F32)


def _dot_nt(a, b):
    return lax.dot_general(a, b, (((1,), (1,)), ((), ())), preferred_element_type=F32)


def _rms(x, g):
    return x * lax.rsqrt(jnp.mean(x * x, axis=-1, keepdims=True) + RMS_EPS) * g


def _swiglu_half_step(x, g_ref, wg_ref, wu_ref, wd_ref):
    xn = _rms(x, g_ref[...]).astype(BF16)
    gate = _dot(xn, wg_ref[...])
    up = _dot(xn, wu_ref[...])
    act = (gate * jax.nn.sigmoid(gate) * up).astype(BF16)
    return x + 0.5 * _dot(act, wd_ref[...])


def _rpb_bucket(dist):
    n = jnp.maximum(dist, 0)
    max_exact = RPB_BUCKETS // 2
    nf = jnp.maximum(n, 1).astype(F32)
    large = max_exact + (jnp.log(nf / max_exact) / math.log(RPB_MAX_DISTANCE / max_exact)
                         * (RPB_BUCKETS - max_exact)).astype(jnp.int32)
    large = jnp.minimum(large, RPB_BUCKETS - 1)
    return jnp.where(n < max_exact, n, large)


def _rpb_bias_tiles(table_ref, h, diag_ref, sub_ref):
    kl = lax.broadcasted_iota(jnp.int32, (MOBA_BLOCK, MOBA_BLOCK), 0)
    ql = lax.broadcasted_iota(jnp.int32, (MOBA_BLOCK, MOBA_BLOCK), 1)
    for out_ref, offset in ((diag_ref, 0), (sub_ref, MOBA_BLOCK)):
        dist = ql - kl + offset
        bucket = _rpb_bucket(dist)
        bias = jnp.zeros((MOBA_BLOCK, MOBA_BLOCK), F32)
        for b in range(RPB_BUCKETS):
            bias = jnp.where(bucket == b, table_ref[b, h], bias)
        if offset == 0:
            bias = jnp.where(dist >= 0, bias, MASK_VALUE)
        out_ref[0] = bias


def _ffn1_qkv_kernel(table_ref, x_ref, g1_ref, wg_ref, wu_ref, wd_ref, gm_ref, wk_ref, wqv_t_ref, *refs):
    later_f32, (h1_ref, un_ref, qt_ref, k_ref, vt_ref, diag_ref, sub_ref), later_bf16 = (
        refs[:N_LATER_WEIGHTS], refs[N_LATER_WEIGHTS:N_LATER_WEIGHTS + 7], refs[N_LATER_WEIGHTS + 7:])
    for src_ref, dst_ref in zip(later_f32, later_bf16):
        dst_ref[...] = src_ref[...].astype(BF16)
    step = pl.program_id(0) * pl.num_programs(1) + pl.program_id(1)

    @pl.when(step < ATTN_HEADS)
    def _():
        _rpb_bias_tiles(table_ref, step, diag_ref, sub_ref)

    h1 = _swiglu_half_step(x_ref[0], g1_ref, wg_ref, wu_ref, wd_ref)
    h1_ref[0] = h1
    un = _rms(h1, gm_ref[...]).astype(BF16)
    un_ref[0] = un
    k_ref[0] = _dot(un, wk_ref[...]).astype(BF16)
    qv_t = _dot_nt(wqv_t_ref[...], un)
    qt_ref[0] = (qv_t[:ATTN_WIDTH] * (HEAD_DIM ** -0.5)).astype(BF16)
    for c in range(FFN_TOKENS // MOBA_BLOCK):
        vt_ref[0, c] = qv_t[ATTN_WIDTH:, c * MOBA_BLOCK:(c + 1) * MOBA_BLOCK].astype(BF16)


def _ffn1_qkv(rpb_table, x, g1, wg, wu, wd, gm, wk, wqv_t, later_weights):
    B, S, D = x.shape
    tm = FFN_TOKENS
    nblk = S // MOBA_BLOCK
    tiles_per_seq = S // tm
    steps = B * tiles_per_seq
    assert len(later_weights) == N_LATER_WEIGHTS and steps >= ATTN_HEADS
    step_of = lambda b, i: b * tiles_per_seq + i

    def slab_view(w):
        n = max(n for n in range(1, steps + 1) if steps % n == 0 and w.shape[0] % (n * BF16_SUBLANES) == 0)
        view = w.reshape(n, w.shape[0] // n, w.shape[1])
        return view, pl.BlockSpec((1,) + view.shape[1:], lambda b, i: (step_of(b, i) // (steps // n), 0, 0))

    slabs, slab_specs = zip(*[slab_view(w) for w in later_weights])
    tok = lambda width, dt: (pl.BlockSpec((1, tm, width), lambda b, i: (b, i, 0)),
                             jax.ShapeDtypeStruct((B, S, width), dt))
    h1_spec, h1_shape = tok(D, F32)
    un_spec, un_shape = tok(D, BF16)
    k_spec, k_shape = tok(ATTN_WIDTH, BF16)
    bias_spec = pl.BlockSpec((1, MOBA_BLOCK, MOBA_BLOCK), lambda b, i: (jnp.minimum(step_of(b, i), ATTN_HEADS - 1), 0, 0))
    bias_shape = jax.ShapeDtypeStruct((ATTN_HEADS, MOBA_BLOCK, MOBA_BLOCK), F32)
    outs = pl.pallas_call(
        _ffn1_qkv_kernel,
        grid=(B, tiles_per_seq),
        in_specs=[pl.BlockSpec(memory_space=pltpu.SMEM),
                  pl.BlockSpec((1, tm, D), lambda b, i: (b, i, 0)),
                  _resident(g1.shape), _resident(wg.shape), _resident(wu.shape), _resident(wd.shape),
                  _resident(gm.shape), _resident(wk.shape), _resident(wqv_t.shape), *slab_specs],
        out_specs=[h1_spec, un_spec,
                   pl.BlockSpec((1, ATTN_WIDTH, tm), lambda b, i: (b, 0, i)),
                   k_spec,
                   pl.BlockSpec((1, tm // MOBA_BLOCK, ATTN_WIDTH, MOBA_BLOCK), lambda b, i: (b, i, 0, 0)),
                   bias_spec, bias_spec, *slab_specs],
        out_shape=[h1_shape, un_shape,
                   jax.ShapeDtypeStruct((B, ATTN_WIDTH, S), BF16),
                   k_shape,
                   jax.ShapeDtypeStruct((B, nblk, ATTN_WIDTH, MOBA_BLOCK), BF16),
                   bias_shape, bias_shape]
                  + [jax.ShapeDtypeStruct(w.shape, BF16) for w in slabs],
        compiler_params=pltpu.CompilerParams(
            dimension_semantics=("arbitrary", "arbitrary"), vmem_limit_bytes=V7X_VMEM_LIMIT_BYTES),
        name="ffn1_qkv",
    )(rpb_table, x, g1, wg, wu, wd, gm, wk, wqv_t, *slabs)
    return outs[:7], [o.reshape(w.shape) for o, w in zip(outs[7:], later_weights)]


def _moba_attn_kernel(t31_ref, qt_ref, k_ref, vt_ref, diag_ref, sub_ref, a_ref,
                      kmhi_ref, kmlo_ref, rb_far_ref, rb_sub_ref, at_ref, qz_ref,
                      m_ref, l_ref, mblk_ref, alpha_ref, s_ref, p_ref):
    qi = pl.program_id(1)
    nblk = k_ref.shape[1] // MOBA_BLOCK
    nq = MOBA_BLOCK

    @pl.when(qi == 0)
    def _():
        blk_row = lax.broadcasted_iota(jnp.int32, (nblk, ATTN_WIDTH), 0)
        km = jnp.zeros((nblk, ATTN_WIDTH), F32)
        for j in range(nblk):
            mean_j = jnp.mean(k_ref[0, j * MOBA_BLOCK:(j + 1) * MOBA_BLOCK, :].astype(F32), axis=0, keepdims=True)
            km = jnp.where(blk_row == j, mean_j, km)
        km = jnp.concatenate([km] * ATTN_HEADS, axis=0)
        row_head = lax.broadcasted_iota(jnp.int32, km.shape, 0) // nblk
        col_head = lax.broadcasted_iota(jnp.int32, km.shape, 1) // HEAD_DIM
        km = jnp.where(row_head == col_head, km, 0.0)
        hi = km.astype(BF16)
        kmhi_ref[...] = hi
        kmlo_ref[...] = (km - hi.astype(F32)).astype(BF16)

    qt = qt_ref[0]
    gates = _dot(kmhi_ref[...], qt) + _dot(kmlo_ref[...], qt)
    jrow = lax.broadcasted_iota(jnp.int32, (nblk, nq), 0)
    for h in range(ATTN_HEADS):
        g = gates[h * nblk:(h + 1) * nblk]
        cnt = jnp.zeros((nblk, nq), jnp.int32)
        for jp in range(nblk):
            gb = jnp.broadcast_to(g[jp:jp + 1], (nblk, nq))
            beats = (gb > g) | ((gb == g) & (jp < jrow))
            cnt = cnt + jnp.where(beats, jnp.where(jp < qi, 1, 0), 0)
        sel = (jrow < qi) & (cnt < MOBA_TOPK)
        far = jnp.where(sel, t31_ref[h], MASK_VALUE)
        sub = jnp.where(sel, 0.0, MASK_VALUE)
        for j in range(nblk):
            rb_far_ref[h * nblk + j] = far[j:j + 1]
            rb_sub_ref[h * nblk + j] = sub[j:j + 1]

    zeros_half = jnp.zeros((HEAD_DIM, nq), BF16)
    for h in range(ATTN_HEADS):
        qh = qt_ref[0, h * HEAD_DIM:(h + 1) * HEAD_DIM, :]
        qz_ref[h] = jnp.concatenate([qh, zeros_half] if h % 2 == 0 else [zeros_half, qh], axis=0)

    def scores_stage(j, kind):
        start = pl.multiple_of(j * MOBA_BLOCK, MOBA_BLOCK)
        for h in range(ATTN_HEADS):
            pair = h // 2
            kblk = k_ref[0, pl.ds(start, MOBA_BLOCK), pair * 128:(pair + 1) * 128]
            s = _dot(kblk, qz_ref[h])
            if kind == "diag":
                s = s + diag_ref[h]
            elif kind == "sub":
                s = s + sub_ref[h]
            s_ref[h] = s
            mblk_ref[h] = jnp.max(s, axis=0, keepdims=True)

    def softmax_stage(row_bias):
        for h in range(ATTN_HEADS):
            rb = row_bias(h)
            m_old = m_ref[h]
            m_blk = mblk_ref[h] if rb is None else mblk_ref[h] + rb
            m_new = jnp.maximum(m_old, m_blk)
            alpha = jnp.exp(m_old - m_new)
            offset = m_new if rb is None else m_new - rb
            p = jnp.exp(s_ref[h] - offset)
            l_ref[h] = alpha * l_ref[h] + jnp.sum(p, axis=0, keepdims=True)
            p_ref[h] = p.astype(BF16)
            alpha_ref[h] = alpha
            m_ref[h] = m_new

    def values_stage(j):
        for h in range(ATTN_HEADS):
            rows = slice(h * HEAD_DIM, (h + 1) * HEAD_DIM)
            pv = _dot(vt_ref[0, j, rows, :], p_ref[h])
            at_ref[rows] = alpha_ref[h] * at_ref[rows] + pv

    m_ref[...] = jnp.full(m_ref.shape, MASK_VALUE, F32)
    l_ref[...] = jnp.zeros(l_ref.shape, F32)
    at_ref[...] = jnp.zeros(at_ref.shape, F32)
    j_sub = jnp.maximum(qi - 1, 0)
    n_far = j_sub

    scores_stage(qi, "diag")
    softmax_stage(lambda h: None)
    scores_stage(j_sub, "sub")
    values_stage(qi)
    softmax_stage(lambda h: rb_sub_ref[h * nblk + j_sub])
    scores_stage(0, "far")

    def far_body(t, carry):
        values_stage(jnp.where(t == 0, j_sub, t - 1))
        softmax_stage(lambda h: rb_far_ref[h * nblk + t])
        scores_stage(jnp.minimum(t + 1, jnp.maximum(n_far - 1, 0)), "far")
        return carry

    lax.fori_loop(0, n_far, far_body, 0)
    values_stage(jnp.where(n_far > 0, n_far - 1, j_sub))
    for h in range(ATTN_HEADS):
        rows = slice(h * HEAD_DIM, (h + 1) * HEAD_DIM)
        at_ref[rows] = at_ref[rows] * (1.0 / l_ref[h])

    a_ref[0] = at_ref[...].T.astype(BF16)


def _moba_attn(qt, k, vt, diag, sub, t31):
    B, S, _ = k.shape
    nblk = S // MOBA_BLOCK
    tile = pl.BlockSpec((ATTN_HEADS, MOBA_BLOCK, MOBA_BLOCK), lambda b, i: (0, 0, 0), pipeline_mode=pl.Buffered(1))
    return pl.pallas_call(
        _moba_attn_kernel,
        grid=(B, nblk),
        in_specs=[pl.BlockSpec(memory_space=pltpu.SMEM),
                  pl.BlockSpec((1, ATTN_WIDTH, MOBA_BLOCK), lambda b, i: (b, 0, i)),
                  pl.BlockSpec((1, S, ATTN_WIDTH), lambda b, i: (b, 0, 0)),
                  pl.BlockSpec((1, nblk, ATTN_WIDTH, MOBA_BLOCK), lambda b, i: (b, 0, 0, 0)),
                  tile, tile],
        out_specs=pl.BlockSpec((1, MOBA_BLOCK, ATTN_WIDTH), lambda b, i: (b, i, 0)),
        out_shape=jax.ShapeDtypeStruct((B, S, ATTN_WIDTH), BF16),
        scratch_shapes=[pltpu.VMEM((ATTN_HEADS * nblk, ATTN_WIDTH), BF16),
                        pltpu.VMEM((ATTN_HEADS * nblk, ATTN_WIDTH), BF16),
                        pltpu.VMEM((ATTN_HEADS * nblk, 1, MOBA_BLOCK), F32),
                        pltpu.VMEM((ATTN_HEADS * nblk, 1, MOBA_BLOCK), F32),
                        pltpu.VMEM((ATTN_WIDTH, MOBA_BLOCK), F32),
                        pltpu.VMEM((ATTN_HEADS, 2 * HEAD_DIM, MOBA_BLOCK), BF16),
                        *[pltpu.VMEM((ATTN_HEADS, 1, MOBA_BLOCK), F32)] * 4,
                        pltpu.VMEM((ATTN_HEADS, MOBA_BLOCK, MOBA_BLOCK), F32),
                        pltpu.VMEM((ATTN_HEADS, MOBA_BLOCK, MOBA_BLOCK), BF16)],
        compiler_params=pltpu.CompilerParams(
            dimension_semantics=("arbitrary", "arbitrary"), vmem_limit_bytes=V7X_VMEM_LIMIT_BYTES),
        name="moba_attn",
    )(t31, qt, k, vt, diag, sub)


def _mix_merge_kernel(h1_ref, un_ref, unprev_ref, a_ref, win_ref, wgrp_ref, scale_ref,
                      wa_ref, wp_ref, wout_ref, h2_ref, zext_ref):
    i = pl.program_id(1)
    tm = MIX_TOKENS
    z0 = 3 * ATTN_WIDTH
    zg = _dot(un_ref[0], win_ref[:, z0:])
    z = zg[:, :POOL_WIDTH]
    zprev = _dot(unprev_ref[0], win_ref[:, z0:z0 + POOL_WIDTH])
    zext_ref[:POOL_HALO] = jnp.where(i > 0, zprev, 0.0)
    zext_ref[POOL_HALO:] = z

    tpos = i * tm + lax.broadcasted_iota(jnp.int32, (tm, POOL_GROUP_WIDTH), 0)
    mixed = []
    for g, w in enumerate(POOL_WINDOWS):
        cols = slice(g * POOL_GROUP_WIDTH, (g + 1) * POOL_GROUP_WIDTH)
        wsum = zext_ref[POOL_HALO:, cols]
        for lag in range(1, w):
            wsum = wsum + zext_ref[POOL_HALO - lag:POOL_HALO - lag + tm, cols]
        mean = wsum / jnp.minimum(tpos + 1, w).astype(F32)
        pooled = (mean - zext_ref[POOL_HALO:, cols]).astype(BF16)
        mixed.append(_dot(pooled, wgrp_ref[g]))
    p = (jnp.concatenate(mixed, axis=1) * scale_ref[...]).astype(BF16)

    g_attn = zg[:, POOL_WIDTH:POOL_WIDTH + D_MODEL]
    g_pool = zg[:, POOL_WIDTH + D_MODEL:]
    merged = (jax.nn.sigmoid(g_attn) * _dot(a_ref[0], wa_ref[...])
              + jax.nn.sigmoid(g_pool) * _dot(p, wp_ref[...]))
    h2_ref[0] = h1_ref[0] + _dot(merged.astype(BF16), wout_ref[...])


def _mix_merge(h1, un, a, win, wgrp, scale, wa, wp, wout):
    B, S, D = h1.shape
    tm = MIX_TOKENS
    halo_per_tile = tm // POOL_HALO
    return pl.pallas_call(
        _mix_merge_kernel,
        grid=(B, S // tm),
        in_specs=[pl.BlockSpec((1, tm, D), lambda b, i: (b, i, 0)),
                  pl.BlockSpec((1, tm, D), lambda b, i: (b, i, 0)),
                  pl.BlockSpec((1, POOL_HALO, D), lambda b, i: (b, jnp.maximum(i * halo_per_tile - 1, 0), 0)),
                  pl.BlockSpec((1, tm, ATTN_WIDTH), lambda b, i: (b, i, 0)),
                  _resident(win.shape), _resident(wgrp.shape), _resident(scale.shape),
                  _resident(wa.shape), _resident(wp.shape), _resident(wout.shape)],
        out_specs=pl.BlockSpec((1, tm, D), lambda b, i: (b, i, 0)),
        out_shape=jax.ShapeDtypeStruct((B, S, D), F32),
        scratch_shapes=[pltpu.VMEM((POOL_HALO + tm, POOL_WIDTH), F32)],
        compiler_params=pltpu.CompilerParams(
            dimension_semantics=("arbitrary", "arbitrary"), vmem_limit_bytes=V7X_VMEM_LIMIT_BYTES),
        name="mix_merge",
    )(h1, un, un, a, win, wgrp, scale, wa, wp, wout)


def _ffn2_final_kernel(h_ref, g2_ref, wg_ref, wu_ref, wd_ref, gf_ref, out_ref):
    h3 = _swiglu_half_step(h_ref[0], g2_ref, wg_ref, wu_ref, wd_ref)
    out_ref[0] = _rms(h3, gf_ref[...])


def _ffn2_final(h, g2, wg, wu, wd, gf):
    B, S, D = h.shape
    tm = FFN_TOKENS
    return pl.pallas_call(
        _ffn2_final_kernel,
        grid=(B, S // tm),
        in_specs=[pl.BlockSpec((1, tm, D), lambda b, i: (b, i, 0)),
                  _resident(g2.shape), _resident(wg.shape), _resident(wu.shape), _resident(wd.shape),
                  _resident(gf.shape)],
        out_specs=pl.BlockSpec((1, tm, D), lambda b, i: (b, i, 0)),
        out_shape=jax.ShapeDtypeStruct((B, S, D), F32),
        compiler_params=pltpu.CompilerParams(
            dimension_semantics=("arbitrary", "arbitrary"), vmem_limit_bytes=V7X_VMEM_LIMIT_BYTES),
        name="ffn2_final",
    )(h, g2, wg, wu, wd, gf)


def kernel(x, ffn1_norm, ffn1_w_gate, ffn1_w_up, ffn1_w_down, mix_norm, w_in, pool_w_group, pool_scale,
           w_branch_attn, w_branch_pool, w_out, ffn2_norm, ffn2_w_gate, ffn2_w_up, ffn2_w_down,
           rpb_table, final_norm):
    B, S, D = x.shape
    assert (D, ffn1_w_gate.shape[0]) == (D_MODEL, 1)
    assert S % MOBA_BLOCK == 0 and S % FFN_TOKENS == 0 and S % MIX_TOKENS == 0
    bf = lambda w: w.astype(BF16)
    row = lambda v: v.reshape(1, -1)
    w_in = w_in[0]
    wq, wk, wv = (w_in[:, c * ATTN_WIDTH:(c + 1) * ATTN_WIDTH] for c in range(3))
    wqv_t = bf(jnp.concatenate([wq, wv], axis=1).T)

    (h1, un, qt, k, vt, diag, sub), later = _ffn1_qkv(
        rpb_table, x, row(ffn1_norm[0]), bf(ffn1_w_gate[0]), bf(ffn1_w_up[0]), bf(ffn1_w_down[0]), row(mix_norm[0]), bf(wk),
        wqv_t, [w_in, w_branch_attn[0], w_branch_pool[0], w_out[0], ffn2_w_gate[0], ffn2_w_up[0], ffn2_w_down[0]])
    win_bf, wa_bf, wp_bf, wout_bf, wg2_bf, wu2_bf, wd2_bf = later
    a = _moba_attn(qt, k, vt, diag, sub, rpb_table[RPB_BUCKETS - 1])
    h2 = _mix_merge(h1, un, a, win_bf, bf(pool_w_group[0]), row(pool_scale[0]), wa_bf, wp_bf, wout_bf)
    return _ffn2_final(h2, row(ffn2_norm[0]), wg2_bf, wu2_bf, wd2_bf, row(final_norm))
```

```python
import math

import jax
import jax.numpy as jnp
from jax import lax
from jax.experimental import pallas as pl
from jax.experimental.pallas import tpu as pltpu

D_MODEL = 1024
HEAD_DIM = 64
ATTN_WIDTH = 512
ATTN_HEADS = 8
MOBA_BLOCK = 256
MOBA_TOPK = 3
POOL_WINDOWS = (2, 4, 8, 16)
POOL_WIDTH = 512
POOL_GROUP_WIDTH = 128
POOL_HALO = 16
D_FF = 2816
RPB_BUCKETS = 32
RPB_MAX_DISTANCE = 128
RMS_EPS = 1e-6
MASK_VALUE = -1e30

V7X_VMEM_LIMIT_BYTES = 56 * 1024 * 1024

FFN_TOKENS = 512
MIX_TOKENS = 512
N_LATER_WEIGHTS = 7
BF16_SUBLANES = 16

BF16 = jnp.bfloat16
F32 = jnp.float32


def _resident(shape):
    nd = len(shape)
    return pl.BlockSpec(shape, lambda *_: (0,) * nd, pipeline_mode=pl.Buffered(1))


def _dot(a, b):
    return jnp.dot(a, b, preferred_element_type=F32)


def _rms(x, g):
    return x * lax.rsqrt(jnp.mean(x * x, axis=-1, keepdims=True) + RMS_EPS) * g


def _swiglu_half_step(x, g_ref, wg_ref, wu_ref, wd_ref):
    xn = _rms(x, g_ref[...]).astype(BF16)
    gate = _dot(xn, wg_ref[...])
    up = _dot(xn, wu_ref[...])
    act = (gate * jax.nn.sigmoid(gate) * up).astype(BF16)
    return x + 0.5 * _dot(act, wd_ref[...])


def _rpb_bucket(dist):
    n = jnp.maximum(dist, 0)
    max_exact = RPB_BUCKETS // 2
    nf = jnp.maximum(n, 1).astype(F32)
    large = max_exact + (jnp.log(nf / max_exact) / math.log(RPB_MAX_DISTANCE / max_exact)
                         * (RPB_BUCKETS - max_exact)).astype(jnp.int32)
    large = jnp.minimum(large, RPB_BUCKETS - 1)
    return jnp.where(n < max_exact, n, large)


def _rpb_bias_kernel(table_ref, diag_ref, sub_ref):
    h = pl.program_id(0)
    width = 4 * MOBA_BLOCK
    dist = lax.broadcasted_iota(jnp.int32, (8, width), 1) - MOBA_BLOCK
    bucket = _rpb_bucket(dist)
    f = jnp.zeros((8, width), F32)
    for b in range(RPB_BUCKETS):
        f = jnp.where(bucket == b, table_ref[b, h], f)
    rows = jnp.broadcast_to(f[0:1], (MOBA_BLOCK, width))
    toeplitz = pltpu.roll(rows, 0, 1, stride=1, stride_axis=0)
    kl = lax.broadcasted_iota(jnp.int32, (MOBA_BLOCK, MOBA_BLOCK), 0)
    ql = lax.broadcasted_iota(jnp.int32, (MOBA_BLOCK, MOBA_BLOCK), 1)
    diag_ref[0] = jnp.where(ql >= kl, toeplitz[:, MOBA_BLOCK:2 * MOBA_BLOCK], MASK_VALUE)
    sub_ref[0] = toeplitz[:, 2 * MOBA_BLOCK:3 * MOBA_BLOCK]


def _rpb_bias(rpb_table):
    tile = jax.ShapeDtypeStruct((ATTN_HEADS, MOBA_BLOCK, MOBA_BLOCK), F32)
    spec = pl.BlockSpec((1, MOBA_BLOCK, MOBA_BLOCK), lambda h: (h, 0, 0))
    return pl.pallas_call(
        _rpb_bias_kernel,
        grid=(ATTN_HEADS,),
        in_specs=[pl.BlockSpec(memory_space=pltpu.SMEM)],
        out_specs=[spec, spec],
        out_shape=[tile, tile],
        name="rpb_bias",
    )(rpb_table)


def _ffn1_qkv_kernel(x_ref, g1_ref, wg_ref, wu_ref, wd_ref, gm_ref, wqkv_ref, *refs):
    later_f32, (h1_ref, un_ref, qt_ref, k_ref, vt_ref), later_bf16 = (
        refs[:N_LATER_WEIGHTS], refs[N_LATER_WEIGHTS:N_LATER_WEIGHTS + 5], refs[N_LATER_WEIGHTS + 5:])
    for src_ref, dst_ref in zip(later_f32, later_bf16):
        dst_ref[...] = src_ref[...].astype(BF16)
    h1 = _swiglu_half_step(x_ref[0], g1_ref, wg_ref, wu_ref, wd_ref)
    h1_ref[0] = h1
    un = _rms(h1, gm_ref[...]).astype(BF16)
    un_ref[0] = un
    qkv = _dot(un, wqkv_ref[...])
    qt_ref[0] = (qkv[:, :ATTN_WIDTH] * (HEAD_DIM ** -0.5)).T.astype(BF16)
    k_ref[0] = qkv[:, ATTN_WIDTH:2 * ATTN_WIDTH].astype(BF16)
    for c in range(FFN_TOKENS // MOBA_BLOCK):
        vt_ref[0, c] = qkv[c * MOBA_BLOCK:(c + 1) * MOBA_BLOCK, 2 * ATTN_WIDTH:].T.astype(BF16)


def _ffn1_qkv(x, g1, wg, wu, wd, gm, wqkv, later_weights):
    B, S, D = x.shape
    tm = FFN_TOKENS
    nblk = S // MOBA_BLOCK
    tiles_per_seq = S // tm
    steps = B * tiles_per_seq
    assert len(later_weights) == N_LATER_WEIGHTS
    step_of = lambda b, i: b * tiles_per_seq + i

    def slab_view(w):
        n = max(n for n in range(1, steps + 1) if steps % n == 0 and w.shape[0] % (n * BF16_SUBLANES) == 0)
        view = w.reshape(n, w.shape[0] // n, w.shape[1])
        return view, pl.BlockSpec((1,) + view.shape[1:], lambda b, i: (step_of(b, i) // (steps // n), 0, 0))

    slabs, slab_specs = zip(*[slab_view(w) for w in later_weights])
    tok = lambda width, dt: (pl.BlockSpec((1, tm, width), lambda b, i: (b, i, 0)),
                             jax.ShapeDtypeStruct((B, S, width), dt))
    h1_spec, h1_shape = tok(D, F32)
    un_spec, un_shape = tok(D, BF16)
    k_spec, k_shape = tok(ATTN_WIDTH, BF16)
    outs = pl.pallas_call(
        _ffn1_qkv_kernel,
        grid=(B, tiles_per_seq),
        in_specs=[pl.BlockSpec((1, tm, D), lambda b, i: (b, i, 0)),
                  _resident(g1.shape), _resident(wg.shape), _resident(wu.shape), _resident(wd.shape),
                  _resident(gm.shape), _resident(wqkv.shape), *slab_specs],
        out_specs=[h1_spec, un_spec,
                   pl.BlockSpec((1, ATTN_WIDTH, tm), lambda b, i: (b, 0, i)),
                   k_spec,
                   pl.BlockSpec((1, tm // MOBA_BLOCK, ATTN_WIDTH, MOBA_BLOCK), lambda b, i: (b, i, 0, 0)),
                   *slab_specs],
        out_shape=[h1_shape, un_shape,
                   jax.ShapeDtypeStruct((B, ATTN_WIDTH, S), BF16),
                   k_shape,
                   jax.ShapeDtypeStruct((B, nblk, ATTN_WIDTH, MOBA_BLOCK), BF16)]
                  + [jax.ShapeDtypeStruct(w.shape, BF16) for w in slabs],
        compiler_params=pltpu.CompilerParams(
            dimension_semantics=("arbitrary", "arbitrary"), vmem_limit_bytes=V7X_VMEM_LIMIT_BYTES),
        name="ffn1_qkv",
    )(x, g1, wg, wu, wd, gm, wqkv, *slabs)
    return outs[:5], [o.reshape(w.shape) for o, w in zip(outs[5:], later_weights)]


def _moba_attn_kernel(t31_ref, qt_ref, k_ref, vt_ref, diag_ref, sub_ref, a_ref,
                      kmhi_ref, kmlo_ref, rb_far_ref, rb_sub_ref, at_ref, qz_ref,
                      m_ref, l_ref, mblk_ref, alpha_ref, s_ref, p_ref):
    qi = pl.program_id(1)
    nblk = k_ref.shape[1] // MOBA_BLOCK
    nq = MOBA_BLOCK

    @pl.when(qi == 0)
    def _():
        blk_row = lax.broadcasted_iota(jnp.int32, (nblk, ATTN_WIDTH), 0)
        km = jnp.zeros((nblk, ATTN_WIDTH), F32)
        for j in range(nblk):
            mean_j = jnp.mean(k_ref[0, j * MOBA_BLOCK:(j + 1) * MOBA_BLOCK, :].astype(F32), axis=0, keepdims=True)
            km = jnp.where(blk_row == j, mean_j, km)
        km = jnp.concatenate([km] * ATTN_HEADS, axis=0)
        row_head = lax.broadcasted_iota(jnp.int32, km.shape, 0) // nblk
        col_head = lax.broadcasted_iota(jnp.int32, km.shape, 1) // HEAD_DIM
        km = jnp.where(row_head == col_head, km, 0.0)
        hi = km.astype(BF16)
        kmhi_ref[...] = hi
        kmlo_ref[...] = (km - hi.astype(F32)).astype(BF16)

    qt = qt_ref[0]
    gates = _dot(kmhi_ref[...], qt) + _dot(kmlo_ref[...], qt)
    jrow = lax.broadcasted_iota(jnp.int32, (nblk, nq), 0)
    for h in range(ATTN_HEADS):
        g = gates[h * nblk:(h + 1) * nblk]
        cnt = jnp.zeros((nblk, nq), jnp.int32)
        for jp in range(nblk):
            gb = jnp.broadcast_to(g[jp:jp + 1], (nblk, nq))
            beats = (gb > g) | ((gb == g) & (jp < jrow))
            cnt = cnt + jnp.where(beats, jnp.where(jp < qi, 1, 0), 0)
        sel = (jrow < qi) & (cnt < MOBA_TOPK)
        far = jnp.where(sel, t31_ref[h], MASK_VALUE)
        sub = jnp.where(sel, 0.0, MASK_VALUE)
        for j in range(nblk):
            rb_far_ref[h * nblk + j] = far[j:j + 1]
            rb_sub_ref[h * nblk + j] = sub[j:j + 1]

    zeros_half = jnp.zeros((HEAD_DIM, nq), BF16)
    for h in range(ATTN_HEADS):
        qh = qt_ref[0, h * HEAD_DIM:(h + 1) * HEAD_DIM, :]
        qz_ref[h] = jnp.concatenate([qh, zeros_half] if h % 2 == 0 else [zeros_half, qh], axis=0)

    def scores_stage(j, kind):
        start = pl.multiple_of(j * MOBA_BLOCK, MOBA_BLOCK)
        for h in range(ATTN_HEADS):
            pair = h // 2
            kblk = k_ref[0, pl.ds(start, MOBA_BLOCK), pair * 128:(pair + 1) * 128]
            s = _dot(kblk, qz_ref[h])
            if kind == "diag":
                s = s + diag_ref[h]
            elif kind == "sub":
                s = s + sub_ref[h]
            s_ref[h] = s
            mblk_ref[h] = jnp.max(s, axis=0, keepdims=True)

    def softmax_stage(row_bias):
        for h in range(ATTN_HEADS):
            rb = row_bias(h)
            m_old = m_ref[h]
            m_blk = mblk_ref[h] if rb is None else mblk_ref[h] + rb
            m_new = jnp.maximum(m_old, m_blk)
            alpha = jnp.exp(m_old - m_new)
            offset = m_new if rb is None else m_new - rb
            p = jnp.exp(s_ref[h] - offset)
            l_ref[h] = alpha * l_ref[h] + jnp.sum(p, axis=0, keepdims=True)
            p_ref[h] = p.astype(BF16)
            alpha_ref[h] = alpha
            m_ref[h] = m_new

    def values_stage(j):
        for h in range(ATTN_HEADS):
            rows = slice(h * HEAD_DIM, (h + 1) * HEAD_DIM)
            pv = _dot(vt_ref[0, j, rows, :], p_ref[h])
            at_ref[rows] = alpha_ref[h] * at_ref[rows] + pv

    m_ref[...] = jnp.full(m_ref.shape, MASK_VALUE, F32)
    l_ref[...] = jnp.zeros(l_ref.shape, F32)
    at_ref[...] = jnp.zeros(at_ref.shape, F32)
    j_sub = jnp.maximum(qi - 1, 0)
    n_far = j_sub

    scores_stage(qi, "diag")
    softmax_stage(lambda h: None)
    scores_stage(j_sub, "sub")
    values_stage(qi)
    softmax_stage(lambda h: rb_sub_ref[h * nblk + j_sub])
    scores_stage(0, "far")

    def far_body(t, carry):
        values_stage(jnp.where(t == 0, j_sub, t - 1))
        softmax_stage(lambda h: rb_far_ref[h * nblk + t])
        scores_stage(jnp.minimum(t + 1, jnp.maximum(n_far - 1, 0)), "far")
        return carry

    lax.fori_loop(0, n_far, far_body, 0)
    values_stage(jnp.where(n_far > 0, n_far - 1, j_sub))
    for h in range(ATTN_HEADS):
        rows = slice(h * HEAD_DIM, (h + 1) * HEAD_DIM)
        at_ref[rows] = at_ref[rows] * (1.0 / l_ref[h])

    a_ref[0] = at_ref[...].T.astype(BF16)


def _moba_attn(qt, k, vt, diag, sub, t31):
    B, S, _ = k.shape
    nblk = S // MOBA_BLOCK
    tile = pl.BlockSpec((ATTN_HEADS, MOBA_BLOCK, MOBA_BLOCK), lambda b, i: (0, 0, 0), pipeline_mode=pl.Buffered(1))
    return pl.pallas_call(
        _moba_attn_kernel,
        grid=(B, nblk),
        in_specs=[pl.BlockSpec(memory_space=pltpu.SMEM),
                  pl.BlockSpec((1, ATTN_WIDTH, MOBA_BLOCK), lambda b, i: (b, 0, i)),
                  pl.BlockSpec((1, S, ATTN_WIDTH), lambda b, i: (b, 0, 0)),
                  pl.BlockSpec((1, nblk, ATTN_WIDTH, MOBA_BLOCK), lambda b, i: (b, 0, 0, 0)),
                  tile, tile],
        out_specs=pl.BlockSpec((1, MOBA_BLOCK, ATTN_WIDTH), lambda b, i: (b, i, 0)),
        out_shape=jax.ShapeDtypeStruct((B, S, ATTN_WIDTH), BF16),
        scratch_shapes=[pltpu.VMEM((ATTN_HEADS * nblk, ATTN_WIDTH), BF16),
                        pltpu.VMEM((ATTN_HEADS * nblk, ATTN_WIDTH), BF16),
                        pltpu.VMEM((ATTN_HEADS * nblk, 1, MOBA_BLOCK), F32),
                        pltpu.VMEM((ATTN_HEADS * nblk, 1, MOBA_BLOCK), F32),
                        pltpu.VMEM((ATTN_WIDTH, MOBA_BLOCK), F32),
                        pltpu.VMEM((ATTN_HEADS, 2 * HEAD_DIM, MOBA_BLOCK), BF16),
                        *[pltpu.VMEM((ATTN_HEADS, 1, MOBA_BLOCK), F32)] * 4,
                        pltpu.VMEM((ATTN_HEADS, MOBA_BLOCK, MOBA_BLOCK), F32),
                        pltpu.VMEM((ATTN_HEADS, MOBA_BLOCK, MOBA_BLOCK), BF16)],
        compiler_params=pltpu.CompilerParams(
            dimension_semantics=("arbitrary", "arbitrary"), vmem_limit_bytes=V7X_VMEM_LIMIT_BYTES),
        name="moba_attn",
    )(t31, qt, k, vt, diag, sub)


def _mix_merge_kernel(h1_ref, un_ref, unprev_ref, a_ref, win_ref, wgrp_ref, scale_ref,
                      wa_ref, wp_ref, wout_ref, h2_ref, zext_ref):
    i = pl.program_id(1)
    tm = MIX_TOKENS
    z0 = 3 * ATTN_WIDTH
    zg = _dot(un_ref[0], win_ref[:, z0:])
    z = zg[:, :POOL_WIDTH]
    zprev = _dot(unprev_ref[0], win_ref[:, z0:z0 + POOL_WIDTH])
    zext_ref[:POOL_HALO] = jnp.where(i > 0, zprev, 0.0)
    zext_ref[POOL_HALO:] = z

    tpos = i * tm + lax.broadcasted_iota(jnp.int32, (tm, POOL_GROUP_WIDTH), 0)
    mixed = []
    for g, w in enumerate(POOL_WINDOWS):
        cols = slice(g * POOL_GROUP_WIDTH, (g + 1) * POOL_GROUP_WIDTH)
        wsum = zext_ref[POOL_HALO:, cols]
        for lag in range(1, w):
            wsum = wsum + zext_ref[POOL_HALO - lag:POOL_HALO - lag + tm, cols]
        mean = wsum / jnp.minimum(tpos + 1, w).astype(F32)
        pooled = (mean - zext_ref[POOL_HALO:, cols]).astype(BF16)
        mixed.append(_dot(pooled, wgrp_ref[g]))
    p = (jnp.concatenate(mixed, axis=1) * scale_ref[...]).astype(BF16)

    g_attn = zg[:, POOL_WIDTH:POOL_WIDTH + D_MODEL]
    g_pool = zg[:, POOL_WIDTH + D_MODEL:]
    merged = (jax.nn.sigmoid(g_attn) * _dot(a_ref[0], wa_ref[...])
              + jax.nn.sigmoid(g_pool) * _dot(p, wp_ref[...]))
    h2_ref[0] = h1_ref[0] + _dot(merged.astype(BF16), wout_ref[...])


def _mix_merge(h1, un, a, win, wgrp, scale, wa, wp, wout):
    B, S, D = h1.shape
    tm = MIX_TOKENS
    halo_per_tile = tm // POOL_HALO
    return pl.pallas_call(
        _mix_merge_kernel,
        grid=(B, S // tm),
        in_specs=[pl.BlockSpec((1, tm, D), lambda b, i: (b, i, 0)),
                  pl.BlockSpec((1, tm, D), lambda b, i: (b, i, 0)),
                  pl.BlockSpec((1, POOL_HALO, D), lambda b, i: (b, jnp.maximum(i * halo_per_tile - 1, 0), 0)),
                  pl.BlockSpec((1, tm, ATTN_WIDTH), lambda b, i: (b, i, 0)),
                  _resident(win.shape), _resident(wgrp.shape), _resident(scale.shape),
                  _resident(wa.shape), _resident(wp.shape), _resident(wout.shape)],
        out_specs=pl.BlockSpec((1, tm, D), lambda b, i: (b, i, 0)),
        out_shape=jax.ShapeDtypeStruct((B, S, D), F32),
        scratch_shapes=[pltpu.VMEM((POOL_HALO + tm, POOL_WIDTH), F32)],
        compiler_params=pltpu.CompilerParams(
            dimension_semantics=("arbitrary", "arbitrary"), vmem_limit_bytes=V7X_VMEM_LIMIT_BYTES),
        name="mix_merge",
    )(h1, un, un, a, win, wgrp, scale, wa, wp, wout)


def _ffn2_final_kernel(h_ref, g2_ref, wg_ref, wu_ref, wd_ref, gf_ref, out_ref):
    h3 = _swiglu_half_step(h_ref[0], g2_ref, wg_ref, wu_ref, wd_ref)
    out_ref[0] = _rms(h3, gf_ref[...])


def _ffn2_final(h, g2, wg, wu, wd, gf):
    B, S, D = h.shape
    tm = FFN_TOKENS
    return pl.pallas_call(
        _ffn2_final_kernel,
        grid=(B, S // tm),
        in_specs=[pl.BlockSpec((1, tm, D), lambda b, i: (b, i, 0)),
                  _resident(g2.shape), _resident(wg.shape), _resident(wu.shape), _resident(wd.shape),
                  _resident(gf.shape)],
        out_specs=pl.BlockSpec((1, tm, D), lambda b, i: (b, i, 0)),
        out_shape=jax.ShapeDtypeStruct((B, S, D), F32),
        compiler_params=pltpu.CompilerParams(
            dimension_semantics=("arbitrary", "arbitrary"), vmem_limit_bytes=V7X_VMEM_LIMIT_BYTES),
        name="ffn2_final",
    )(h, g2, wg, wu, wd, gf)


def kernel(x, ffn1_norm, ffn1_w_gate, ffn1_w_up, ffn1_w_down, mix_norm, w_in, pool_w_group, pool_scale,
           w_branch_attn, w_branch_pool, w_out, ffn2_norm, ffn2_w_gate, ffn2_w_up, ffn2_w_down,
           rpb_table, final_norm):
    B, S, D = x.shape
    assert (D, ffn1_w_gate.shape[0]) == (D_MODEL, 1)
    assert S % MOBA_BLOCK == 0 and S % FFN_TOKENS == 0 and S % MIX_TOKENS == 0
    bf = lambda w: w.astype(BF16)
    row = lambda v: v.reshape(1, -1)
    w_in = w_in[0]
    wqkv = bf(w_in[:, :3 * ATTN_WIDTH])

    diag, sub = _rpb_bias(rpb_table)
    (h1, un, qt, k, vt), later = _ffn1_qkv(
        x, row(ffn1_norm[0]), bf(ffn1_w_gate[0]), bf(ffn1_w_up[0]), bf(ffn1_w_down[0]), row(mix_norm[0]), wqkv,
        [w_in, w_branch_attn[0], w_branch_pool[0], w_out[0], ffn2_w_gate[0], ffn2_w_up[0], ffn2_w_down[0]])
    win_bf, wa_bf, wp_bf, wout_bf, wg2_bf, wu2_bf, wd2_bf = later
    a = _moba_attn(qt, k, vt, diag, sub, rpb_table[RPB_BUCKETS - 1])
    h2 = _mix_merge(h1, un, a, win_bf, bf(pool_w_group[0]), row(pool_scale[0]), wa_bf, wp_bf, wout_bf)
    return _ffn2_final(h2, row(ffn2_norm[0]), wg2_bf, wu2_bf, wd2_bf, row(final_norm))
```

```python
import math

import jax
import jax.numpy as jnp
from jax import lax
from jax.experimental import pallas as pl
from jax.experimental.pallas import tpu as pltpu

D_MODEL = 1024
HEAD_DIM = 64
ATTN_WIDTH = 512
ATTN_HEADS = 8
MOBA_BLOCK = 256
MOBA_TOPK = 3
POOL_WINDOWS = (2, 4, 8, 16)
POOL_WIDTH = 512
POOL_GROUP_WIDTH = 128
POOL_HALO = 16
D_FF = 2816
RPB_BUCKETS = 32
RPB_MAX_DISTANCE = 128
RMS_EPS = 1e-6
MASK_VALUE = -1e30
LOG2E = math.log2(math.e)
Q_SCALE = HEAD_DIM ** -0.5 * LOG2E
V7X_VMEM_LIMIT_BYTES = 56 * 1024 * 1024

FFN_TOKENS = 512
MIX_TOKENS = 512
N_LATER_WEIGHTS = 7
BF16_SUBLANES = 16

BF16 = jnp.bfloat16
F32 = jnp.float32


def _resident(shape):
    nd = len(shape)
    return pl.BlockSpec(shape, lambda *_: (0,) * nd, pipeline_mode=pl.Buffered(1))


def _dot(a, b):
    return jnp.dot(a, b, preferred_element_type=F32)


def _rms(x, g):
    return x * lax.rsqrt(jnp.mean(x * x, axis=-1, keepdims=True) + RMS_EPS) * g


def _swiglu_half_step(x, g_ref, wg_ref, wu_ref, wd_ref):
    xn = _rms(x, g_ref[...]).astype(BF16)
    gate = _dot(xn, wg_ref[...])
    up = _dot(xn, wu_ref[...])
    act = (gate * jax.nn.sigmoid(gate) * up).astype(BF16)
    return x + 0.5 * _dot(act, wd_ref[...])


def _rpb_bucket(dist):
    n = jnp.maximum(dist, 0)
    max_exact = RPB_BUCKETS // 2
    nf = jnp.maximum(n, 1).astype(F32)
    large = max_exact + (jnp.log(nf / max_exact) / math.log(RPB_MAX_DISTANCE / max_exact)
                         * (RPB_BUCKETS - max_exact)).astype(jnp.int32)
    large = jnp.minimum(large, RPB_BUCKETS - 1)
    return jnp.where(n < max_exact, n, large)


def _rpb_bias_kernel(table_ref, diag_ref, sub_ref):
    h = pl.program_id(0)
    width = 4 * MOBA_BLOCK
    dist = lax.broadcasted_iota(jnp.int32, (8, width), 1) - MOBA_BLOCK
    bucket = _rpb_bucket(dist)
    f = jnp.zeros((8, width), F32)
    for b in range(RPB_BUCKETS):
        f = jnp.where(bucket == b, table_ref[b, h], f)
    rows = jnp.broadcast_to(f[0:1] * LOG2E, (MOBA_BLOCK, width))
    toeplitz = pltpu.roll(rows, 0, 1, stride=1, stride_axis=0)
    kl = lax.broadcasted_iota(jnp.int32, (MOBA_BLOCK, MOBA_BLOCK), 0)
    ql = lax.broadcasted_iota(jnp.int32, (MOBA_BLOCK, MOBA_BLOCK), 1)
    diag_ref[0] = jnp.where(ql >= kl, toeplitz[:, MOBA_BLOCK:2 * MOBA_BLOCK], MASK_VALUE)
    sub_ref[0] = toeplitz[:, 2 * MOBA_BLOCK:3 * MOBA_BLOCK]


def _rpb_bias(rpb_table):
    tile = jax.ShapeDtypeStruct((ATTN_HEADS, MOBA_BLOCK, MOBA_BLOCK), F32)
    spec = pl.BlockSpec((1, MOBA_BLOCK, MOBA_BLOCK), lambda h: (h, 0, 0))
    return pl.pallas_call(
        _rpb_bias_kernel,
        grid=(ATTN_HEADS,),
        in_specs=[pl.BlockSpec(memory_space=pltpu.SMEM)],
        out_specs=[spec, spec],
        out_shape=[tile, tile],
        name="rpb_bias",
    )(rpb_table)


def _ffn1_qkv_kernel(x_ref, g1_ref, wg_ref, wu_ref, wd_ref, gm_ref, wqkv_ref, *refs):
    later_f32, (h1_ref, un_ref, qt_ref, k_ref, vt_ref), later_bf16 = (
        refs[:N_LATER_WEIGHTS], refs[N_LATER_WEIGHTS:N_LATER_WEIGHTS + 5], refs[N_LATER_WEIGHTS + 5:])
    for src_ref, dst_ref in zip(later_f32, later_bf16):
        dst_ref[...] = src_ref[...].astype(BF16)
    h1 = _swiglu_half_step(x_ref[0], g1_ref, wg_ref, wu_ref, wd_ref)
    h1_ref[0] = h1
    un = _rms(h1, gm_ref[...]).astype(BF16)
    un_ref[0] = un
    qkv = _dot(un, wqkv_ref[...])
    qt_ref[0] = (qkv[:, :ATTN_WIDTH] * Q_SCALE).T.astype(BF16)
    k_ref[0] = qkv[:, ATTN_WIDTH:2 * ATTN_WIDTH].astype(BF16)
    for c in range(FFN_TOKENS // MOBA_BLOCK):
        vt_ref[0, c] = qkv[c * MOBA_BLOCK:(c + 1) * MOBA_BLOCK, 2 * ATTN_WIDTH:].T.astype(BF16)


def _ffn1_qkv(x, g1, wg, wu, wd, gm, wqkv, later_weights):
    B, S, D = x.shape
    tm = FFN_TOKENS
    nblk = S // MOBA_BLOCK
    tiles_per_seq = S // tm
    steps = B * tiles_per_seq
    assert len(later_weights) == N_LATER_WEIGHTS
    step_of = lambda b, i: b * tiles_per_seq + i

    def slab_view(w):
        n = max(n for n in range(1, steps + 1) if steps % n == 0 and w.shape[0] % (n * BF16_SUBLANES) == 0)
        view = w.reshape(n, w.shape[0] // n, w.shape[1])
        return view, pl.BlockSpec((1,) + view.shape[1:], lambda b, i: (step_of(b, i) // (steps // n), 0, 0))

    slabs, slab_specs = zip(*[slab_view(w) for w in later_weights])
    tok = lambda width, dt: (pl.BlockSpec((1, tm, width), lambda b, i: (b, i, 0)),
                             jax.ShapeDtypeStruct((B, S, width), dt))
    h1_spec, h1_shape = tok(D, F32)
    un_spec, un_shape = tok(D, BF16)
    k_spec, k_shape = tok(ATTN_WIDTH, BF16)
    outs = pl.pallas_call(
        _ffn1_qkv_kernel,
        grid=(B, tiles_per_seq),
        in_specs=[pl.BlockSpec((1, tm, D), lambda b, i: (b, i, 0)),
                  _resident(g1.shape), _resident(wg.shape), _resident(wu.shape), _resident(wd.shape),
                  _resident(gm.shape), _resident(wqkv.shape), *slab_specs],
        out_specs=[h1_spec, un_spec,
                   pl.BlockSpec((1, ATTN_WIDTH, tm), lambda b, i: (b, 0, i)),
                   k_spec,
                   pl.BlockSpec((1, tm // MOBA_BLOCK, ATTN_WIDTH, MOBA_BLOCK), lambda b, i: (b, i, 0, 0)),
                   *slab_specs],
        out_shape=[h1_shape, un_shape,
                   jax.ShapeDtypeStruct((B, ATTN_WIDTH, S), BF16),
                   k_shape,
                   jax.ShapeDtypeStruct((B, nblk, ATTN_WIDTH, MOBA_BLOCK), BF16)]
                  + [jax.ShapeDtypeStruct(w.shape, BF16) for w in slabs],
        compiler_params=pltpu.CompilerParams(
            dimension_semantics=("arbitrary", "arbitrary"), vmem_limit_bytes=V7X_VMEM_LIMIT_BYTES),
        name="ffn1_qkv",
    )(x, g1, wg, wu, wd, gm, wqkv, *slabs)
    return outs[:5], [o.reshape(w.shape) for o, w in zip(outs[5:], later_weights)]


def _moba_attn_kernel(t31_ref, qt_ref, k_ref, vt_ref, diag_ref, sub_ref, a_ref,
                      kmhi_ref, kmlo_ref, rb_far_ref, rb_sub_ref, at_ref, qz_ref,
                      m_ref, l_ref, mblk_ref, alpha_ref, s_ref, p_ref):
    qi = pl.program_id(1)
    nblk = k_ref.shape[1] // MOBA_BLOCK
    nq = MOBA_BLOCK

    @pl.when(qi == 0)
    def _():
        blk_row = lax.broadcasted_iota(jnp.int32, (nblk, ATTN_WIDTH), 0)
        km = jnp.zeros((nblk, ATTN_WIDTH), F32)
        for j in range(nblk):
            mean_j = jnp.mean(k_ref[0, j * MOBA_BLOCK:(j + 1) * MOBA_BLOCK, :].astype(F32), axis=0, keepdims=True)
            km = jnp.where(blk_row == j, mean_j, km)
        km = jnp.concatenate([km] * ATTN_HEADS, axis=0)
        row_head = lax.broadcasted_iota(jnp.int32, km.shape, 0) // nblk
        col_head = lax.broadcasted_iota(jnp.int32, km.shape, 1) // HEAD_DIM
        km = jnp.where(row_head == col_head, km, 0.0)
        hi = km.astype(BF16)
        kmhi_ref[...] = hi
        kmlo_ref[...] = (km - hi.astype(F32)).astype(BF16)

    qt = qt_ref[0]
    gates = _dot(kmhi_ref[...], qt) + _dot(kmlo_ref[...], qt)
    jrow = lax.broadcasted_iota(jnp.int32, (nblk, nq), 0)
    for h in range(ATTN_HEADS):
        g = gates[h * nblk:(h + 1) * nblk]
        cnt = jnp.zeros((nblk, nq), jnp.int32)
        for jp in range(nblk):
            gb = jnp.broadcast_to(g[jp:jp + 1], (nblk, nq))
            beats = (gb > g) | ((gb == g) & (jp < jrow))
            cnt = cnt + jnp.where(beats, jnp.where(jp < qi, 1, 0), 0)
        sel = (jrow < qi) & (cnt < MOBA_TOPK)
        far = jnp.where(sel, t31_ref[h] * LOG2E, MASK_VALUE)
        sub = jnp.where(sel, 0.0, MASK_VALUE)
        for j in range(nblk):
            rb_far_ref[h * nblk + j] = far[j:j + 1]
            rb_sub_ref[h * nblk + j] = sub[j:j + 1]

    zeros_half = jnp.zeros((HEAD_DIM, nq), BF16)
    for h in range(ATTN_HEADS):
        qh = qt_ref[0, h * HEAD_DIM:(h + 1) * HEAD_DIM, :]
        qz_ref[h] = jnp.concatenate([qh, zeros_half] if h % 2 == 0 else [zeros_half, qh], axis=0)

    def scores_stage(j, kind):
        start = pl.multiple_of(j * MOBA_BLOCK, MOBA_BLOCK)
        for h in range(ATTN_HEADS):
            pair = h // 2
            kblk = k_ref[0, pl.ds(start, MOBA_BLOCK), pair * 128:(pair + 1) * 128]
            s = _dot(kblk, qz_ref[h])
            if kind == "diag":
                s = s + diag_ref[h]
            elif kind == "sub":
                s = s + sub_ref[h]
            s_ref[h] = s
            mblk_ref[h] = jnp.max(s, axis=0, keepdims=True)

    def softmax_stage(row_bias):
        for h in range(ATTN_HEADS):
            rb = row_bias(h)
            m_old = m_ref[h]
            m_blk = mblk_ref[h] if rb is None else mblk_ref[h] + rb
            m_new = jnp.maximum(m_old, m_blk)
            offset = m_new if rb is None else m_new - rb
            p_ref[h] = jnp.exp2(s_ref[h] - offset).astype(BF16)
            alpha_ref[h] = jnp.exp2(m_old - m_new)
            m_ref[h] = m_new

    ones_rows = jnp.ones((BF16_SUBLANES, nq), BF16)

    def values_stage(j):
        for h in range(ATTN_HEADS):
            rows = slice(h * HEAD_DIM, (h + 1) * HEAD_DIM)
            v_ones = jnp.concatenate([vt_ref[0, j, rows, :], ones_rows], axis=0)
            pv = _dot(v_ones, p_ref[h])
            alpha = alpha_ref[h]
            at_ref[rows] = alpha * at_ref[rows] + pv[:HEAD_DIM]
            l_ref[h] = alpha * l_ref[h] + pv[HEAD_DIM:HEAD_DIM + 1]

    m_ref[...] = jnp.full(m_ref.shape, MASK_VALUE, F32)
    l_ref[...] = jnp.zeros(l_ref.shape, F32)
    at_ref[...] = jnp.zeros(at_ref.shape, F32)
    j_sub = jnp.maximum(qi - 1, 0)
    n_far = j_sub

    scores_stage(qi, "diag")
    softmax_stage(lambda h: None)
    scores_stage(j_sub, "sub")
    values_stage(qi)
    softmax_stage(lambda h: rb_sub_ref[h * nblk + j_sub])
    scores_stage(0, "far")

    def far_body(t, carry):
        values_stage(jnp.where(t == 0, j_sub, t - 1))
        softmax_stage(lambda h: rb_far_ref[h * nblk + t])
        scores_stage(jnp.minimum(t + 1, jnp.maximum(n_far - 1, 0)), "far")
        return carry

    lax.fori_loop(0, n_far, far_body, 0)
    values_stage(jnp.where(n_far > 0, n_far - 1, j_sub))
    for h in range(ATTN_HEADS):
        rows = slice(h * HEAD_DIM, (h + 1) * HEAD_DIM)
        at_ref[rows] = at_ref[rows] * (1.0 / l_ref[h])

    a_ref[0] = at_ref[...].T.astype(BF16)


def _moba_attn(qt, k, vt, diag, sub, t31):
    B, S, _ = k.shape
    nblk = S // MOBA_BLOCK
    tile = pl.BlockSpec((ATTN_HEADS, MOBA_BLOCK, MOBA_BLOCK), lambda b, i: (0, 0, 0), pipeline_mode=pl.Buffered(1))
    return pl.pallas_call(
        _moba_attn_kernel,
        grid=(B, nblk),
        in_specs=[pl.BlockSpec(memory_space=pltpu.SMEM),
                  pl.BlockSpec((1, ATTN_WIDTH, MOBA_BLOCK), lambda b, i: (b, 0, i)),
                  pl.BlockSpec((1, S, ATTN_WIDTH), lambda b, i: (b, 0, 0)),
                  pl.BlockSpec((1, nblk, ATTN_WIDTH, MOBA_BLOCK), lambda b, i: (b, 0, 0, 0)),
                  tile, tile],
        out_specs=pl.BlockSpec((1, MOBA_BLOCK, ATTN_WIDTH), lambda b, i: (b, i, 0)),
        out_shape=jax.ShapeDtypeStruct((B, S, ATTN_WIDTH), BF16),
        scratch_shapes=[pltpu.VMEM((ATTN_HEADS * nblk, ATTN_WIDTH), BF16),
                        pltpu.VMEM((ATTN_HEADS * nblk, ATTN_WIDTH), BF16),
                        pltpu.VMEM((ATTN_HEADS * nblk, 1, MOBA_BLOCK), F32),
                        pltpu.VMEM((ATTN_HEADS * nblk, 1, MOBA_BLOCK), F32),
                        pltpu.VMEM((ATTN_WIDTH, MOBA_BLOCK), F32),
                        pltpu.VMEM((ATTN_HEADS, 2 * HEAD_DIM, MOBA_BLOCK), BF16),
                        *[pltpu.VMEM((ATTN_HEADS, 1, MOBA_BLOCK), F32)] * 4,
                        pltpu.VMEM((ATTN_HEADS, MOBA_BLOCK, MOBA_BLOCK), F32),
                        pltpu.VMEM((ATTN_HEADS, MOBA_BLOCK, MOBA_BLOCK), BF16)],
        compiler_params=pltpu.CompilerParams(
            dimension_semantics=("arbitrary", "arbitrary"), vmem_limit_bytes=V7X_VMEM_LIMIT_BYTES),
        name="moba_attn",
    )(t31, qt, k, vt, diag, sub)


def _mix_merge_kernel(h1_ref, un_ref, unprev_ref, a_ref, win_ref, wgrp_ref, scale_ref,
                      wa_ref, wp_ref, wout_ref, h2_ref, zext_ref):
    i = pl.program_id(1)
    tm = MIX_TOKENS
    z0 = 3 * ATTN_WIDTH
    zg = _dot(un_ref[0], win_ref[:, z0:])
    z = zg[:, :POOL_WIDTH]
    zprev = _dot(unprev_ref[0], win_ref[:, z0:z0 + POOL_WIDTH])
    zext_ref[:POOL_HALO] = jnp.where(i > 0, zprev, 0.0)
    zext_ref[POOL_HALO:] = z

    tpos = i * tm + lax.broadcasted_iota(jnp.int32, (tm, POOL_GROUP_WIDTH), 0)
    mixed = []
    for g, w in enumerate(POOL_WINDOWS):
        cols = slice(g * POOL_GROUP_WIDTH, (g + 1) * POOL_GROUP_WIDTH)
        wsum = zext_ref[POOL_HALO:, cols]
        for lag in range(1, w):
            wsum = wsum + zext_ref[POOL_HALO - lag:POOL_HALO - lag + tm, cols]
        mean = wsum / jnp.minimum(tpos + 1, w).astype(F32)
        pooled = (mean - zext_ref[POOL_HALO:, cols]).astype(BF16)
        mixed.append(_dot(pooled, wgrp_ref[g]))
    p = (jnp.concatenate(mixed, axis=1) * scale_ref[...]).astype(BF16)

    g_attn = zg[:, POOL_WIDTH:POOL_WIDTH + D_MODEL]
    g_pool = zg[:, POOL_WIDTH + D_MODEL:]
    merged = (jax.nn.sigmoid(g_attn) * _dot(a_ref[0], wa_ref[...])
              + jax.nn.sigmoid(g_pool) * _dot(p, wp_ref[...]))
    h2_ref[0] = h1_ref[0] + _dot(merged.astype(BF16), wout_ref[...])


def _mix_merge(h1, un, a, win, wgrp, scale, wa, wp, wout):
    B, S, D = h1.shape
    tm = MIX_TOKENS
    halo_per_tile = tm // POOL_HALO
    return pl.pallas_call(
        _mix_merge_kernel,
        grid=(B, S // tm),
        in_specs=[pl.BlockSpec((1, tm, D), lambda b, i: (b, i, 0)),
                  pl.BlockSpec((1, tm, D), lambda b, i: (b, i, 0)),
                  pl.BlockSpec((1, POOL_HALO, D), lambda b, i: (b, jnp.maximum(i * halo_per_tile - 1, 0), 0)),
                  pl.BlockSpec((1, tm, ATTN_WIDTH), lambda b, i: (b, i, 0)),
                  _resident(win.shape), _resident(wgrp.shape), _resident(scale.shape),
                  _resident(wa.shape), _resident(wp.shape), _resident(wout.shape)],
        out_specs=pl.BlockSpec((1, tm, D), lambda b, i: (b, i, 0)),
        out_shape=jax.ShapeDtypeStruct((B, S, D), F32),
        scratch_shapes=[pltpu.VMEM((POOL_HALO + tm, POOL_WIDTH), F32)],
        compiler_params=pltpu.CompilerParams(
            dimension_semantics=("arbitrary", "arbitrary"), vmem_limit_bytes=V7X_VMEM_LIMIT_BYTES),
        name="mix_merge",
    )(h1, un, un, a, win, wgrp, scale, wa, wp, wout)


def _ffn2_final_kernel(h_ref, g2_ref, wg_ref, wu_ref, wd_ref, gf_ref, out_ref):
    h3 = _swiglu_half_step(h_ref[0], g2_ref, wg_ref, wu_ref, wd_ref)
    out_ref[0] = _rms(h3, gf_ref[...])


def _ffn2_final(h, g2, wg, wu, wd, gf):
    B, S, D = h.shape
    tm = FFN_TOKENS
    return pl.pallas_call(
        _ffn2_final_kernel,
        grid=(B, S // tm),
        in_specs=[pl.BlockSpec((1, tm, D), lambda b, i: (b, i, 0)),
                  _resident(g2.shape), _resident(wg.shape), _resident(wu.shape), _resident(wd.shape),
                  _resident(gf.shape)],
        out_specs=pl.BlockSpec((1, tm, D), lambda b, i: (b, i, 0)),
        out_shape=jax.ShapeDtypeStruct((B, S, D), F32),
        compiler_params=pltpu.CompilerParams(
            dimension_semantics=("arbitrary", "arbitrary"), vmem_limit_bytes=V7X_VMEM_LIMIT_BYTES),
        name="ffn2_final",
    )(h, g2, wg, wu, wd, gf)


def kernel(x, ffn1_norm, ffn1_w_gate, ffn1_w_up, ffn1_w_down, mix_norm, w_in, pool_w_group, pool_scale,
           w_branch_attn, w_branch_pool, w_out, ffn2_norm, ffn2_w_gate, ffn2_w_up, ffn2_w_down,
           rpb_table, final_norm):
    B, S, D = x.shape
    assert (D, ffn1_w_gate.shape[0]) == (D_MODEL, 1)
    assert S % MOBA_BLOCK == 0 and S % FFN_TOKENS == 0 and S % MIX_TOKENS == 0
    bf = lambda w: w.astype(BF16)
    row = lambda v: v.reshape(1, -1)
    w_in = w_in[0]
    wqkv = bf(w_in[:, :3 * ATTN_WIDTH])

    diag, sub = _rpb_bias(rpb_table)
    (h1, un, qt, k, vt), later = _ffn1_qkv(
        x, row(ffn1_norm[0]), bf(ffn1_w_gate[0]), bf(ffn1_w_up[0]), bf(ffn1_w_down[0]), row(mix_norm[0]), wqkv,
        [w_in, w_branch_attn[0], w_branch_pool[0], w_out[0], ffn2_w_gate[0], ffn2_w_up[0], ffn2_w_down[0]])
    win_bf, wa_bf, wp_bf, wout_bf, wg2_bf, wu2_bf, wd2_bf = later
    a = _moba_attn(qt, k, vt, diag, sub, rpb_table[RPB_BUCKETS - 1])
    h2 = _mix_merge(h1, un, a, win_bf, bf(pool_w_group[0]), row(pool_scale[0]), wa_bf, wp_bf, wout_bf)
    return _ffn2_final(h2, row(ffn2_norm[0]), wg2_bf, wu2_bf, wd2_bf, row(final_norm))
```

```python
import math

import jax
import jax.numpy as jnp
from jax import lax
from jax.experimental import pallas as pl
from jax.experimental.pallas import tpu as pltpu

D_MODEL = 1024
HEAD_DIM = 64
ATTN_WIDTH = 512
ATTN_HEADS = 8
MOBA_BLOCK = 256
MOBA_TOPK = 3
POOL_WINDOWS = (2, 4, 8, 16)
POOL_WIDTH = 512
POOL_GROUP_WIDTH = 128
POOL_HALO = 16
D_FF = 2816
RPB_BUCKETS = 32
RPB_MAX_DISTANCE = 128
RMS_EPS = 1e-6
MASK_VALUE = -1e30
LOG2E = math.log2(math.e)
Q_SCALE = HEAD_DIM ** -0.5 * LOG2E
ATTN_ROWS = 2
V7X_VMEM_LIMIT_BYTES = 56 * 1024 * 1024

FFN_TOKENS = 512
MIX_TOKENS = 512
N_LATER_WEIGHTS = 7
BF16_SUBLANES = 16

BF16 = jnp.bfloat16
F32 = jnp.float32


def _resident(shape):
    nd = len(shape)
    return pl.BlockSpec(shape, lambda *_: (0,) * nd, pipeline_mode=pl.Buffered(1))


def _dot(a, b):
    return jnp.dot(a, b, preferred_element_type=F32)


def _rms(x, g):
    return x * lax.rsqrt(jnp.mean(x * x, axis=-1, keepdims=True) + RMS_EPS) * g


def _swiglu_half_step(x, g_ref, wg_ref, wu_ref, wd_ref):
    xn = _rms(x, g_ref[...]).astype(BF16)
    gate = _dot(xn, wg_ref[...])
    up = _dot(xn, wu_ref[...])
    act = (gate * jax.nn.sigmoid(gate) * up).astype(BF16)
    return x + 0.5 * _dot(act, wd_ref[...])


def _rpb_bucket(dist):
    n = jnp.maximum(dist, 0)
    max_exact = RPB_BUCKETS // 2
    nf = jnp.maximum(n, 1).astype(F32)
    large = max_exact + (jnp.log(nf / max_exact) / math.log(RPB_MAX_DISTANCE / max_exact)
                         * (RPB_BUCKETS - max_exact)).astype(jnp.int32)
    large = jnp.minimum(large, RPB_BUCKETS - 1)
    return jnp.where(n < max_exact, n, large)


def _rpb_bias_kernel(table_ref, diag_ref, sub_ref):
    h = pl.program_id(0)
    width = 4 * MOBA_BLOCK
    dist = lax.broadcasted_iota(jnp.int32, (8, width), 1) - MOBA_BLOCK
    bucket = _rpb_bucket(dist)
    f = jnp.zeros((8, width), F32)
    for b in range(RPB_BUCKETS):
        f = jnp.where(bucket == b, table_ref[b, h], f)
    rows = jnp.broadcast_to(f[0:1] * LOG2E, (MOBA_BLOCK, width))
    toeplitz = pltpu.roll(rows, 0, 1, stride=1, stride_axis=0)
    kl = lax.broadcasted_iota(jnp.int32, (MOBA_BLOCK, MOBA_BLOCK), 0)
    ql = lax.broadcasted_iota(jnp.int32, (MOBA_BLOCK, MOBA_BLOCK), 1)
    diag_ref[0] = jnp.where(ql >= kl, toeplitz[:, MOBA_BLOCK:2 * MOBA_BLOCK], MASK_VALUE)
    sub_ref[0] = toeplitz[:, 2 * MOBA_BLOCK:3 * MOBA_BLOCK]


def _rpb_bias(rpb_table):
    tile = jax.ShapeDtypeStruct((ATTN_HEADS, MOBA_BLOCK, MOBA_BLOCK), F32)
    spec = pl.BlockSpec((1, MOBA_BLOCK, MOBA_BLOCK), lambda h: (h, 0, 0))
    return pl.pallas_call(
        _rpb_bias_kernel,
        grid=(ATTN_HEADS,),
        in_specs=[pl.BlockSpec(memory_space=pltpu.SMEM)],
        out_specs=[spec, spec],
        out_shape=[tile, tile],
        name="rpb_bias",
    )(rpb_table)


def _ffn1_qkv_kernel(x_ref, g1_ref, wg_ref, wu_ref, wd_ref, gm_ref, wqkv_ref, *refs):
    later_f32, (h1_ref, un_ref, qt_ref, k_ref, vt_ref), later_bf16 = (
        refs[:N_LATER_WEIGHTS], refs[N_LATER_WEIGHTS:N_LATER_WEIGHTS + 5], refs[N_LATER_WEIGHTS + 5:])
    for src_ref, dst_ref in zip(later_f32, later_bf16):
        dst_ref[...] = src_ref[...].astype(BF16)
    h1 = _swiglu_half_step(x_ref[0], g1_ref, wg_ref, wu_ref, wd_ref)
    h1_ref[0] = h1
    un = _rms(h1, gm_ref[...]).astype(BF16)
    un_ref[0] = un
    qkv = _dot(un, wqkv_ref[...])
    qt_ref[0] = (qkv[:, :ATTN_WIDTH] * Q_SCALE).T.astype(BF16)
    k_ref[0] = qkv[:, ATTN_WIDTH:2 * ATTN_WIDTH].astype(BF16)
    for c in range(FFN_TOKENS // MOBA_BLOCK):
        vt_ref[0, c] = qkv[c * MOBA_BLOCK:(c + 1) * MOBA_BLOCK, 2 * ATTN_WIDTH:].T.astype(BF16)


def _ffn1_qkv(x, g1, wg, wu, wd, gm, wqkv, later_weights):
    B, S, D = x.shape
    tm = FFN_TOKENS
    nblk = S // MOBA_BLOCK
    tiles_per_seq = S // tm
    steps = B * tiles_per_seq
    assert len(later_weights) == N_LATER_WEIGHTS
    step_of = lambda b, i: b * tiles_per_seq + i

    def slab_view(w):
        n = max(n for n in range(1, steps + 1) if steps % n == 0 and w.shape[0] % (n * BF16_SUBLANES) == 0)
        view = w.reshape(n, w.shape[0] // n, w.shape[1])
        return view, pl.BlockSpec((1,) + view.shape[1:], lambda b, i: (step_of(b, i) // (steps // n), 0, 0))

    slabs, slab_specs = zip(*[slab_view(w) for w in later_weights])
    tok = lambda width, dt: (pl.BlockSpec((1, tm, width), lambda b, i: (b, i, 0)),
                             jax.ShapeDtypeStruct((B, S, width), dt))
    h1_spec, h1_shape = tok(D, F32)
    un_spec, un_shape = tok(D, BF16)
    k_spec, k_shape = tok(ATTN_WIDTH, BF16)
    outs = pl.pallas_call(
        _ffn1_qkv_kernel,
        grid=(B, tiles_per_seq),
        in_specs=[pl.BlockSpec((1, tm, D), lambda b, i: (b, i, 0)),
                  _resident(g1.shape), _resident(wg.shape), _resident(wu.shape), _resident(wd.shape),
                  _resident(gm.shape), _resident(wqkv.shape), *slab_specs],
        out_specs=[h1_spec, un_spec,
                   pl.BlockSpec((1, ATTN_WIDTH, tm), lambda b, i: (b, 0, i)),
                   k_spec,
                   pl.BlockSpec((1, tm // MOBA_BLOCK, ATTN_WIDTH, MOBA_BLOCK), lambda b, i: (b, i, 0, 0)),
                   *slab_specs],
        out_shape=[h1_shape, un_shape,
                   jax.ShapeDtypeStruct((B, ATTN_WIDTH, S), BF16),
                   k_shape,
                   jax.ShapeDtypeStruct((B, nblk, ATTN_WIDTH, MOBA_BLOCK), BF16)]
                  + [jax.ShapeDtypeStruct(w.shape, BF16) for w in slabs],
        compiler_params=pltpu.CompilerParams(
            dimension_semantics=("arbitrary", "arbitrary"), vmem_limit_bytes=V7X_VMEM_LIMIT_BYTES),
        name="ffn1_qkv",
    )(x, g1, wg, wu, wd, gm, wqkv, *slabs)
    return outs[:5], [o.reshape(w.shape) for o, w in zip(outs[5:], later_weights)]


def _moba_attn_kernel(t31_ref, qt_ref, k_ref, vt_ref, diag_ref, sub_ref, a_ref,
                      kmhi_ref, kmlo_ref, rb_far_ref, rb_sub_ref, at_ref, qz_ref,
                      m_ref, l_ref, mblk_ref, alpha_ref, s_ref, p_ref):
    qi = pl.program_id(1)
    nblk = k_ref.shape[2] // MOBA_BLOCK
    nq = MOBA_BLOCK
    units = [(r, h) for r in range(ATTN_ROWS) for h in range(ATTN_HEADS)]

    @pl.when(qi == 0)
    def _():
        for r in range(ATTN_ROWS):
            blk_row = lax.broadcasted_iota(jnp.int32, (nblk, ATTN_WIDTH), 0)
            km = jnp.zeros((nblk, ATTN_WIDTH), F32)
            for j in range(nblk):
                mean_j = jnp.mean(k_ref[r, 0, j * MOBA_BLOCK:(j + 1) * MOBA_BLOCK, :].astype(F32), axis=0,
                                  keepdims=True)
                km = jnp.where(blk_row == j, mean_j, km)
            km = jnp.concatenate([km] * ATTN_HEADS, axis=0)
            row_head = lax.broadcasted_iota(jnp.int32, km.shape, 0) // nblk
            col_head = lax.broadcasted_iota(jnp.int32, km.shape, 1) // HEAD_DIM
            km = jnp.where(row_head == col_head, km, 0.0)
            hi = km.astype(BF16)
            kmhi_ref[r] = hi
            kmlo_ref[r] = (km - hi.astype(F32)).astype(BF16)

    jrow = lax.broadcasted_iota(jnp.int32, (nblk, nq), 0)
    for r in range(ATTN_ROWS):
        qt = qt_ref[r, 0]
        gates = _dot(kmhi_ref[r], qt) + _dot(kmlo_ref[r], qt)
        for h in range(ATTN_HEADS):
            u = r * ATTN_HEADS + h
            g = gates[h * nblk:(h + 1) * nblk]
            cnt = jnp.zeros((nblk, nq), jnp.int32)
            for jp in range(nblk):
                gb = jnp.broadcast_to(g[jp:jp + 1], (nblk, nq))
                beats = (gb > g) | ((gb == g) & (jp < jrow))
                cnt = cnt + jnp.where(beats, jnp.where(jp < qi, 1, 0), 0)
            sel = (jrow < qi) & (cnt < MOBA_TOPK)
            far = jnp.where(sel, t31_ref[h] * LOG2E, MASK_VALUE)
            sub = jnp.where(sel, 0.0, MASK_VALUE)
            for j in range(nblk):
                rb_far_ref[u * nblk + j] = far[j:j + 1]
                rb_sub_ref[u * nblk + j] = sub[j:j + 1]

    zeros_half = jnp.zeros((HEAD_DIM, nq), BF16)
    for r, h in units:
        qh = qt_ref[r, 0, h * HEAD_DIM:(h + 1) * HEAD_DIM, :]
        qz_ref[r * ATTN_HEADS + h] = jnp.concatenate([qh, zeros_half] if h % 2 == 0 else [zeros_half, qh], axis=0)

    def scores_stage(j, kind):
        start = pl.multiple_of(j * MOBA_BLOCK, MOBA_BLOCK)
        for u, (r, h) in enumerate(units):
            pair = h // 2
            kblk = k_ref[r, 0, pl.ds(start, MOBA_BLOCK), pair * 128:(pair + 1) * 128]
            s = _dot(kblk, qz_ref[u])
            if kind == "diag":
                s = s + diag_ref[h]
            elif kind == "sub":
                s = s + sub_ref[h]
            s_ref[u] = s
            mblk_ref[u] = jnp.max(s, axis=0, keepdims=True)

    def softmax_stage(row_bias):
        for u in range(len(units)):
            rb = row_bias(u)
            m_old = m_ref[u]
            m_blk = mblk_ref[u] if rb is None else mblk_ref[u] + rb
            m_new = jnp.maximum(m_old, m_blk)
            offset = m_new if rb is None else m_new - rb
            p_ref[u] = jnp.exp2(s_ref[u] - offset).astype(BF16)
            alpha_ref[u] = jnp.exp2(m_old - m_new)
            m_ref[u] = m_new

    ones_rows = jnp.ones((BF16_SUBLANES, nq), BF16)

    def values_stage(j):
        for u, (r, h) in enumerate(units):
            rows = slice(u * HEAD_DIM, (u + 1) * HEAD_DIM)
            v_h = vt_ref[r, 0, j, h * HEAD_DIM:(h + 1) * HEAD_DIM, :]
            pv = _dot(jnp.concatenate([v_h, ones_rows], axis=0), p_ref[u])
            alpha = alpha_ref[u]
            at_ref[rows] = alpha * at_ref[rows] + pv[:HEAD_DIM]
            l_ref[u] = alpha * l_ref[u] + pv[HEAD_DIM:HEAD_DIM + 1]

    m_ref[...] = jnp.full(m_ref.shape, MASK_VALUE, F32)
    l_ref[...] = jnp.zeros(l_ref.shape, F32)
    at_ref[...] = jnp.zeros(at_ref.shape, F32)
    j_sub = jnp.maximum(qi - 1, 0)
    n_far = j_sub

    scores_stage(qi, "diag")
    softmax_stage(lambda u: None)
    scores_stage(j_sub, "sub")
    values_stage(qi)
    softmax_stage(lambda u: rb_sub_ref[u * nblk + j_sub])
    scores_stage(0, "far")

    def far_body(t, carry):
        values_stage(jnp.where(t == 0, j_sub, t - 1))
        softmax_stage(lambda u: rb_far_ref[u * nblk + t])
        scores_stage(jnp.minimum(t + 1, jnp.maximum(n_far - 1, 0)), "far")
        return carry

    lax.fori_loop(0, n_far, far_body, 0)
    values_stage(jnp.where(n_far > 0, n_far - 1, j_sub))
    for u in range(len(units)):
        rows = slice(u * HEAD_DIM, (u + 1) * HEAD_DIM)
        at_ref[rows] = at_ref[rows] * (1.0 / l_ref[u])
    for r in range(ATTN_ROWS):
        a_ref[r, 0] = at_ref[r * ATTN_WIDTH:(r + 1) * ATTN_WIDTH, :].T.astype(BF16)


def _moba_attn(qt, k, vt, diag, sub, t31):
    B, S, _ = k.shape
    nblk = S // MOBA_BLOCK
    groups = B // ATTN_ROWS
    n_units = ATTN_ROWS * ATTN_HEADS
    split = lambda x: x.reshape((ATTN_ROWS, groups) + x.shape[1:])
    tile = pl.BlockSpec((ATTN_HEADS, MOBA_BLOCK, MOBA_BLOCK), lambda g, i: (0, 0, 0), pipeline_mode=pl.Buffered(1))
    a = pl.pallas_call(
        _moba_attn_kernel,
        grid=(groups, nblk),
        in_specs=[pl.BlockSpec(memory_space=pltpu.SMEM),
                  pl.BlockSpec((ATTN_ROWS, 1, ATTN_WIDTH, MOBA_BLOCK), lambda g, i: (0, g, 0, i)),
                  pl.BlockSpec((ATTN_ROWS, 1, S, ATTN_WIDTH), lambda g, i: (0, g, 0, 0)),
                  pl.BlockSpec((ATTN_ROWS, 1, nblk, ATTN_WIDTH, MOBA_BLOCK), lambda g, i: (0, g, 0, 0, 0)),
                  tile, tile],
        out_specs=pl.BlockSpec((ATTN_ROWS, 1, MOBA_BLOCK, ATTN_WIDTH), lambda g, i: (0, g, i, 0)),
        out_shape=jax.ShapeDtypeStruct((ATTN_ROWS, groups, S, ATTN_WIDTH), BF16),
        scratch_shapes=[pltpu.VMEM((ATTN_ROWS, ATTN_HEADS * nblk, ATTN_WIDTH), BF16),
                        pltpu.VMEM((ATTN_ROWS, ATTN_HEADS * nblk, ATTN_WIDTH), BF16),
                        pltpu.VMEM((n_units * nblk, 1, MOBA_BLOCK), F32),
                        pltpu.VMEM((n_units * nblk, 1, MOBA_BLOCK), F32),
                        pltpu.VMEM((n_units * HEAD_DIM, MOBA_BLOCK), F32),
                        pltpu.VMEM((n_units, 2 * HEAD_DIM, MOBA_BLOCK), BF16),
                        *[pltpu.VMEM((n_units, 1, MOBA_BLOCK), F32)] * 4,
                        pltpu.VMEM((n_units, MOBA_BLOCK, MOBA_BLOCK), F32),
                        pltpu.VMEM((n_units, MOBA_BLOCK, MOBA_BLOCK), BF16)],
        compiler_params=pltpu.CompilerParams(
            dimension_semantics=("arbitrary", "arbitrary"), vmem_limit_bytes=V7X_VMEM_LIMIT_BYTES),
        name="moba_attn",
    )(t31, split(qt), split(k), split(vt), diag, sub)
    return a.reshape(B, S, ATTN_WIDTH)


def _mix_merge_kernel(h1_ref, un_ref, unprev_ref, a_ref, win_ref, wgrp_ref, scale_ref,
                      wa_ref, wp_ref, wout_ref, h2_ref, zext_ref):
    i = pl.program_id(1)
    tm = MIX_TOKENS
    z0 = 3 * ATTN_WIDTH
    zg = _dot(un_ref[0], win_ref[:, z0:])
    z = zg[:, :POOL_WIDTH]
    zprev = _dot(unprev_ref[0], win_ref[:, z0:z0 + POOL_WIDTH])
    zext_ref[:POOL_HALO] = jnp.where(i > 0, zprev, 0.0)
    zext_ref[POOL_HALO:] = z

    tpos = i * tm + lax.broadcasted_iota(jnp.int32, (tm, POOL_GROUP_WIDTH), 0)
    mixed = []
    for g, w in enumerate(POOL_WINDOWS):
        cols = slice(g * POOL_GROUP_WIDTH, (g + 1) * POOL_GROUP_WIDTH)
        wsum = zext_ref[POOL_HALO:, cols]
        for lag in range(1, w):
            wsum = wsum + zext_ref[POOL_HALO - lag:POOL_HALO - lag + tm, cols]
        mean = wsum / jnp.minimum(tpos + 1, w).astype(F32)
        pooled = (mean - zext_ref[POOL_HALO:, cols]).astype(BF16)
        mixed.append(_dot(pooled, wgrp_ref[g]))
    p = (jnp.concatenate(mixed, axis=1) * scale_ref[...]).astype(BF16)

    g_attn = zg[:, POOL_WIDTH:POOL_WIDTH + D_MODEL]
    g_pool = zg[:, POOL_WIDTH + D_MODEL:]
    merged = (jax.nn.sigmoid(g_attn) * _dot(a_ref[0], wa_ref[...])
              + jax.nn.sigmoid(g_pool) * _dot(p, wp_ref[...]))
    h2_ref[0] = h1_ref[0] + _dot(merged.astype(BF16), wout_ref[...])


def _mix_merge(h1, un, a, win, wgrp, scale, wa, wp, wout):
    B, S, D = h1.shape
    tm = MIX_TOKENS
    halo_per_tile = tm // POOL_HALO
    return pl.pallas_call(
        _mix_merge_kernel,
        grid=(B, S // tm),
        in_specs=[pl.BlockSpec((1, tm, D), lambda b, i: (b, i, 0)),
                  pl.BlockSpec((1, tm, D), lambda b, i: (b, i, 0)),
                  pl.BlockSpec((1, POOL_HALO, D), lambda b, i: (b, jnp.maximum(i * halo_per_tile - 1, 0), 0)),
                  pl.BlockSpec((1, tm, ATTN_WIDTH), lambda b, i: (b, i, 0)),
                  _resident(win.shape), _resident(wgrp.shape), _resident(scale.shape),
                  _resident(wa.shape), _resident(wp.shape), _resident(wout.shape)],
        out_specs=pl.BlockSpec((1, tm, D), lambda b, i: (b, i, 0)),
        out_shape=jax.ShapeDtypeStruct((B, S, D), F32),
        scratch_shapes=[pltpu.VMEM((POOL_HALO + tm, POOL_WIDTH), F32)],
        compiler_params=pltpu.CompilerParams(
            dimension_semantics=("arbitrary", "arbitrary"), vmem_limit_bytes=V7X_VMEM_LIMIT_BYTES),
        name="mix_merge",
    )(h1, un, un, a, win, wgrp, scale, wa, wp, wout)


def _ffn2_final_kernel(h_ref, g2_ref, wg_ref, wu_ref, wd_ref, gf_ref, out_ref):
    h3 = _swiglu_half_step(h_ref[0], g2_ref, wg_ref, wu_ref, wd_ref)
    out_ref[0] = _rms(h3, gf_ref[...])


def _ffn2_final(h, g2, wg, wu, wd, gf):
    B, S, D = h.shape
    tm = FFN_TOKENS
    return pl.pallas_call(
        _ffn2_final_kernel,
        grid=(B, S // tm),
        in_specs=[pl.BlockSpec((1, tm, D), lambda b, i: (b, i, 0)),
                  _resident(g2.shape), _resident(wg.shape), _resident(wu.shape), _resident(wd.shape),
                  _resident(gf.shape)],
        out_specs=pl.BlockSpec((1, tm, D), lambda b, i: (b, i, 0)),
        out_shape=jax.ShapeDtypeStruct((B, S, D), F32),
        compiler_params=pltpu.CompilerParams(
            dimension_semantics=("arbitrary", "arbitrary"), vmem_limit_bytes=V7X_VMEM_LIMIT_BYTES),
        name="ffn2_final",
    )(h, g2, wg, wu, wd, gf)


def kernel(x, ffn1_norm, ffn1_w_gate, ffn1_w_up, ffn1_w_down, mix_norm, w_in, pool_w_group, pool_scale,
           w_branch_attn, w_branch_pool, w_out, ffn2_norm, ffn2_w_gate, ffn2_w_up, ffn2_w_down,
           rpb_table, final_norm):
    B, S, D = x.shape
    assert (D, ffn1_w_gate.shape[0]) == (D_MODEL, 1)
    assert S % MOBA_BLOCK == 0 and S % FFN_TOKENS == 0 and S % MIX_TOKENS == 0 and B % ATTN_ROWS == 0
    bf = lambda w: w.astype(BF16)
    row = lambda v: v.reshape(1, -1)
    w_in = w_in[0]
    wqkv = bf(w_in[:, :3 * ATTN_WIDTH])

    diag, sub = _rpb_bias(rpb_table)
    (h1, un, qt, k, vt), later = _ffn1_qkv(
        x, row(ffn1_norm[0]), bf(ffn1_w_gate[0]), bf(ffn1_w_up[0]), bf(ffn1_w_down[0]), row(mix_norm[0]), wqkv,
        [w_in, w_branch_attn[0], w_branch_pool[0], w_out[0], ffn2_w_gate[0], ffn2_w_up[0], ffn2_w_down[0]])
    win_bf, wa_bf, wp_bf, wout_bf, wg2_bf, wu2_bf, wd2_bf = later
    a = _moba_attn(qt, k, vt, diag, sub, rpb_table[RPB_BUCKETS - 1])
    h2 = _mix_merge(h1, un, a, win_bf, bf(pool_w_group[0]), row(pool_scale[0]), wa_bf, wp_bf, wout_bf)
    return _ffn2_final(h2, row(ffn2_norm[0]), wg2_bf, wu2_bf, wd2_bf, row(final_norm))
```

```python
import math

import jax
import jax.numpy as jnp
from jax import lax
from jax.experimental import pallas as pl
from jax.experimental.pallas import tpu as pltpu

D_MODEL = 1024
HEAD_DIM = 64
ATTN_WIDTH = 512
ATTN_HEADS = 8
MOBA_BLOCK = 256
MOBA_TOPK = 3
POOL_WINDOWS = (2, 4, 8, 16)
POOL_WIDTH = 512
POOL_GROUP_WIDTH = 128
POOL_HALO = 16
D_FF = 2816
RPB_BUCKETS = 32
RPB_MAX_DISTANCE = 128
RMS_EPS = 1e-6
MASK_VALUE = -1e30
LOG2E = math.log2(math.e)
Q_SCALE = HEAD_DIM ** -0.5 * LOG2E
ATTN_ROWS = 2
V7X_VMEM_LIMIT_BYTES = 56 * 1024 * 1024

FFN_TOKENS = 512
FFN2_SUBTILES = 2
MIX_TOKENS = 512
MIX_SUBTILES = 2
N_LATER_WEIGHTS = 7
BF16_SUBLANES = 16

BF16 = jnp.bfloat16
F32 = jnp.float32


def _resident(shape):
    nd = len(shape)
    return pl.BlockSpec(shape, lambda *_: (0,) * nd, pipeline_mode=pl.Buffered(1))


def _dot(a, b):
    return jnp.dot(a, b, preferred_element_type=F32)


def _rms(x, g):
    return x * lax.rsqrt(jnp.mean(x * x, axis=-1, keepdims=True) + RMS_EPS) * g


def _swiglu_half_step(x, g_ref, wg_ref, wu_ref, wd_ref):
    xn = _rms(x, g_ref[...]).astype(BF16)
    gate = _dot(xn, wg_ref[...])
    up = _dot(xn, wu_ref[...])
    act = (gate * jax.nn.sigmoid(gate) * up).astype(BF16)
    return x + 0.5 * _dot(act, wd_ref[...])


def _rpb_bucket(dist):
    n = jnp.maximum(dist, 0)
    max_exact = RPB_BUCKETS // 2
    nf = jnp.maximum(n, 1).astype(F32)
    large = max_exact + (jnp.log(nf / max_exact) / math.log(RPB_MAX_DISTANCE / max_exact)
                         * (RPB_BUCKETS - max_exact)).astype(jnp.int32)
    large = jnp.minimum(large, RPB_BUCKETS - 1)
    return jnp.where(n < max_exact, n, large)


def _rpb_bias_kernel(table_ref, diag_ref, sub_ref):
    h = pl.program_id(0)
    width = 4 * MOBA_BLOCK
    dist = lax.broadcasted_iota(jnp.int32, (8, width), 1) - MOBA_BLOCK
    bucket = _rpb_bucket(dist)
    f = jnp.zeros((8, width), F32)
    for b in range(RPB_BUCKETS):
        f = jnp.where(bucket == b, table_ref[b, h], f)
    rows = jnp.broadcast_to(f[0:1] * LOG2E, (MOBA_BLOCK, width))
    toeplitz = pltpu.roll(rows, 0, 1, stride=1, stride_axis=0)
    kl = lax.broadcasted_iota(jnp.int32, (MOBA_BLOCK, MOBA_BLOCK), 0)
    ql = lax.broadcasted_iota(jnp.int32, (MOBA_BLOCK, MOBA_BLOCK), 1)
    diag_ref[0] = jnp.where(ql >= kl, toeplitz[:, MOBA_BLOCK:2 * MOBA_BLOCK], MASK_VALUE)
    sub_ref[0] = toeplitz[:, 2 * MOBA_BLOCK:3 * MOBA_BLOCK]


def _rpb_bias(rpb_table):
    tile = jax.ShapeDtypeStruct((ATTN_HEADS, MOBA_BLOCK, MOBA_BLOCK), F32)
    spec = pl.BlockSpec((1, MOBA_BLOCK, MOBA_BLOCK), lambda h: (h, 0, 0))
    return pl.pallas_call(
        _rpb_bias_kernel,
        grid=(ATTN_HEADS,),
        in_specs=[pl.BlockSpec(memory_space=pltpu.SMEM)],
        out_specs=[spec, spec],
        out_shape=[tile, tile],
        name="rpb_bias",
    )(rpb_table)


def _ffn1_qkv_kernel(x_ref, g1_ref, wg_ref, wu_ref, wd_ref, gm_ref, wqkv_ref, *refs):
    later_f32, (h1_ref, un_ref, qt_ref, k_ref, vt_ref), later_bf16 = (
        refs[:N_LATER_WEIGHTS], refs[N_LATER_WEIGHTS:N_LATER_WEIGHTS + 5], refs[N_LATER_WEIGHTS + 5:])
    for src_ref, dst_ref in zip(later_f32, later_bf16):
        dst_ref[...] = src_ref[...].astype(BF16)
    h1 = _swiglu_half_step(x_ref[0], g1_ref, wg_ref, wu_ref, wd_ref)
    h1_ref[0] = h1
    un = _rms(h1, gm_ref[...]).astype(BF16)
    un_ref[0] = un
    qkv = _dot(un, wqkv_ref[...])
    qt_ref[0] = (qkv[:, :ATTN_WIDTH] * Q_SCALE).T.astype(BF16)
    k_ref[0] = qkv[:, ATTN_WIDTH:2 * ATTN_WIDTH].astype(BF16)
    for c in range(FFN_TOKENS // MOBA_BLOCK):
        vt_ref[0, c] = qkv[c * MOBA_BLOCK:(c + 1) * MOBA_BLOCK, 2 * ATTN_WIDTH:].T.astype(BF16)


def _ffn1_qkv(x, g1, wg, wu, wd, gm, wqkv, later_weights):
    B, S, D = x.shape
    tm = FFN_TOKENS
    nblk = S // MOBA_BLOCK
    tiles_per_seq = S // tm
    steps = B * tiles_per_seq
    assert len(later_weights) == N_LATER_WEIGHTS
    step_of = lambda b, i: b * tiles_per_seq + i

    def slab_view(w):
        n = max(n for n in range(1, steps + 1) if steps % n == 0 and w.shape[0] % (n * BF16_SUBLANES) == 0)
        view = w.reshape(n, w.shape[0] // n, w.shape[1])
        return view, pl.BlockSpec((1,) + view.shape[1:], lambda b, i: (step_of(b, i) // (steps // n), 0, 0))

    slabs, slab_specs = zip(*[slab_view(w) for w in later_weights])
    tok = lambda width, dt: (pl.BlockSpec((1, tm, width), lambda b, i: (b, i, 0)),
                             jax.ShapeDtypeStruct((B, S, width), dt))
    h1_spec, h1_shape = tok(D, F32)
    un_spec, un_shape = tok(D, BF16)
    k_spec, k_shape = tok(ATTN_WIDTH, BF16)
    outs = pl.pallas_call(
        _ffn1_qkv_kernel,
        grid=(B, tiles_per_seq),
        in_specs=[pl.BlockSpec((1, tm, D), lambda b, i: (b, i, 0)),
                  _resident(g1.shape), _resident(wg.shape), _resident(wu.shape), _resident(wd.shape),
                  _resident(gm.shape), _resident(wqkv.shape), *slab_specs],
        out_specs=[h1_spec, un_spec,
                   pl.BlockSpec((1, ATTN_WIDTH, tm), lambda b, i: (b, 0, i)),
                   k_spec,
                   pl.BlockSpec((1, tm // MOBA_BLOCK, ATTN_WIDTH, MOBA_BLOCK), lambda b, i: (b, i, 0, 0)),
                   *slab_specs],
        out_shape=[h1_shape, un_shape,
                   jax.ShapeDtypeStruct((B, ATTN_WIDTH, S), BF16),
                   k_shape,
                   jax.ShapeDtypeStruct((B, nblk, ATTN_WIDTH, MOBA_BLOCK), BF16)]
                  + [jax.ShapeDtypeStruct(w.shape, BF16) for w in slabs],
        compiler_params=pltpu.CompilerParams(
            dimension_semantics=("arbitrary", "arbitrary"), vmem_limit_bytes=V7X_VMEM_LIMIT_BYTES),
        name="ffn1_qkv",
    )(x, g1, wg, wu, wd, gm, wqkv, *slabs)
    return outs[:5], [o.reshape(w.shape) for o, w in zip(outs[5:], later_weights)]


def _moba_attn_kernel(t31_ref, qt_ref, k_ref, vt_ref, diag_ref, sub_ref, a_ref,
                      kmhi_ref, kmlo_ref, rb_far_ref, rb_sub_ref, at_ref, qz_ref,
                      m_ref, l_ref, mblk_ref, alpha_ref, s_ref, p_ref):
    qi = pl.program_id(1)
    nblk = k_ref.shape[2] // MOBA_BLOCK
    nq = MOBA_BLOCK
    units = [(r, h) for r in range(ATTN_ROWS) for h in range(ATTN_HEADS)]

    @pl.when(qi == 0)
    def _():
        for r in range(ATTN_ROWS):
            blk_row = lax.broadcasted_iota(jnp.int32, (nblk, ATTN_WIDTH), 0)
            km = jnp.zeros((nblk, ATTN_WIDTH), F32)
            for j in range(nblk):
                mean_j = jnp.mean(k_ref[r, 0, j * MOBA_BLOCK:(j + 1) * MOBA_BLOCK, :].astype(F32), axis=0,
                                  keepdims=True)
                km = jnp.where(blk_row == j, mean_j, km)
            km = jnp.concatenate([km] * ATTN_HEADS, axis=0)
            row_head = lax.broadcasted_iota(jnp.int32, km.shape, 0) // nblk
            col_head = lax.broadcasted_iota(jnp.int32, km.shape, 1) // HEAD_DIM
            km = jnp.where(row_head == col_head, km, 0.0)
            hi = km.astype(BF16)
            kmhi_ref[r] = hi
            kmlo_ref[r] = (km - hi.astype(F32)).astype(BF16)

    jrow = lax.broadcasted_iota(jnp.int32, (nblk, nq), 0)
    for r in range(ATTN_ROWS):
        qt = qt_ref[r, 0]
        gates = _dot(kmhi_ref[r], qt) + _dot(kmlo_ref[r], qt)
        for h in range(ATTN_HEADS):
            u = r * ATTN_HEADS + h
            g = gates[h * nblk:(h + 1) * nblk]
            cnt = jnp.zeros((nblk, nq), jnp.int32)
            for jp in range(nblk):
                gb = jnp.broadcast_to(g[jp:jp + 1], (nblk, nq))
                beats = (gb > g) | ((gb == g) & (jp < jrow))
                cnt = cnt + jnp.where(beats, jnp.where(jp < qi, 1, 0), 0)
            sel = (jrow < qi) & (cnt < MOBA_TOPK)
            far = jnp.where(sel, t31_ref[h] * LOG2E, MASK_VALUE)
            sub = jnp.where(sel, 0.0, MASK_VALUE)
            for j in range(nblk):
                rb_far_ref[u * nblk + j] = far[j:j + 1]
                rb_sub_ref[u * nblk + j] = sub[j:j + 1]

    zeros_half = jnp.zeros((HEAD_DIM, nq), BF16)
    for r, h in units:
        qh = qt_ref[r, 0, h * HEAD_DIM:(h + 1) * HEAD_DIM, :]
        qz_ref[r * ATTN_HEADS + h] = jnp.concatenate([qh, zeros_half] if h % 2 == 0 else [zeros_half, qh], axis=0)

    def scores_stage(j, kind):
        start = pl.multiple_of(j * MOBA_BLOCK, MOBA_BLOCK)
        for u, (r, h) in enumerate(units):
            pair = h // 2
            kblk = k_ref[r, 0, pl.ds(start, MOBA_BLOCK), pair * 128:(pair + 1) * 128]
            s = _dot(kblk, qz_ref[u])
            if kind == "diag":
                s = s + diag_ref[h]
            elif kind == "sub":
                s = s + sub_ref[h]
            s_ref[u] = s
            mblk_ref[u] = jnp.max(s, axis=0, keepdims=True)

    def softmax_stage(row_bias):
        for u in range(len(units)):
            rb = row_bias(u)
            m_old = m_ref[u]
            m_blk = mblk_ref[u] if rb is None else mblk_ref[u] + rb
            m_new = jnp.maximum(m_old, m_blk)
            offset = m_new if rb is None else m_new - rb
            p_ref[u] = jnp.exp2(s_ref[u] - offset).astype(BF16)
            alpha_ref[u] = jnp.exp2(m_old - m_new)
            m_ref[u] = m_new

    ones_rows = jnp.ones((BF16_SUBLANES, nq), BF16)

    def values_stage(j):
        for u, (r, h) in enumerate(units):
            rows = slice(u * HEAD_DIM, (u + 1) * HEAD_DIM)
            v_h = vt_ref[r, 0, j, h * HEAD_DIM:(h + 1) * HEAD_DIM, :]
            pv = _dot(jnp.concatenate([v_h, ones_rows], axis=0), p_ref[u])
            alpha = alpha_ref[u]
            at_ref[rows] = alpha * at_ref[rows] + pv[:HEAD_DIM]
            l_ref[u] = alpha * l_ref[u] + pv[HEAD_DIM:HEAD_DIM + 1]

    m_ref[...] = jnp.full(m_ref.shape, MASK_VALUE, F32)
    l_ref[...] = jnp.zeros(l_ref.shape, F32)
    at_ref[...] = jnp.zeros(at_ref.shape, F32)
    j_sub = jnp.maximum(qi - 1, 0)
    n_far = j_sub

    scores_stage(qi, "diag")
    softmax_stage(lambda u: None)
    scores_stage(j_sub, "sub")
    values_stage(qi)
    softmax_stage(lambda u: rb_sub_ref[u * nblk + j_sub])
    scores_stage(0, "far")

    def far_body(t, carry):
        values_stage(jnp.where(t == 0, j_sub, t - 1))
        softmax_stage(lambda u: rb_far_ref[u * nblk + t])
        scores_stage(jnp.minimum(t + 1, jnp.maximum(n_far - 1, 0)), "far")
        return carry

    lax.fori_loop(0, n_far, far_body, 0)
    values_stage(jnp.where(n_far > 0, n_far - 1, j_sub))
    for u in range(len(units)):
        rows = slice(u * HEAD_DIM, (u + 1) * HEAD_DIM)
        at_ref[rows] = at_ref[rows] * (1.0 / l_ref[u])
    for r in range(ATTN_ROWS):
        a_ref[r, 0] = at_ref[r * ATTN_WIDTH:(r + 1) * ATTN_WIDTH, :].T.astype(BF16)


def _moba_attn(qt, k, vt, diag, sub, t31):
    B, S, _ = k.shape
    nblk = S // MOBA_BLOCK
    groups = B // ATTN_ROWS
    n_units = ATTN_ROWS * ATTN_HEADS
    split = lambda x: x.reshape((ATTN_ROWS, groups) + x.shape[1:])
    tile = pl.BlockSpec((ATTN_HEADS, MOBA_BLOCK, MOBA_BLOCK), lambda g, i: (0, 0, 0), pipeline_mode=pl.Buffered(1))
    a = pl.pallas_call(
        _moba_attn_kernel,
        grid=(groups, nblk),
        in_specs=[pl.BlockSpec(memory_space=pltpu.SMEM),
                  pl.BlockSpec((ATTN_ROWS, 1, ATTN_WIDTH, MOBA_BLOCK), lambda g, i: (0, g, 0, i)),
                  pl.BlockSpec((ATTN_ROWS, 1, S, ATTN_WIDTH), lambda g, i: (0, g, 0, 0)),
                  pl.BlockSpec((ATTN_ROWS, 1, nblk, ATTN_WIDTH, MOBA_BLOCK), lambda g, i: (0, g, 0, 0, 0)),
                  tile, tile],
        out_specs=pl.BlockSpec((ATTN_ROWS, 1, MOBA_BLOCK, ATTN_WIDTH), lambda g, i: (0, g, i, 0)),
        out_shape=jax.ShapeDtypeStruct((ATTN_ROWS, groups, S, ATTN_WIDTH), BF16),
        scratch_shapes=[pltpu.VMEM((ATTN_ROWS, ATTN_HEADS * nblk, ATTN_WIDTH), BF16),
                        pltpu.VMEM((ATTN_ROWS, ATTN_HEADS * nblk, ATTN_WIDTH), BF16),
                        pltpu.VMEM((n_units * nblk, 1, MOBA_BLOCK), F32),
                        pltpu.VMEM((n_units * nblk, 1, MOBA_BLOCK), F32),
                        pltpu.VMEM((n_units * HEAD_DIM, MOBA_BLOCK), F32),
                        pltpu.VMEM((n_units, 2 * HEAD_DIM, MOBA_BLOCK), BF16),
                        *[pltpu.VMEM((n_units, 1, MOBA_BLOCK), F32)] * 4,
                        pltpu.VMEM((n_units, MOBA_BLOCK, MOBA_BLOCK), F32),
                        pltpu.VMEM((n_units, MOBA_BLOCK, MOBA_BLOCK), BF16)],
        compiler_params=pltpu.CompilerParams(
            dimension_semantics=("arbitrary", "arbitrary"), vmem_limit_bytes=V7X_VMEM_LIMIT_BYTES),
        name="moba_attn",
    )(t31, split(qt), split(k), split(vt), diag, sub)
    return a.reshape(B, S, ATTN_WIDTH)


def _mix_merge_kernel(h1_ref, un_ref, unprev_ref, a_ref, win_ref, wgrp_ref, scale_ref,
                      wa_ref, wp_ref, wout_ref, h2_ref, zext_ref):
    i = pl.program_id(1)
    tm = MIX_TOKENS
    z0 = 3 * ATTN_WIDTH
    for c in range(MIX_SUBTILES):
        rows = slice(c * tm, (c + 1) * tm)
        zx_ref = zext_ref.at[c]
        zg = _dot(un_ref[0, rows], win_ref[:, z0:])
        halo = unprev_ref[0] if c == 0 else un_ref[0, c * tm - POOL_HALO:c * tm]
        zprev = _dot(halo, win_ref[:, z0:z0 + POOL_WIDTH])
        zx_ref[:POOL_HALO] = jnp.where(i > 0, zprev, 0.0) if c == 0 else zprev
        zx_ref[POOL_HALO:] = zg[:, :POOL_WIDTH]

        tpos = (i * MIX_SUBTILES + c) * tm + lax.broadcasted_iota(jnp.int32, (tm, POOL_GROUP_WIDTH), 0)
        mixed = []
        for g, w in enumerate(POOL_WINDOWS):
            cols = slice(g * POOL_GROUP_WIDTH, (g + 1) * POOL_GROUP_WIDTH)
            wsum = zx_ref[POOL_HALO:, cols]
            for lag in range(1, w):
                wsum = wsum + zx_ref[POOL_HALO - lag:POOL_HALO - lag + tm, cols]
            mean = wsum / jnp.minimum(tpos + 1, w).astype(F32)
            pooled = (mean - zx_ref[POOL_HALO:, cols]).astype(BF16)
            mixed.append(_dot(pooled, wgrp_ref[g]))
        p = (jnp.concatenate(mixed, axis=1) * scale_ref[...]).astype(BF16)

        g_attn = zg[:, POOL_WIDTH:POOL_WIDTH + D_MODEL]
        g_pool = zg[:, POOL_WIDTH + D_MODEL:]
        merged = (jax.nn.sigmoid(g_attn) * _dot(a_ref[0, rows], wa_ref[...])
                  + jax.nn.sigmoid(g_pool) * _dot(p, wp_ref[...]))
        h2_ref[0, rows] = h1_ref[0, rows] + _dot(merged.astype(BF16), wout_ref[...])


def _mix_merge(h1, un, a, win, wgrp, scale, wa, wp, wout):
    B, S, D = h1.shape
    tm = MIX_SUBTILES * MIX_TOKENS
    halo_per_tile = tm // POOL_HALO
    return pl.pallas_call(
        _mix_merge_kernel,
        grid=(B, S // tm),
        in_specs=[pl.BlockSpec((1, tm, D), lambda b, i: (b, i, 0)),
                  pl.BlockSpec((1, tm, D), lambda b, i: (b, i, 0)),
                  pl.BlockSpec((1, POOL_HALO, D), lambda b, i: (b, jnp.maximum(i * halo_per_tile - 1, 0), 0)),
                  pl.BlockSpec((1, tm, ATTN_WIDTH), lambda b, i: (b, i, 0)),
                  _resident(win.shape), _resident(wgrp.shape), _resident(scale.shape),
                  _resident(wa.shape), _resident(wp.shape), _resident(wout.shape)],
        out_specs=pl.BlockSpec((1, tm, D), lambda b, i: (b, i, 0)),
        out_shape=jax.ShapeDtypeStruct((B, S, D), F32),
        scratch_shapes=[pltpu.VMEM((MIX_SUBTILES, POOL_HALO + MIX_TOKENS, POOL_WIDTH), F32)],
        compiler_params=pltpu.CompilerParams(
            dimension_semantics=("arbitrary", "arbitrary"), vmem_limit_bytes=V7X_VMEM_LIMIT_BYTES),
        name="mix_merge",
    )(h1, un, un, a, win, wgrp, scale, wa, wp, wout)


def _ffn2_final_kernel(h_ref, g2_ref, wg_ref, wu_ref, wd_ref, gf_ref, out_ref):
    for c in range(FFN2_SUBTILES):
        rows = slice(c * FFN_TOKENS, (c + 1) * FFN_TOKENS)
        h3 = _swiglu_half_step(h_ref[0, rows], g2_ref, wg_ref, wu_ref, wd_ref)
        out_ref[0, rows] = _rms(h3, gf_ref[...])


def _ffn2_final(h, g2, wg, wu, wd, gf):
    B, S, D = h.shape
    tm = FFN2_SUBTILES * FFN_TOKENS
    return pl.pallas_call(
        _ffn2_final_kernel,
        grid=(B, S // tm),
        in_specs=[pl.BlockSpec((1, tm, D), lambda b, i: (b, i, 0)),
                  _resident(g2.shape), _resident(wg.shape), _resident(wu.shape), _resident(wd.shape),
                  _resident(gf.shape)],
        out_specs=pl.BlockSpec((1, tm, D), lambda b, i: (b, i, 0)),
        out_shape=jax.ShapeDtypeStruct((B, S, D), F32),
        compiler_params=pltpu.CompilerParams(
            dimension_semantics=("arbitrary", "arbitrary"), vmem_limit_bytes=V7X_VMEM_LIMIT_BYTES),
        name="ffn2_final",
    )(h, g2, wg, wu, wd, gf)


def kernel(x, ffn1_norm, ffn1_w_gate, ffn1_w_up, ffn1_w_down, mix_norm, w_in, pool_w_group, pool_scale,
           w_branch_attn, w_branch_pool, w_out, ffn2_norm, ffn2_w_gate, ffn2_w_up, ffn2_w_down,
           rpb_table, final_norm):
    B, S, D = x.shape
    assert (D, ffn1_w_gate.shape[0]) == (D_MODEL, 1)
    assert S % MOBA_BLOCK == 0 and S % (FFN2_SUBTILES * FFN_TOKENS) == 0 and S % (MIX_SUBTILES * MIX_TOKENS) == 0
    assert B % ATTN_ROWS == 0
    bf = lambda w: w.astype(BF16)
    row = lambda v: v.reshape(1, -1)
    w_in = w_in[0]
    wqkv = bf(w_in[:, :3 * ATTN_WIDTH])

    diag, sub = _rpb_bias(rpb_table)
    (h1, un, qt, k, vt), later = _ffn1_qkv(
        x, row(ffn1_norm[0]), bf(ffn1_w_gate[0]), bf(ffn1_w_up[0]), bf(ffn1_w_down[0]), row(mix_norm[0]), wqkv,
        [w_in, w_branch_attn[0], w_branch_pool[0], w_out[0], ffn2_w_gate[0], ffn2_w_up[0], ffn2_w_down[0]])
    win_bf, wa_bf, wp_bf, wout_bf, wg2_bf, wu2_bf, wd2_bf = later
    a = _moba_attn(qt, k, vt, diag, sub, rpb_table[RPB_BUCKETS - 1])
    h2 = _mix_merge(h1, un, a, win_bf, bf(pool_w_group[0]), row(pool_scale[0]), wa_bf, wp_bf, wout_bf)
    return _ffn2_final(h2, row(ffn2_norm[0]), wg2_bf, wu2_bf, wd2_bf, row(final_norm))
```

```python
import math

import jax
import jax.numpy as jnp
from jax import lax
from jax.experimental import pallas as pl
from jax.experimental.pallas import tpu as pltpu

D_MODEL = 1024
HEAD_DIM = 64
ATTN_WIDTH = 512
ATTN_HEADS = 8
MOBA_BLOCK = 256
MOBA_TOPK = 3
POOL_WINDOWS = (2, 4, 8, 16)
POOL_WIDTH = 512
POOL_GROUP_WIDTH = 128
POOL_HALO = 16
D_FF = 2816
RPB_BUCKETS = 32
RPB_MAX_DISTANCE = 128
RMS_EPS = 1e-6
MASK_VALUE = -1e30
LOG2E = math.log2(math.e)
Q_SCALE = HEAD_DIM ** -0.5 * LOG2E
ATTN_ROWS = 2
V7X_VMEM_LIMIT_BYTES = 56 * 1024 * 1024

FFN_TOKENS = 512
FFN2_SUBTILES = 2
MIX_TOKENS = 512
MIX_SUBTILES = 2
N_LATER_WEIGHTS = 7
BF16_SUBLANES = 16

BF16 = jnp.bfloat16
F32 = jnp.float32


def _resident(shape):
    nd = len(shape)
    return pl.BlockSpec(shape, lambda *_: (0,) * nd, pipeline_mode=pl.Buffered(1))


def _dot(a, b):
    return jnp.dot(a, b, preferred_element_type=F32)


def _rms(x, g):
    return x * lax.rsqrt(jnp.mean(x * x, axis=-1, keepdims=True) + RMS_EPS) * g


def _swiglu_half_step(x, g_ref, wg_ref, wu_ref, wd_ref):
    xn = _rms(x, g_ref[...]).astype(BF16)
    gate = _dot(xn, wg_ref[...])
    up = _dot(xn, wu_ref[...])
    act = (gate * jax.nn.sigmoid(gate) * up).astype(BF16)
    return x + 0.5 * _dot(act, wd_ref[...])


def _rpb_bucket(dist):
    n = jnp.maximum(dist, 0)
    max_exact = RPB_BUCKETS // 2
    nf = jnp.maximum(n, 1).astype(F32)
    large = max_exact + (jnp.log(nf / max_exact) / math.log(RPB_MAX_DISTANCE / max_exact)
                         * (RPB_BUCKETS - max_exact)).astype(jnp.int32)
    large = jnp.minimum(large, RPB_BUCKETS - 1)
    return jnp.where(n < max_exact, n, large)


def _rpb_bias_kernel(table_ref, diag_ref, sub_ref):
    h = pl.program_id(0)
    width = 4 * MOBA_BLOCK
    dist = lax.broadcasted_iota(jnp.int32, (8, width), 1) - MOBA_BLOCK
    bucket = _rpb_bucket(dist)
    f = jnp.zeros((8, width), F32)
    for b in range(RPB_BUCKETS):
        f = jnp.where(bucket == b, table_ref[b, h], f)
    rows = jnp.broadcast_to(f[0:1] * LOG2E, (MOBA_BLOCK, width))
    toeplitz = pltpu.roll(rows, 0, 1, stride=1, stride_axis=0)
    kl = lax.broadcasted_iota(jnp.int32, (MOBA_BLOCK, MOBA_BLOCK), 0)
    ql = lax.broadcasted_iota(jnp.int32, (MOBA_BLOCK, MOBA_BLOCK), 1)
    diag_ref[0] = jnp.where(ql >= kl, toeplitz[:, MOBA_BLOCK:2 * MOBA_BLOCK], MASK_VALUE)
    sub_ref[0] = toeplitz[:, 2 * MOBA_BLOCK:3 * MOBA_BLOCK]


def _rpb_bias(rpb_table):
    tile = jax.ShapeDtypeStruct((ATTN_HEADS, MOBA_BLOCK, MOBA_BLOCK), F32)
    spec = pl.BlockSpec((1, MOBA_BLOCK, MOBA_BLOCK), lambda h: (h, 0, 0))
    return pl.pallas_call(
        _rpb_bias_kernel,
        grid=(ATTN_HEADS,),
        in_specs=[pl.BlockSpec(memory_space=pltpu.SMEM)],
        out_specs=[spec, spec],
        out_shape=[tile, tile],
        name="rpb_bias",
    )(rpb_table)


def _ffn1_qkv_kernel(x_ref, g1_ref, wg_ref, wu_ref, wd_ref, gm_ref, wqkv_ref, *refs):
    later_f32, (h1_ref, un_ref, qt_ref, k_ref, vt_ref), later_bf16 = (
        refs[:N_LATER_WEIGHTS], refs[N_LATER_WEIGHTS:N_LATER_WEIGHTS + 5], refs[N_LATER_WEIGHTS + 5:])
    for src_ref, dst_ref in zip(later_f32, later_bf16):
        dst_ref[...] = src_ref[...].astype(BF16)
    h1 = _swiglu_half_step(x_ref[0], g1_ref, wg_ref, wu_ref, wd_ref)
    h1_ref[0] = h1
    un = _rms(h1, gm_ref[...]).astype(BF16)
    un_ref[0] = un
    qkv = _dot(un, wqkv_ref[...])
    qt_ref[0] = (qkv[:, :ATTN_WIDTH] * Q_SCALE).T.astype(BF16)
    k_ref[0] = qkv[:, ATTN_WIDTH:2 * ATTN_WIDTH].astype(BF16)
    for c in range(FFN_TOKENS // MOBA_BLOCK):
        vt_ref[0, c] = qkv[c * MOBA_BLOCK:(c + 1) * MOBA_BLOCK, 2 * ATTN_WIDTH:].T.astype(BF16)


def _ffn1_qkv(x, g1, wg, wu, wd, gm, wqkv, later_weights):
    B, S, D = x.shape
    tm = FFN_TOKENS
    nblk = S // MOBA_BLOCK
    tiles_per_seq = S // tm
    steps = B * tiles_per_seq
    assert len(later_weights) == N_LATER_WEIGHTS
    step_of = lambda b, i: b * tiles_per_seq + i

    def slab_view(w):
        n = max(n for n in range(1, steps + 1) if steps % n == 0 and w.shape[0] % (n * BF16_SUBLANES) == 0)
        view = w.reshape(n, w.shape[0] // n, w.shape[1])
        return view, pl.BlockSpec((1,) + view.shape[1:], lambda b, i: (step_of(b, i) // (steps // n), 0, 0))

    slabs, slab_specs = zip(*[slab_view(w) for w in later_weights])
    tok = lambda width, dt: (pl.BlockSpec((1, tm, width), lambda b, i: (b, i, 0)),
                             jax.ShapeDtypeStruct((B, S, width), dt))
    h1_spec, h1_shape = tok(D, F32)
    un_spec, un_shape = tok(D, BF16)
    k_spec, k_shape = tok(ATTN_WIDTH, BF16)
    outs = pl.pallas_call(
        _ffn1_qkv_kernel,
        grid=(B, tiles_per_seq),
        in_specs=[pl.BlockSpec((1, tm, D), lambda b, i: (b, i, 0)),
                  _resident(g1.shape), _resident(wg.shape), _resident(wu.shape), _resident(wd.shape),
                  _resident(gm.shape), _resident(wqkv.shape), *slab_specs],
        out_specs=[h1_spec, un_spec,
                   pl.BlockSpec((1, ATTN_WIDTH, tm), lambda b, i: (b, 0, i)),
                   k_spec,
                   pl.BlockSpec((1, tm // MOBA_BLOCK, ATTN_WIDTH, MOBA_BLOCK), lambda b, i: (b, i, 0, 0)),
                   *slab_specs],
        out_shape=[h1_shape, un_shape,
                   jax.ShapeDtypeStruct((B, ATTN_WIDTH, S), BF16),
                   k_shape,
                   jax.ShapeDtypeStruct((B, nblk, ATTN_WIDTH, MOBA_BLOCK), BF16)]
                  + [jax.ShapeDtypeStruct(w.shape, BF16) for w in slabs],
        compiler_params=pltpu.CompilerParams(
            dimension_semantics=("arbitrary", "arbitrary"), vmem_limit_bytes=V7X_VMEM_LIMIT_BYTES),
        name="ffn1_qkv",
    )(x, g1, wg, wu, wd, gm, wqkv, *slabs)
    return outs[:5], [o.reshape(w.shape) for o, w in zip(outs[5:], later_weights)]


def _moba_attn_kernel(t31_ref, qt_ref, k_ref, vt_ref, diag_ref, sub_ref, a_ref,
                      kmhi_ref, kmlo_ref, rb_far_ref, rb_sub_ref, at_ref, qz_ref,
                      m_ref, l_ref, mblk_ref, alpha_ref, s_ref, p_ref):
    qi = pl.program_id(1)
    nblk = k_ref.shape[2] // MOBA_BLOCK
    nq = MOBA_BLOCK
    units = [(r, h) for r in range(ATTN_ROWS) for h in range(ATTN_HEADS)]

    @pl.when(qi == 0)
    def _():
        for r in range(ATTN_ROWS):
            blk_row = lax.broadcasted_iota(jnp.int32, (nblk, ATTN_WIDTH), 0)
            km = jnp.zeros((nblk, ATTN_WIDTH), F32)
            for j in range(nblk):
                mean_j = jnp.mean(k_ref[r, 0, j * MOBA_BLOCK:(j + 1) * MOBA_BLOCK, :].astype(F32), axis=0,
                                  keepdims=True)
                km = jnp.where(blk_row == j, mean_j, km)
            km = jnp.concatenate([km] * ATTN_HEADS, axis=0)
            row_head = lax.broadcasted_iota(jnp.int32, km.shape, 0) // nblk
            col_head = lax.broadcasted_iota(jnp.int32, km.shape, 1) // HEAD_DIM
            km = jnp.where(row_head == col_head, km, 0.0)
            hi = km.astype(BF16)
            kmhi_ref[r] = hi
            kmlo_ref[r] = (km - hi.astype(F32)).astype(BF16)

    jrow = lax.broadcasted_iota(jnp.int32, (nblk, nq), 0)
    for r in range(ATTN_ROWS):
        qt = qt_ref[r, 0]
        gates = _dot(kmhi_ref[r], qt) + _dot(kmlo_ref[r], qt)
        for h in range(ATTN_HEADS):
            u = r * ATTN_HEADS + h
            g = gates[h * nblk:(h + 1) * nblk]
            cnt = jnp.zeros((nblk, nq), jnp.int32)
            for jp in range(nblk):
                gb = jnp.broadcast_to(g[jp:jp + 1], (nblk, nq))
                beats = (gb > g) | ((gb == g) & (jp < jrow))
                cnt = cnt + jnp.where(beats, jnp.where(jp < qi, 1, 0), 0)
            sel = (jrow < qi) & (cnt < MOBA_TOPK)
            far = jnp.where(sel, t31_ref[h] * LOG2E, MASK_VALUE)
            sub = jnp.where(sel, 0.0, MASK_VALUE)
            for j in range(nblk):
                rb_far_ref[u * nblk + j] = far[j:j + 1]
                rb_sub_ref[u * nblk + j] = sub[j:j + 1]

    zeros_half = jnp.zeros((HEAD_DIM, nq), BF16)
    for r, h in units:
        qh = qt_ref[r, 0, h * HEAD_DIM:(h + 1) * HEAD_DIM, :]
        qz_ref[r * ATTN_HEADS + h] = jnp.concatenate([qh, zeros_half] if h % 2 == 0 else [zeros_half, qh], axis=0)

    def scores_stage(j, kind):
        start = pl.multiple_of(j * MOBA_BLOCK, MOBA_BLOCK)
        for u, (r, h) in enumerate(units):
            pair = h // 2
            kblk = k_ref[r, 0, pl.ds(start, MOBA_BLOCK), pair * 128:(pair + 1) * 128]
            s = _dot(kblk, qz_ref[u])
            if kind == "diag":
                s = s + diag_ref[h]
            elif kind == "sub":
                s = s + sub_ref[h]
            s_ref[u] = s
            mblk_ref[u] = jnp.max(s, axis=0, keepdims=True)

    def softmax_stage(row_bias):
        for u in range(len(units)):
            rb = row_bias(u)
            m_old = m_ref[u]
            m_blk = mblk_ref[u] if rb is None else mblk_ref[u] + rb
            m_new = jnp.maximum(m_old, m_blk)
            offset = m_new if rb is None else m_new - rb
            p_ref[u] = jnp.exp2(s_ref[u] - offset).astype(BF16)
            alpha_ref[u] = jnp.exp2(m_old - m_new)
            m_ref[u] = m_new

    ones_rows = jnp.ones((BF16_SUBLANES, nq), BF16)

    def values_stage(j):
        for u, (r, h) in enumerate(units):
            rows = slice(u * HEAD_DIM, (u + 1) * HEAD_DIM)
            v_h = vt_ref[r, 0, j, h * HEAD_DIM:(h + 1) * HEAD_DIM, :]
            pv = _dot(jnp.concatenate([v_h, ones_rows], axis=0), p_ref[u])
            alpha = alpha_ref[u]
            at_ref[rows] = alpha * at_ref[rows] + pv[:HEAD_DIM]
            l_ref[u] = alpha * l_ref[u] + pv[HEAD_DIM:HEAD_DIM + 1]

    m_ref[...] = jnp.full(m_ref.shape, MASK_VALUE, F32)
    l_ref[...] = jnp.zeros(l_ref.shape, F32)
    at_ref[...] = jnp.zeros(at_ref.shape, F32)
    j_sub = jnp.maximum(qi - 1, 0)
    n_far = j_sub

    scores_stage(qi, "diag")
    softmax_stage(lambda u: None)
    scores_stage(j_sub, "sub")
    values_stage(qi)
    softmax_stage(lambda u: rb_sub_ref[u * nblk + j_sub])
    scores_stage(0, "far")

    def far_body(t, carry):
        values_stage(jnp.where(t == 0, j_sub, t - 1))
        softmax_stage(lambda u: rb_far_ref[u * nblk + t])
        scores_stage(t + 1, "far")
        return carry

    lax.fori_loop(0, n_far - 1, far_body, 0)

    @pl.when(n_far > 0)
    def _():
        values_stage(jnp.where(n_far == 1, j_sub, n_far - 2))
        softmax_stage(lambda u: rb_far_ref[u * nblk + n_far - 1])

    values_stage(jnp.where(n_far > 0, n_far - 1, j_sub))
    for u in range(len(units)):
        rows = slice(u * HEAD_DIM, (u + 1) * HEAD_DIM)
        at_ref[rows] = at_ref[rows] * (1.0 / l_ref[u])
    for r in range(ATTN_ROWS):
        a_ref[r, 0] = at_ref[r * ATTN_WIDTH:(r + 1) * ATTN_WIDTH, :].T.astype(BF16)


def _moba_attn(qt, k, vt, diag, sub, t31):
    B, S, _ = k.shape
    nblk = S // MOBA_BLOCK
    groups = B // ATTN_ROWS
    n_units = ATTN_ROWS * ATTN_HEADS
    split = lambda x: x.reshape((ATTN_ROWS, groups) + x.shape[1:])
    tile = pl.BlockSpec((ATTN_HEADS, MOBA_BLOCK, MOBA_BLOCK), lambda g, i: (0, 0, 0), pipeline_mode=pl.Buffered(1))
    a = pl.pallas_call(
        _moba_attn_kernel,
        grid=(groups, nblk),
        in_specs=[pl.BlockSpec(memory_space=pltpu.SMEM),
                  pl.BlockSpec((ATTN_ROWS, 1, ATTN_WIDTH, MOBA_BLOCK), lambda g, i: (0, g, 0, i)),
                  pl.BlockSpec((ATTN_ROWS, 1, S, ATTN_WIDTH), lambda g, i: (0, g, 0, 0)),
                  pl.BlockSpec((ATTN_ROWS, 1, nblk, ATTN_WIDTH, MOBA_BLOCK), lambda g, i: (0, g, 0, 0, 0)),
                  tile, tile],
        out_specs=pl.BlockSpec((ATTN_ROWS, 1, MOBA_BLOCK, ATTN_WIDTH), lambda g, i: (0, g, i, 0)),
        out_shape=jax.ShapeDtypeStruct((ATTN_ROWS, groups, S, ATTN_WIDTH), BF16),
        scratch_shapes=[pltpu.VMEM((ATTN_ROWS, ATTN_HEADS * nblk, ATTN_WIDTH), BF16),
                        pltpu.VMEM((ATTN_ROWS, ATTN_HEADS * nblk, ATTN_WIDTH), BF16),
                        pltpu.VMEM((n_units * nblk, 1, MOBA_BLOCK), F32),
                        pltpu.VMEM((n_units * nblk, 1, MOBA_BLOCK), F32),
                        pltpu.VMEM((n_units * HEAD_DIM, MOBA_BLOCK), F32),
                        pltpu.VMEM((n_units, 2 * HEAD_DIM, MOBA_BLOCK), BF16),
                        *[pltpu.VMEM((n_units, 1, MOBA_BLOCK), F32)] * 4,
                        pltpu.VMEM((n_units, MOBA_BLOCK, MOBA_BLOCK), F32),
                        pltpu.VMEM((n_units, MOBA_BLOCK, MOBA_BLOCK), BF16)],
        compiler_params=pltpu.CompilerParams(
            dimension_semantics=("arbitrary", "arbitrary"), vmem_limit_bytes=V7X_VMEM_LIMIT_BYTES),
        name="moba_attn",
    )(t31, split(qt), split(k), split(vt), diag, sub)
    return a.reshape(B, S, ATTN_WIDTH)


def _mix_merge_kernel(h1_ref, un_ref, unprev_ref, a_ref, win_ref, wgrp_ref, scale_ref,
                      wa_ref, wp_ref, wout_ref, h2_ref, zext_ref):
    i = pl.program_id(1)
    tm = MIX_TOKENS
    z0 = 3 * ATTN_WIDTH
    for c in range(MIX_SUBTILES):
        rows = slice(c * tm, (c + 1) * tm)
        zx_ref = zext_ref.at[c]
        zg = _dot(un_ref[0, rows], win_ref[:, z0:])
        halo = unprev_ref[0] if c == 0 else un_ref[0, c * tm - POOL_HALO:c * tm]
        zprev = _dot(halo, win_ref[:, z0:z0 + POOL_WIDTH])
        zx_ref[:POOL_HALO] = jnp.where(i > 0, zprev, 0.0) if c == 0 else zprev
        zx_ref[POOL_HALO:] = zg[:, :POOL_WIDTH]

        tpos = (i * MIX_SUBTILES + c) * tm + lax.broadcasted_iota(jnp.int32, (tm, POOL_GROUP_WIDTH), 0)
        mixed = []
        for g, w in enumerate(POOL_WINDOWS):
            cols = slice(g * POOL_GROUP_WIDTH, (g + 1) * POOL_GROUP_WIDTH)
            wsum = zx_ref[POOL_HALO:, cols]
            for lag in range(1, w):
                wsum = wsum + zx_ref[POOL_HALO - lag:POOL_HALO - lag + tm, cols]
            mean = wsum / jnp.minimum(tpos + 1, w).astype(F32)
            pooled = (mean - zx_ref[POOL_HALO:, cols]).astype(BF16)
            mixed.append(_dot(pooled, wgrp_ref[g]))
        p = (jnp.concatenate(mixed, axis=1) * scale_ref[...]).astype(BF16)

        g_attn = zg[:, POOL_WIDTH:POOL_WIDTH + D_MODEL]
        g_pool = zg[:, POOL_WIDTH + D_MODEL:]
        merged = (jax.nn.sigmoid(g_attn) * _dot(a_ref[0, rows], wa_ref[...])
                  + jax.nn.sigmoid(g_pool) * _dot(p, wp_ref[...]))
        h2_ref[0, rows] = h1_ref[0, rows] + _dot(merged.astype(BF16), wout_ref[...])


def _mix_merge(h1, un, a, win, wgrp, scale, wa, wp, wout):
    B, S, D = h1.shape
    tm = MIX_SUBTILES * MIX_TOKENS
    halo_per_tile = tm // POOL_HALO
    return pl.pallas_call(
        _mix_merge_kernel,
        grid=(B, S // tm),
        in_specs=[pl.BlockSpec((1, tm, D), lambda b, i: (b, i, 0)),
                  pl.BlockSpec((1, tm, D), lambda b, i: (b, i, 0)),
                  pl.BlockSpec((1, POOL_HALO, D), lambda b, i: (b, jnp.maximum(i * halo_per_tile - 1, 0), 0)),
                  pl.BlockSpec((1, tm, ATTN_WIDTH), lambda b, i: (b, i, 0)),
                  _resident(win.shape), _resident(wgrp.shape), _resident(scale.shape),
                  _resident(wa.shape), _resident(wp.shape), _resident(wout.shape)],
        out_specs=pl.BlockSpec((1, tm, D), lambda b, i: (b, i, 0)),
        out_shape=jax.ShapeDtypeStruct((B, S, D), F32),
        scratch_shapes=[pltpu.VMEM((MIX_SUBTILES, POOL_HALO + MIX_TOKENS, POOL_WIDTH), F32)],
        compiler_params=pltpu.CompilerParams(
            dimension_semantics=("arbitrary", "arbitrary"), vmem_limit_bytes=V7X_VMEM_LIMIT_BYTES),
        name="mix_merge",
    )(h1, un, un, a, win, wgrp, scale, wa, wp, wout)


def _ffn2_final_kernel(h_ref, g2_ref, wg_ref, wu_ref, wd_ref, gf_ref, out_ref):
    for c in range(FFN2_SUBTILES):
        rows = slice(c * FFN_TOKENS, (c + 1) * FFN_TOKENS)
        h3 = _swiglu_half_step(h_ref[0, rows], g2_ref, wg_ref, wu_ref, wd_ref)
        out_ref[0, rows] = _rms(h3, gf_ref[...])


def _ffn2_final(h, g2, wg, wu, wd, gf):
    B, S, D = h.shape
    tm = FFN2_SUBTILES * FFN_TOKENS
    return pl.pallas_call(
        _ffn2_final_kernel,
        grid=(B, S // tm),
        in_specs=[pl.BlockSpec((1, tm, D), lambda b, i: (b, i, 0)),
                  _resident(g2.shape), _resident(wg.shape), _resident(wu.shape), _resident(wd.shape),
                  _resident(gf.shape)],
        out_specs=pl.BlockSpec((1, tm, D), lambda b, i: (b, i, 0)),
        out_shape=jax.ShapeDtypeStruct((B, S, D), F32),
        compiler_params=pltpu.CompilerParams(
            dimension_semantics=("arbitrary", "arbitrary"), vmem_limit_bytes=V7X_VMEM_LIMIT_BYTES),
        name="ffn2_final",
    )(h, g2, wg, wu, wd, gf)


def kernel(x, ffn1_norm, ffn1_w_gate, ffn1_w_up, ffn1_w_down, mix_norm, w_in, pool_w_group, pool_scale,
           w_branch_attn, w_branch_pool, w_out, ffn2_norm, ffn2_w_gate, ffn2_w_up, ffn2_w_down,
           rpb_table, final_norm):
    B, S, D = x.shape
    assert (D, ffn1_w_gate.shape[0]) == (D_MODEL, 1)
    assert S % MOBA_BLOCK == 0 and S % (FFN2_SUBTILES * FFN_TOKENS) == 0 and S % (MIX_SUBTILES * MIX_TOKENS) == 0
    assert B % ATTN_ROWS == 0
    bf = lambda w: w.astype(BF16)
    row = lambda v: v.reshape(1, -1)
    w_in = w_in[0]
    wqkv = bf(w_in[:, :3 * ATTN_WIDTH])

    diag, sub = _rpb_bias(rpb_table)
    (h1, un, qt, k, vt), later = _ffn1_qkv(
        x, row(ffn1_norm[0]), bf(ffn1_w_gate[0]), bf(ffn1_w_up[0]), bf(ffn1_w_down[0]), row(mix_norm[0]), wqkv,
        [w_in, w_branch_attn[0], w_branch_pool[0], w_out[0], ffn2_w_gate[0], ffn2_w_up[0], ffn2_w_down[0]])
    win_bf, wa_bf, wp_bf, wout_bf, wg2_bf, wu2_bf, wd2_bf = later
    a = _moba_attn(qt, k, vt, diag, sub, rpb_table[RPB_BUCKETS - 1])
    h2 = _mix_merge(h1, un, a, win_bf, bf(pool_w_group[0]), row(pool_scale[0]), wa_bf, wp_bf, wout_bf)
    return _ffn2_final(h2, row(ffn2_norm[0]), wg2_bf, wu2_bf, wd2_bf, row(final_norm))
```

```python
import math

import jax
import jax.numpy as jnp
from jax import lax
from jax.experimental import pallas as pl
from jax.experimental.pallas import tpu as pltpu

D_MODEL = 1024
HEAD_DIM = 64
ATTN_WIDTH = 512
ATTN_HEADS = 8
MOBA_BLOCK = 256
MOBA_TOPK = 3
POOL_WINDOWS = (2, 4, 8, 16)
POOL_WIDTH = 512
POOL_GROUP_WIDTH = 128
POOL_HALO = 16
D_FF = 2816
RPB_BUCKETS = 32
RPB_MAX_DISTANCE = 128
RMS_EPS = 1e-6
MASK_VALUE = -1e30
LOG2E = math.log2(math.e)
Q_SCALE = HEAD_DIM ** -0.5 * LOG2E
ATTN_ROWS = 2
V7X_VMEM_LIMIT_BYTES = 56 * 1024 * 1024

FFN_TOKENS = 512
FFN2_SUBTILES = 2
MIX_TOKENS = 512
MIX_SUBTILES = 2
N_LATER_WEIGHTS = 7
LATER_GAIN_OPERAND = (0, None, None, None, 1, 1, None)
BF16_SUBLANES = 16

BF16 = jnp.bfloat16
F32 = jnp.float32


def _resident(shape):
    nd = len(shape)
    return pl.BlockSpec(shape, lambda *_: (0,) * nd, pipeline_mode=pl.Buffered(1))


def _dot(a, b):
    return jnp.dot(a, b, preferred_element_type=F32)


def _rms(x, g):
    return x * lax.rsqrt(jnp.mean(x * x, axis=-1, keepdims=True) + RMS_EPS) * g


def _inv_rms(x):
    return lax.rsqrt(jnp.mean(x * x, axis=-1, keepdims=True) + RMS_EPS)


def _norm_dot(x, w_ref, cols=None):
    w = w_ref[...] if cols is None else w_ref[:, cols]
    return _dot(x.astype(BF16), w) * _inv_rms(x)


def _swiglu_half_step(x, wg_ref, wu_ref, wd_ref):
    xb = x.astype(BF16)
    r = _inv_rms(x)
    gate = _dot(xb, wg_ref[...]) * r
    up = _dot(xb, wu_ref[...]) * r
    act = (gate * jax.nn.sigmoid(gate) * up).astype(BF16)
    return x + 0.5 * _dot(act, wd_ref[...])


def _rpb_bucket(dist):
    n = jnp.maximum(dist, 0)
    max_exact = RPB_BUCKETS // 2
    nf = jnp.maximum(n, 1).astype(F32)
    large = max_exact + (jnp.log(nf / max_exact) / math.log(RPB_MAX_DISTANCE / max_exact)
                         * (RPB_BUCKETS - max_exact)).astype(jnp.int32)
    large = jnp.minimum(large, RPB_BUCKETS - 1)
    return jnp.where(n < max_exact, n, large)


def _rpb_bias_kernel(table_ref, diag_ref, sub_ref):
    h = pl.program_id(0)
    width = 4 * MOBA_BLOCK
    dist = lax.broadcasted_iota(jnp.int32, (8, width), 1) - MOBA_BLOCK
    bucket = _rpb_bucket(dist)
    f = jnp.zeros((8, width), F32)
    for b in range(RPB_BUCKETS):
        f = jnp.where(bucket == b, table_ref[b, h], f)
    rows = jnp.broadcast_to(f[0:1] * LOG2E, (MOBA_BLOCK, width))
    toeplitz = pltpu.roll(rows, 0, 1, stride=1, stride_axis=0)
    kl = lax.broadcasted_iota(jnp.int32, (MOBA_BLOCK, MOBA_BLOCK), 0)
    ql = lax.broadcasted_iota(jnp.int32, (MOBA_BLOCK, MOBA_BLOCK), 1)
    diag_ref[0] = jnp.where(ql >= kl, toeplitz[:, MOBA_BLOCK:2 * MOBA_BLOCK], MASK_VALUE)
    sub_ref[0] = toeplitz[:, 2 * MOBA_BLOCK:3 * MOBA_BLOCK]


def _rpb_bias(rpb_table):
    tile = jax.ShapeDtypeStruct((ATTN_HEADS, MOBA_BLOCK, MOBA_BLOCK), F32)
    spec = pl.BlockSpec((1, MOBA_BLOCK, MOBA_BLOCK), lambda h: (h, 0, 0))
    return pl.pallas_call(
        _rpb_bias_kernel,
        grid=(ATTN_HEADS,),
        in_specs=[pl.BlockSpec(memory_space=pltpu.SMEM)],
        out_specs=[spec, spec],
        out_shape=[tile, tile],
        name="rpb_bias",
    )(rpb_table)


def _ffn1_qkv_kernel(x_ref, wg_ref, wu_ref, wd_ref, wqkv_ref, *refs):
    n, ng = N_LATER_WEIGHTS, len(set(LATER_GAIN_OPERAND) - {None})
    later_f32, gain_cols, (h1_ref, qt_ref, k_ref, vt_ref), later_bf16 = (
        refs[:n], refs[n:n + ng], refs[n + ng:n + ng + 4], refs[n + ng + 4:])
    for src_ref, dst_ref, gain in zip(later_f32, later_bf16, LATER_GAIN_OPERAND):
        w = src_ref[...] if gain is None else src_ref[...] * gain_cols[gain][...]
        dst_ref[...] = w.astype(BF16)
    h1 = _swiglu_half_step(x_ref[0], wg_ref, wu_ref, wd_ref)
    h1_ref[0] = h1
    qkv = _norm_dot(h1, wqkv_ref)
    qt_ref[0] = (qkv[:, :ATTN_WIDTH] * Q_SCALE).T.astype(BF16)
    k_ref[0] = qkv[:, ATTN_WIDTH:2 * ATTN_WIDTH].astype(BF16)
    for c in range(FFN_TOKENS // MOBA_BLOCK):
        vt_ref[0, c] = qkv[c * MOBA_BLOCK:(c + 1) * MOBA_BLOCK, 2 * ATTN_WIDTH:].T.astype(BF16)


def _ffn1_qkv(x, wg, wu, wd, wqkv, later_weights, later_gains):
    B, S, D = x.shape
    tm = FFN_TOKENS
    nblk = S // MOBA_BLOCK
    tiles_per_seq = S // tm
    steps = B * tiles_per_seq
    assert len(later_weights) == N_LATER_WEIGHTS
    step_of = lambda b, i: b * tiles_per_seq + i

    def slab_view(w):
        n = max(n for n in range(1, steps + 1) if steps % n == 0 and w.shape[0] % (n * BF16_SUBLANES) == 0)
        view = w.reshape(n, w.shape[0] // n, w.shape[1])
        return view, pl.BlockSpec((1,) + view.shape[1:], lambda b, i: (step_of(b, i) // (steps // n), 0, 0))

    slabs, slab_specs = zip(*[slab_view(w) for w in later_weights])
    gain_cols, gain_specs = zip(*[slab_view(g.reshape(-1, 1)) for g in later_gains])
    for w, gain in zip(slabs, LATER_GAIN_OPERAND):
        assert gain is None or gain_cols[gain].shape[:2] == w.shape[:2]
    tok = lambda width, dt: (pl.BlockSpec((1, tm, width), lambda b, i: (b, i, 0)),
                             jax.ShapeDtypeStruct((B, S, width), dt))
    h1_spec, h1_shape = tok(D, F32)
    k_spec, k_shape = tok(ATTN_WIDTH, BF16)
    outs = pl.pallas_call(
        _ffn1_qkv_kernel,
        grid=(B, tiles_per_seq),
        in_specs=[pl.BlockSpec((1, tm, D), lambda b, i: (b, i, 0)),
                  _resident(wg.shape), _resident(wu.shape), _resident(wd.shape), _resident(wqkv.shape),
                  *slab_specs, *gain_specs],
        out_specs=[h1_spec,
                   pl.BlockSpec((1, ATTN_WIDTH, tm), lambda b, i: (b, 0, i)),
                   k_spec,
                   pl.BlockSpec((1, tm // MOBA_BLOCK, ATTN_WIDTH, MOBA_BLOCK), lambda b, i: (b, i, 0, 0)),
                   *slab_specs],
        out_shape=[h1_shape,
                   jax.ShapeDtypeStruct((B, ATTN_WIDTH, S), BF16),
                   k_shape,
                   jax.ShapeDtypeStruct((B, nblk, ATTN_WIDTH, MOBA_BLOCK), BF16)]
                  + [jax.ShapeDtypeStruct(w.shape, BF16) for w in slabs],
        compiler_params=pltpu.CompilerParams(
            dimension_semantics=("arbitrary", "arbitrary"), vmem_limit_bytes=V7X_VMEM_LIMIT_BYTES),
        name="ffn1_qkv",
    )(x, wg, wu, wd, wqkv, *slabs, *gain_cols)
    return outs[:4], [o.reshape(w.shape) for o, w in zip(outs[4:], later_weights)]


def _moba_attn_kernel(t31_ref, qt_ref, k_ref, vt_ref, diag_ref, sub_ref, a_ref,
                      kmhi_ref, kmlo_ref, rb_far_ref, rb_sub_ref, at_ref, qz_ref,
                      m_ref, l_ref, mblk_ref, alpha_ref, s_ref, p_ref):
    qi = pl.program_id(1)
    nblk = k_ref.shape[2] // MOBA_BLOCK
    nq = MOBA_BLOCK
    units = [(r, h) for r in range(ATTN_ROWS) for h in range(ATTN_HEADS)]

    @pl.when(qi == 0)
    def _():
        for r in range(ATTN_ROWS):
            blk_row = lax.broadcasted_iota(jnp.int32, (nblk, ATTN_WIDTH), 0)
            km = jnp.zeros((nblk, ATTN_WIDTH), F32)
            for j in range(nblk):
                mean_j = jnp.mean(k_ref[r, 0, j * MOBA_BLOCK:(j + 1) * MOBA_BLOCK, :].astype(F32), axis=0,
                                  keepdims=True)
                km = jnp.where(blk_row == j, mean_j, km)
            km = jnp.concatenate([km] * ATTN_HEADS, axis=0)
            row_head = lax.broadcasted_iota(jnp.int32, km.shape, 0) // nblk
            col_head = lax.broadcasted_iota(jnp.int32, km.shape, 1) // HEAD_DIM
            km = jnp.where(row_head == col_head, km, 0.0)
            hi = km.astype(BF16)
            kmhi_ref[r] = hi
            kmlo_ref[r] = (km - hi.astype(F32)).astype(BF16)

    jrow = lax.broadcasted_iota(jnp.int32, (nblk, nq), 0)
    for r in range(ATTN_ROWS):
        qt = qt_ref[r, 0]
        gates = _dot(kmhi_ref[r], qt) + _dot(kmlo_ref[r], qt)
        for h in range(ATTN_HEADS):
            u = r * ATTN_HEADS + h
            g = gates[h * nblk:(h + 1) * nblk]
            cnt = jnp.zeros((nblk, nq), jnp.int32)
            for jp in range(nblk):
                gb = jnp.broadcast_to(g[jp:jp + 1], (nblk, nq))
                beats = (gb > g) | ((gb == g) & (jp < jrow))
                cnt = cnt + jnp.where(beats, jnp.where(jp < qi, 1, 0), 0)
            sel = (jrow < qi) & (cnt < MOBA_TOPK)
            far = jnp.where(sel, t31_ref[h] * LOG2E, MASK_VALUE)
            sub = jnp.where(sel, 0.0, MASK_VALUE)
            for j in range(nblk):
                rb_far_ref[u * nblk + j] = far[j:j + 1]
                rb_sub_ref[u * nblk + j] = sub[j:j + 1]

    zeros_half = jnp.zeros((HEAD_DIM, nq), BF16)
    for r, h in units:
        qh = qt_ref[r, 0, h * HEAD_DIM:(h + 1) * HEAD_DIM, :]
        qz_ref[r * ATTN_HEADS + h] = jnp.concatenate([qh, zeros_half] if h % 2 == 0 else [zeros_half, qh], axis=0)

    def scores_stage(j, kind):
        start = pl.multiple_of(j * MOBA_BLOCK, MOBA_BLOCK)
        for u, (r, h) in enumerate(units):
            pair = h // 2
            kblk = k_ref[r, 0, pl.ds(start, MOBA_BLOCK), pair * 128:(pair + 1) * 128]
            s = _dot(kblk, qz_ref[u])
            if kind == "diag":
                s = s + diag_ref[h]
            elif kind == "sub":
                s = s + sub_ref[h]
            s_ref[u] = s
            mblk_ref[u] = jnp.max(s, axis=0, keepdims=True)

    def softmax_stage(row_bias):
        for u in range(len(units)):
            rb = row_bias(u)
            m_old = m_ref[u]
            m_blk = mblk_ref[u] if rb is None else mblk_ref[u] + rb
            m_new = jnp.maximum(m_old, m_blk)
            offset = m_new if rb is None else m_new - rb
            p_ref[u] = jnp.exp2(s_ref[u] - offset).astype(BF16)
            alpha_ref[u] = jnp.exp2(m_old - m_new)
            m_ref[u] = m_new

    ones_rows = jnp.ones((BF16_SUBLANES, nq), BF16)

    def values_stage(j):
        for u, (r, h) in enumerate(units):
            rows = slice(u * HEAD_DIM, (u + 1) * HEAD_DIM)
            v_h = vt_ref[r, 0, j, h * HEAD_DIM:(h + 1) * HEAD_DIM, :]
            pv = _dot(jnp.concatenate([v_h, ones_rows], axis=0), p_ref[u])
            alpha = alpha_ref[u]
            at_ref[rows] = alpha * at_ref[rows] + pv[:HEAD_DIM]
            l_ref[u] = alpha * l_ref[u] + pv[HEAD_DIM:HEAD_DIM + 1]

    m_ref[...] = jnp.full(m_ref.shape, MASK_VALUE, F32)
    l_ref[...] = jnp.zeros(l_ref.shape, F32)
    at_ref[...] = jnp.zeros(at_ref.shape, F32)
    j_sub = jnp.maximum(qi - 1, 0)
    n_far = j_sub

    scores_stage(qi, "diag")
    softmax_stage(lambda u: None)
    scores_stage(j_sub, "sub")
    values_stage(qi)
    softmax_stage(lambda u: rb_sub_ref[u * nblk + j_sub])
    scores_stage(0, "far")

    def far_body(t, carry):
        values_stage(jnp.where(t == 0, j_sub, t - 1))
        softmax_stage(lambda u: rb_far_ref[u * nblk + t])
        scores_stage(t + 1, "far")
        return carry

    lax.fori_loop(0, n_far - 1, far_body, 0)

    @pl.when(n_far > 0)
    def _():
        values_stage(jnp.where(n_far == 1, j_sub, n_far - 2))
        softmax_stage(lambda u: rb_far_ref[u * nblk + n_far - 1])

    values_stage(jnp.where(n_far > 0, n_far - 1, j_sub))
    for u in range(len(units)):
        rows = slice(u * HEAD_DIM, (u + 1) * HEAD_DIM)
        at_ref[rows] = at_ref[rows] * (1.0 / l_ref[u])
    for r in range(ATTN_ROWS):
        a_ref[r, 0] = at_ref[r * ATTN_WIDTH:(r + 1) * ATTN_WIDTH, :].T.astype(BF16)


def _moba_attn(qt, k, vt, diag, sub, t31):
    B, S, _ = k.shape
    nblk = S // MOBA_BLOCK
    groups = B // ATTN_ROWS
    n_units = ATTN_ROWS * ATTN_HEADS
    split = lambda x: x.reshape((ATTN_ROWS, groups) + x.shape[1:])
    tile = pl.BlockSpec((ATTN_HEADS, MOBA_BLOCK, MOBA_BLOCK), lambda g, i: (0, 0, 0), pipeline_mode=pl.Buffered(1))
    a = pl.pallas_call(
        _moba_attn_kernel,
        grid=(groups, nblk),
        in_specs=[pl.BlockSpec(memory_space=pltpu.SMEM),
                  pl.BlockSpec((ATTN_ROWS, 1, ATTN_WIDTH, MOBA_BLOCK), lambda g, i: (0, g, 0, i)),
                  pl.BlockSpec((ATTN_ROWS, 1, S, ATTN_WIDTH), lambda g, i: (0, g, 0, 0)),
                  pl.BlockSpec((ATTN_ROWS, 1, nblk, ATTN_WIDTH, MOBA_BLOCK), lambda g, i: (0, g, 0, 0, 0)),
                  tile, tile],
        out_specs=pl.BlockSpec((ATTN_ROWS, 1, MOBA_BLOCK, ATTN_WIDTH), lambda g, i: (0, g, i, 0)),
        out_shape=jax.ShapeDtypeStruct((ATTN_ROWS, groups, S, ATTN_WIDTH), BF16),
        scratch_shapes=[pltpu.VMEM((ATTN_ROWS, ATTN_HEADS * nblk, ATTN_WIDTH), BF16),
                        pltpu.VMEM((ATTN_ROWS, ATTN_HEADS * nblk, ATTN_WIDTH), BF16),
                        pltpu.VMEM((n_units * nblk, 1, MOBA_BLOCK), F32),
                        pltpu.VMEM((n_units * nblk, 1, MOBA_BLOCK), F32),
                        pltpu.VMEM((n_units * HEAD_DIM, MOBA_BLOCK), F32),
                        pltpu.VMEM((n_units, 2 * HEAD_DIM, MOBA_BLOCK), BF16),
                        *[pltpu.VMEM((n_units, 1, MOBA_BLOCK), F32)] * 4,
                        pltpu.VMEM((n_units, MOBA_BLOCK, MOBA_BLOCK), F32),
                        pltpu.VMEM((n_units, MOBA_BLOCK, MOBA_BLOCK), BF16)],
        compiler_params=pltpu.CompilerParams(
            dimension_semantics=("arbitrary", "arbitrary"), vmem_limit_bytes=V7X_VMEM_LIMIT_BYTES),
        name="moba_attn",
    )(t31, split(qt), split(k), split(vt), diag, sub)
    return a.reshape(B, S, ATTN_WIDTH)


def _mix_merge_kernel(h1_ref, h1prev_ref, a_ref, win_ref, wgrp_ref, scale_ref,
                      wa_ref, wp_ref, wout_ref, h2_ref, zext_ref):
    i = pl.program_id(1)
    tm = MIX_TOKENS
    z0 = 3 * ATTN_WIDTH
    for c in range(MIX_SUBTILES):
        rows = slice(c * tm, (c + 1) * tm)
        zx_ref = zext_ref.at[c]
        zg = _norm_dot(h1_ref[0, rows], win_ref, slice(z0, None))
        halo = h1prev_ref[0] if c == 0 else h1_ref[0, c * tm - POOL_HALO:c * tm]
        zprev = _norm_dot(halo, win_ref, slice(z0, z0 + POOL_WIDTH))
        zx_ref[:POOL_HALO] = jnp.where(i > 0, zprev, 0.0) if c == 0 else zprev
        zx_ref[POOL_HALO:] = zg[:, :POOL_WIDTH]

        tpos = (i * MIX_SUBTILES + c) * tm + lax.broadcasted_iota(jnp.int32, (tm, POOL_GROUP_WIDTH), 0)
        mixed = []
        for g, w in enumerate(POOL_WINDOWS):
            cols = slice(g * POOL_GROUP_WIDTH, (g + 1) * POOL_GROUP_WIDTH)
            wsum = zx_ref[POOL_HALO:, cols]
            for lag in range(1, w):
                wsum = wsum + zx_ref[POOL_HALO - lag:POOL_HALO - lag + tm, cols]
            mean = wsum / jnp.minimum(tpos + 1, w).astype(F32)
            pooled = (mean - zx_ref[POOL_HALO:, cols]).astype(BF16)
            mixed.append(_dot(pooled, wgrp_ref[g]))
        p = (jnp.concatenate(mixed, axis=1) * scale_ref[...]).astype(BF16)

        g_attn = zg[:, POOL_WIDTH:POOL_WIDTH + D_MODEL]
        g_pool = zg[:, POOL_WIDTH + D_MODEL:]
        merged = (jax.nn.sigmoid(g_attn) * _dot(a_ref[0, rows], wa_ref[...])
                  + jax.nn.sigmoid(g_pool) * _dot(p, wp_ref[...]))
        h2_ref[0, rows] = h1_ref[0, rows] + _dot(merged.astype(BF16), wout_ref[...])


def _mix_merge(h1, a, win, wgrp, scale, wa, wp, wout):
    B, S, D = h1.shape
    tm = MIX_SUBTILES * MIX_TOKENS
    halo_per_tile = tm // POOL_HALO
    return pl.pallas_call(
        _mix_merge_kernel,
        grid=(B, S // tm),
        in_specs=[pl.BlockSpec((1, tm, D), lambda b, i: (b, i, 0)),
                  pl.BlockSpec((1, POOL_HALO, D), lambda b, i: (b, jnp.maximum(i * halo_per_tile - 1, 0), 0)),
                  pl.BlockSpec((1, tm, ATTN_WIDTH), lambda b, i: (b, i, 0)),
                  _resident(win.shape), _resident(wgrp.shape), _resident(scale.shape),
                  _resident(wa.shape), _resident(wp.shape), _resident(wout.shape)],
        out_specs=pl.BlockSpec((1, tm, D), lambda b, i: (b, i, 0)),
        out_shape=jax.ShapeDtypeStruct((B, S, D), F32),
        scratch_shapes=[pltpu.VMEM((MIX_SUBTILES, POOL_HALO + MIX_TOKENS, POOL_WIDTH), F32)],
        compiler_params=pltpu.CompilerParams(
            dimension_semantics=("arbitrary", "arbitrary"), vmem_limit_bytes=V7X_VMEM_LIMIT_BYTES),
        name="mix_merge",
    )(h1, h1, a, win, wgrp, scale, wa, wp, wout)


def _ffn2_final_kernel(h_ref, wg_ref, wu_ref, wd_ref, gf_ref, out_ref):
    for c in range(FFN2_SUBTILES):
        rows = slice(c * FFN_TOKENS, (c + 1) * FFN_TOKENS)
        h3 = _swiglu_half_step(h_ref[0, rows], wg_ref, wu_ref, wd_ref)
        out_ref[0, rows] = _rms(h3, gf_ref[...])


def _ffn2_final(h, wg, wu, wd, gf):
    B, S, D = h.shape
    tm = FFN2_SUBTILES * FFN_TOKENS
    return pl.pallas_call(
        _ffn2_final_kernel,
        grid=(B, S // tm),
        in_specs=[pl.BlockSpec((1, tm, D), lambda b, i: (b, i, 0)),
                  _resident(wg.shape), _resident(wu.shape), _resident(wd.shape), _resident(gf.shape)],
        out_specs=pl.BlockSpec((1, tm, D), lambda b, i: (b, i, 0)),
        out_shape=jax.ShapeDtypeStruct((B, S, D), F32),
        compiler_params=pltpu.CompilerParams(
            dimension_semantics=("arbitrary", "arbitrary"), vmem_limit_bytes=V7X_VMEM_LIMIT_BYTES),
        name="ffn2_final",
    )(h, wg, wu, wd, gf)


def kernel(x, ffn1_norm, ffn1_w_gate, ffn1_w_up, ffn1_w_down, mix_norm, w_in, pool_w_group, pool_scale,
           w_branch_attn, w_branch_pool, w_out, ffn2_norm, ffn2_w_gate, ffn2_w_up, ffn2_w_down,
           rpb_table, final_norm):
    B, S, D = x.shape
    assert (D, ffn1_w_gate.shape[0]) == (D_MODEL, 1)
    assert S % MOBA_BLOCK == 0 and S % (FFN2_SUBTILES * FFN_TOKENS) == 0 and S % (MIX_SUBTILES * MIX_TOKENS) == 0
    assert B % ATTN_ROWS == 0
    bf = lambda w: w.astype(BF16)
    row = lambda v: v.reshape(1, -1)
    w_in = w_in[0]
    gained = lambda g, w: bf(g[:, None] * w)
    wqkv = gained(mix_norm[0], w_in[:, :3 * ATTN_WIDTH])

    diag, sub = _rpb_bias(rpb_table)
    (h1, qt, k, vt), later = _ffn1_qkv(
        x, gained(ffn1_norm[0], ffn1_w_gate[0]), gained(ffn1_norm[0], ffn1_w_up[0]), bf(ffn1_w_down[0]), wqkv,
        [w_in, w_branch_attn[0], w_branch_pool[0], w_out[0], ffn2_w_gate[0], ffn2_w_up[0], ffn2_w_down[0]],
        [mix_norm[0], ffn2_norm[0]])
    win_bf, wa_bf, wp_bf, wout_bf, wg2_bf, wu2_bf, wd2_bf = later
    a = _moba_attn(qt, k, vt, diag, sub, rpb_table[RPB_BUCKETS - 1])
    h2 = _mix_merge(h1, a, win_bf, bf(pool_w_group[0]), row(pool_scale[0]), wa_bf, wp_bf, wout_bf)
    return _ffn2_final(h2, wg2_bf, wu2_bf, wd2_bf, row(final_norm))
```

```python
import math

import jax
import jax.numpy as jnp
from jax import lax
from jax.experimental import pallas as pl
from jax.experimental.pallas import tpu as pltpu

D_MODEL = 1024
HEAD_DIM = 64
ATTN_WIDTH = 512
ATTN_HEADS = 8
MOBA_BLOCK = 256
MOBA_TOPK = 3
POOL_WINDOWS = (2, 4, 8, 16)
POOL_WIDTH = 512
POOL_GROUP_WIDTH = 128
POOL_HALO = 16
RPB_BUCKETS = 32
RPB_MAX_DISTANCE = 128
RMS_EPS = 1e-6
MASK_VALUE = -1e30
LOG2E = math.log2(math.e)
Q_SCALE = HEAD_DIM ** -0.5 * LOG2E
ATTN_ROWS = 2
V7X_VMEM_LIMIT_BYTES = 56 * 1024 * 1024

FFN_TOKENS = 512
FFN2_SUBTILES = 2
MIX_TOKENS = 512
MIX_SUBTILES = 2
N_LATER_WEIGHTS = 7
BF16_SUBLANES = 16

BF16 = jnp.bfloat16
F32 = jnp.float32


def _resident(shape):
    nd = len(shape)
    return pl.BlockSpec(shape, lambda *_: (0,) * nd, pipeline_mode=pl.Buffered(1))


def _dot(a, b):
    return jnp.dot(a, b, preferred_element_type=F32)


def _rms(x, g):
    return x * lax.rsqrt(jnp.mean(x * x, axis=-1, keepdims=True) + RMS_EPS) * g


def _swiglu_half_step(x, g_ref, wg_ref, wu_ref, wd_ref):
    xn = _rms(x, g_ref[...]).astype(BF16)
    gate = _dot(xn, wg_ref[...])
    up = _dot(xn, wu_ref[...])
    act = (gate * jax.nn.sigmoid(gate) * up).astype(BF16)
    return x + 0.5 * _dot(act, wd_ref[...])


def _rpb_bucket(dist):
    n = jnp.maximum(dist, 0)
    max_exact = RPB_BUCKETS // 2
    nf = jnp.maximum(n, 1).astype(F32)
    large = max_exact + jnp.floor(jnp.log(nf / max_exact) / math.log(RPB_MAX_DISTANCE / max_exact)
                                  * (RPB_BUCKETS - max_exact)).astype(jnp.int32)
    large = jnp.minimum(large, RPB_BUCKETS - 1)
    return jnp.where(n < max_exact, n, large)


def _rpb_bias_kernel(table_ref, diag_ref, sub_ref):
    h = pl.program_id(0)
    width = 4 * MOBA_BLOCK
    sublanes = 8
    dist = lax.broadcasted_iota(jnp.int32, (sublanes, width), 1) - MOBA_BLOCK
    bucket = _rpb_bucket(dist)
    f = jnp.zeros((sublanes, width), F32)
    for b in range(RPB_BUCKETS):
        f = jnp.where(bucket == b, table_ref[b, h], f)
    f = f * LOG2E
    sub = lax.broadcasted_iota(jnp.int32, (sublanes, width), 0)
    group = f
    for r in range(1, sublanes):
        group = jnp.where(sub == r, pltpu.roll(f, r, 1), group)
    ql = lax.broadcasted_iota(jnp.int32, (sublanes, MOBA_BLOCK), 1)
    kl0 = lax.broadcasted_iota(jnp.int32, (sublanes, MOBA_BLOCK), 0)
    for g in range(MOBA_BLOCK // sublanes):
        rows = slice(g * sublanes, (g + 1) * sublanes)
        blk = pltpu.roll(group, g * sublanes, 1) if g else group
        diag_ref[0, rows] = jnp.where(ql >= kl0 + g * sublanes, blk[:, MOBA_BLOCK:2 * MOBA_BLOCK], MASK_VALUE)
        sub_ref[0, rows] = blk[:, 2 * MOBA_BLOCK:3 * MOBA_BLOCK]


def _rpb_bias(rpb_table):
    tile = jax.ShapeDtypeStruct((ATTN_HEADS, MOBA_BLOCK, MOBA_BLOCK), F32)
    spec = pl.BlockSpec((1, MOBA_BLOCK, MOBA_BLOCK), lambda h: (h, 0, 0))
    return pl.pallas_call(
        _rpb_bias_kernel,
        grid=(ATTN_HEADS,),
        in_specs=[pl.BlockSpec(memory_space=pltpu.SMEM)],
        out_specs=[spec, spec],
        out_shape=[tile, tile],
        name="rpb_bias",
    )(rpb_table)


def _ffn1_qkv_kernel(x_ref, g1_ref, wg_ref, wu_ref, wd_ref, gm_ref, wqkv_ref, *refs):
    later_f32, (h1_ref, un_ref, qt_ref, k_ref, vt_ref), later_bf16 = (
        refs[:N_LATER_WEIGHTS], refs[N_LATER_WEIGHTS:N_LATER_WEIGHTS + 5], refs[N_LATER_WEIGHTS + 5:])
    for src_ref, dst_ref in zip(later_f32, later_bf16):
        dst_ref[...] = src_ref[...].astype(BF16)
    h1 = _swiglu_half_step(x_ref[0], g1_ref, wg_ref, wu_ref, wd_ref)
    h1_ref[0] = h1
    un = _rms(h1, gm_ref[...]).astype(BF16)
    un_ref[0] = un
    qkv = _dot(un, wqkv_ref[...])
    qt_ref[0] = (qkv[:, :ATTN_WIDTH] * Q_SCALE).T.astype(BF16)
    k_ref[0] = qkv[:, ATTN_WIDTH:2 * ATTN_WIDTH].astype(BF16)
    for c in range(FFN_TOKENS // MOBA_BLOCK):
        vt_ref[0, c] = qkv[c * MOBA_BLOCK:(c + 1) * MOBA_BLOCK, 2 * ATTN_WIDTH:].T.astype(BF16)


def _ffn1_qkv(x, g1, wg, wu, wd, gm, wqkv, later_weights):
    B, S, D = x.shape
    tm = FFN_TOKENS
    nblk = S // MOBA_BLOCK
    tiles_per_seq = S // tm
    steps = B * tiles_per_seq
    assert len(later_weights) == N_LATER_WEIGHTS
    step_of = lambda b, i: b * tiles_per_seq + i

    def slab_view(w):
        n = max(n for n in range(1, steps + 1) if steps % n == 0 and w.shape[0] % (n * BF16_SUBLANES) == 0)
        view = w.reshape(n, w.shape[0] // n, w.shape[1])
        return view, pl.BlockSpec((1,) + view.shape[1:], lambda b, i: (step_of(b, i) // (steps // n), 0, 0))

    slabs, slab_specs = zip(*[slab_view(w) for w in later_weights])
    tok = lambda width, dt: (pl.BlockSpec((1, tm, width), lambda b, i: (b, i, 0)),
                             jax.ShapeDtypeStruct((B, S, width), dt))
    h1_spec, h1_shape = tok(D, F32)
    un_spec, un_shape = tok(D, BF16)
    k_spec, k_shape = tok(ATTN_WIDTH, BF16)
    outs = pl.pallas_call(
        _ffn1_qkv_kernel,
        grid=(B, tiles_per_seq),
        in_specs=[pl.BlockSpec((1, tm, D), lambda b, i: (b, i, 0)),
                  _resident(g1.shape), _resident(wg.shape), _resident(wu.shape), _resident(wd.shape),
                  _resident(gm.shape), _resident(wqkv.shape), *slab_specs],
        out_specs=[h1_spec, un_spec,
                   pl.BlockSpec((1, ATTN_WIDTH, tm), lambda b, i: (b, 0, i)),
                   k_spec,
                   pl.BlockSpec((1, tm // MOBA_BLOCK, ATTN_WIDTH, MOBA_BLOCK), lambda b, i: (b, i, 0, 0)),
                   *slab_specs],
        out_shape=[h1_shape, un_shape,
                   jax.ShapeDtypeStruct((B, ATTN_WIDTH, S), BF16),
                   k_shape,
                   jax.ShapeDtypeStruct((B, nblk, ATTN_WIDTH, MOBA_BLOCK), BF16)]
                  + [jax.ShapeDtypeStruct(w.shape, BF16) for w in slabs],
        compiler_params=pltpu.CompilerParams(
            dimension_semantics=("arbitrary", "arbitrary"), vmem_limit_bytes=V7X_VMEM_LIMIT_BYTES),
        name="ffn1_qkv",
    )(x, g1, wg, wu, wd, gm, wqkv, *slabs)
    return outs[:5], [o.reshape(w.shape) for o, w in zip(outs[5:], later_weights)]


def _moba_attn_kernel(t31_ref, qt_ref, k_ref, vt_ref, diag_ref, sub_ref, a_ref,
                      kmhi_ref, kmlo_ref, rb_far_ref, rb_sub_ref, at_ref, qz_ref,
                      m_ref, l_ref, mblk_ref, alpha_ref, s_ref, p_ref):
    qi = pl.program_id(1)
    nblk = k_ref.shape[2] // MOBA_BLOCK
    nq = MOBA_BLOCK
    units = [(r, h) for r in range(ATTN_ROWS) for h in range(ATTN_HEADS)]

    @pl.when(qi == 0)
    def _():
        for r in range(ATTN_ROWS):
            blk_row = lax.broadcasted_iota(jnp.int32, (nblk, ATTN_WIDTH), 0)
            km = jnp.zeros((nblk, ATTN_WIDTH), F32)
            for j in range(nblk):
                mean_j = jnp.mean(k_ref[r, 0, j * MOBA_BLOCK:(j + 1) * MOBA_BLOCK, :].astype(F32), axis=0,
                                  keepdims=True)
                km = jnp.where(blk_row == j, mean_j, km)
            km = jnp.concatenate([km] * ATTN_HEADS, axis=0)
            row_head = lax.broadcasted_iota(jnp.int32, km.shape, 0) // nblk
            col_head = lax.broadcasted_iota(jnp.int32, km.shape, 1) // HEAD_DIM
            km = jnp.where(row_head == col_head, km, 0.0)
            hi = km.astype(BF16)
            kmhi_ref[r] = hi
            kmlo_ref[r] = (km - hi.astype(F32)).astype(BF16)

    jrow = lax.broadcasted_iota(jnp.int32, (nblk, nq), 0)
    for r in range(ATTN_ROWS):
        qt = qt_ref[r, 0]
        gates = _dot(kmhi_ref[r], qt) + _dot(kmlo_ref[r], qt)
        for h in range(ATTN_HEADS):
            u = r * ATTN_HEADS + h
            g = gates[h * nblk:(h + 1) * nblk]
            cnt = jnp.zeros((nblk, nq), jnp.int32)
            for jp in range(nblk):
                gb = jnp.broadcast_to(g[jp:jp + 1], (nblk, nq))
                beats = (gb > g) | ((gb == g) & (jp < jrow))
                cnt = cnt + jnp.where(beats, jnp.where(jp < qi, 1, 0), 0)
            sel = (jrow < qi) & (cnt < MOBA_TOPK)
            far = jnp.where(sel, t31_ref[h] * LOG2E, MASK_VALUE)
            sub = jnp.where(sel, 0.0, MASK_VALUE)
            for j in range(nblk):
                rb_far_ref[u * nblk + j] = far[j:j + 1]
                rb_sub_ref[u * nblk + j] = sub[j:j + 1]

    zeros_half = jnp.zeros((HEAD_DIM, nq), BF16)
    for r, h in units:
        qh = qt_ref[r, 0, h * HEAD_DIM:(h + 1) * HEAD_DIM, :]
        qz_ref[r * ATTN_HEADS + h] = jnp.concatenate([qh, zeros_half] if h % 2 == 0 else [zeros_half, qh], axis=0)

    def scores_stage(j, kind):
        start = pl.multiple_of(j * MOBA_BLOCK, MOBA_BLOCK)
        for u, (r, h) in enumerate(units):
            pair = h // 2
            kblk = k_ref[r, 0, pl.ds(start, MOBA_BLOCK), pair * 128:(pair + 1) * 128]
            s = _dot(kblk, qz_ref[u])
            if kind == "diag":
                s = s + diag_ref[h]
            elif kind == "sub":
                s = s + sub_ref[h]
            s_ref[u] = s
            mblk_ref[u] = jnp.max(s, axis=0, keepdims=True)

    def softmax_stage(row_bias):
        for u in range(len(units)):
            rb = row_bias(u)
            m_old = m_ref[u]
            m_blk = mblk_ref[u] if rb is None else mblk_ref[u] + rb
            m_new = jnp.maximum(m_old, m_blk)
            offset = m_new if rb is None else m_new - rb
            p_ref[u] = jnp.exp2(s_ref[u] - offset).astype(BF16)
            alpha_ref[u] = jnp.exp2(m_old - m_new)
            m_ref[u] = m_new

    ones_rows = jnp.ones((BF16_SUBLANES, nq), BF16)

    def values_stage(j):
        for u, (r, h) in enumerate(units):
            rows = slice(u * HEAD_DIM, (u + 1) * HEAD_DIM)
            v_h = vt_ref[r, 0, j, h * HEAD_DIM:(h + 1) * HEAD_DIM, :]
            pv = _dot(jnp.concatenate([v_h, ones_rows], axis=0), p_ref[u])
            alpha = alpha_ref[u]
            at_ref[rows] = alpha * at_ref[rows] + pv[:HEAD_DIM]
            l_ref[u] = alpha * l_ref[u] + pv[HEAD_DIM:HEAD_DIM + 1]

    m_ref[...] = jnp.full(m_ref.shape, MASK_VALUE, F32)
    l_ref[...] = jnp.zeros(l_ref.shape, F32)
    at_ref[...] = jnp.zeros(at_ref.shape, F32)
    j_sub = jnp.maximum(qi - 1, 0)
    n_far = j_sub

    scores_stage(qi, "diag")
    softmax_stage(lambda u: None)
    scores_stage(j_sub, "sub")
    values_stage(qi)
    softmax_stage(lambda u: rb_sub_ref[u * nblk + j_sub])
    scores_stage(0, "far")

    def far_body(t, carry):
        values_stage(jnp.where(t == 0, j_sub, t - 1))
        softmax_stage(lambda u: rb_far_ref[u * nblk + t])
        scores_stage(t + 1, "far")
        return carry

    lax.fori_loop(0, jnp.maximum(n_far - 1, 0), far_body, 0)

    @pl.when(n_far > 0)
    def _():
        values_stage(jnp.where(n_far == 1, j_sub, n_far - 2))
        softmax_stage(lambda u: rb_far_ref[u * nblk + n_far - 1])

    values_stage(jnp.where(n_far > 0, n_far - 1, j_sub))
    for u in range(len(units)):
        rows = slice(u * HEAD_DIM, (u + 1) * HEAD_DIM)
        at_ref[rows] = at_ref[rows] * (1.0 / l_ref[u])
    for r in range(ATTN_ROWS):
        a_ref[r, 0] = at_ref[r * ATTN_WIDTH:(r + 1) * ATTN_WIDTH, :].T.astype(BF16)


def _moba_attn(qt, k, vt, diag, sub, t31):
    B, S, _ = k.shape
    nblk = S // MOBA_BLOCK
    groups = B // ATTN_ROWS
    n_units = ATTN_ROWS * ATTN_HEADS
    split = lambda x: x.reshape((ATTN_ROWS, groups) + x.shape[1:])
    tile = pl.BlockSpec((ATTN_HEADS, MOBA_BLOCK, MOBA_BLOCK), lambda g, i: (0, 0, 0), pipeline_mode=pl.Buffered(1))
    a = pl.pallas_call(
        _moba_attn_kernel,
        grid=(groups, nblk),
        in_specs=[pl.BlockSpec(memory_space=pltpu.SMEM),
                  pl.BlockSpec((ATTN_ROWS, 1, ATTN_WIDTH, MOBA_BLOCK), lambda g, i: (0, g, 0, i)),
                  pl.BlockSpec((ATTN_ROWS, 1, S, ATTN_WIDTH), lambda g, i: (0, g, 0, 0)),
                  pl.BlockSpec((ATTN_ROWS, 1, nblk, ATTN_WIDTH, MOBA_BLOCK), lambda g, i: (0, g, 0, 0, 0)),
                  tile, tile],
        out_specs=pl.BlockSpec((ATTN_ROWS, 1, MOBA_BLOCK, ATTN_WIDTH), lambda g, i: (0, g, i, 0)),
        out_shape=jax.ShapeDtypeStruct((ATTN_ROWS, groups, S, ATTN_WIDTH), BF16),
        scratch_shapes=[pltpu.VMEM((ATTN_ROWS, ATTN_HEADS * nblk, ATTN_WIDTH), BF16),
                        pltpu.VMEM((ATTN_ROWS, ATTN_HEADS * nblk, ATTN_WIDTH), BF16),
                        pltpu.VMEM((n_units * nblk, 1, MOBA_BLOCK), F32),
                        pltpu.VMEM((n_units * nblk, 1, MOBA_BLOCK), F32),
                        pltpu.VMEM((n_units * HEAD_DIM, MOBA_BLOCK), F32),
                        pltpu.VMEM((n_units, 2 * HEAD_DIM, MOBA_BLOCK), BF16),
                        *[pltpu.VMEM((n_units, 1, MOBA_BLOCK), F32)] * 4,
                        pltpu.VMEM((n_units, MOBA_BLOCK, MOBA_BLOCK), F32),
                        pltpu.VMEM((n_units, MOBA_BLOCK, MOBA_BLOCK), BF16)],
        compiler_params=pltpu.CompilerParams(
            dimension_semantics=("arbitrary", "arbitrary"), vmem_limit_bytes=V7X_VMEM_LIMIT_BYTES),
        name="moba_attn",
    )(t31, split(qt), split(k), split(vt), diag, sub)
    return a.reshape(B, S, ATTN_WIDTH)


def _mix_merge_kernel(h1_ref, un_ref, unprev_ref, a_ref, win_ref, wgrp_ref, scale_ref,
                      wa_ref, wp_ref, wout_ref, h2_ref, zext_ref):
    i = pl.program_id(1)
    tm = MIX_TOKENS
    z0 = 3 * ATTN_WIDTH
    for c in range(MIX_SUBTILES):
        rows = slice(c * tm, (c + 1) * tm)
        zx_ref = zext_ref.at[c]
        zg = _dot(un_ref[0, rows], win_ref[:, z0:])
        halo = unprev_ref[0] if c == 0 else un_ref[0, c * tm - POOL_HALO:c * tm]
        zprev = _dot(halo, win_ref[:, z0:z0 + POOL_WIDTH])
        zx_ref[:POOL_HALO] = jnp.where(i > 0, zprev, 0.0) if c == 0 else zprev
        zx_ref[POOL_HALO:] = zg[:, :POOL_WIDTH]

        tpos = (i * MIX_SUBTILES + c) * tm + lax.broadcasted_iota(jnp.int32, (tm, POOL_GROUP_WIDTH), 0)
        mixed = []
        for g, w in enumerate(POOL_WINDOWS):
            cols = slice(g * POOL_GROUP_WIDTH, (g + 1) * POOL_GROUP_WIDTH)
            wsum = zx_ref[POOL_HALO:, cols]
            for lag in range(1, w):
                wsum = wsum + zx_ref[POOL_HALO - lag:POOL_HALO - lag + tm, cols]
            mean = wsum / jnp.minimum(tpos + 1, w).astype(F32)
            pooled = (mean - zx_ref[POOL_HALO:, cols]).astype(BF16)
            mixed.append(_dot(pooled, wgrp_ref[g]))
        p = (jnp.concatenate(mixed, axis=1) * scale_ref[...]).astype(BF16)

        g_attn = zg[:, POOL_WIDTH:POOL_WIDTH + D_MODEL]
        g_pool = zg[:, POOL_WIDTH + D_MODEL:]
        merged = (jax.nn.sigmoid(g_attn) * _dot(a_ref[0, rows], wa_ref[...])
                  + jax.nn.sigmoid(g_pool) * _dot(p, wp_ref[...]))
        h2_ref[0, rows] = h1_ref[0, rows] + _dot(merged.astype(BF16), wout_ref[...])


def _mix_merge(h1, un, a, win, wgrp, scale, wa, wp, wout):
    B, S, D = h1.shape
    tm = MIX_SUBTILES * MIX_TOKENS
    halo_per_tile = tm // POOL_HALO
    return pl.pallas_call(
        _mix_merge_kernel,
        grid=(B, S // tm),
        in_specs=[pl.BlockSpec((1, tm, D), lambda b, i: (b, i, 0)),
                  pl.BlockSpec((1, tm, D), lambda b, i: (b, i, 0)),
                  pl.BlockSpec((1, POOL_HALO, D), lambda b, i: (b, jnp.maximum(i * halo_per_tile - 1, 0), 0)),
                  pl.BlockSpec((1, tm, ATTN_WIDTH), lambda b, i: (b, i, 0)),
                  _resident(win.shape), _resident(wgrp.shape), _resident(scale.shape),
                  _resident(wa.shape), _resident(wp.shape), _resident(wout.shape)],
        out_specs=pl.BlockSpec((1, tm, D), lambda b, i: (b, i, 0)),
        out_shape=jax.ShapeDtypeStruct((B, S, D), F32),
        scratch_shapes=[pltpu.VMEM((MIX_SUBTILES, POOL_HALO + MIX_TOKENS, POOL_WIDTH), F32)],
        compiler_params=pltpu.CompilerParams(
            dimension_semantics=("arbitrary", "arbitrary"), vmem_limit_bytes=V7X_VMEM_LIMIT_BYTES),
        name="mix_merge",
    )(h1, un, un, a, win, wgrp, scale, wa, wp, wout)


def _ffn2_final_kernel(h_ref, g2_ref, wg_ref, wu_ref, wd_ref, gf_ref, out_ref):
    for c in range(FFN2_SUBTILES):
        rows = slice(c * FFN_TOKENS, (c + 1) * FFN_TOKENS)
        h3 = _swiglu_half_step(h_ref[0, rows], g2_ref, wg_ref, wu_ref, wd_ref)
        out_ref[0, rows] = _rms(h3, gf_ref[...])


def _ffn2_final(h, g2, wg, wu, wd, gf):
    B, S, D = h.shape
    tm = FFN2_SUBTILES * FFN_TOKENS
    return pl.pallas_call(
        _ffn2_final_kernel,
        grid=(B, S // tm),
        in_specs=[pl.BlockSpec((1, tm, D), lambda b, i: (b, i, 0)),
                  _resident(g2.shape), _resident(wg.shape), _resident(wu.shape), _resident(wd.shape),
                  _resident(gf.shape)],
        out_specs=pl.BlockSpec((1, tm, D), lambda b, i: (b, i, 0)),
        out_shape=jax.ShapeDtypeStruct((B, S, D), F32),
        compiler_params=pltpu.CompilerParams(
            dimension_semantics=("arbitrary", "arbitrary"), vmem_limit_bytes=V7X_VMEM_LIMIT_BYTES),
        name="ffn2_final",
    )(h, g2, wg, wu, wd, gf)


def kernel(x, ffn1_norm, ffn1_w_gate, ffn1_w_up, ffn1_w_down, mix_norm, w_in, pool_w_group, pool_scale,
           w_branch_attn, w_branch_pool, w_out, ffn2_norm, ffn2_w_gate, ffn2_w_up, ffn2_w_down,
           rpb_table, final_norm):
    B, S, D = x.shape
    assert (D, ffn1_w_gate.shape[0]) == (D_MODEL, 1)
    assert S % MOBA_BLOCK == 0 and S % (FFN2_SUBTILES * FFN_TOKENS) == 0 and S % (MIX_SUBTILES * MIX_TOKENS) == 0
    assert B % ATTN_ROWS == 0
    bf = lambda w: w.astype(BF16)
    row = lambda v: v.reshape(1, -1)
    w_in = w_in[0]
    wqkv = bf(w_in[:, :3 * ATTN_WIDTH])

    diag, sub = _rpb_bias(rpb_table)
    (h1, un, qt, k, vt), later = _ffn1_qkv(
        x, row(ffn1_norm[0]), bf(ffn1_w_gate[0]), bf(ffn1_w_up[0]), bf(ffn1_w_down[0]), row(mix_norm[0]), wqkv,
        [w_in, w_branch_attn[0], w_branch_pool[0], w_out[0], ffn2_w_gate[0], ffn2_w_up[0], ffn2_w_down[0]])
    win_bf, wa_bf, wp_bf, wout_bf, wg2_bf, wu2_bf, wd2_bf = later
    a = _moba_attn(qt, k, vt, diag, sub, rpb_table[RPB_BUCKETS - 1])
    h2 = _mix_merge(h1, un, a, win_bf, bf(pool_w_group[0]), row(pool_scale[0]), wa_bf, wp_bf, wout_bf)
    return _ffn2_final(h2, row(ffn2_norm[0]), wg2_bf, wu2_bf, wd2_bf, row(final_norm))
```

```python
import math

import jax
import jax.numpy as jnp
from jax import lax
from jax.experimental import pallas as pl
from jax.experimental.pallas import tpu as pltpu

D_MODEL = 1024
HEAD_DIM = 64
ATTN_WIDTH = 512
ATTN_HEADS = 8
MOBA_BLOCK = 256
MOBA_TOPK = 3
POOL_WINDOWS = (2, 4, 8, 16)
POOL_WIDTH = 512
POOL_GROUP_WIDTH = 128
POOL_HALO = 16
RPB_BUCKETS = 32
RPB_MAX_DISTANCE = 128
RMS_EPS = 1e-6
MASK_VALUE = -1e30
LOG2E = math.log2(math.e)
Q_SCALE = HEAD_DIM ** -0.5 * LOG2E
ATTN_ROWS = 2
V7X_VMEM_LIMIT_BYTES = 56 * 1024 * 1024

FFN_TOKENS = 512
FFN2_SUBTILES = 2
MIX_TOKENS = 512
MIX_SUBTILES = 2
N_LATER_WEIGHTS = 7
BF16_SUBLANES = 16

BF16 = jnp.bfloat16
F32 = jnp.float32


def _resident(shape):
    nd = len(shape)
    return pl.BlockSpec(shape, lambda *_: (0,) * nd, pipeline_mode=pl.Buffered(1))


def _dot(a, b):
    return jnp.dot(a, b, preferred_element_type=F32)


def _rms(x, g):
    return x * lax.rsqrt(jnp.mean(x * x, axis=-1, keepdims=True) + RMS_EPS) * g


def _swiglu_half_step(x, g_ref, wg_ref, wu_ref, wd_ref):
    xn = _rms(x, g_ref[...]).astype(BF16)
    gate = _dot(xn, wg_ref[...])
    up = _dot(xn, wu_ref[...])
    act = (gate * jax.nn.sigmoid(gate) * up).astype(BF16)
    return x + 0.5 * _dot(act, wd_ref[...])


def _rpb_bucket(dist):
    n = jnp.maximum(dist, 0)
    max_exact = RPB_BUCKETS // 2
    nf = jnp.maximum(n, 1).astype(F32)
    large = max_exact + jnp.floor(jnp.log(nf / max_exact) / math.log(RPB_MAX_DISTANCE / max_exact)
                                  * (RPB_BUCKETS - max_exact)).astype(jnp.int32)
    large = jnp.minimum(large, RPB_BUCKETS - 1)
    return jnp.where(n < max_exact, n, large)


def _rpb_bias_kernel(table_ref, diag_ref, sub_ref):
    h = pl.program_id(0)
    width = 4 * MOBA_BLOCK
    sublanes = 8
    dist = lax.broadcasted_iota(jnp.int32, (sublanes, width), 1) - MOBA_BLOCK
    bucket = _rpb_bucket(dist)
    f = jnp.zeros((sublanes, width), F32)
    for b in range(RPB_BUCKETS):
        f = jnp.where(bucket == b, table_ref[b, h], f)
    f = f * LOG2E
    sub = lax.broadcasted_iota(jnp.int32, (sublanes, width), 0)
    group = f
    for r in range(1, sublanes):
        group = jnp.where(sub == r, pltpu.roll(f, r, 1), group)
    ql = lax.broadcasted_iota(jnp.int32, (sublanes, MOBA_BLOCK), 1)
    kl0 = lax.broadcasted_iota(jnp.int32, (sublanes, MOBA_BLOCK), 0)
    for g in range(MOBA_BLOCK // sublanes):
        rows = slice(g * sublanes, (g + 1) * sublanes)
        blk = pltpu.roll(group, g * sublanes, 1) if g else group
        diag_ref[0, rows] = jnp.where(ql >= kl0 + g * sublanes, blk[:, MOBA_BLOCK:2 * MOBA_BLOCK], MASK_VALUE)
        sub_ref[0, rows] = blk[:, 2 * MOBA_BLOCK:3 * MOBA_BLOCK]


def _rpb_bias(rpb_table):
    tile = jax.ShapeDtypeStruct((ATTN_HEADS, MOBA_BLOCK, MOBA_BLOCK), F32)
    spec = pl.BlockSpec((1, MOBA_BLOCK, MOBA_BLOCK), lambda h: (h, 0, 0))
    return pl.pallas_call(
        _rpb_bias_kernel,
        grid=(ATTN_HEADS,),
        in_specs=[pl.BlockSpec(memory_space=pltpu.SMEM)],
        out_specs=[spec, spec],
        out_shape=[tile, tile],
        name="rpb_bias",
    )(rpb_table)


def _ffn1_qkv_kernel(x_ref, g1_ref, wg_ref, wu_ref, wd_ref, gm_ref, wqkv_f32_ref, *refs):
    later_f32, (h1_ref, un_ref, qt_ref, k_ref, vt_ref), later_bf16, wqkv_ref = (
        refs[:N_LATER_WEIGHTS], refs[N_LATER_WEIGHTS:N_LATER_WEIGHTS + 5], refs[N_LATER_WEIGHTS + 5:-1], refs[-1])

    @pl.when((pl.program_id(0) == 0) & (pl.program_id(1) == 0))
    def _():
        wqkv_ref[...] = wqkv_f32_ref[...].astype(BF16)

    for src_ref, dst_ref in zip(later_f32, later_bf16):
        dst_ref[...] = src_ref[...].astype(BF16)
    h1 = _swiglu_half_step(x_ref[0], g1_ref, wg_ref, wu_ref, wd_ref)
    h1_ref[0] = h1
    un = _rms(h1, gm_ref[...]).astype(BF16)
    un_ref[0] = un
    qkv = _dot(un, wqkv_ref[...])
    qt_ref[0] = (qkv[:, :ATTN_WIDTH] * Q_SCALE).T.astype(BF16)
    k_ref[0] = qkv[:, ATTN_WIDTH:2 * ATTN_WIDTH].astype(BF16)
    for c in range(FFN_TOKENS // MOBA_BLOCK):
        vt_ref[0, c] = qkv[c * MOBA_BLOCK:(c + 1) * MOBA_BLOCK, 2 * ATTN_WIDTH:].T.astype(BF16)


def _ffn1_qkv(x, g1, wg, wu, wd, gm, w_in, later_weights):
    B, S, D = x.shape
    tm = FFN_TOKENS
    nblk = S // MOBA_BLOCK
    tiles_per_seq = S // tm
    steps = B * tiles_per_seq
    assert len(later_weights) == N_LATER_WEIGHTS
    step_of = lambda b, i: b * tiles_per_seq + i

    def slab_view(w):
        n = max(n for n in range(1, steps + 1) if steps % n == 0 and w.shape[0] % (n * BF16_SUBLANES) == 0)
        view = w.reshape(n, w.shape[0] // n, w.shape[1])
        return view, pl.BlockSpec((1,) + view.shape[1:], lambda b, i: (step_of(b, i) // (steps // n), 0, 0))

    slabs, slab_specs = zip(*[slab_view(w) for w in later_weights])
    tok = lambda width, dt: (pl.BlockSpec((1, tm, width), lambda b, i: (b, i, 0)),
                             jax.ShapeDtypeStruct((B, S, width), dt))
    h1_spec, h1_shape = tok(D, F32)
    un_spec, un_shape = tok(D, BF16)
    k_spec, k_shape = tok(ATTN_WIDTH, BF16)
    outs = pl.pallas_call(
        _ffn1_qkv_kernel,
        grid=(B, tiles_per_seq),
        in_specs=[pl.BlockSpec((1, tm, D), lambda b, i: (b, i, 0)),
                  _resident(g1.shape), _resident(wg.shape), _resident(wu.shape), _resident(wd.shape),
                  _resident(gm.shape), _resident((D, 3 * ATTN_WIDTH)), *slab_specs],
        out_specs=[h1_spec, un_spec,
                   pl.BlockSpec((1, ATTN_WIDTH, tm), lambda b, i: (b, 0, i)),
                   k_spec,
                   pl.BlockSpec((1, tm // MOBA_BLOCK, ATTN_WIDTH, MOBA_BLOCK), lambda b, i: (b, i, 0, 0)),
                   *slab_specs],
        out_shape=[h1_shape, un_shape,
                   jax.ShapeDtypeStruct((B, ATTN_WIDTH, S), BF16),
                   k_shape,
                   jax.ShapeDtypeStruct((B, nblk, ATTN_WIDTH, MOBA_BLOCK), BF16)]
                  + [jax.ShapeDtypeStruct(w.shape, BF16) for w in slabs],
        scratch_shapes=[pltpu.VMEM((D, 3 * ATTN_WIDTH), BF16)],
        compiler_params=pltpu.CompilerParams(
            dimension_semantics=("arbitrary", "arbitrary"), vmem_limit_bytes=V7X_VMEM_LIMIT_BYTES),
        name="ffn1_qkv",
    )(x, g1, wg, wu, wd, gm, w_in, *slabs)
    return outs[:5], [o.reshape(w.shape) for o, w in zip(outs[5:], later_weights)]


def _moba_attn_kernel(t31_ref, qt_ref, k_ref, vt_ref, diag_ref, sub_ref, a_ref,
                      kmhi_ref, kmlo_ref, rb_far_ref, rb_sub_ref, at_ref, qz_ref,
                      m_ref, l_ref, mblk_ref, alpha_ref, s_ref, p_ref):
    qi = pl.program_id(1)
    nblk = k_ref.shape[2] // MOBA_BLOCK
    nq = MOBA_BLOCK
    units = [(r, h) for r in range(ATTN_ROWS) for h in range(ATTN_HEADS)]

    @pl.when(qi == 0)
    def _():
        for r in range(ATTN_ROWS):
            blk_row = lax.broadcasted_iota(jnp.int32, (nblk, ATTN_WIDTH), 0)
            km = jnp.zeros((nblk, ATTN_WIDTH), F32)
            for j in range(nblk):
                mean_j = jnp.mean(k_ref[r, 0, j * MOBA_BLOCK:(j + 1) * MOBA_BLOCK, :].astype(F32), axis=0,
                                  keepdims=True)
                km = jnp.where(blk_row == j, mean_j, km)
            km = jnp.concatenate([km] * ATTN_HEADS, axis=0)
            row_head = lax.broadcasted_iota(jnp.int32, km.shape, 0) // nblk
            col_head = lax.broadcasted_iota(jnp.int32, km.shape, 1) // HEAD_DIM
            km = jnp.where(row_head == col_head, km, 0.0)
            hi = km.astype(BF16)
            kmhi_ref[r] = hi
            kmlo_ref[r] = (km - hi.astype(F32)).astype(BF16)

    jrow = lax.broadcasted_iota(jnp.int32, (nblk, nq), 0)
    for r in range(ATTN_ROWS):
        qt = qt_ref[r, 0]
        gates = _dot(kmhi_ref[r], qt) + _dot(kmlo_ref[r], qt)
        for h in range(ATTN_HEADS):
            u = r * ATTN_HEADS + h
            g = gates[h * nblk:(h + 1) * nblk]
            cnt = jnp.zeros((nblk, nq), jnp.int32)
            for jp in range(nblk):
                gb = jnp.broadcast_to(g[jp:jp + 1], (nblk, nq))
                beats = (gb > g) | ((gb == g) & (jp < jrow))
                cnt = cnt + jnp.where(beats, jnp.where(jp < qi, 1, 0), 0)
            sel = (jrow < qi) & (cnt < MOBA_TOPK)
            far = jnp.where(sel, t31_ref[h] * LOG2E, MASK_VALUE)
            sub = jnp.where(sel, 0.0, MASK_VALUE)
            for j in range(nblk):
                rb_far_ref[u * nblk + j] = far[j:j + 1]
                rb_sub_ref[u * nblk + j] = sub[j:j + 1]

    zeros_half = jnp.zeros((HEAD_DIM, nq), BF16)
    for r, h in units:
        qh = qt_ref[r, 0, h * HEAD_DIM:(h + 1) * HEAD_DIM, :]
        qz_ref[r * ATTN_HEADS + h] = jnp.concatenate([qh, zeros_half] if h % 2 == 0 else [zeros_half, qh], axis=0)

    def scores_stage(j, kind):
        start = pl.multiple_of(j * MOBA_BLOCK, MOBA_BLOCK)
        for u, (r, h) in enumerate(units):
            pair = h // 2
            kblk = k_ref[r, 0, pl.ds(start, MOBA_BLOCK), pair * 128:(pair + 1) * 128]
            s = _dot(kblk, qz_ref[u])
            if kind == "diag":
                s = s + diag_ref[h]
            elif kind == "sub":
                s = s + sub_ref[h]
            s_ref[u] = s
            mblk_ref[u] = jnp.max(s, axis=0, keepdims=True)

    def softmax_stage(row_bias):
        for u in range(len(units)):
            rb = row_bias(u)
            m_old = m_ref[u]
            m_blk = mblk_ref[u] if rb is None else mblk_ref[u] + rb
            m_new = jnp.maximum(m_old, m_blk)
            offset = m_new if rb is None else m_new - rb
            p_ref[u] = jnp.exp2(s_ref[u] - offset).astype(BF16)
            alpha_ref[u] = jnp.exp2(m_old - m_new)
            m_ref[u] = m_new

    ones_rows = jnp.ones((BF16_SUBLANES, nq), BF16)

    def values_stage(j):
        for u, (r, h) in enumerate(units):
            rows = slice(u * HEAD_DIM, (u + 1) * HEAD_DIM)
            v_h = vt_ref[r, 0, j, h * HEAD_DIM:(h + 1) * HEAD_DIM, :]
            pv = _dot(jnp.concatenate([v_h, ones_rows], axis=0), p_ref[u])
            alpha = alpha_ref[u]
            at_ref[rows] = alpha * at_ref[rows] + pv[:HEAD_DIM]
            l_ref[u] = alpha * l_ref[u] + pv[HEAD_DIM:HEAD_DIM + 1]

    m_ref[...] = jnp.full(m_ref.shape, MASK_VALUE, F32)
    l_ref[...] = jnp.zeros(l_ref.shape, F32)
    at_ref[...] = jnp.zeros(at_ref.shape, F32)
    j_sub = jnp.maximum(qi - 1, 0)
    n_far = j_sub

    scores_stage(qi, "diag")
    softmax_stage(lambda u: None)
    scores_stage(j_sub, "sub")
    values_stage(qi)
    softmax_stage(lambda u: rb_sub_ref[u * nblk + j_sub])
    scores_stage(0, "far")

    def far_body(t, carry):
        values_stage(jnp.where(t == 0, j_sub, t - 1))
        softmax_stage(lambda u: rb_far_ref[u * nblk + t])
        scores_stage(t + 1, "far")
        return carry

    lax.fori_loop(0, jnp.maximum(n_far - 1, 0), far_body, 0)

    @pl.when(n_far > 0)
    def _():
        values_stage(jnp.where(n_far == 1, j_sub, n_far - 2))
        softmax_stage(lambda u: rb_far_ref[u * nblk + n_far - 1])

    values_stage(jnp.where(n_far > 0, n_far - 1, j_sub))
    for u in range(len(units)):
        rows = slice(u * HEAD_DIM, (u + 1) * HEAD_DIM)
        at_ref[rows] = at_ref[rows] * (1.0 / l_ref[u])
    for r in range(ATTN_ROWS):
        a_ref[r, 0] = at_ref[r * ATTN_WIDTH:(r + 1) * ATTN_WIDTH, :].T.astype(BF16)


def _moba_attn(qt, k, vt, diag, sub, t31):
    B, S, _ = k.shape
    nblk = S // MOBA_BLOCK
    groups = B // ATTN_ROWS
    n_units = ATTN_ROWS * ATTN_HEADS
    split = lambda x: x.reshape((ATTN_ROWS, groups) + x.shape[1:])
    tile = pl.BlockSpec((ATTN_HEADS, MOBA_BLOCK, MOBA_BLOCK), lambda g, i: (0, 0, 0), pipeline_mode=pl.Buffered(1))
    a = pl.pallas_call(
        _moba_attn_kernel,
        grid=(groups, nblk),
        in_specs=[pl.BlockSpec(memory_space=pltpu.SMEM),
                  pl.BlockSpec((ATTN_ROWS, 1, ATTN_WIDTH, MOBA_BLOCK), lambda g, i: (0, g, 0, i)),
                  pl.BlockSpec((ATTN_ROWS, 1, S, ATTN_WIDTH), lambda g, i: (0, g, 0, 0)),
                  pl.BlockSpec((ATTN_ROWS, 1, nblk, ATTN_WIDTH, MOBA_BLOCK), lambda g, i: (0, g, 0, 0, 0)),
                  tile, tile],
        out_specs=pl.BlockSpec((ATTN_ROWS, 1, MOBA_BLOCK, ATTN_WIDTH), lambda g, i: (0, g, i, 0)),
        out_shape=jax.ShapeDtypeStruct((ATTN_ROWS, groups, S, ATTN_WIDTH), BF16),
        scratch_shapes=[pltpu.VMEM((ATTN_ROWS, ATTN_HEADS * nblk, ATTN_WIDTH), BF16),
                        pltpu.VMEM((ATTN_ROWS, ATTN_HEADS * nblk, ATTN_WIDTH), BF16),
                        pltpu.VMEM((n_units * nblk, 1, MOBA_BLOCK), F32),
                        pltpu.VMEM((n_units * nblk, 1, MOBA_BLOCK), F32),
                        pltpu.VMEM((n_units * HEAD_DIM, MOBA_BLOCK), F32),
                        pltpu.VMEM((n_units, 2 * HEAD_DIM, MOBA_BLOCK), BF16),
                        *[pltpu.VMEM((n_units, 1, MOBA_BLOCK), F32)] * 4,
                        pltpu.VMEM((n_units, MOBA_BLOCK, MOBA_BLOCK), F32),
                        pltpu.VMEM((n_units, MOBA_BLOCK, MOBA_BLOCK), BF16)],
        compiler_params=pltpu.CompilerParams(
            dimension_semantics=("arbitrary", "arbitrary"), vmem_limit_bytes=V7X_VMEM_LIMIT_BYTES),
        name="moba_attn",
    )(t31, split(qt), split(k), split(vt), diag, sub)
    return a.reshape(B, S, ATTN_WIDTH)


def _mix_merge_kernel(h1_ref, un_ref, unprev_ref, a_ref, win_ref, wgrp_ref, scale_ref,
                      wa_ref, wp_ref, wout_ref, h2_ref, zext_ref):
    i = pl.program_id(1)
    tm = MIX_TOKENS
    z0 = 3 * ATTN_WIDTH
    for c in range(MIX_SUBTILES):
        rows = slice(c * tm, (c + 1) * tm)
        zx_ref = zext_ref.at[c]
        zg = _dot(un_ref[0, rows], win_ref[:, z0:])
        halo = unprev_ref[0] if c == 0 else un_ref[0, c * tm - POOL_HALO:c * tm]
        zprev = _dot(halo, win_ref[:, z0:z0 + POOL_WIDTH])
        zx_ref[:POOL_HALO] = jnp.where(i > 0, zprev, 0.0) if c == 0 else zprev
        zx_ref[POOL_HALO:] = zg[:, :POOL_WIDTH]

        tpos = (i * MIX_SUBTILES + c) * tm + lax.broadcasted_iota(jnp.int32, (tm, POOL_GROUP_WIDTH), 0)
        mixed = []
        for g, w in enumerate(POOL_WINDOWS):
            cols = slice(g * POOL_GROUP_WIDTH, (g + 1) * POOL_GROUP_WIDTH)
            wsum = zx_ref[POOL_HALO:, cols]
            for lag in range(1, w):
                wsum = wsum + zx_ref[POOL_HALO - lag:POOL_HALO - lag + tm, cols]
            mean = wsum / jnp.minimum(tpos + 1, w).astype(F32)
            pooled = (mean - zx_ref[POOL_HALO:, cols]).astype(BF16)
            mixed.append(_dot(pooled, wgrp_ref[g]))
        p = (jnp.concatenate(mixed, axis=1) * scale_ref[...]).astype(BF16)

        g_attn = zg[:, POOL_WIDTH:POOL_WIDTH + D_MODEL]
        g_pool = zg[:, POOL_WIDTH + D_MODEL:]
        merged = (jax.nn.sigmoid(g_attn) * _dot(a_ref[0, rows], wa_ref[...])
                  + jax.nn.sigmoid(g_pool) * _dot(p, wp_ref[...]))
        h2_ref[0, rows] = h1_ref[0, rows] + _dot(merged.astype(BF16), wout_ref[...])


def _mix_merge(h1, un, a, win, wgrp, scale, wa, wp, wout):
    B, S, D = h1.shape
    tm = MIX_SUBTILES * MIX_TOKENS
    halo_per_tile = tm // POOL_HALO
    return pl.pallas_call(
        _mix_merge_kernel,
        grid=(B, S // tm),
        in_specs=[pl.BlockSpec((1, tm, D), lambda b, i: (b, i, 0)),
                  pl.BlockSpec((1, tm, D), lambda b, i: (b, i, 0)),
                  pl.BlockSpec((1, POOL_HALO, D), lambda b, i: (b, jnp.maximum(i * halo_per_tile - 1, 0), 0)),
                  pl.BlockSpec((1, tm, ATTN_WIDTH), lambda b, i: (b, i, 0)),
                  _resident(win.shape), _resident(wgrp.shape), _resident(scale.shape),
                  _resident(wa.shape), _resident(wp.shape), _resident(wout.shape)],
        out_specs=pl.BlockSpec((1, tm, D), lambda b, i: (b, i, 0)),
        out_shape=jax.ShapeDtypeStruct((B, S, D), F32),
        scratch_shapes=[pltpu.VMEM((MIX_SUBTILES, POOL_HALO + MIX_TOKENS, POOL_WIDTH), F32)],
        compiler_params=pltpu.CompilerParams(
            dimension_semantics=("arbitrary", "arbitrary"), vmem_limit_bytes=V7X_VMEM_LIMIT_BYTES),
        name="mix_merge",
    )(h1, un, un, a, win, wgrp, scale, wa, wp, wout)


def _ffn2_final_kernel(h_ref, g2_ref, wg_ref, wu_ref, wd_ref, gf_ref, out_ref):
    for c in range(FFN2_SUBTILES):
        rows = slice(c * FFN_TOKENS, (c + 1) * FFN_TOKENS)
        h3 = _swiglu_half_step(h_ref[0, rows], g2_ref, wg_ref, wu_ref, wd_ref)
        out_ref[0, rows] = _rms(h3, gf_ref[...])


def _ffn2_final(h, g2, wg, wu, wd, gf):
    B, S, D = h.shape
    tm = FFN2_SUBTILES * FFN_TOKENS
    return pl.pallas_call(
        _ffn2_final_kernel,
        grid=(B, S // tm),
        in_specs=[pl.BlockSpec((1, tm, D), lambda b, i: (b, i, 0)),
                  _resident(g2.shape), _resident(wg.shape), _resident(wu.shape), _resident(wd.shape),
                  _resident(gf.shape)],
        out_specs=pl.BlockSpec((1, tm, D), lambda b, i: (b, i, 0)),
        out_shape=jax.ShapeDtypeStruct((B, S, D), F32),
        compiler_params=pltpu.CompilerParams(
            dimension_semantics=("arbitrary", "arbitrary"), vmem_limit_bytes=V7X_VMEM_LIMIT_BYTES),
        name="ffn2_final",
    )(h, g2, wg, wu, wd, gf)


def kernel(x, ffn1_norm, ffn1_w_gate, ffn1_w_up, ffn1_w_down, mix_norm, w_in, pool_w_group, pool_scale,
           w_branch_attn, w_branch_pool, w_out, ffn2_norm, ffn2_w_gate, ffn2_w_up, ffn2_w_down,
           rpb_table, final_norm):
    B, S, D = x.shape
    assert (D, ffn1_w_gate.shape[0]) == (D_MODEL, 1)
    assert S % MOBA_BLOCK == 0 and S % (FFN2_SUBTILES * FFN_TOKENS) == 0 and S % (MIX_SUBTILES * MIX_TOKENS) == 0
    assert B % ATTN_ROWS == 0
    bf = lambda w: w.astype(BF16)
    row = lambda v: v.reshape(1, -1)
    w_in = w_in[0]

    diag, sub = _rpb_bias(rpb_table)
    (h1, un, qt, k, vt), later = _ffn1_qkv(
        x, row(ffn1_norm[0]), bf(ffn1_w_gate[0]), bf(ffn1_w_up[0]), bf(ffn1_w_down[0]), row(mix_norm[0]), w_in,
        [w_in, w_branch_attn[0], w_branch_pool[0], w_out[0], ffn2_w_gate[0], ffn2_w_up[0], ffn2_w_down[0]])
    win_bf, wa_bf, wp_bf, wout_bf, wg2_bf, wu2_bf, wd2_bf = later
    a = _moba_attn(qt, k, vt, diag, sub, rpb_table[RPB_BUCKETS - 1])
    h2 = _mix_merge(h1, un, a, win_bf, bf(pool_w_group[0]), row(pool_scale[0]), wa_bf, wp_bf, wout_bf)
    return _ffn2_final(h2, row(ffn2_norm[0]), wg2_bf, wu2_bf, wd2_bf, row(final_norm))
```

```python
import math

import jax
import jax.numpy as jnp
from jax import lax
from jax.experimental import pallas as pl
from jax.experimental.pallas import tpu as pltpu

D_MODEL = 1024
HEAD_DIM = 64
ATTN_WIDTH = 512
ATTN_HEADS = 8
MOBA_BLOCK = 256
MOBA_TOPK = 3
POOL_WINDOWS = (2, 4, 8, 16)
POOL_WIDTH = 512
POOL_GROUP_WIDTH = 128
POOL_HALO = 16
RPB_BUCKETS = 32
RPB_MAX_DISTANCE = 128
RMS_EPS = 1e-6
MASK_VALUE = -1e30
LOG2E = math.log2(math.e)
Q_SCALE = HEAD_DIM ** -0.5 * LOG2E
ATTN_ROWS = 2
V7X_VMEM_LIMIT_BYTES = 56 * 1024 * 1024

FFN_TOKENS = 512
FFN2_SUBTILES = 2
MIX_TOKENS = 512
MIX_SUBTILES = 2
N_LATER_WEIGHTS = 7
BF16_SUBLANES = 16

BF16 = jnp.bfloat16
F32 = jnp.float32


def _resident(shape):
    nd = len(shape)
    return pl.BlockSpec(shape, lambda *_: (0,) * nd, pipeline_mode=pl.Buffered(1))


def _dot(a, b):
    return jnp.dot(a, b, preferred_element_type=F32)


def _rms(x, g):
    return x * lax.rsqrt(jnp.mean(x * x, axis=-1, keepdims=True) + RMS_EPS) * g


def _swiglu_half_step(x, g_ref, wg_ref, wu_ref, wd_ref):
    xn = _rms(x, g_ref[...]).astype(BF16)
    gate = _dot(xn, wg_ref[...])
    up = _dot(xn, wu_ref[...])
    act = (gate * jax.nn.sigmoid(gate) * up).astype(BF16)
    return x + 0.5 * _dot(act, wd_ref[...])


def _rpb_bucket(dist):
    n = jnp.maximum(dist, 0)
    max_exact = RPB_BUCKETS // 2
    nf = jnp.maximum(n, 1).astype(F32)
    large = max_exact + jnp.floor(jnp.log(nf / max_exact) / math.log(RPB_MAX_DISTANCE / max_exact)
                                  * (RPB_BUCKETS - max_exact)).astype(jnp.int32)
    large = jnp.minimum(large, RPB_BUCKETS - 1)
    return jnp.where(n < max_exact, n, large)


def _rpb_bias_kernel(table_ref, diag_ref, sub_ref):
    h = pl.program_id(0)
    width = 4 * MOBA_BLOCK
    sublanes = 8
    dist = lax.broadcasted_iota(jnp.int32, (sublanes, width), 1) - MOBA_BLOCK
    bucket = _rpb_bucket(dist)
    f = jnp.zeros((sublanes, width), F32)
    for b in range(RPB_BUCKETS):
        f = jnp.where(bucket == b, table_ref[b, h], f)
    f = f * LOG2E
    sub = lax.broadcasted_iota(jnp.int32, (sublanes, width), 0)
    group = f
    for r in range(1, sublanes):
        group = jnp.where(sub == r, pltpu.roll(f, r, 1), group)
    ql = lax.broadcasted_iota(jnp.int32, (sublanes, MOBA_BLOCK), 1)
    kl0 = lax.broadcasted_iota(jnp.int32, (sublanes, MOBA_BLOCK), 0)
    for g in range(MOBA_BLOCK // sublanes):
        rows = slice(g * sublanes, (g + 1) * sublanes)
        blk = pltpu.roll(group, g * sublanes, 1) if g else group
        diag_ref[0, rows] = jnp.where(ql >= kl0 + g * sublanes, blk[:, MOBA_BLOCK:2 * MOBA_BLOCK], MASK_VALUE)
        sub_ref[0, rows] = blk[:, 2 * MOBA_BLOCK:3 * MOBA_BLOCK]


def _prep_kernel(table_ref, *refs):
    n = (len(refs) - 2) // 2
    weights_f32, (diag_ref, sub_ref), weights_bf16 = refs[:n], refs[n:n + 2], refs[n + 2:]
    _rpb_bias_kernel(table_ref, diag_ref, sub_ref)
    for src_ref, dst_ref in zip(weights_f32, weights_bf16):
        dst_ref[...] = src_ref[...].astype(BF16)


def _prep(rpb_table, weights):
    steps = ATTN_HEADS
    tile = jax.ShapeDtypeStruct((ATTN_HEADS, MOBA_BLOCK, MOBA_BLOCK), F32)
    tile_spec = pl.BlockSpec((1, MOBA_BLOCK, MOBA_BLOCK), lambda h: (h, 0, 0))
    slabs = [w.reshape(steps, w.shape[0] // steps, w.shape[1]) for w in weights]
    assert all(w.shape[1] % BF16_SUBLANES == 0 for w in slabs)
    slab_specs = [pl.BlockSpec((1,) + w.shape[1:], lambda h: (h, 0, 0)) for w in slabs]
    outs = pl.pallas_call(
        _prep_kernel,
        grid=(steps,),
        in_specs=[pl.BlockSpec(memory_space=pltpu.SMEM), *slab_specs],
        out_specs=[tile_spec, tile_spec, *slab_specs],
        out_shape=[tile, tile] + [jax.ShapeDtypeStruct(w.shape, BF16) for w in slabs],
        compiler_params=pltpu.CompilerParams(
            dimension_semantics=("arbitrary",), vmem_limit_bytes=V7X_VMEM_LIMIT_BYTES),
        name="prep",
    )(rpb_table, *slabs)
    return outs[0], outs[1], [o.reshape(w.shape) for o, w in zip(outs[2:], weights)]


def _ffn1_qkv_kernel(x_ref, g1_ref, wg_ref, wu_ref, wd_ref, gm_ref, wqkv_f32_ref, *refs):
    later_f32, (h1_ref, un_ref, qt_ref, k_ref, vt_ref), later_bf16, wqkv_ref = (
        refs[:N_LATER_WEIGHTS], refs[N_LATER_WEIGHTS:N_LATER_WEIGHTS + 5], refs[N_LATER_WEIGHTS + 5:-1], refs[-1])

    @pl.when((pl.program_id(0) == 0) & (pl.program_id(1) == 0))
    def _():
        wqkv_ref[...] = wqkv_f32_ref[...].astype(BF16)

    for src_ref, dst_ref in zip(later_f32, later_bf16):
        dst_ref[...] = src_ref[...].astype(BF16)
    h1 = _swiglu_half_step(x_ref[0], g1_ref, wg_ref, wu_ref, wd_ref)
    h1_ref[0] = h1
    un = _rms(h1, gm_ref[...]).astype(BF16)
    un_ref[0] = un
    qkv = _dot(un, wqkv_ref[...])
    qt_ref[0] = (qkv[:, :ATTN_WIDTH] * Q_SCALE).T.astype(BF16)
    k_ref[0] = qkv[:, ATTN_WIDTH:2 * ATTN_WIDTH].astype(BF16)
    for c in range(FFN_TOKENS // MOBA_BLOCK):
        vt_ref[0, c] = qkv[c * MOBA_BLOCK:(c + 1) * MOBA_BLOCK, 2 * ATTN_WIDTH:].T.astype(BF16)


def _ffn1_qkv(x, g1, wg, wu, wd, gm, w_in, later_weights):
    B, S, D = x.shape
    tm = FFN_TOKENS
    nblk = S // MOBA_BLOCK
    tiles_per_seq = S // tm
    steps = B * tiles_per_seq
    assert len(later_weights) == N_LATER_WEIGHTS
    step_of = lambda b, i: b * tiles_per_seq + i

    def slab_view(w):
        n = max(n for n in range(1, steps + 1) if steps % n == 0 and w.shape[0] % (n * BF16_SUBLANES) == 0)
        view = w.reshape(n, w.shape[0] // n, w.shape[1])
        return view, pl.BlockSpec((1,) + view.shape[1:], lambda b, i: (step_of(b, i) // (steps // n), 0, 0))

    slabs, slab_specs = zip(*[slab_view(w) for w in later_weights])
    tok = lambda width, dt: (pl.BlockSpec((1, tm, width), lambda b, i: (b, i, 0)),
                             jax.ShapeDtypeStruct((B, S, width), dt))
    h1_spec, h1_shape = tok(D, F32)
    un_spec, un_shape = tok(D, BF16)
    k_spec, k_shape = tok(ATTN_WIDTH, BF16)
    outs = pl.pallas_call(
        _ffn1_qkv_kernel,
        grid=(B, tiles_per_seq),
        in_specs=[pl.BlockSpec((1, tm, D), lambda b, i: (b, i, 0)),
                  _resident(g1.shape), _resident(wg.shape), _resident(wu.shape), _resident(wd.shape),
                  _resident(gm.shape), _resident((D, 3 * ATTN_WIDTH)), *slab_specs],
        out_specs=[h1_spec, un_spec,
                   pl.BlockSpec((1, ATTN_WIDTH, tm), lambda b, i: (b, 0, i)),
                   k_spec,
                   pl.BlockSpec((1, tm // MOBA_BLOCK, ATTN_WIDTH, MOBA_BLOCK), lambda b, i: (b, i, 0, 0)),
                   *slab_specs],
        out_shape=[h1_shape, un_shape,
                   jax.ShapeDtypeStruct((B, ATTN_WIDTH, S), BF16),
                   k_shape,
                   jax.ShapeDtypeStruct((B, nblk, ATTN_WIDTH, MOBA_BLOCK), BF16)]
                  + [jax.ShapeDtypeStruct(w.shape, BF16) for w in slabs],
        scratch_shapes=[pltpu.VMEM((D, 3 * ATTN_WIDTH), BF16)],
        compiler_params=pltpu.CompilerParams(
            dimension_semantics=("arbitrary", "arbitrary"), vmem_limit_bytes=V7X_VMEM_LIMIT_BYTES),
        name="ffn1_qkv",
    )(x, g1, wg, wu, wd, gm, w_in, *slabs)
    return outs[:5], [o.reshape(w.shape) for o, w in zip(outs[5:], later_weights)]


def _moba_attn_kernel(t31_ref, qt_ref, k_ref, vt_ref, diag_ref, sub_ref, a_ref,
                      kmhi_ref, kmlo_ref, rb_far_ref, rb_sub_ref, at_ref, qz_ref,
                      m_ref, l_ref, mblk_ref, alpha_ref, s_ref, p_ref):
    qi = pl.program_id(1)
    nblk = k_ref.shape[2] // MOBA_BLOCK
    nq = MOBA_BLOCK
    units = [(r, h) for r in range(ATTN_ROWS) for h in range(ATTN_HEADS)]

    @pl.when(qi == 0)
    def _():
        for r in range(ATTN_ROWS):
            blk_row = lax.broadcasted_iota(jnp.int32, (nblk, ATTN_WIDTH), 0)
            km = jnp.zeros((nblk, ATTN_WIDTH), F32)
            for j in range(nblk):
                mean_j = jnp.mean(k_ref[r, 0, j * MOBA_BLOCK:(j + 1) * MOBA_BLOCK, :].astype(F32), axis=0,
                                  keepdims=True)
                km = jnp.where(blk_row == j, mean_j, km)
            km = jnp.concatenate([km] * ATTN_HEADS, axis=0)
            row_head = lax.broadcasted_iota(jnp.int32, km.shape, 0) // nblk
            col_head = lax.broadcasted_iota(jnp.int32, km.shape, 1) // HEAD_DIM
            km = jnp.where(row_head == col_head, km, 0.0)
            hi = km.astype(BF16)
            kmhi_ref[r] = hi
            kmlo_ref[r] = (km - hi.astype(F32)).astype(BF16)

    jrow = lax.broadcasted_iota(jnp.int32, (nblk, nq), 0)
    for r in range(ATTN_ROWS):
        qt = qt_ref[r, 0]
        gates = _dot(kmhi_ref[r], qt) + _dot(kmlo_ref[r], qt)
        for h in range(ATTN_HEADS):
            u = r * ATTN_HEADS + h
            g = gates[h * nblk:(h + 1) * nblk]
            cnt = jnp.zeros((nblk, nq), jnp.int32)
            for jp in range(nblk):
                gb = jnp.broadcast_to(g[jp:jp + 1], (nblk, nq))
                beats = (gb > g) | ((gb == g) & (jp < jrow))
                cnt = cnt + jnp.where(beats, jnp.where(jp < qi, 1, 0), 0)
            sel = (jrow < qi) & (cnt < MOBA_TOPK)
            far = jnp.where(sel, t31_ref[h] * LOG2E, MASK_VALUE)
            sub = jnp.where(sel, 0.0, MASK_VALUE)
            for j in range(nblk):
                rb_far_ref[u * nblk + j] = far[j:j + 1]
                rb_sub_ref[u * nblk + j] = sub[j:j + 1]

    zeros_half = jnp.zeros((HEAD_DIM, nq), BF16)
    for r, h in units:
        qh = qt_ref[r, 0, h * HEAD_DIM:(h + 1) * HEAD_DIM, :]
        qz_ref[r * ATTN_HEADS + h] = jnp.concatenate([qh, zeros_half] if h % 2 == 0 else [zeros_half, qh], axis=0)

    def scores_stage(j, kind):
        start = pl.multiple_of(j * MOBA_BLOCK, MOBA_BLOCK)
        for u, (r, h) in enumerate(units):
            pair = h // 2
            kblk = k_ref[r, 0, pl.ds(start, MOBA_BLOCK), pair * 128:(pair + 1) * 128]
            s = _dot(kblk, qz_ref[u])
            if kind == "diag":
                s = s + diag_ref[h]
            elif kind == "sub":
                s = s + sub_ref[h]
            s_ref[u] = s
            mblk_ref[u] = jnp.max(s, axis=0, keepdims=True)

    def softmax_stage(row_bias):
        for u in range(len(units)):
            rb = row_bias(u)
            m_old = m_ref[u]
            m_blk = mblk_ref[u] if rb is None else mblk_ref[u] + rb
            m_new = jnp.maximum(m_old, m_blk)
            offset = m_new if rb is None else m_new - rb
            p_ref[u] = jnp.exp2(s_ref[u] - offset).astype(BF16)
            alpha_ref[u] = jnp.exp2(m_old - m_new)
            m_ref[u] = m_new

    ones_rows = jnp.ones((BF16_SUBLANES, nq), BF16)

    def values_stage(j):
        for u, (r, h) in enumerate(units):
            rows = slice(u * HEAD_DIM, (u + 1) * HEAD_DIM)
            v_h = vt_ref[r, 0, j, h * HEAD_DIM:(h + 1) * HEAD_DIM, :]
            pv = _dot(jnp.concatenate([v_h, ones_rows], axis=0), p_ref[u])
            alpha = alpha_ref[u]
            at_ref[rows] = alpha * at_ref[rows] + pv[:HEAD_DIM]
            l_ref[u] = alpha * l_ref[u] + pv[HEAD_DIM:HEAD_DIM + 1]

    m_ref[...] = jnp.full(m_ref.shape, MASK_VALUE, F32)
    l_ref[...] = jnp.zeros(l_ref.shape, F32)
    at_ref[...] = jnp.zeros(at_ref.shape, F32)
    j_sub = jnp.maximum(qi - 1, 0)
    n_far = j_sub

    scores_stage(qi, "diag")
    softmax_stage(lambda u: None)
    scores_stage(j_sub, "sub")
    values_stage(qi)
    softmax_stage(lambda u: rb_sub_ref[u * nblk + j_sub])
    scores_stage(0, "far")

    def far_body(t, carry):
        values_stage(jnp.where(t == 0, j_sub, t - 1))
        softmax_stage(lambda u: rb_far_ref[u * nblk + t])
        scores_stage(t + 1, "far")
        return carry

    lax.fori_loop(0, jnp.maximum(n_far - 1, 0), far_body, 0)

    @pl.when(n_far > 0)
    def _():
        values_stage(jnp.where(n_far == 1, j_sub, n_far - 2))
        softmax_stage(lambda u: rb_far_ref[u * nblk + n_far - 1])

    values_stage(jnp.where(n_far > 0, n_far - 1, j_sub))
    for u in range(len(units)):
        rows = slice(u * HEAD_DIM, (u + 1) * HEAD_DIM)
        at_ref[rows] = at_ref[rows] * (1.0 / l_ref[u])
    for r in range(ATTN_ROWS):
        a_ref[r, 0] = at_ref[r * ATTN_WIDTH:(r + 1) * ATTN_WIDTH, :].T.astype(BF16)


def _moba_attn(qt, k, vt, diag, sub, t31):
    B, S, _ = k.shape
    nblk = S // MOBA_BLOCK
    groups = B // ATTN_ROWS
    n_units = ATTN_ROWS * ATTN_HEADS
    split = lambda x: x.reshape((ATTN_ROWS, groups) + x.shape[1:])
    tile = pl.BlockSpec((ATTN_HEADS, MOBA_BLOCK, MOBA_BLOCK), lambda g, i: (0, 0, 0), pipeline_mode=pl.Buffered(1))
    a = pl.pallas_call(
        _moba_attn_kernel,
        grid=(groups, nblk),
        in_specs=[pl.BlockSpec(memory_space=pltpu.SMEM),
                  pl.BlockSpec((ATTN_ROWS, 1, ATTN_WIDTH, MOBA_BLOCK), lambda g, i: (0, g, 0, i)),
                  pl.BlockSpec((ATTN_ROWS, 1, S, ATTN_WIDTH), lambda g, i: (0, g, 0, 0)),
                  pl.BlockSpec((ATTN_ROWS, 1, nblk, ATTN_WIDTH, MOBA_BLOCK), lambda g, i: (0, g, 0, 0, 0)),
                  tile, tile],
        out_specs=pl.BlockSpec((ATTN_ROWS, 1, MOBA_BLOCK, ATTN_WIDTH), lambda g, i: (0, g, i, 0)),
        out_shape=jax.ShapeDtypeStruct((ATTN_ROWS, groups, S, ATTN_WIDTH), BF16),
        scratch_shapes=[pltpu.VMEM((ATTN_ROWS, ATTN_HEADS * nblk, ATTN_WIDTH), BF16),
                        pltpu.VMEM((ATTN_ROWS, ATTN_HEADS * nblk, ATTN_WIDTH), BF16),
                        pltpu.VMEM((n_units * nblk, 1, MOBA_BLOCK), F32),
                        pltpu.VMEM((n_units * nblk, 1, MOBA_BLOCK), F32),
                        pltpu.VMEM((n_units * HEAD_DIM, MOBA_BLOCK), F32),
                        pltpu.VMEM((n_units, 2 * HEAD_DIM, MOBA_BLOCK), BF16),
                        *[pltpu.VMEM((n_units, 1, MOBA_BLOCK), F32)] * 4,
                        pltpu.VMEM((n_units, MOBA_BLOCK, MOBA_BLOCK), F32),
                        pltpu.VMEM((n_units, MOBA_BLOCK, MOBA_BLOCK), BF16)],
        compiler_params=pltpu.CompilerParams(
            dimension_semantics=("arbitrary", "arbitrary"), vmem_limit_bytes=V7X_VMEM_LIMIT_BYTES),
        name="moba_attn",
    )(t31, split(qt), split(k), split(vt), diag, sub)
    return a.reshape(B, S, ATTN_WIDTH)


def _mix_merge_kernel(h1_ref, un_ref, unprev_ref, a_ref, win_ref, wgrp_ref, scale_ref,
                      wa_ref, wp_ref, wout_ref, h2_ref, zext_ref):
    i = pl.program_id(1)
    tm = MIX_TOKENS
    z0 = 3 * ATTN_WIDTH
    for c in range(MIX_SUBTILES):
        rows = slice(c * tm, (c + 1) * tm)
        zx_ref = zext_ref.at[c]
        zg = _dot(un_ref[0, rows], win_ref[:, z0:])
        halo = unprev_ref[0] if c == 0 else un_ref[0, c * tm - POOL_HALO:c * tm]
        zprev = _dot(halo, win_ref[:, z0:z0 + POOL_WIDTH])
        zx_ref[:POOL_HALO] = jnp.where(i > 0, zprev, 0.0) if c == 0 else zprev
        zx_ref[POOL_HALO:] = zg[:, :POOL_WIDTH]

        tpos = (i * MIX_SUBTILES + c) * tm + lax.broadcasted_iota(jnp.int32, (tm, POOL_GROUP_WIDTH), 0)
        mixed = []
        for g, w in enumerate(POOL_WINDOWS):
            cols = slice(g * POOL_GROUP_WIDTH, (g + 1) * POOL_GROUP_WIDTH)
            wsum = zx_ref[POOL_HALO:, cols]
            for lag in range(1, w):
                wsum = wsum + zx_ref[POOL_HALO - lag:POOL_HALO - lag + tm, cols]
            mean = wsum / jnp.minimum(tpos + 1, w).astype(F32)
            pooled = (mean - zx_ref[POOL_HALO:, cols]).astype(BF16)
            mixed.append(_dot(pooled, wgrp_ref[g]))
        p = (jnp.concatenate(mixed, axis=1) * scale_ref[...]).astype(BF16)

        g_attn = zg[:, POOL_WIDTH:POOL_WIDTH + D_MODEL]
        g_pool = zg[:, POOL_WIDTH + D_MODEL:]
        merged = (jax.nn.sigmoid(g_attn) * _dot(a_ref[0, rows], wa_ref[...])
                  + jax.nn.sigmoid(g_pool) * _dot(p, wp_ref[...]))
        h2_ref[0, rows] = h1_ref[0, rows] + _dot(merged.astype(BF16), wout_ref[...])


def _mix_merge(h1, un, a, win, wgrp, scale, wa, wp, wout):
    B, S, D = h1.shape
    tm = MIX_SUBTILES * MIX_TOKENS
    halo_per_tile = tm // POOL_HALO
    return pl.pallas_call(
        _mix_merge_kernel,
        grid=(B, S // tm),
        in_specs=[pl.BlockSpec((1, tm, D), lambda b, i: (b, i, 0)),
                  pl.BlockSpec((1, tm, D), lambda b, i: (b, i, 0)),
                  pl.BlockSpec((1, POOL_HALO, D), lambda b, i: (b, jnp.maximum(i * halo_per_tile - 1, 0), 0)),
                  pl.BlockSpec((1, tm, ATTN_WIDTH), lambda b, i: (b, i, 0)),
                  _resident(win.shape), _resident(wgrp.shape), _resident(scale.shape),
                  _resident(wa.shape), _resident(wp.shape), _resident(wout.shape)],
        out_specs=pl.BlockSpec((1, tm, D), lambda b, i: (b, i, 0)),
        out_shape=jax.ShapeDtypeStruct((B, S, D), F32),
        scratch_shapes=[pltpu.VMEM((MIX_SUBTILES, POOL_HALO + MIX_TOKENS, POOL_WIDTH), F32)],
        compiler_params=pltpu.CompilerParams(
            dimension_semantics=("arbitrary", "arbitrary"), vmem_limit_bytes=V7X_VMEM_LIMIT_BYTES),
        name="mix_merge",
    )(h1, un, un, a, win, wgrp, scale, wa, wp, wout)


def _ffn2_final_kernel(h_ref, g2_ref, wg_ref, wu_ref, wd_ref, gf_ref, out_ref):
    for c in range(FFN2_SUBTILES):
        rows = slice(c * FFN_TOKENS, (c + 1) * FFN_TOKENS)
        h3 = _swiglu_half_step(h_ref[0, rows], g2_ref, wg_ref, wu_ref, wd_ref)
        out_ref[0, rows] = _rms(h3, gf_ref[...])


def _ffn2_final(h, g2, wg, wu, wd, gf):
    B, S, D = h.shape
    tm = FFN2_SUBTILES * FFN_TOKENS
    return pl.pallas_call(
        _ffn2_final_kernel,
        grid=(B, S // tm),
        in_specs=[pl.BlockSpec((1, tm, D), lambda b, i: (b, i, 0)),
                  _resident(g2.shape), _resident(wg.shape), _resident(wu.shape), _resident(wd.shape),
                  _resident(gf.shape)],
        out_specs=pl.BlockSpec((1, tm, D), lambda b, i: (b, i, 0)),
        out_shape=jax.ShapeDtypeStruct((B, S, D), F32),
        compiler_params=pltpu.CompilerParams(
            dimension_semantics=("arbitrary", "arbitrary"), vmem_limit_bytes=V7X_VMEM_LIMIT_BYTES),
        name="ffn2_final",
    )(h, g2, wg, wu, wd, gf)


def kernel(x, ffn1_norm, ffn1_w_gate, ffn1_w_up, ffn1_w_down, mix_norm, w_in, pool_w_group, pool_scale,
           w_branch_attn, w_branch_pool, w_out, ffn2_norm, ffn2_w_gate, ffn2_w_up, ffn2_w_down,
           rpb_table, final_norm):
    B, S, D = x.shape
    assert (D, ffn1_w_gate.shape[0]) == (D_MODEL, 1)
    assert S % MOBA_BLOCK == 0 and S % (FFN2_SUBTILES * FFN_TOKENS) == 0 and S % (MIX_SUBTILES * MIX_TOKENS) == 0
    assert B % ATTN_ROWS == 0
    bf = lambda w: w.astype(BF16)
    row = lambda v: v.reshape(1, -1)
    w_in = w_in[0]

    diag, sub, (wg1_bf, wu1_bf, wd1_bf) = _prep(rpb_table, [ffn1_w_gate[0], ffn1_w_up[0], ffn1_w_down[0]])
    (h1, un, qt, k, vt), later = _ffn1_qkv(
        x, row(ffn1_norm[0]), wg1_bf, wu1_bf, wd1_bf, row(mix_norm[0]), w_in,
        [w_in, w_branch_attn[0], w_branch_pool[0], w_out[0], ffn2_w_gate[0], ffn2_w_up[0], ffn2_w_down[0]])
    win_bf, wa_bf, wp_bf, wout_bf, wg2_bf, wu2_bf, wd2_bf = later
    a = _moba_attn(qt, k, vt, diag, sub, rpb_table[RPB_BUCKETS - 1])
    h2 = _mix_merge(h1, un, a, win_bf, bf(pool_w_group[0]), row(pool_scale[0]), wa_bf, wp_bf, wout_bf)
    return _ffn2_final(h2, row(ffn2_norm[0]), wg2_bf, wu2_bf, wd2_bf, row(final_norm))
```

```python
import math

import jax
import jax.numpy as jnp
from jax import lax
from jax.experimental import pallas as pl
from jax.experimental.pallas import tpu as pltpu

D_MODEL = 1024
HEAD_DIM = 64
ATTN_WIDTH = 512
ATTN_HEADS = 8
MOBA_BLOCK = 256
MOBA_TOPK = 3
POOL_WINDOWS = (2, 4, 8, 16)
POOL_WIDTH = 512
POOL_GROUP_WIDTH = 128
POOL_HALO = 16
RPB_BUCKETS = 32
RPB_MAX_DISTANCE = 128
RMS_EPS = 1e-6
MASK_VALUE = -1e30
LOG2E = math.log2(math.e)
Q_SCALE = HEAD_DIM ** -0.5 * LOG2E
ATTN_ROWS = 2
V7X_VMEM_LIMIT_BYTES = 56 * 1024 * 1024

FFN_TOKENS = 512
FFN2_SUBTILES = 2
MIX_TOKENS = 512
MIX_SUBTILES = 2
N_LATER_WEIGHTS = 7
BF16_SUBLANES = 16
PREP_STEPS_PER_HEAD = 2

BF16 = jnp.bfloat16
F32 = jnp.float32


def _resident(shape):
    nd = len(shape)
    return pl.BlockSpec(shape, lambda *_: (0,) * nd, pipeline_mode=pl.Buffered(1))


def _dot(a, b):
    return jnp.dot(a, b, preferred_element_type=F32)


def _rms(x, g):
    return x * lax.rsqrt(jnp.mean(x * x, axis=-1, keepdims=True) + RMS_EPS) * g


def _swiglu_half_step(x, g_ref, wg_ref, wu_ref, wd_ref):
    xn = _rms(x, g_ref[...]).astype(BF16)
    gate = _dot(xn, wg_ref[...])
    up = _dot(xn, wu_ref[...])
    act = (gate * jax.nn.sigmoid(gate) * up).astype(BF16)
    return x + 0.5 * _dot(act, wd_ref[...])


def _rpb_bucket(dist):
    n = jnp.maximum(dist, 0)
    max_exact = RPB_BUCKETS // 2
    nf = jnp.maximum(n, 1).astype(F32)
    large = max_exact + jnp.floor(jnp.log(nf / max_exact) / math.log(RPB_MAX_DISTANCE / max_exact)
                                  * (RPB_BUCKETS - max_exact)).astype(jnp.int32)
    large = jnp.minimum(large, RPB_BUCKETS - 1)
    return jnp.where(n < max_exact, n, large)


def _rpb_bias_tiles(table_ref, h, diag_ref, sub_ref):
    width = 4 * MOBA_BLOCK
    sublanes = 8
    dist = lax.broadcasted_iota(jnp.int32, (sublanes, width), 1) - MOBA_BLOCK
    bucket = _rpb_bucket(dist)
    f = jnp.zeros((sublanes, width), F32)
    for b in range(RPB_BUCKETS):
        f = jnp.where(bucket == b, table_ref[b, h], f)
    f = f * LOG2E
    sub = lax.broadcasted_iota(jnp.int32, (sublanes, width), 0)
    group = f
    for r in range(1, sublanes):
        group = jnp.where(sub == r, pltpu.roll(f, r, 1), group)
    ql = lax.broadcasted_iota(jnp.int32, (sublanes, MOBA_BLOCK), 1)
    kl0 = lax.broadcasted_iota(jnp.int32, (sublanes, MOBA_BLOCK), 0)
    for g in range(MOBA_BLOCK // sublanes):
        rows = slice(g * sublanes, (g + 1) * sublanes)
        blk = pltpu.roll(group, g * sublanes, 1) if g else group
        diag_ref[0, rows] = jnp.where(ql >= kl0 + g * sublanes, blk[:, MOBA_BLOCK:2 * MOBA_BLOCK], MASK_VALUE)
        sub_ref[0, rows] = blk[:, 2 * MOBA_BLOCK:3 * MOBA_BLOCK]


def _prep_kernel(table_ref, *refs):
    n = (len(refs) - 2) // 2
    weights_f32, (diag_ref, sub_ref), weights_bf16 = refs[:n], refs[n:n + 2], refs[n + 2:]
    _rpb_bias_tiles(table_ref, pl.program_id(0) // PREP_STEPS_PER_HEAD, diag_ref, sub_ref)
    for src_ref, dst_ref in zip(weights_f32, weights_bf16):
        dst_ref[...] = src_ref[...].astype(BF16)


def _prep(rpb_table, weights):
    steps = ATTN_HEADS * PREP_STEPS_PER_HEAD
    tile = jax.ShapeDtypeStruct((ATTN_HEADS, MOBA_BLOCK, MOBA_BLOCK), F32)
    tile_spec = pl.BlockSpec((1, MOBA_BLOCK, MOBA_BLOCK), lambda s: (s // PREP_STEPS_PER_HEAD, 0, 0))
    slabs = [w.reshape(steps, w.shape[0] // steps, w.shape[1]) for w in weights]
    assert all(w.shape[1] % BF16_SUBLANES == 0 for w in slabs)
    slab_specs = [pl.BlockSpec((1,) + w.shape[1:], lambda s: (s, 0, 0)) for w in slabs]
    outs = pl.pallas_call(
        _prep_kernel,
        grid=(steps,),
        in_specs=[pl.BlockSpec(memory_space=pltpu.SMEM), *slab_specs],
        out_specs=[tile_spec, tile_spec, *slab_specs],
        out_shape=[tile, tile] + [jax.ShapeDtypeStruct(w.shape, BF16) for w in slabs],
        compiler_params=pltpu.CompilerParams(
            dimension_semantics=("arbitrary",), vmem_limit_bytes=V7X_VMEM_LIMIT_BYTES),
        name="prep",
    )(rpb_table, *slabs)
    return outs[0], outs[1], [o.reshape(w.shape) for o, w in zip(outs[2:], weights)]


def _ffn1_qkv_kernel(x_ref, g1_ref, wg_ref, wu_ref, wd_ref, gm_ref, wqkv_f32_ref, *refs):
    later_f32, (h1_ref, un_ref, qt_ref, k_ref, vt_ref), later_bf16, wqkv_ref = (
        refs[:N_LATER_WEIGHTS], refs[N_LATER_WEIGHTS:N_LATER_WEIGHTS + 5], refs[N_LATER_WEIGHTS + 5:-1], refs[-1])

    @pl.when((pl.program_id(0) == 0) & (pl.program_id(1) == 0))
    def _():
        wqkv_ref[...] = wqkv_f32_ref[...].astype(BF16)

    for src_ref, dst_ref in zip(later_f32, later_bf16):
        dst_ref[...] = src_ref[...].astype(BF16)
    h1 = _swiglu_half_step(x_ref[0], g1_ref, wg_ref, wu_ref, wd_ref)
    h1_ref[0] = h1
    un = _rms(h1, gm_ref[...]).astype(BF16)
    un_ref[0] = un
    qkv = _dot(un, wqkv_ref[...])
    qt_ref[0] = (qkv[:, :ATTN_WIDTH] * Q_SCALE).T.astype(BF16)
    k_ref[0] = qkv[:, ATTN_WIDTH:2 * ATTN_WIDTH].astype(BF16)
    for c in range(FFN_TOKENS // MOBA_BLOCK):
        vt_ref[0, c] = qkv[c * MOBA_BLOCK:(c + 1) * MOBA_BLOCK, 2 * ATTN_WIDTH:].T.astype(BF16)


def _ffn1_qkv(x, g1, wg, wu, wd, gm, w_in, later_weights):
    B, S, D = x.shape
    tm = FFN_TOKENS
    nblk = S // MOBA_BLOCK
    tiles_per_seq = S // tm
    steps = B * tiles_per_seq
    assert len(later_weights) == N_LATER_WEIGHTS
    step_of = lambda b, i: b * tiles_per_seq + i

    def slab_view(w):
        n = max(n for n in range(1, steps + 1) if steps % n == 0 and w.shape[0] % (n * BF16_SUBLANES) == 0)
        view = w.reshape(n, w.shape[0] // n, w.shape[1])
        return view, pl.BlockSpec((1,) + view.shape[1:], lambda b, i: (step_of(b, i) // (steps // n), 0, 0))

    slabs, slab_specs = zip(*[slab_view(w) for w in later_weights])
    tok = lambda width, dt: (pl.BlockSpec((1, tm, width), lambda b, i: (b, i, 0)),
                             jax.ShapeDtypeStruct((B, S, width), dt))
    h1_spec, h1_shape = tok(D, F32)
    un_spec, un_shape = tok(D, BF16)
    k_spec, k_shape = tok(ATTN_WIDTH, BF16)
    outs = pl.pallas_call(
        _ffn1_qkv_kernel,
        grid=(B, tiles_per_seq),
        in_specs=[pl.BlockSpec((1, tm, D), lambda b, i: (b, i, 0)),
                  _resident(g1.shape), _resident(wg.shape), _resident(wu.shape), _resident(wd.shape),
                  _resident(gm.shape), _resident((D, 3 * ATTN_WIDTH)), *slab_specs],
        out_specs=[h1_spec, un_spec,
                   pl.BlockSpec((1, ATTN_WIDTH, tm), lambda b, i: (b, 0, i)),
                   k_spec,
                   pl.BlockSpec((1, tm // MOBA_BLOCK, ATTN_WIDTH, MOBA_BLOCK), lambda b, i: (b, i, 0, 0)),
                   *slab_specs],
        out_shape=[h1_shape, un_shape,
                   jax.ShapeDtypeStruct((B, ATTN_WIDTH, S), BF16),
                   k_shape,
                   jax.ShapeDtypeStruct((B, nblk, ATTN_WIDTH, MOBA_BLOCK), BF16)]
                  + [jax.ShapeDtypeStruct(w.shape, BF16) for w in slabs],
        scratch_shapes=[pltpu.VMEM((D, 3 * ATTN_WIDTH), BF16)],
        compiler_params=pltpu.CompilerParams(
            dimension_semantics=("arbitrary", "arbitrary"), vmem_limit_bytes=V7X_VMEM_LIMIT_BYTES),
        name="ffn1_qkv",
    )(x, g1, wg, wu, wd, gm, w_in, *slabs)
    return outs[:5], [o.reshape(w.shape) for o, w in zip(outs[5:], later_weights)]


def _moba_attn_kernel(t31_ref, qt_ref, k_ref, vt_ref, diag_ref, sub_ref, a_ref,
                      kmhi_ref, kmlo_ref, rb_far_ref, rb_sub_ref, at_ref, qz_ref,
                      m_ref, l_ref, mblk_ref, alpha_ref, s_ref, p_ref):
    qi = pl.program_id(1)
    nblk = k_ref.shape[2] // MOBA_BLOCK
    nq = MOBA_BLOCK
    units = [(r, h) for r in range(ATTN_ROWS) for h in range(ATTN_HEADS)]

    @pl.when(qi == 0)
    def _():
        for r in range(ATTN_ROWS):
            blk_row = lax.broadcasted_iota(jnp.int32, (nblk, ATTN_WIDTH), 0)
            km = jnp.zeros((nblk, ATTN_WIDTH), F32)
            for j in range(nblk):
                mean_j = jnp.mean(k_ref[r, 0, j * MOBA_BLOCK:(j + 1) * MOBA_BLOCK, :].astype(F32), axis=0,
                                  keepdims=True)
                km = jnp.where(blk_row == j, mean_j, km)
            km = jnp.concatenate([km] * ATTN_HEADS, axis=0)
            row_head = lax.broadcasted_iota(jnp.int32, km.shape, 0) // nblk
            col_head = lax.broadcasted_iota(jnp.int32, km.shape, 1) // HEAD_DIM
            km = jnp.where(row_head == col_head, km, 0.0)
            hi = km.astype(BF16)
            kmhi_ref[r] = hi
            kmlo_ref[r] = (km - hi.astype(F32)).astype(BF16)

    jrow = lax.broadcasted_iota(jnp.int32, (nblk, nq), 0)
    for r in range(ATTN_ROWS):
        qt = qt_ref[r, 0]
        gates = _dot(kmhi_ref[r], qt) + _dot(kmlo_ref[r], qt)
        for h in range(ATTN_HEADS):
            u = r * ATTN_HEADS + h
            g = gates[h * nblk:(h + 1) * nblk]
            cnt = jnp.zeros((nblk, nq), jnp.int32)
            for jp in range(nblk):
                gb = jnp.broadcast_to(g[jp:jp + 1], (nblk, nq))
                beats = (gb > g) | ((gb == g) & (jp < jrow))
                cnt = cnt + jnp.where(beats, jnp.where(jp < qi, 1, 0), 0)
            sel = (jrow < qi) & (cnt < MOBA_TOPK)
            far = jnp.where(sel, t31_ref[h] * LOG2E, MASK_VALUE)
            sub = jnp.where(sel, 0.0, MASK_VALUE)
            for j in range(nblk):
                rb_far_ref[u * nblk + j] = far[j:j + 1]
                rb_sub_ref[u * nblk + j] = sub[j:j + 1]

    zeros_half = jnp.zeros((HEAD_DIM, nq), BF16)
    for r, h in units:
        qh = qt_ref[r, 0, h * HEAD_DIM:(h + 1) * HEAD_DIM, :]
        qz_ref[r * ATTN_HEADS + h] = jnp.concatenate([qh, zeros_half] if h % 2 == 0 else [zeros_half, qh], axis=0)

    def scores_stage(j, kind):
        start = pl.multiple_of(j * MOBA_BLOCK, MOBA_BLOCK)
        for u, (r, h) in enumerate(units):
            pair = h // 2
            kblk = k_ref[r, 0, pl.ds(start, MOBA_BLOCK), pair * 128:(pair + 1) * 128]
            s = _dot(kblk, qz_ref[u])
            if kind == "diag":
                s = s + diag_ref[h]
            elif kind == "sub":
                s = s + sub_ref[h]
            s_ref[u] = s
            mblk_ref[u] = jnp.max(s, axis=0, keepdims=True)

    def softmax_stage(row_bias):
        for u in range(len(units)):
            rb = row_bias(u)
            m_old = m_ref[u]
            m_blk = mblk_ref[u] if rb is None else mblk_ref[u] + rb
            m_new = jnp.maximum(m_old, m_blk)
            offset = m_new if rb is None else m_new - rb
            p_ref[u] = jnp.exp2(s_ref[u] - offset).astype(BF16)
            alpha_ref[u] = jnp.exp2(m_old - m_new)
            m_ref[u] = m_new

    ones_rows = jnp.ones((BF16_SUBLANES, nq), BF16)

    def values_stage(j):
        for u, (r, h) in enumerate(units):
            rows = slice(u * HEAD_DIM, (u + 1) * HEAD_DIM)
            v_h = vt_ref[r, 0, j, h * HEAD_DIM:(h + 1) * HEAD_DIM, :]
            pv = _dot(jnp.concatenate([v_h, ones_rows], axis=0), p_ref[u])
            alpha = alpha_ref[u]
            at_ref[rows] = alpha * at_ref[rows] + pv[:HEAD_DIM]
            l_ref[u] = alpha * l_ref[u] + pv[HEAD_DIM:HEAD_DIM + 1]

    m_ref[...] = jnp.full(m_ref.shape, MASK_VALUE, F32)
    l_ref[...] = jnp.zeros(l_ref.shape, F32)
    at_ref[...] = jnp.zeros(at_ref.shape, F32)
    j_sub = jnp.maximum(qi - 1, 0)
    n_far = j_sub

    scores_stage(qi, "diag")
    softmax_stage(lambda u: None)
    scores_stage(j_sub, "sub")
    values_stage(qi)
    softmax_stage(lambda u: rb_sub_ref[u * nblk + j_sub])
    scores_stage(0, "far")

    def far_body(t, carry):
        values_stage(jnp.where(t == 0, j_sub, t - 1))
        softmax_stage(lambda u: rb_far_ref[u * nblk + t])
        scores_stage(t + 1, "far")
        return carry

    lax.fori_loop(0, jnp.maximum(n_far - 1, 0), far_body, 0)

    @pl.when(n_far > 0)
    def _():
        values_stage(jnp.where(n_far == 1, j_sub, n_far - 2))
        softmax_stage(lambda u: rb_far_ref[u * nblk + n_far - 1])

    values_stage(jnp.where(n_far > 0, n_far - 1, j_sub))
    for u in range(len(units)):
        rows = slice(u * HEAD_DIM, (u + 1) * HEAD_DIM)
        at_ref[rows] = at_ref[rows] * (1.0 / l_ref[u])
    for r in range(ATTN_ROWS):
        a_ref[r, 0] = at_ref[r * ATTN_WIDTH:(r + 1) * ATTN_WIDTH, :].T.astype(BF16)


def _moba_attn(qt, k, vt, diag, sub, t31):
    B, S, _ = k.shape
    nblk = S // MOBA_BLOCK
    groups = B // ATTN_ROWS
    n_units = ATTN_ROWS * ATTN_HEADS
    split = lambda x: x.reshape((ATTN_ROWS, groups) + x.shape[1:])
    tile = pl.BlockSpec((ATTN_HEADS, MOBA_BLOCK, MOBA_BLOCK), lambda g, i: (0, 0, 0), pipeline_mode=pl.Buffered(1))
    a = pl.pallas_call(
        _moba_attn_kernel,
        grid=(groups, nblk),
        in_specs=[pl.BlockSpec(memory_space=pltpu.SMEM),
                  pl.BlockSpec((ATTN_ROWS, 1, ATTN_WIDTH, MOBA_BLOCK), lambda g, i: (0, g, 0, i)),
                  pl.BlockSpec((ATTN_ROWS, 1, S, ATTN_WIDTH), lambda g, i: (0, g, 0, 0)),
                  pl.BlockSpec((ATTN_ROWS, 1, nblk, ATTN_WIDTH, MOBA_BLOCK), lambda g, i: (0, g, 0, 0, 0)),
                  tile, tile],
        out_specs=pl.BlockSpec((ATTN_ROWS, 1, MOBA_BLOCK, ATTN_WIDTH), lambda g, i: (0, g, i, 0)),
        out_shape=jax.ShapeDtypeStruct((ATTN_ROWS, groups, S, ATTN_WIDTH), BF16),
        scratch_shapes=[pltpu.VMEM((ATTN_ROWS, ATTN_HEADS * nblk, ATTN_WIDTH), BF16),
                        pltpu.VMEM((ATTN_ROWS, ATTN_HEADS * nblk, ATTN_WIDTH), BF16),
                        pltpu.VMEM((n_units * nblk, 1, MOBA_BLOCK), F32),
                        pltpu.VMEM((n_units * nblk, 1, MOBA_BLOCK), F32),
                        pltpu.VMEM((n_units * HEAD_DIM, MOBA_BLOCK), F32),
                        pltpu.VMEM((n_units, 2 * HEAD_DIM, MOBA_BLOCK), BF16),
                        *[pltpu.VMEM((n_units, 1, MOBA_BLOCK), F32)] * 4,
                        pltpu.VMEM((n_units, MOBA_BLOCK, MOBA_BLOCK), F32),
                        pltpu.VMEM((n_units, MOBA_BLOCK, MOBA_BLOCK), BF16)],
        compiler_params=pltpu.CompilerParams(
            dimension_semantics=("arbitrary", "arbitrary"), vmem_limit_bytes=V7X_VMEM_LIMIT_BYTES),
        name="moba_attn",
    )(t31, split(qt), split(k), split(vt), diag, sub)
    return a.reshape(B, S, ATTN_WIDTH)


def _mix_merge_kernel(h1_ref, un_ref, unprev_ref, a_ref, win_ref, wgrp_ref, scale_ref,
                      wa_ref, wp_ref, wout_ref, h2_ref, zext_ref):
    i = pl.program_id(1)
    tm = MIX_TOKENS
    z0 = 3 * ATTN_WIDTH
    for c in range(MIX_SUBTILES):
        rows = slice(c * tm, (c + 1) * tm)
        zx_ref = zext_ref.at[c]
        zg = _dot(un_ref[0, rows], win_ref[:, z0:])
        halo = unprev_ref[0] if c == 0 else un_ref[0, c * tm - POOL_HALO:c * tm]
        zprev = _dot(halo, win_ref[:, z0:z0 + POOL_WIDTH])
        zx_ref[:POOL_HALO] = jnp.where(i > 0, zprev, 0.0) if c == 0 else zprev
        zx_ref[POOL_HALO:] = zg[:, :POOL_WIDTH]

        tpos = (i * MIX_SUBTILES + c) * tm + lax.broadcasted_iota(jnp.int32, (tm, POOL_GROUP_WIDTH), 0)
        mixed = []
        for g, w in enumerate(POOL_WINDOWS):
            cols = slice(g * POOL_GROUP_WIDTH, (g + 1) * POOL_GROUP_WIDTH)
            wsum = zx_ref[POOL_HALO:, cols]
            for lag in range(1, w):
                wsum = wsum + zx_ref[POOL_HALO - lag:POOL_HALO - lag + tm, cols]
            mean = wsum / jnp.minimum(tpos + 1, w).astype(F32)
            pooled = (mean - zx_ref[POOL_HALO:, cols]).astype(BF16)
            mixed.append(_dot(pooled, wgrp_ref[g]))
        p = (jnp.concatenate(mixed, axis=1) * scale_ref[...]).astype(BF16)

        g_attn = zg[:, POOL_WIDTH:POOL_WIDTH + D_MODEL]
        g_pool = zg[:, POOL_WIDTH + D_MODEL:]
        merged = (jax.nn.sigmoid(g_attn) * _dot(a_ref[0, rows], wa_ref[...])
                  + jax.nn.sigmoid(g_pool) * _dot(p, wp_ref[...]))
        h2_ref[0, rows] = h1_ref[0, rows] + _dot(merged.astype(BF16), wout_ref[...])


def _mix_merge(h1, un, a, win, wgrp, scale, wa, wp, wout):
    B, S, D = h1.shape
    tm = MIX_SUBTILES * MIX_TOKENS
    halo_per_tile = tm // POOL_HALO
    return pl.pallas_call(
        _mix_merge_kernel,
        grid=(B, S // tm),
        in_specs=[pl.BlockSpec((1, tm, D), lambda b, i: (b, i, 0)),
                  pl.BlockSpec((1, tm, D), lambda b, i: (b, i, 0)),
                  pl.BlockSpec((1, POOL_HALO, D), lambda b, i: (b, jnp.maximum(i * halo_per_tile - 1, 0), 0)),
                  pl.BlockSpec((1, tm, ATTN_WIDTH), lambda b, i: (b, i, 0)),
                  _resident(win.shape), _resident(wgrp.shape), _resident(scale.shape),
                  _resident(wa.shape), _resident(wp.shape), _resident(wout.shape)],
        out_specs=pl.BlockSpec((1, tm, D), lambda b, i: (b, i, 0)),
        out_shape=jax.ShapeDtypeStruct((B, S, D), F32),
        scratch_shapes=[pltpu.VMEM((MIX_SUBTILES, POOL_HALO + MIX_TOKENS, POOL_WIDTH), F32)],
        compiler_params=pltpu.CompilerParams(
            dimension_semantics=("arbitrary", "arbitrary"), vmem_limit_bytes=V7X_VMEM_LIMIT_BYTES),
        name="mix_merge",
    )(h1, un, un, a, win, wgrp, scale, wa, wp, wout)


def _ffn2_final_kernel(h_ref, g2_ref, wg_ref, wu_ref, wd_ref, gf_ref, out_ref):
    for c in range(FFN2_SUBTILES):
        rows = slice(c * FFN_TOKENS, (c + 1) * FFN_TOKENS)
        h3 = _swiglu_half_step(h_ref[0, rows], g2_ref, wg_ref, wu_ref, wd_ref)
        out_ref[0, rows] = _rms(h3, gf_ref[...])


def _ffn2_final(h, g2, wg, wu, wd, gf):
    B, S, D = h.shape
    tm = FFN2_SUBTILES * FFN_TOKENS
    return pl.pallas_call(
        _ffn2_final_kernel,
        grid=(B, S // tm),
        in_specs=[pl.BlockSpec((1, tm, D), lambda b, i: (b, i, 0)),
                  _resident(g2.shape), _resident(wg.shape), _resident(wu.shape), _resident(wd.shape),
                  _resident(gf.shape)],
        out_specs=pl.BlockSpec((1, tm, D), lambda b, i: (b, i, 0)),
        out_shape=jax.ShapeDtypeStruct((B, S, D), F32),
        compiler_params=pltpu.CompilerParams(
            dimension_semantics=("arbitrary", "arbitrary"), vmem_limit_bytes=V7X_VMEM_LIMIT_BYTES),
        name="ffn2_final",
    )(h, g2, wg, wu, wd, gf)


def kernel(x, ffn1_norm, ffn1_w_gate, ffn1_w_up, ffn1_w_down, mix_norm, w_in, pool_w_group, pool_scale,
           w_branch_attn, w_branch_pool, w_out, ffn2_norm, ffn2_w_gate, ffn2_w_up, ffn2_w_down,
           rpb_table, final_norm):
    B, S, D = x.shape
    assert (D, ffn1_w_gate.shape[0]) == (D_MODEL, 1)
    assert S % MOBA_BLOCK == 0 and S % (FFN2_SUBTILES * FFN_TOKENS) == 0 and S % (MIX_SUBTILES * MIX_TOKENS) == 0
    assert B % ATTN_ROWS == 0
    bf = lambda w: w.astype(BF16)
    row = lambda v: v.reshape(1, -1)
    w_in = w_in[0]

    diag, sub, (wg1_bf, wu1_bf, wd1_bf) = _prep(rpb_table, [ffn1_w_gate[0], ffn1_w_up[0], ffn1_w_down[0]])
    (h1, un, qt, k, vt), later = _ffn1_qkv(
        x, row(ffn1_norm[0]), wg1_bf, wu1_bf, wd1_bf, row(mix_norm[0]), w_in,
        [w_in, w_branch_attn[0], w_branch_pool[0], w_out[0], ffn2_w_gate[0], ffn2_w_up[0], ffn2_w_down[0]])
    win_bf, wa_bf, wp_bf, wout_bf, wg2_bf, wu2_bf, wd2_bf = later
    a = _moba_attn(qt, k, vt, diag, sub, rpb_table[RPB_BUCKETS - 1])
    h2 = _mix_merge(h1, un, a, win_bf, bf(pool_w_group[0]), row(pool_scale[0]), wa_bf, wp_bf, wout_bf)
    return _ffn2_final(h2, row(ffn2_norm[0]), wg2_bf, wu2_bf, wd2_bf, row(final_norm))
```

```python
import math

import jax
import jax.numpy as jnp
from jax import lax
from jax.experimental import pallas as pl
from jax.experimental.pallas import tpu as pltpu

D_MODEL = 1024
HEAD_DIM = 64
ATTN_WIDTH = 512
ATTN_HEADS = 8
MOBA_BLOCK = 256
MOBA_TOPK = 3
POOL_WINDOWS = (2, 4, 8, 16)
POOL_WIDTH = 512
POOL_GROUP_WIDTH = 128
POOL_HALO = 16
RPB_BUCKETS = 32
RPB_MAX_DISTANCE = 128
RMS_EPS = 1e-6
MASK_VALUE = -1e30
LOG2E = math.log2(math.e)
Q_SCALE = HEAD_DIM ** -0.5 * LOG2E
ATTN_ROWS = 2
V7X_VMEM_LIMIT_BYTES = 56 * 1024 * 1024

FFN_TOKENS = 512
FFN2_SUBTILES = 2
MIX_TOKENS = 512
MIX_SUBTILES = 2
N_LATER_WEIGHTS = 7
BF16_SUBLANES = 16
PREP_HEADS_PER_STEP = 2

BF16 = jnp.bfloat16
F32 = jnp.float32


def _resident(shape):
    nd = len(shape)
    return pl.BlockSpec(shape, lambda *_: (0,) * nd, pipeline_mode=pl.Buffered(1))


def _dot(a, b):
    return jnp.dot(a, b, preferred_element_type=F32)


def _rms(x, g):
    return x * lax.rsqrt(jnp.mean(x * x, axis=-1, keepdims=True) + RMS_EPS) * g


def _swiglu_half_step(x, g_ref, wg_ref, wu_ref, wd_ref):
    xn = _rms(x, g_ref[...]).astype(BF16)
    gate = _dot(xn, wg_ref[...])
    up = _dot(xn, wu_ref[...])
    act = (gate * jax.nn.sigmoid(gate) * up).astype(BF16)
    return x + 0.5 * _dot(act, wd_ref[...])


def _rpb_bucket(dist):
    n = jnp.maximum(dist, 0)
    max_exact = RPB_BUCKETS // 2
    nf = jnp.maximum(n, 1).astype(F32)
    large = max_exact + jnp.floor(jnp.log(nf / max_exact) / math.log(RPB_MAX_DISTANCE / max_exact)
                                  * (RPB_BUCKETS - max_exact)).astype(jnp.int32)
    large = jnp.minimum(large, RPB_BUCKETS - 1)
    return jnp.where(n < max_exact, n, large)


def _rpb_bias_tiles(table_ref, h, slot, diag_ref, sub_ref):
    width = 4 * MOBA_BLOCK
    sublanes = 8
    dist = lax.broadcasted_iota(jnp.int32, (sublanes, width), 1) - MOBA_BLOCK
    bucket = _rpb_bucket(dist)
    f = jnp.zeros((sublanes, width), F32)
    for b in range(RPB_BUCKETS):
        f = jnp.where(bucket == b, table_ref[b, h], f)
    f = f * LOG2E
    sub = lax.broadcasted_iota(jnp.int32, (sublanes, width), 0)
    group = f
    for r in range(1, sublanes):
        group = jnp.where(sub == r, pltpu.roll(f, r, 1), group)
    ql = lax.broadcasted_iota(jnp.int32, (sublanes, MOBA_BLOCK), 1)
    kl0 = lax.broadcasted_iota(jnp.int32, (sublanes, MOBA_BLOCK), 0)
    for g in range(MOBA_BLOCK // sublanes):
        rows = slice(g * sublanes, (g + 1) * sublanes)
        blk = pltpu.roll(group, g * sublanes, 1) if g else group
        diag_ref[slot, rows] = jnp.where(ql >= kl0 + g * sublanes, blk[:, MOBA_BLOCK:2 * MOBA_BLOCK], MASK_VALUE)
        sub_ref[slot, rows] = blk[:, 2 * MOBA_BLOCK:3 * MOBA_BLOCK]


def _prep_kernel(table_ref, *refs):
    n = (len(refs) - 2) // 2
    weights_f32, (diag_ref, sub_ref), weights_bf16 = refs[:n], refs[n:n + 2], refs[n + 2:]
    for slot in range(PREP_HEADS_PER_STEP):
        _rpb_bias_tiles(table_ref, pl.program_id(0) * PREP_HEADS_PER_STEP + slot, slot, diag_ref, sub_ref)
    for src_ref, dst_ref in zip(weights_f32, weights_bf16):
        dst_ref[...] = src_ref[...].astype(BF16)


def _prep(rpb_table, weights):
    steps = ATTN_HEADS // PREP_HEADS_PER_STEP
    tile = jax.ShapeDtypeStruct((ATTN_HEADS, MOBA_BLOCK, MOBA_BLOCK), F32)
    tile_spec = pl.BlockSpec((PREP_HEADS_PER_STEP, MOBA_BLOCK, MOBA_BLOCK), lambda s: (s, 0, 0))
    slabs = [w.reshape(steps, w.shape[0] // steps, w.shape[1]) for w in weights]
    assert all(w.shape[1] % BF16_SUBLANES == 0 for w in slabs)
    slab_specs = [pl.BlockSpec((1,) + w.shape[1:], lambda s: (s, 0, 0)) for w in slabs]
    outs = pl.pallas_call(
        _prep_kernel,
        grid=(steps,),
        in_specs=[pl.BlockSpec(memory_space=pltpu.SMEM), *slab_specs],
        out_specs=[tile_spec, tile_spec, *slab_specs],
        out_shape=[tile, tile] + [jax.ShapeDtypeStruct(w.shape, BF16) for w in slabs],
        compiler_params=pltpu.CompilerParams(
            dimension_semantics=("arbitrary",), vmem_limit_bytes=V7X_VMEM_LIMIT_BYTES),
        name="prep",
    )(rpb_table, *slabs)
    return outs[0], outs[1], [o.reshape(w.shape) for o, w in zip(outs[2:], weights)]


def _ffn1_qkv_kernel(x_ref, g1_ref, wg_ref, wu_ref, wd_ref, gm_ref, wqkv_f32_ref, *refs):
    later_f32, (h1_ref, un_ref, qt_ref, k_ref, vt_ref), later_bf16, wqkv_ref = (
        refs[:N_LATER_WEIGHTS], refs[N_LATER_WEIGHTS:N_LATER_WEIGHTS + 5], refs[N_LATER_WEIGHTS + 5:-1], refs[-1])

    @pl.when((pl.program_id(0) == 0) & (pl.program_id(1) == 0))
    def _():
        wqkv_ref[...] = wqkv_f32_ref[...].astype(BF16)

    for src_ref, dst_ref in zip(later_f32, later_bf16):
        dst_ref[...] = src_ref[...].astype(BF16)
    h1 = _swiglu_half_step(x_ref[0], g1_ref, wg_ref, wu_ref, wd_ref)
    h1_ref[0] = h1
    un = _rms(h1, gm_ref[...]).astype(BF16)
    un_ref[0] = un
    qkv = _dot(un, wqkv_ref[...])
    qt_ref[0] = (qkv[:, :ATTN_WIDTH] * Q_SCALE).T.astype(BF16)
    k_ref[0] = qkv[:, ATTN_WIDTH:2 * ATTN_WIDTH].astype(BF16)
    for c in range(FFN_TOKENS // MOBA_BLOCK):
        vt_ref[0, c] = qkv[c * MOBA_BLOCK:(c + 1) * MOBA_BLOCK, 2 * ATTN_WIDTH:].T.astype(BF16)


def _ffn1_qkv(x, g1, wg, wu, wd, gm, w_in, later_weights):
    B, S, D = x.shape
    tm = FFN_TOKENS
    nblk = S // MOBA_BLOCK
    tiles_per_seq = S // tm
    steps = B * tiles_per_seq
    assert len(later_weights) == N_LATER_WEIGHTS
    step_of = lambda b, i: b * tiles_per_seq + i

    def slab_view(w):
        n = max(n for n in range(1, steps + 1) if steps % n == 0 and w.shape[0] % (n * BF16_SUBLANES) == 0)
        view = w.reshape(n, w.shape[0] // n, w.shape[1])
        return view, pl.BlockSpec((1,) + view.shape[1:], lambda b, i: (step_of(b, i) // (steps // n), 0, 0))

    slabs, slab_specs = zip(*[slab_view(w) for w in later_weights])
    tok = lambda width, dt: (pl.BlockSpec((1, tm, width), lambda b, i: (b, i, 0)),
                             jax.ShapeDtypeStruct((B, S, width), dt))
    h1_spec, h1_shape = tok(D, F32)
    un_spec, un_shape = tok(D, BF16)
    k_spec, k_shape = tok(ATTN_WIDTH, BF16)
    outs = pl.pallas_call(
        _ffn1_qkv_kernel,
        grid=(B, tiles_per_seq),
        in_specs=[pl.BlockSpec((1, tm, D), lambda b, i: (b, i, 0)),
                  _resident(g1.shape), _resident(wg.shape), _resident(wu.shape), _resident(wd.shape),
                  _resident(gm.shape), _resident((D, 3 * ATTN_WIDTH)), *slab_specs],
        out_specs=[h1_spec, un_spec,
                   pl.BlockSpec((1, ATTN_WIDTH, tm), lambda b, i: (b, 0, i)),
                   k_spec,
                   pl.BlockSpec((1, tm // MOBA_BLOCK, ATTN_WIDTH, MOBA_BLOCK), lambda b, i: (b, i, 0, 0)),
                   *slab_specs],
        out_shape=[h1_shape, un_shape,
                   jax.ShapeDtypeStruct((B, ATTN_WIDTH, S), BF16),
                   k_shape,
                   jax.ShapeDtypeStruct((B, nblk, ATTN_WIDTH, MOBA_BLOCK), BF16)]
                  + [jax.ShapeDtypeStruct(w.shape, BF16) for w in slabs],
        scratch_shapes=[pltpu.VMEM((D, 3 * ATTN_WIDTH), BF16)],
        compiler_params=pltpu.CompilerParams(
            dimension_semantics=("arbitrary", "arbitrary"), vmem_limit_bytes=V7X_VMEM_LIMIT_BYTES),
        name="ffn1_qkv",
    )(x, g1, wg, wu, wd, gm, w_in, *slabs)
    return outs[:5], [o.reshape(w.shape) for o, w in zip(outs[5:], later_weights)]


def _moba_attn_kernel(t31_ref, qt_ref, k_ref, vt_ref, diag_ref, sub_ref, a_ref,
                      kmhi_ref, kmlo_ref, rb_far_ref, rb_sub_ref, at_ref, qz_ref,
                      m_ref, l_ref, mblk_ref, alpha_ref, s_ref, p_ref):
    qi = pl.program_id(1)
    nblk = k_ref.shape[2] // MOBA_BLOCK
    nq = MOBA_BLOCK
    units = [(r, h) for r in range(ATTN_ROWS) for h in range(ATTN_HEADS)]

    @pl.when(qi == 0)
    def _():
        for r in range(ATTN_ROWS):
            blk_row = lax.broadcasted_iota(jnp.int32, (nblk, ATTN_WIDTH), 0)
            km = jnp.zeros((nblk, ATTN_WIDTH), F32)
            for j in range(nblk):
                mean_j = jnp.mean(k_ref[r, 0, j * MOBA_BLOCK:(j + 1) * MOBA_BLOCK, :].astype(F32), axis=0,
                                  keepdims=True)
                km = jnp.where(blk_row == j, mean_j, km)
            km = jnp.concatenate([km] * ATTN_HEADS, axis=0)
            row_head = lax.broadcasted_iota(jnp.int32, km.shape, 0) // nblk
            col_head = lax.broadcasted_iota(jnp.int32, km.shape, 1) // HEAD_DIM
            km = jnp.where(row_head == col_head, km, 0.0)
            hi = km.astype(BF16)
            kmhi_ref[r] = hi
            kmlo_ref[r] = (km - hi.astype(F32)).astype(BF16)

    jrow = lax.broadcasted_iota(jnp.int32, (nblk, nq), 0)
    for r in range(ATTN_ROWS):
        qt = qt_ref[r, 0]
        gates = _dot(kmhi_ref[r], qt) + _dot(kmlo_ref[r], qt)
        for h in range(ATTN_HEADS):
            u = r * ATTN_HEADS + h
            g = gates[h * nblk:(h + 1) * nblk]
            cnt = jnp.zeros((nblk, nq), jnp.int32)
            for jp in range(nblk):
                gb = jnp.broadcast_to(g[jp:jp + 1], (nblk, nq))
                beats = (gb > g) | ((gb == g) & (jp < jrow))
                cnt = cnt + jnp.where(beats, jnp.where(jp < qi, 1, 0), 0)
            sel = (jrow < qi) & (cnt < MOBA_TOPK)
            far = jnp.where(sel, t31_ref[h] * LOG2E, MASK_VALUE)
            sub = jnp.where(sel, 0.0, MASK_VALUE)
            for j in range(nblk):
                rb_far_ref[u * nblk + j] = far[j:j + 1]
                rb_sub_ref[u * nblk + j] = sub[j:j + 1]

    zeros_half = jnp.zeros((HEAD_DIM, nq), BF16)
    for r, h in units:
        qh = qt_ref[r, 0, h * HEAD_DIM:(h + 1) * HEAD_DIM, :]
        qz_ref[r * ATTN_HEADS + h] = jnp.concatenate([qh, zeros_half] if h % 2 == 0 else [zeros_half, qh], axis=0)

    def scores_stage(j, kind):
        start = pl.multiple_of(j * MOBA_BLOCK, MOBA_BLOCK)
        for u, (r, h) in enumerate(units):
            pair = h // 2
            kblk = k_ref[r, 0, pl.ds(start, MOBA_BLOCK), pair * 128:(pair + 1) * 128]
            s = _dot(kblk, qz_ref[u])
            if kind == "diag":
                s = s + diag_ref[h]
            elif kind == "sub":
                s = s + sub_ref[h]
            s_ref[u] = s
            mblk_ref[u] = jnp.max(s, axis=0, keepdims=True)

    def softmax_stage(row_bias):
        for u in range(len(units)):
            rb = row_bias(u)
            m_old = m_ref[u]
            m_blk = mblk_ref[u] if rb is None else mblk_ref[u] + rb
            m_new = jnp.maximum(m_old, m_blk)
            offset = m_new if rb is None else m_new - rb
            p_ref[u] = jnp.exp2(s_ref[u] - offset).astype(BF16)
            alpha_ref[u] = jnp.exp2(m_old - m_new)
            m_ref[u] = m_new

    ones_rows = jnp.ones((BF16_SUBLANES, nq), BF16)

    def values_stage(j):
        for u, (r, h) in enumerate(units):
            rows = slice(u * HEAD_DIM, (u + 1) * HEAD_DIM)
            v_h = vt_ref[r, 0, j, h * HEAD_DIM:(h + 1) * HEAD_DIM, :]
            pv = _dot(jnp.concatenate([v_h, ones_rows], axis=0), p_ref[u])
            alpha = alpha_ref[u]
            at_ref[rows] = alpha * at_ref[rows] + pv[:HEAD_DIM]
            l_ref[u] = alpha * l_ref[u] + pv[HEAD_DIM:HEAD_DIM + 1]

    m_ref[...] = jnp.full(m_ref.shape, MASK_VALUE, F32)
    l_ref[...] = jnp.zeros(l_ref.shape, F32)
    at_ref[...] = jnp.zeros(at_ref.shape, F32)
    j_sub = jnp.maximum(qi - 1, 0)
    n_far = j_sub

    scores_stage(qi, "diag")
    softmax_stage(lambda u: None)
    scores_stage(j_sub, "sub")
    values_stage(qi)
    softmax_stage(lambda u: rb_sub_ref[u * nblk + j_sub])
    scores_stage(0, "far")

    def far_body(t, carry):
        values_stage(jnp.where(t == 0, j_sub, t - 1))
        softmax_stage(lambda u: rb_far_ref[u * nblk + t])
        scores_stage(t + 1, "far")
        return carry

    lax.fori_loop(0, jnp.maximum(n_far - 1, 0), far_body, 0)

    @pl.when(n_far > 0)
    def _():
        values_stage(jnp.where(n_far == 1, j_sub, n_far - 2))
        softmax_stage(lambda u: rb_far_ref[u * nblk + n_far - 1])

    values_stage(jnp.where(n_far > 0, n_far - 1, j_sub))
    for u in range(len(units)):
        rows = slice(u * HEAD_DIM, (u + 1) * HEAD_DIM)
        at_ref[rows] = at_ref[rows] * (1.0 / l_ref[u])
    for r in range(ATTN_ROWS):
        a_ref[r, 0] = at_ref[r * ATTN_WIDTH:(r + 1) * ATTN_WIDTH, :].T.astype(BF16)


def _moba_attn(qt, k, vt, diag, sub, t31):
    B, S, _ = k.shape
    nblk = S // MOBA_BLOCK
    groups = B // ATTN_ROWS
    n_units = ATTN_ROWS * ATTN_HEADS
    split = lambda x: x.reshape((ATTN_ROWS, groups) + x.shape[1:])
    tile = pl.BlockSpec((ATTN_HEADS, MOBA_BLOCK, MOBA_BLOCK), lambda g, i: (0, 0, 0), pipeline_mode=pl.Buffered(1))
    a = pl.pallas_call(
        _moba_attn_kernel,
        grid=(groups, nblk),
        in_specs=[pl.BlockSpec(memory_space=pltpu.SMEM),
                  pl.BlockSpec((ATTN_ROWS, 1, ATTN_WIDTH, MOBA_BLOCK), lambda g, i: (0, g, 0, i)),
                  pl.BlockSpec((ATTN_ROWS, 1, S, ATTN_WIDTH), lambda g, i: (0, g, 0, 0)),
                  pl.BlockSpec((ATTN_ROWS, 1, nblk, ATTN_WIDTH, MOBA_BLOCK), lambda g, i: (0, g, 0, 0, 0)),
                  tile, tile],
        out_specs=pl.BlockSpec((ATTN_ROWS, 1, MOBA_BLOCK, ATTN_WIDTH), lambda g, i: (0, g, i, 0)),
        out_shape=jax.ShapeDtypeStruct((ATTN_ROWS, groups, S, ATTN_WIDTH), BF16),
        scratch_shapes=[pltpu.VMEM((ATTN_ROWS, ATTN_HEADS * nblk, ATTN_WIDTH), BF16),
                        pltpu.VMEM((ATTN_ROWS, ATTN_HEADS * nblk, ATTN_WIDTH), BF16),
                        pltpu.VMEM((n_units * nblk, 1, MOBA_BLOCK), F32),
                        pltpu.VMEM((n_units * nblk, 1, MOBA_BLOCK), F32),
                        pltpu.VMEM((n_units * HEAD_DIM, MOBA_BLOCK), F32),
                        pltpu.VMEM((n_units, 2 * HEAD_DIM, MOBA_BLOCK), BF16),
                        *[pltpu.VMEM((n_units, 1, MOBA_BLOCK), F32)] * 4,
                        pltpu.VMEM((n_units, MOBA_BLOCK, MOBA_BLOCK), F32),
                        pltpu.VMEM((n_units, MOBA_BLOCK, MOBA_BLOCK), BF16)],
        compiler_params=pltpu.CompilerParams(
            dimension_semantics=("arbitrary", "arbitrary"), vmem_limit_bytes=V7X_VMEM_LIMIT_BYTES),
        name="moba_attn",
    )(t31, split(qt), split(k), split(vt), diag, sub)
    return a.reshape(B, S, ATTN_WIDTH)


def _mix_merge_kernel(h1_ref, un_ref, unprev_ref, a_ref, win_ref, wgrp_ref, scale_ref,
                      wa_ref, wp_ref, wout_ref, h2_ref, zext_ref):
    i = pl.program_id(1)
    tm = MIX_TOKENS
    z0 = 3 * ATTN_WIDTH
    for c in range(MIX_SUBTILES):
        rows = slice(c * tm, (c + 1) * tm)
        zx_ref = zext_ref.at[c]
        zg = _dot(un_ref[0, rows], win_ref[:, z0:])
        halo = unprev_ref[0] if c == 0 else un_ref[0, c * tm - POOL_HALO:c * tm]
        zprev = _dot(halo, win_ref[:, z0:z0 + POOL_WIDTH])
        zx_ref[:POOL_HALO] = jnp.where(i > 0, zprev, 0.0) if c == 0 else zprev
        zx_ref[POOL_HALO:] = zg[:, :POOL_WIDTH]

        tpos = (i * MIX_SUBTILES + c) * tm + lax.broadcasted_iota(jnp.int32, (tm, POOL_GROUP_WIDTH), 0)
        mixed = []
        for g, w in enumerate(POOL_WINDOWS):
            cols = slice(g * POOL_GROUP_WIDTH, (g + 1) * POOL_GROUP_WIDTH)
            wsum = zx_ref[POOL_HALO:, cols]
            for lag in range(1, w):
                wsum = wsum + zx_ref[POOL_HALO - lag:POOL_HALO - lag + tm, cols]
            mean = wsum / jnp.minimum(tpos + 1, w).astype(F32)
            pooled = (mean - zx_ref[POOL_HALO:, cols]).astype(BF16)
            mixed.append(_dot(pooled, wgrp_ref[g]))
        p = (jnp.concatenate(mixed, axis=1) * scale_ref[...]).astype(BF16)

        g_attn = zg[:, POOL_WIDTH:POOL_WIDTH + D_MODEL]
        g_pool = zg[:, POOL_WIDTH + D_MODEL:]
        merged = (jax.nn.sigmoid(g_attn) * _dot(a_ref[0, rows], wa_ref[...])
                  + jax.nn.sigmoid(g_pool) * _dot(p, wp_ref[...]))
        h2_ref[0, rows] = h1_ref[0, rows] + _dot(merged.astype(BF16), wout_ref[...])


def _mix_merge(h1, un, a, win, wgrp, scale, wa, wp, wout):
    B, S, D = h1.shape
    tm = MIX_SUBTILES * MIX_TOKENS
    halo_per_tile = tm // POOL_HALO
    return pl.pallas_call(
        _mix_merge_kernel,
        grid=(B, S // tm),
        in_specs=[pl.BlockSpec((1, tm, D), lambda b, i: (b, i, 0)),
                  pl.BlockSpec((1, tm, D), lambda b, i: (b, i, 0)),
                  pl.BlockSpec((1, POOL_HALO, D), lambda b, i: (b, jnp.maximum(i * halo_per_tile - 1, 0), 0)),
                  pl.BlockSpec((1, tm, ATTN_WIDTH), lambda b, i: (b, i, 0)),
                  _resident(win.shape), _resident(wgrp.shape), _resident(scale.shape),
                  _resident(wa.shape), _resident(wp.shape), _resident(wout.shape)],
        out_specs=pl.BlockSpec((1, tm, D), lambda b, i: (b, i, 0)),
        out_shape=jax.ShapeDtypeStruct((B, S, D), F32),
        scratch_shapes=[pltpu.VMEM((MIX_SUBTILES, POOL_HALO + MIX_TOKENS, POOL_WIDTH), F32)],
        compiler_params=pltpu.CompilerParams(
            dimension_semantics=("arbitrary", "arbitrary"), vmem_limit_bytes=V7X_VMEM_LIMIT_BYTES),
        name="mix_merge",
    )(h1, un, un, a, win, wgrp, scale, wa, wp, wout)


def _ffn2_final_kernel(h_ref, g2_ref, wg_ref, wu_ref, wd_ref, gf_ref, out_ref):
    for c in range(FFN2_SUBTILES):
        rows = slice(c * FFN_TOKENS, (c + 1) * FFN_TOKENS)
        h3 = _swiglu_half_step(h_ref[0, rows], g2_ref, wg_ref, wu_ref, wd_ref)
        out_ref[0, rows] = _rms(h3, gf_ref[...])


def _ffn2_final(h, g2, wg, wu, wd, gf):
    B, S, D = h.shape
    tm = FFN2_SUBTILES * FFN_TOKENS
    return pl.pallas_call(
        _ffn2_final_kernel,
        grid=(B, S // tm),
        in_specs=[pl.BlockSpec((1, tm, D), lambda b, i: (b, i, 0)),
                  _resident(g2.shape), _resident(wg.shape), _resident(wu.shape), _resident(wd.shape),
                  _resident(gf.shape)],
        out_specs=pl.BlockSpec((1, tm, D), lambda b, i: (b, i, 0)),
        out_shape=jax.ShapeDtypeStruct((B, S, D), F32),
        compiler_params=pltpu.CompilerParams(
            dimension_semantics=("arbitrary", "arbitrary"), vmem_limit_bytes=V7X_VMEM_LIMIT_BYTES),
        name="ffn2_final",
    )(h, g2, wg, wu, wd, gf)


def kernel(x, ffn1_norm, ffn1_w_gate, ffn1_w_up, ffn1_w_down, mix_norm, w_in, pool_w_group, pool_scale,
           w_branch_attn, w_branch_pool, w_out, ffn2_norm, ffn2_w_gate, ffn2_w_up, ffn2_w_down,
           rpb_table, final_norm):
    B, S, D = x.shape
    assert (D, ffn1_w_gate.shape[0]) == (D_MODEL, 1)
    assert S % MOBA_BLOCK == 0 and S % (FFN2_SUBTILES * FFN_TOKENS) == 0 and S % (MIX_SUBTILES * MIX_TOKENS) == 0
    assert B % ATTN_ROWS == 0
    bf = lambda w: w.astype(BF16)
    row = lambda v: v.reshape(1, -1)
    w_in = w_in[0]

    diag, sub, (wg1_bf, wu1_bf, wd1_bf) = _prep(rpb_table, [ffn1_w_gate[0], ffn1_w_up[0], ffn1_w_down[0]])
    (h1, un, qt, k, vt), later = _ffn1_qkv(
        x, row(ffn1_norm[0]), wg1_bf, wu1_bf, wd1_bf, row(mix_norm[0]), w_in,
        [w_in, w_branch_attn[0], w_branch_pool[0], w_out[0], ffn2_w_gate[0], ffn2_w_up[0], ffn2_w_down[0]])
    win_bf, wa_bf, wp_bf, wout_bf, wg2_bf, wu2_bf, wd2_bf = later
    a = _moba_attn(qt, k, vt, diag, sub, rpb_table[RPB_BUCKETS - 1])
    h2 = _mix_merge(h1, un, a, win_bf, bf(pool_w_group[0]), row(pool_scale[0]), wa_bf, wp_bf, wout_bf)
    return _ffn2_final(h2, row(ffn2_norm[0]), wg2_bf, wu2_bf, wd2_bf, row(final_norm))
```

```python
import math

import jax
import jax.numpy as jnp
from jax import lax
from jax.experimental import pallas as pl
from jax.experimental.pallas import tpu as pltpu

D_MODEL = 1024
HEAD_DIM = 64
ATTN_WIDTH = 512
ATTN_HEADS = 8
MOBA_BLOCK = 256
MOBA_TOPK = 3
POOL_WINDOWS = (2, 4, 8, 16)
POOL_WIDTH = 512
POOL_GROUP_WIDTH = 128
POOL_HALO = 16
RPB_BUCKETS = 32
RPB_MAX_DISTANCE = 128
RMS_EPS = 1e-6
MASK_VALUE = -1e30
LOG2E = math.log2(math.e)
Q_SCALE = HEAD_DIM ** -0.5 * LOG2E
ATTN_ROWS = 2
V7X_VMEM_LIMIT_BYTES = 56 * 1024 * 1024

FFN_TOKENS = 512
FFN2_SUBTILES = 2
MIX_TOKENS = 512
MIX_SUBTILES = 2
N_LATER_WEIGHTS = 7
BF16_SUBLANES = 16
PREP_HEADS_PER_STEP = 2

BF16 = jnp.bfloat16
F32 = jnp.float32


def _resident(shape):
    nd = len(shape)
    return pl.BlockSpec(shape, lambda *_: (0,) * nd, pipeline_mode=pl.Buffered(1))


def _dot(a, b):
    return jnp.dot(a, b, preferred_element_type=F32)


def _rms(x, g):
    return x * lax.rsqrt(jnp.mean(x * x, axis=-1, keepdims=True) + RMS_EPS) * g


def _swiglu_half_step(x, g_ref, wg_ref, wu_ref, wd_ref):
    xn = _rms(x, g_ref[...]).astype(BF16)
    gate = _dot(xn, wg_ref[...])
    up = _dot(xn, wu_ref[...])
    act = (gate * jax.nn.sigmoid(gate) * up).astype(BF16)
    return x + 0.5 * _dot(act, wd_ref[...])


def _rpb_bucket(dist):
    n = jnp.maximum(dist, 0)
    max_exact = RPB_BUCKETS // 2
    nf = jnp.maximum(n, 1).astype(F32)
    large = max_exact + jnp.floor(jnp.log(nf / max_exact) / math.log(RPB_MAX_DISTANCE / max_exact)
                                  * (RPB_BUCKETS - max_exact)).astype(jnp.int32)
    large = jnp.minimum(large, RPB_BUCKETS - 1)
    return jnp.where(n < max_exact, n, large)


def _rpb_bias_tiles(table_ref, h, slot, diag_ref, sub_ref):
    width = 4 * MOBA_BLOCK
    sublanes = 8
    dist = lax.broadcasted_iota(jnp.int32, (sublanes, width), 1) - MOBA_BLOCK
    bucket = _rpb_bucket(dist)
    f = jnp.zeros((sublanes, width), F32)
    for b in range(RPB_BUCKETS):
        f = jnp.where(bucket == b, table_ref[b, h], f)
    f = f * LOG2E
    sub = lax.broadcasted_iota(jnp.int32, (sublanes, width), 0)
    group = f
    for r in range(1, sublanes):
        group = jnp.where(sub == r, pltpu.roll(f, r, 1), group)
    ql = lax.broadcasted_iota(jnp.int32, (sublanes, MOBA_BLOCK), 1)
    kl0 = lax.broadcasted_iota(jnp.int32, (sublanes, MOBA_BLOCK), 0)
    for g in range(MOBA_BLOCK // sublanes):
        rows = slice(g * sublanes, (g + 1) * sublanes)
        blk = pltpu.roll(group, g * sublanes, 1) if g else group
        diag_ref[slot, rows] = jnp.where(ql >= kl0 + g * sublanes, blk[:, MOBA_BLOCK:2 * MOBA_BLOCK], MASK_VALUE)
        sub_ref[slot, rows] = blk[:, 2 * MOBA_BLOCK:3 * MOBA_BLOCK]


def _prep_kernel(table_ref, *refs):
    n = (len(refs) - 2) // 2
    weights_f32, (diag_ref, sub_ref), weights_bf16 = refs[:n], refs[n:n + 2], refs[n + 2:]
    for slot in range(PREP_HEADS_PER_STEP):
        _rpb_bias_tiles(table_ref, pl.program_id(0) * PREP_HEADS_PER_STEP + slot, slot, diag_ref, sub_ref)
    for src_ref, dst_ref in zip(weights_f32, weights_bf16):
        dst_ref[...] = src_ref[...].astype(BF16)


def _prep(rpb_table, weights):
    steps = ATTN_HEADS // PREP_HEADS_PER_STEP
    tile = jax.ShapeDtypeStruct((ATTN_HEADS, MOBA_BLOCK, MOBA_BLOCK), F32)
    tile_spec = pl.BlockSpec((PREP_HEADS_PER_STEP, MOBA_BLOCK, MOBA_BLOCK), lambda s: (s, 0, 0))
    slabs = [w.reshape(steps, w.shape[0] // steps, w.shape[1]) for w in weights]
    assert all(w.shape[1] % BF16_SUBLANES == 0 for w in slabs)
    slab_specs = [pl.BlockSpec((1,) + w.shape[1:], lambda s: (s, 0, 0)) for w in slabs]
    outs = pl.pallas_call(
        _prep_kernel,
        grid=(steps,),
        in_specs=[pl.BlockSpec(memory_space=pltpu.SMEM), *slab_specs],
        out_specs=[tile_spec, tile_spec, *slab_specs],
        out_shape=[tile, tile] + [jax.ShapeDtypeStruct(w.shape, BF16) for w in slabs],
        compiler_params=pltpu.CompilerParams(
            dimension_semantics=("arbitrary",), vmem_limit_bytes=V7X_VMEM_LIMIT_BYTES),
        name="prep",
    )(rpb_table, *slabs)
    return outs[0], outs[1], [o.reshape(w.shape) for o, w in zip(outs[2:], weights)]


def _ffn1_qkv_kernel(x_ref, g1_ref, wg_ref, wu_ref, wd_ref, gm_ref, wqkv_f32_ref, *refs):
    later_f32, (h1_ref, un_ref, qt_ref, k_ref, vt_ref), later_bf16, wqkv_ref = (
        refs[:N_LATER_WEIGHTS], refs[N_LATER_WEIGHTS:N_LATER_WEIGHTS + 5], refs[N_LATER_WEIGHTS + 5:-1], refs[-1])

    @pl.when((pl.program_id(0) == 0) & (pl.program_id(1) == 0))
    def _():
        wqkv_ref[...] = wqkv_f32_ref[...].astype(BF16)

    for src_ref, dst_ref in zip(later_f32, later_bf16):
        dst_ref[...] = src_ref[...].astype(BF16)
    h1 = _swiglu_half_step(x_ref[0], g1_ref, wg_ref, wu_ref, wd_ref)
    h1_ref[0] = h1
    un = _rms(h1, gm_ref[...]).astype(BF16)
    un_ref[0] = un
    qkv = _dot(un, wqkv_ref[...])
    qt_ref[0] = (qkv[:, :ATTN_WIDTH] * Q_SCALE).T.astype(BF16)
    k_ref[0] = qkv[:, ATTN_WIDTH:2 * ATTN_WIDTH].astype(BF16)
    for c in range(FFN_TOKENS // MOBA_BLOCK):
        vt_ref[0, c] = qkv[c * MOBA_BLOCK:(c + 1) * MOBA_BLOCK, 2 * ATTN_WIDTH:].T.astype(BF16)


def _ffn1_qkv(x, g1, wg, wu, wd, gm, w_in, later_weights):
    B, S, D = x.shape
    tm = FFN_TOKENS
    nblk = S // MOBA_BLOCK
    tiles_per_seq = S // tm
    steps = B * tiles_per_seq
    assert len(later_weights) == N_LATER_WEIGHTS
    step_of = lambda b, i: b * tiles_per_seq + i

    def slab_view(w):
        n = max(n for n in range(1, steps + 1) if steps % n == 0 and w.shape[0] % (n * BF16_SUBLANES) == 0)
        view = w.reshape(n, w.shape[0] // n, w.shape[1])
        return view, pl.BlockSpec((1,) + view.shape[1:], lambda b, i: (step_of(b, i) // (steps // n), 0, 0))

    slabs, slab_specs = zip(*[slab_view(w) for w in later_weights])
    tok = lambda width, dt: (pl.BlockSpec((1, tm, width), lambda b, i: (b, i, 0)),
                             jax.ShapeDtypeStruct((B, S, width), dt))
    h1_spec, h1_shape = tok(D, F32)
    un_spec, un_shape = tok(D, BF16)
    k_spec, k_shape = tok(ATTN_WIDTH, BF16)
    outs = pl.pallas_call(
        _ffn1_qkv_kernel,
        grid=(B, tiles_per_seq),
        in_specs=[pl.BlockSpec((1, tm, D), lambda b, i: (b, i, 0)),
                  _resident(g1.shape), _resident(wg.shape), _resident(wu.shape), _resident(wd.shape),
                  _resident(gm.shape), _resident((D, 3 * ATTN_WIDTH)), *slab_specs],
        out_specs=[h1_spec, un_spec,
                   pl.BlockSpec((1, ATTN_WIDTH, tm), lambda b, i: (b, 0, i)),
                   k_spec,
                   pl.BlockSpec((1, tm // MOBA_BLOCK, ATTN_WIDTH, MOBA_BLOCK), lambda b, i: (b, i, 0, 0)),
                   *slab_specs],
        out_shape=[h1_shape, un_shape,
                   jax.ShapeDtypeStruct((B, ATTN_WIDTH, S), BF16),
                   k_shape,
                   jax.ShapeDtypeStruct((B, nblk, ATTN_WIDTH, MOBA_BLOCK), BF16)]
                  + [jax.ShapeDtypeStruct(w.shape, BF16) for w in slabs],
        scratch_shapes=[pltpu.VMEM((D, 3 * ATTN_WIDTH), BF16)],
        compiler_params=pltpu.CompilerParams(
            dimension_semantics=("arbitrary", "arbitrary"), vmem_limit_bytes=V7X_VMEM_LIMIT_BYTES),
        name="ffn1_qkv",
    )(x, g1, wg, wu, wd, gm, w_in, *slabs)
    return outs[:5], [o.reshape(w.shape) for o, w in zip(outs[5:], later_weights)]


def _moba_attn_kernel(t31_ref, qt_ref, k_ref, vt_ref, diag_ref, sub_ref, a_ref,
                      kmhi_ref, kmlo_ref, rb_far_ref, rb_sub_ref, at_ref, qz_ref,
                      m_ref, l_ref, mblk_ref, alpha_ref, s_ref, p_ref):
    qi = pl.program_id(1)
    nblk = k_ref.shape[2] // MOBA_BLOCK
    nq = MOBA_BLOCK
    units = [(r, h) for r in range(ATTN_ROWS) for h in range(ATTN_HEADS)]

    @pl.when(qi == 0)
    def _():
        for r in range(ATTN_ROWS):
            blk_row = lax.broadcasted_iota(jnp.int32, (nblk, ATTN_WIDTH), 0)
            km = jnp.zeros((nblk, ATTN_WIDTH), F32)
            for j in range(nblk):
                mean_j = jnp.mean(k_ref[r, 0, j * MOBA_BLOCK:(j + 1) * MOBA_BLOCK, :].astype(F32), axis=0,
                                  keepdims=True)
                km = jnp.where(blk_row == j, mean_j, km)
            km = jnp.concatenate([km] * ATTN_HEADS, axis=0)
            row_head = lax.broadcasted_iota(jnp.int32, km.shape, 0) // nblk
            col_head = lax.broadcasted_iota(jnp.int32, km.shape, 1) // HEAD_DIM
            km = jnp.where(row_head == col_head, km, 0.0)
            hi = km.astype(BF16)
            kmhi_ref[r] = hi
            kmlo_ref[r] = (km - hi.astype(F32)).astype(BF16)

    def select_blocks():
        jrow = lax.broadcasted_iota(jnp.int32, (nblk, nq), 0)
        for r in range(ATTN_ROWS):
            qt = qt_ref[r, 0]
            gates = _dot(kmhi_ref[r], qt) + _dot(kmlo_ref[r], qt)
            for h in range(ATTN_HEADS):
                u = r * ATTN_HEADS + h
                g = gates[h * nblk:(h + 1) * nblk]
                cnt = jnp.zeros((nblk, nq), jnp.int32)
                for jp in range(nblk):
                    gb = jnp.broadcast_to(g[jp:jp + 1], (nblk, nq))
                    beats = (gb > g) | ((gb == g) & (jp < jrow))
                    cnt = cnt + jnp.where(beats, jnp.where(jp < qi, 1, 0), 0)
                sel = (jrow < qi) & (cnt < MOBA_TOPK)
                far = jnp.where(sel, t31_ref[h] * LOG2E, MASK_VALUE)
                sub = jnp.where(sel, 0.0, MASK_VALUE)
                for j in range(nblk):
                    rb_far_ref[u * nblk + j] = far[j:j + 1]
                    rb_sub_ref[u * nblk + j] = sub[j:j + 1]

    zeros_half = jnp.zeros((HEAD_DIM, nq), BF16)
    for r, h in units:
        qh = qt_ref[r, 0, h * HEAD_DIM:(h + 1) * HEAD_DIM, :]
        qz_ref[r * ATTN_HEADS + h] = jnp.concatenate([qh, zeros_half] if h % 2 == 0 else [zeros_half, qh], axis=0)

    def scores_stage(j, kind):
        start = pl.multiple_of(j * MOBA_BLOCK, MOBA_BLOCK)
        for u, (r, h) in enumerate(units):
            pair = h // 2
            kblk = k_ref[r, 0, pl.ds(start, MOBA_BLOCK), pair * 128:(pair + 1) * 128]
            s = _dot(kblk, qz_ref[u])
            if kind == "diag":
                s = s + diag_ref[h]
            elif kind == "sub":
                s = s + sub_ref[h]
            s_ref[u] = s
            mblk_ref[u] = jnp.max(s, axis=0, keepdims=True)

    def softmax_stage(row_bias):
        for u in range(len(units)):
            rb = row_bias(u)
            m_old = m_ref[u]
            m_blk = mblk_ref[u] if rb is None else mblk_ref[u] + rb
            m_new = jnp.maximum(m_old, m_blk)
            offset = m_new if rb is None else m_new - rb
            p_ref[u] = jnp.exp2(s_ref[u] - offset).astype(BF16)
            alpha_ref[u] = jnp.exp2(m_old - m_new)
            m_ref[u] = m_new

    ones_rows = jnp.ones((BF16_SUBLANES, nq), BF16)

    def values_stage(j):
        for u, (r, h) in enumerate(units):
            rows = slice(u * HEAD_DIM, (u + 1) * HEAD_DIM)
            v_h = vt_ref[r, 0, j, h * HEAD_DIM:(h + 1) * HEAD_DIM, :]
            pv = _dot(jnp.concatenate([v_h, ones_rows], axis=0), p_ref[u])
            alpha = alpha_ref[u]
            at_ref[rows] = alpha * at_ref[rows] + pv[:HEAD_DIM]
            l_ref[u] = alpha * l_ref[u] + pv[HEAD_DIM:HEAD_DIM + 1]

    def finish():
        for u in range(len(units)):
            rows = slice(u * HEAD_DIM, (u + 1) * HEAD_DIM)
            at_ref[rows] = at_ref[rows] * (1.0 / l_ref[u])
        for r in range(ATTN_ROWS):
            a_ref[r, 0] = at_ref[r * ATTN_WIDTH:(r + 1) * ATTN_WIDTH, :].T.astype(BF16)

    m_ref[...] = jnp.full(m_ref.shape, MASK_VALUE, F32)
    l_ref[...] = jnp.zeros(l_ref.shape, F32)
    at_ref[...] = jnp.zeros(at_ref.shape, F32)

    @pl.when(qi == 0)
    def _():
        scores_stage(qi, "diag")
        softmax_stage(lambda u: None)
        values_stage(qi)
        finish()

    @pl.when(qi > 0)
    def _():
        select_blocks()
        j_sub = qi - 1
        n_far = j_sub

        scores_stage(qi, "diag")
        softmax_stage(lambda u: None)
        scores_stage(j_sub, "sub")
        values_stage(qi)
        softmax_stage(lambda u: rb_sub_ref[u * nblk + j_sub])
        scores_stage(0, "far")

        def far_body(t, carry):
            values_stage(jnp.where(t == 0, j_sub, t - 1))
            softmax_stage(lambda u: rb_far_ref[u * nblk + t])
            scores_stage(t + 1, "far")
            return carry

        lax.fori_loop(0, jnp.maximum(n_far - 1, 0), far_body, 0)

        @pl.when(n_far > 0)
        def _():
            values_stage(jnp.where(n_far == 1, j_sub, n_far - 2))
            softmax_stage(lambda u: rb_far_ref[u * nblk + n_far - 1])

        values_stage(jnp.where(n_far > 0, n_far - 1, j_sub))
        finish()


def _moba_attn(qt, k, vt, diag, sub, t31):
    B, S, _ = k.shape
    nblk = S // MOBA_BLOCK
    groups = B // ATTN_ROWS
    n_units = ATTN_ROWS * ATTN_HEADS
    split = lambda x: x.reshape((ATTN_ROWS, groups) + x.shape[1:])
    tile = pl.BlockSpec((ATTN_HEADS, MOBA_BLOCK, MOBA_BLOCK), lambda g, i: (0, 0, 0), pipeline_mode=pl.Buffered(1))
    a = pl.pallas_call(
        _moba_attn_kernel,
        grid=(groups, nblk),
        in_specs=[pl.BlockSpec(memory_space=pltpu.SMEM),
                  pl.BlockSpec((ATTN_ROWS, 1, ATTN_WIDTH, MOBA_BLOCK), lambda g, i: (0, g, 0, i)),
                  pl.BlockSpec((ATTN_ROWS, 1, S, ATTN_WIDTH), lambda g, i: (0, g, 0, 0)),
                  pl.BlockSpec((ATTN_ROWS, 1, nblk, ATTN_WIDTH, MOBA_BLOCK), lambda g, i: (0, g, 0, 0, 0)),
                  tile, tile],
        out_specs=pl.BlockSpec((ATTN_ROWS, 1, MOBA_BLOCK, ATTN_WIDTH), lambda g, i: (0, g, i, 0)),
        out_shape=jax.ShapeDtypeStruct((ATTN_ROWS, groups, S, ATTN_WIDTH), BF16),
        scratch_shapes=[pltpu.VMEM((ATTN_ROWS, ATTN_HEADS * nblk, ATTN_WIDTH), BF16),
                        pltpu.VMEM((ATTN_ROWS, ATTN_HEADS * nblk, ATTN_WIDTH), BF16),
                        pltpu.VMEM((n_units * nblk, 1, MOBA_BLOCK), F32),
                        pltpu.VMEM((n_units * nblk, 1, MOBA_BLOCK), F32),
                        pltpu.VMEM((n_units * HEAD_DIM, MOBA_BLOCK), F32),
                        pltpu.VMEM((n_units, 2 * HEAD_DIM, MOBA_BLOCK), BF16),
                        *[pltpu.VMEM((n_units, 1, MOBA_BLOCK), F32)] * 4,
                        pltpu.VMEM((n_units, MOBA_BLOCK, MOBA_BLOCK), F32),
                        pltpu.VMEM((n_units, MOBA_BLOCK, MOBA_BLOCK), BF16)],
        compiler_params=pltpu.CompilerParams(
            dimension_semantics=("arbitrary", "arbitrary"), vmem_limit_bytes=V7X_VMEM_LIMIT_BYTES),
        name="moba_attn",
    )(t31, split(qt), split(k), split(vt), diag, sub)
    return a.reshape(B, S, ATTN_WIDTH)


def _mix_merge_kernel(h1_ref, un_ref, unprev_ref, a_ref, win_ref, wgrp_ref, scale_ref,
                      wa_ref, wp_ref, wout_ref, h2_ref, zext_ref):
    i = pl.program_id(1)
    tm = MIX_TOKENS
    z0 = 3 * ATTN_WIDTH
    for c in range(MIX_SUBTILES):
        rows = slice(c * tm, (c + 1) * tm)
        zx_ref = zext_ref.at[c]
        zg = _dot(un_ref[0, rows], win_ref[:, z0:])
        halo = unprev_ref[0] if c == 0 else un_ref[0, c * tm - POOL_HALO:c * tm]
        zprev = _dot(halo, win_ref[:, z0:z0 + POOL_WIDTH])
        zx_ref[:POOL_HALO] = jnp.where(i > 0, zprev, 0.0) if c == 0 else zprev
        zx_ref[POOL_HALO:] = zg[:, :POOL_WIDTH]

        tpos = (i * MIX_SUBTILES + c) * tm + lax.broadcasted_iota(jnp.int32, (tm, POOL_GROUP_WIDTH), 0)
        mixed = []
        for g, w in enumerate(POOL_WINDOWS):
            cols = slice(g * POOL_GROUP_WIDTH, (g + 1) * POOL_GROUP_WIDTH)
            wsum = zx_ref[POOL_HALO:, cols]
            for lag in range(1, w):
                wsum = wsum + zx_ref[POOL_HALO - lag:POOL_HALO - lag + tm, cols]
            mean = wsum / jnp.minimum(tpos + 1, w).astype(F32)
            pooled = (mean - zx_ref[POOL_HALO:, cols]).astype(BF16)
            mixed.append(_dot(pooled, wgrp_ref[g]))
        p = (jnp.concatenate(mixed, axis=1) * scale_ref[...]).astype(BF16)

        g_attn = zg[:, POOL_WIDTH:POOL_WIDTH + D_MODEL]
        g_pool = zg[:, POOL_WIDTH + D_MODEL:]
        merged = (jax.nn.sigmoid(g_attn) * _dot(a_ref[0, rows], wa_ref[...])
                  + jax.nn.sigmoid(g_pool) * _dot(p, wp_ref[...]))
        h2_ref[0, rows] = h1_ref[0, rows] + _dot(merged.astype(BF16), wout_ref[...])


def _mix_merge(h1, un, a, win, wgrp, scale, wa, wp, wout):
    B, S, D = h1.shape
    tm = MIX_SUBTILES * MIX_TOKENS
    halo_per_tile = tm // POOL_HALO
    return pl.pallas_call(
        _mix_merge_kernel,
        grid=(B, S // tm),
        in_specs=[pl.BlockSpec((1, tm, D), lambda b, i: (b, i, 0)),
                  pl.BlockSpec((1, tm, D), lambda b, i: (b, i, 0)),
                  pl.BlockSpec((1, POOL_HALO, D), lambda b, i: (b, jnp.maximum(i * halo_per_tile - 1, 0), 0)),
                  pl.BlockSpec((1, tm, ATTN_WIDTH), lambda b, i: (b, i, 0)),
                  _resident(win.shape), _resident(wgrp.shape), _resident(scale.shape),
                  _resident(wa.shape), _resident(wp.shape), _resident(wout.shape)],
        out_specs=pl.BlockSpec((1, tm, D), lambda b, i: (b, i, 0)),
        out_shape=jax.ShapeDtypeStruct((B, S, D), F32),
        scratch_shapes=[pltpu.VMEM((MIX_SUBTILES, POOL_HALO + MIX_TOKENS, POOL_WIDTH), F32)],
        compiler_params=pltpu.CompilerParams(
            dimension_semantics=("arbitrary", "arbitrary"), vmem_limit_bytes=V7X_VMEM_LIMIT_BYTES),
        name="mix_merge",
    )(h1, un, un, a, win, wgrp, scale, wa, wp, wout)


def _ffn2_final_kernel(h_ref, g2_ref, wg_ref, wu_ref, wd_ref, gf_ref, out_ref):
    for c in range(FFN2_SUBTILES):
        rows = slice(c * FFN_TOKENS, (c + 1) * FFN_TOKENS)
        h3 = _swiglu_half_step(h_ref[0, rows], g2_ref, wg_ref, wu_ref, wd_ref)
        out_ref[0, rows] = _rms(h3, gf_ref[...])


def _ffn2_final(h, g2, wg, wu, wd, gf):
    B, S, D = h.shape
    tm = FFN2_SUBTILES * FFN_TOKENS
    return pl.pallas_call(
        _ffn2_final_kernel,
        grid=(B, S // tm),
        in_specs=[pl.BlockSpec((1, tm, D), lambda b, i: (b, i, 0)),
                  _resident(g2.shape), _resident(wg.shape), _resident(wu.shape), _resident(wd.shape),
                  _resident(gf.shape)],
        out_specs=pl.BlockSpec((1, tm, D), lambda b, i: (b, i, 0)),
        out_shape=jax.ShapeDtypeStruct((B, S, D), F32),
        compiler_params=pltpu.CompilerParams(
            dimension_semantics=("arbitrary", "arbitrary"), vmem_limit_bytes=V7X_VMEM_LIMIT_BYTES),
        name="ffn2_final",
    )(h, g2, wg, wu, wd, gf)


def kernel(x, ffn1_norm, ffn1_w_gate, ffn1_w_up, ffn1_w_down, mix_norm, w_in, pool_w_group, pool_scale,
           w_branch_attn, w_branch_pool, w_out, ffn2_norm, ffn2_w_gate, ffn2_w_up, ffn2_w_down,
           rpb_table, final_norm):
    B, S, D = x.shape
    assert (D, ffn1_w_gate.shape[0]) == (D_MODEL, 1)
    assert S % MOBA_BLOCK == 0 and S % (FFN2_SUBTILES * FFN_TOKENS) == 0 and S % (MIX_SUBTILES * MIX_TOKENS) == 0
    assert B % ATTN_ROWS == 0
    bf = lambda w: w.astype(BF16)
    row = lambda v: v.reshape(1, -1)
    w_in = w_in[0]

    diag, sub, (wg1_bf, wu1_bf, wd1_bf) = _prep(rpb_table, [ffn1_w_gate[0], ffn1_w_up[0], ffn1_w_down[0]])
    (h1, un, qt, k, vt), later = _ffn1_qkv(
        x, row(ffn1_norm[0]), wg1_bf, wu1_bf, wd1_bf, row(mix_norm[0]), w_in,
        [w_in, w_branch_attn[0], w_branch_pool[0], w_out[0], ffn2_w_gate[0], ffn2_w_up[0], ffn2_w_down[0]])
    win_bf, wa_bf, wp_bf, wout_bf, wg2_bf, wu2_bf, wd2_bf = later
    a = _moba_attn(qt, k, vt, diag, sub, rpb_table[RPB_BUCKETS - 1])
    h2 = _mix_merge(h1, un, a, win_bf, bf(pool_w_group[0]), row(pool_scale[0]), wa_bf, wp_bf, wout_bf)
    return _ffn2_final(h2, row(ffn2_norm[0]), wg2_bf, wu2_bf, wd2_bf, row(final_norm))
```

```python
import math

import jax
import jax.numpy as jnp
from jax import lax
from jax.experimental import pallas as pl
from jax.experimental.pallas import tpu as pltpu

D_MODEL = 1024
HEAD_DIM = 64
ATTN_WIDTH = 512
ATTN_HEADS = 8
MOBA_BLOCK = 256
MOBA_TOPK = 3
POOL_WINDOWS = (2, 4, 8, 16)
POOL_WIDTH = 512
POOL_GROUP_WIDTH = 128
POOL_HALO = 16
RPB_BUCKETS = 32
RPB_MAX_DISTANCE = 128
RMS_EPS = 1e-6
MASK_VALUE = -1e30
LOG2E = math.log2(math.e)
Q_SCALE = HEAD_DIM ** -0.5 * LOG2E
ATTN_ROWS = 2
V7X_VMEM_LIMIT_BYTES = 56 * 1024 * 1024

FFN_TOKENS = 512
FFN2_SUBTILES = 2
MIX_TOKENS = 512
MIX_SUBTILES = 2
N_LATER_WEIGHTS = 7
F32_SUBLANES = 8
BF16_SUBLANES = 16
PAIR_WIDTH = 2 * HEAD_DIM
PREP_HEADS_PER_STEP = 2

BF16 = jnp.bfloat16
F32 = jnp.float32


def _resident(shape):
    nd = len(shape)
    return pl.BlockSpec(shape, lambda *_: (0,) * nd, pipeline_mode=pl.Buffered(1))


def _dot(a, b):
    return jnp.dot(a, b, preferred_element_type=F32)


def _rms(x, g):
    return x * lax.rsqrt(jnp.mean(x * x, axis=-1, keepdims=True) + RMS_EPS) * g


def _swiglu_half_step(x, g_ref, wg_ref, wu_ref, wd_ref):
    xn = _rms(x, g_ref[...]).astype(BF16)
    gate = _dot(xn, wg_ref[...])
    up = _dot(xn, wu_ref[...])
    act = (gate * jax.nn.sigmoid(gate) * up).astype(BF16)
    return x + 0.5 * _dot(act, wd_ref[...])


def _rpb_bucket(dist):
    n = jnp.maximum(dist, 0)
    max_exact = RPB_BUCKETS // 2
    nf = jnp.maximum(n, 1).astype(F32)
    large = max_exact + jnp.floor(jnp.log(nf / max_exact) / math.log(RPB_MAX_DISTANCE / max_exact)
                                  * (RPB_BUCKETS - max_exact)).astype(jnp.int32)
    large = jnp.minimum(large, RPB_BUCKETS - 1)
    return jnp.where(n < max_exact, n, large)


def _rpb_bias_tiles(table_ref, h, slot, diag_ref, sub_ref):
    width = 4 * MOBA_BLOCK
    sublanes = F32_SUBLANES
    dist = lax.broadcasted_iota(jnp.int32, (sublanes, width), 1) - MOBA_BLOCK
    bucket = _rpb_bucket(dist)
    f = jnp.zeros((sublanes, width), F32)
    for b in range(RPB_BUCKETS):
        f = jnp.where(bucket == b, table_ref[b, h], f)
    f = f * LOG2E
    sub = lax.broadcasted_iota(jnp.int32, (sublanes, width), 0)
    group = f
    for r in range(1, sublanes):
        group = jnp.where(sub == r, pltpu.roll(f, r, 1), group)
    ql = lax.broadcasted_iota(jnp.int32, (sublanes, MOBA_BLOCK), 1)
    kl0 = lax.broadcasted_iota(jnp.int32, (sublanes, MOBA_BLOCK), 0)
    for g in range(MOBA_BLOCK // sublanes):
        rows = slice(g * sublanes, (g + 1) * sublanes)
        blk = pltpu.roll(group, g * sublanes, 1) if g else group
        diag_ref[slot, rows] = jnp.where(ql >= kl0 + g * sublanes, blk[:, MOBA_BLOCK:2 * MOBA_BLOCK], MASK_VALUE)
        sub_ref[slot, rows] = blk[:, 2 * MOBA_BLOCK:3 * MOBA_BLOCK]


def _prep_kernel(table_ref, *refs):
    n = (len(refs) - 2) // 2
    weights_f32, (diag_ref, sub_ref), weights_bf16 = refs[:n], refs[n:n + 2], refs[n + 2:]
    for slot in range(PREP_HEADS_PER_STEP):
        _rpb_bias_tiles(table_ref, pl.program_id(0) * PREP_HEADS_PER_STEP + slot, slot, diag_ref, sub_ref)
    for src_ref, dst_ref in zip(weights_f32, weights_bf16):
        dst_ref[...] = src_ref[...].astype(BF16)


def _prep(rpb_table, weights):
    steps = ATTN_HEADS // PREP_HEADS_PER_STEP
    tile = jax.ShapeDtypeStruct((ATTN_HEADS, MOBA_BLOCK, MOBA_BLOCK), F32)
    tile_spec = pl.BlockSpec((PREP_HEADS_PER_STEP, MOBA_BLOCK, MOBA_BLOCK), lambda s: (s, 0, 0))
    slabs = [w.reshape(steps, w.shape[0] // steps, w.shape[1]) for w in weights]
    assert all(w.shape[1] % BF16_SUBLANES == 0 for w in slabs)
    slab_specs = [pl.BlockSpec((1,) + w.shape[1:], lambda s: (s, 0, 0)) for w in slabs]
    outs = pl.pallas_call(
        _prep_kernel,
        grid=(steps,),
        in_specs=[pl.BlockSpec(memory_space=pltpu.SMEM), *slab_specs],
        out_specs=[tile_spec, tile_spec, *slab_specs],
        out_shape=[tile, tile] + [jax.ShapeDtypeStruct(w.shape, BF16) for w in slabs],
        compiler_params=pltpu.CompilerParams(
            dimension_semantics=("arbitrary",), vmem_limit_bytes=V7X_VMEM_LIMIT_BYTES),
        name="prep",
    )(rpb_table, *slabs)
    return outs[0], outs[1], [o.reshape(w.shape) for o, w in zip(outs[2:], weights)]


def _ffn1_qkv_kernel(x_ref, g1_ref, wg_ref, wu_ref, wd_ref, gm_ref, wqkv_f32_ref, *refs):
    later_f32, (h1_ref, un_ref, qt_ref, k_ref, vt_ref), later_bf16, wqkv_ref = (
        refs[:N_LATER_WEIGHTS], refs[N_LATER_WEIGHTS:N_LATER_WEIGHTS + 5], refs[N_LATER_WEIGHTS + 5:-1], refs[-1])

    @pl.when((pl.program_id(0) == 0) & (pl.program_id(1) == 0))
    def _():
        wqkv_ref[...] = wqkv_f32_ref[...].astype(BF16)

    for src_ref, dst_ref in zip(later_f32, later_bf16):
        dst_ref[...] = src_ref[...].astype(BF16)
    h1 = _swiglu_half_step(x_ref[0], g1_ref, wg_ref, wu_ref, wd_ref)
    h1_ref[0] = h1
    un = _rms(h1, gm_ref[...]).astype(BF16)
    un_ref[0] = un
    qkv = _dot(un, wqkv_ref[...])
    qt_ref[0] = (qkv[:, :ATTN_WIDTH] * Q_SCALE).T.astype(BF16)
    k_ref[0] = qkv[:, ATTN_WIDTH:2 * ATTN_WIDTH].astype(BF16)
    for c in range(FFN_TOKENS // MOBA_BLOCK):
        vt_ref[0, c] = qkv[c * MOBA_BLOCK:(c + 1) * MOBA_BLOCK, 2 * ATTN_WIDTH:].T.astype(BF16)


def _ffn1_qkv(x, g1, wg, wu, wd, gm, w_in, later_weights):
    B, S, D = x.shape
    tm = FFN_TOKENS
    nblk = S // MOBA_BLOCK
    tiles_per_seq = S // tm
    steps = B * tiles_per_seq
    assert len(later_weights) == N_LATER_WEIGHTS
    step_of = lambda b, i: b * tiles_per_seq + i

    def slab_view(w):
        n = max(n for n in range(1, steps + 1) if steps % n == 0 and w.shape[0] % (n * BF16_SUBLANES) == 0)
        view = w.reshape(n, w.shape[0] // n, w.shape[1])
        return view, pl.BlockSpec((1,) + view.shape[1:], lambda b, i: (step_of(b, i) // (steps // n), 0, 0))

    slabs, slab_specs = zip(*[slab_view(w) for w in later_weights])
    tok = lambda width, dt: (pl.BlockSpec((1, tm, width), lambda b, i: (b, i, 0)),
                             jax.ShapeDtypeStruct((B, S, width), dt))
    h1_spec, h1_shape = tok(D, F32)
    un_spec, un_shape = tok(D, BF16)
    k_spec, k_shape = tok(ATTN_WIDTH, BF16)
    outs = pl.pallas_call(
        _ffn1_qkv_kernel,
        grid=(B, tiles_per_seq),
        in_specs=[pl.BlockSpec((1, tm, D), lambda b, i: (b, i, 0)),
                  _resident(g1.shape), _resident(wg.shape), _resident(wu.shape), _resident(wd.shape),
                  _resident(gm.shape), _resident((D, 3 * ATTN_WIDTH)), *slab_specs],
        out_specs=[h1_spec, un_spec,
                   pl.BlockSpec((1, ATTN_WIDTH, tm), lambda b, i: (b, 0, i)),
                   k_spec,
                   pl.BlockSpec((1, tm // MOBA_BLOCK, ATTN_WIDTH, MOBA_BLOCK), lambda b, i: (b, i, 0, 0)),
                   *slab_specs],
        out_shape=[h1_shape, un_shape,
                   jax.ShapeDtypeStruct((B, ATTN_WIDTH, S), BF16),
                   k_shape,
                   jax.ShapeDtypeStruct((B, nblk, ATTN_WIDTH, MOBA_BLOCK), BF16)]
                  + [jax.ShapeDtypeStruct(w.shape, BF16) for w in slabs],
        scratch_shapes=[pltpu.VMEM((D, 3 * ATTN_WIDTH), BF16)],
        compiler_params=pltpu.CompilerParams(
            dimension_semantics=("arbitrary", "arbitrary"), vmem_limit_bytes=V7X_VMEM_LIMIT_BYTES),
        name="ffn1_qkv",
    )(x, g1, wg, wu, wd, gm, w_in, *slabs)
    return outs[:5], [o.reshape(w.shape) for o, w in zip(outs[5:], later_weights)]


def _moba_attn_kernel(t31_ref, qt_ref, k_ref, vt_ref, diag_ref, sub_ref, a_ref,
                      kmhi_ref, kmlo_ref, rb_far_ref, rb_sub_ref, at_ref, qz_ref,
                      m_ref, l_ref, mblk_ref, alpha_ref, s_ref, p_ref):
    qi = pl.program_id(1)
    nblk = k_ref.shape[2] // MOBA_BLOCK
    nq = MOBA_BLOCK
    units = [(r, h) for r in range(ATTN_ROWS) for h in range(ATTN_HEADS)]

    @pl.when(qi == 0)
    def _():
        for r in range(ATTN_ROWS):
            blk_row = lax.broadcasted_iota(jnp.int32, (nblk, ATTN_WIDTH), 0)
            km = jnp.zeros((nblk, ATTN_WIDTH), F32)
            for j in range(nblk):
                mean_j = jnp.mean(k_ref[r, 0, j * MOBA_BLOCK:(j + 1) * MOBA_BLOCK, :].astype(F32), axis=0,
                                  keepdims=True)
                km = jnp.where(blk_row == j, mean_j, km)
            km = jnp.concatenate([km] * ATTN_HEADS, axis=0)
            row_head = lax.broadcasted_iota(jnp.int32, km.shape, 0) // nblk
            col_head = lax.broadcasted_iota(jnp.int32, km.shape, 1) // HEAD_DIM
            km = jnp.where(row_head == col_head, km, 0.0)
            hi = km.astype(BF16)
            kmhi_ref[r] = hi
            kmlo_ref[r] = (km - hi.astype(F32)).astype(BF16)

    def select_blocks():
        jrow = lax.broadcasted_iota(jnp.int32, (nblk, nq), 0)
        for r in range(ATTN_ROWS):
            qt = qt_ref[r, 0]
            gates = _dot(kmhi_ref[r], qt) + _dot(kmlo_ref[r], qt)
            for h in range(ATTN_HEADS):
                u = r * ATTN_HEADS + h
                g = gates[h * nblk:(h + 1) * nblk]
                cnt = jnp.zeros((nblk, nq), jnp.int32)
                for jp in range(nblk):
                    gb = jnp.broadcast_to(g[jp:jp + 1], (nblk, nq))
                    beats = (gb > g) | ((gb == g) & (jp < jrow))
                    cnt = cnt + jnp.where(beats, jnp.where(jp < qi, 1, 0), 0)
                sel = (jrow < qi) & (cnt < MOBA_TOPK)
                far = jnp.where(sel, t31_ref[h] * LOG2E, MASK_VALUE)
                sub = jnp.where(sel, 0.0, MASK_VALUE)
                for j in range(nblk):
                    rb_far_ref[u * nblk + j] = far[j:j + 1]
                    rb_sub_ref[u * nblk + j] = sub[j:j + 1]

    zeros_half = jnp.zeros((HEAD_DIM, nq), BF16)
    for r, h in units:
        qh = qt_ref[r, 0, h * HEAD_DIM:(h + 1) * HEAD_DIM, :]
        qz_ref[r * ATTN_HEADS + h] = jnp.concatenate([qh, zeros_half] if h % 2 == 0 else [zeros_half, qh], axis=0)

    def scores_stage(j, kind):
        start = pl.multiple_of(j * MOBA_BLOCK, MOBA_BLOCK)
        for u, (r, h) in enumerate(units):
            pair = h // 2
            kblk = k_ref[r, 0, pl.ds(start, MOBA_BLOCK), pair * PAIR_WIDTH:(pair + 1) * PAIR_WIDTH]
            s = _dot(kblk, qz_ref[u])
            if kind == "diag":
                s = s + diag_ref[h]
            elif kind == "sub":
                s = s + sub_ref[h]
            s_ref[u] = s
            mblk_ref[u] = jnp.max(s, axis=0, keepdims=True)

    def softmax_stage(row_bias):
        for u in range(len(units)):
            rb = row_bias(u)
            m_old = m_ref[u]
            m_blk = mblk_ref[u] if rb is None else mblk_ref[u] + rb
            m_new = jnp.maximum(m_old, m_blk)
            offset = m_new if rb is None else m_new - rb
            p_ref[u] = jnp.exp2(s_ref[u] - offset).astype(BF16)
            alpha_ref[u] = jnp.exp2(m_old - m_new)
            m_ref[u] = m_new

    ones_rows = jnp.ones((BF16_SUBLANES, nq), BF16)

    def values_stage(j):
        for u, (r, h) in enumerate(units):
            rows = slice(u * HEAD_DIM, (u + 1) * HEAD_DIM)
            v_h = vt_ref[r, 0, j, h * HEAD_DIM:(h + 1) * HEAD_DIM, :]
            pv = _dot(jnp.concatenate([v_h, ones_rows], axis=0), p_ref[u])
            alpha = alpha_ref[u]
            at_ref[rows] = alpha * at_ref[rows] + pv[:HEAD_DIM]
            l_ref[u] = alpha * l_ref[u] + pv[HEAD_DIM:HEAD_DIM + 1]

    def finish():
        for u in range(len(units)):
            rows = slice(u * HEAD_DIM, (u + 1) * HEAD_DIM)
            at_ref[rows] = at_ref[rows] * (1.0 / l_ref[u])
        for r in range(ATTN_ROWS):
            a_ref[r, 0] = at_ref[r * ATTN_WIDTH:(r + 1) * ATTN_WIDTH, :].T.astype(BF16)

    m_ref[...] = jnp.full(m_ref.shape, MASK_VALUE, F32)
    l_ref[...] = jnp.zeros(l_ref.shape, F32)
    at_ref[...] = jnp.zeros(at_ref.shape, F32)

    @pl.when(qi == 0)
    def _():
        scores_stage(qi, "diag")
        softmax_stage(lambda u: None)
        values_stage(qi)
        finish()

    @pl.when(qi > 0)
    def _():
        select_blocks()
        j_sub = qi - 1
        n_far = j_sub

        scores_stage(qi, "diag")
        softmax_stage(lambda u: None)
        scores_stage(j_sub, "sub")
        values_stage(qi)
        softmax_stage(lambda u: rb_sub_ref[u * nblk + j_sub])
        scores_stage(0, "far")

        def far_body(t, carry):
            values_stage(jnp.where(t == 0, j_sub, t - 1))
            softmax_stage(lambda u: rb_far_ref[u * nblk + t])
            scores_stage(t + 1, "far")
            return carry

        lax.fori_loop(0, jnp.maximum(n_far - 1, 0), far_body, 0)

        @pl.when(n_far > 0)
        def _():
            values_stage(jnp.where(n_far == 1, j_sub, n_far - 2))
            softmax_stage(lambda u: rb_far_ref[u * nblk + n_far - 1])

        values_stage(jnp.where(n_far > 0, n_far - 1, j_sub))
        finish()


def _moba_attn(qt, k, vt, diag, sub, t31):
    B, S, _ = k.shape
    nblk = S // MOBA_BLOCK
    groups = B // ATTN_ROWS
    n_units = ATTN_ROWS * ATTN_HEADS
    split = lambda x: x.reshape((ATTN_ROWS, groups) + x.shape[1:])
    tile = pl.BlockSpec((ATTN_HEADS, MOBA_BLOCK, MOBA_BLOCK), lambda g, i: (0, 0, 0), pipeline_mode=pl.Buffered(1))
    a = pl.pallas_call(
        _moba_attn_kernel,
        grid=(groups, nblk),
        in_specs=[pl.BlockSpec(memory_space=pltpu.SMEM),
                  pl.BlockSpec((ATTN_ROWS, 1, ATTN_WIDTH, MOBA_BLOCK), lambda g, i: (0, g, 0, i)),
                  pl.BlockSpec((ATTN_ROWS, 1, S, ATTN_WIDTH), lambda g, i: (0, g, 0, 0)),
                  pl.BlockSpec((ATTN_ROWS, 1, nblk, ATTN_WIDTH, MOBA_BLOCK), lambda g, i: (0, g, 0, 0, 0)),
                  tile, tile],
        out_specs=pl.BlockSpec((ATTN_ROWS, 1, MOBA_BLOCK, ATTN_WIDTH), lambda g, i: (0, g, i, 0)),
        out_shape=jax.ShapeDtypeStruct((ATTN_ROWS, groups, S, ATTN_WIDTH), BF16),
        scratch_shapes=[pltpu.VMEM((ATTN_ROWS, ATTN_HEADS * nblk, ATTN_WIDTH), BF16),
                        pltpu.VMEM((ATTN_ROWS, ATTN_HEADS * nblk, ATTN_WIDTH), BF16),
                        pltpu.VMEM((n_units * nblk, 1, MOBA_BLOCK), F32),
                        pltpu.VMEM((n_units * nblk, 1, MOBA_BLOCK), F32),
                        pltpu.VMEM((n_units * HEAD_DIM, MOBA_BLOCK), F32),
                        pltpu.VMEM((n_units, 2 * HEAD_DIM, MOBA_BLOCK), BF16),
                        *[pltpu.VMEM((n_units, 1, MOBA_BLOCK), F32)] * 4,
                        pltpu.VMEM((n_units, MOBA_BLOCK, MOBA_BLOCK), F32),
                        pltpu.VMEM((n_units, MOBA_BLOCK, MOBA_BLOCK), BF16)],
        compiler_params=pltpu.CompilerParams(
            dimension_semantics=("arbitrary", "arbitrary"), vmem_limit_bytes=V7X_VMEM_LIMIT_BYTES),
        name="moba_attn",
    )(t31, split(qt), split(k), split(vt), diag, sub)
    return a.reshape(B, S, ATTN_WIDTH)


def _mix_merge_kernel(h1_ref, un_ref, unprev_ref, a_ref, win_ref, wgrp_ref, scale_ref,
                      wa_ref, wp_ref, wout_ref, h2_ref, zext_ref):
    i = pl.program_id(1)
    tm = MIX_TOKENS
    z0 = 3 * ATTN_WIDTH
    for c in range(MIX_SUBTILES):
        rows = slice(c * tm, (c + 1) * tm)
        zx_ref = zext_ref.at[c]
        zg = _dot(un_ref[0, rows], win_ref[:, z0:])
        halo = unprev_ref[0] if c == 0 else un_ref[0, c * tm - POOL_HALO:c * tm]
        zprev = _dot(halo, win_ref[:, z0:z0 + POOL_WIDTH])
        zx_ref[:POOL_HALO] = jnp.where(i > 0, zprev, 0.0) if c == 0 else zprev
        zx_ref[POOL_HALO:] = zg[:, :POOL_WIDTH]

        tpos = (i * MIX_SUBTILES + c) * tm + lax.broadcasted_iota(jnp.int32, (tm, POOL_GROUP_WIDTH), 0)
        mixed = []
        for g, w in enumerate(POOL_WINDOWS):
            cols = slice(g * POOL_GROUP_WIDTH, (g + 1) * POOL_GROUP_WIDTH)
            wsum = zx_ref[POOL_HALO:, cols]
            for lag in range(1, w):
                wsum = wsum + zx_ref[POOL_HALO - lag:POOL_HALO - lag + tm, cols]
            mean = wsum / jnp.minimum(tpos + 1, w).astype(F32)
            pooled = (mean - zx_ref[POOL_HALO:, cols]).astype(BF16)
            mixed.append(_dot(pooled, wgrp_ref[g]))
        p = (jnp.concatenate(mixed, axis=1) * scale_ref[...]).astype(BF16)

        g_attn = zg[:, POOL_WIDTH:POOL_WIDTH + D_MODEL]
        g_pool = zg[:, POOL_WIDTH + D_MODEL:]
        merged = (jax.nn.sigmoid(g_attn) * _dot(a_ref[0, rows], wa_ref[...])
                  + jax.nn.sigmoid(g_pool) * _dot(p, wp_ref[...]))
        h2_ref[0, rows] = h1_ref[0, rows] + _dot(merged.astype(BF16), wout_ref[...])


def _mix_merge(h1, un, a, win, wgrp, scale, wa, wp, wout):
    B, S, D = h1.shape
    tm = MIX_SUBTILES * MIX_TOKENS
    halo_per_tile = tm // POOL_HALO
    return pl.pallas_call(
        _mix_merge_kernel,
        grid=(B, S // tm),
        in_specs=[pl.BlockSpec((1, tm, D), lambda b, i: (b, i, 0)),
                  pl.BlockSpec((1, tm, D), lambda b, i: (b, i, 0)),
                  pl.BlockSpec((1, POOL_HALO, D), lambda b, i: (b, jnp.maximum(i * halo_per_tile - 1, 0), 0)),
                  pl.BlockSpec((1, tm, ATTN_WIDTH), lambda b, i: (b, i, 0)),
                  _resident(win.shape), _resident(wgrp.shape), _resident(scale.shape),
                  _resident(wa.shape), _resident(wp.shape), _resident(wout.shape)],
        out_specs=pl.BlockSpec((1, tm, D), lambda b, i: (b, i, 0)),
        out_shape=jax.ShapeDtypeStruct((B, S, D), F32),
        scratch_shapes=[pltpu.VMEM((MIX_SUBTILES, POOL_HALO + MIX_TOKENS, POOL_WIDTH), F32)],
        compiler_params=pltpu.CompilerParams(
            dimension_semantics=("arbitrary", "arbitrary"), vmem_limit_bytes=V7X_VMEM_LIMIT_BYTES),
        name="mix_merge",
    )(h1, un, un, a, win, wgrp, scale, wa, wp, wout)


def _ffn2_final_kernel(h_ref, g2_ref, wg_ref, wu_ref, wd_ref, gf_ref, out_ref):
    for c in range(FFN2_SUBTILES):
        rows = slice(c * FFN_TOKENS, (c + 1) * FFN_TOKENS)
        h3 = _swiglu_half_step(h_ref[0, rows], g2_ref, wg_ref, wu_ref, wd_ref)
        out_ref[0, rows] = _rms(h3, gf_ref[...])


def _ffn2_final(h, g2, wg, wu, wd, gf):
    B, S, D = h.shape
    tm = FFN2_SUBTILES * FFN_TOKENS
    return pl.pallas_call(
        _ffn2_final_kernel,
        grid=(B, S // tm),
        in_specs=[pl.BlockSpec((1, tm, D), lambda b, i: (b, i, 0)),
                  _resident(g2.shape), _resident(wg.shape), _resident(wu.shape), _resident(wd.shape),
                  _resident(gf.shape)],
        out_specs=pl.BlockSpec((1, tm, D), lambda b, i: (b, i, 0)),
        out_shape=jax.ShapeDtypeStruct((B, S, D), F32),
        compiler_params=pltpu.CompilerParams(
            dimension_semantics=("arbitrary", "arbitrary"), vmem_limit_bytes=V7X_VMEM_LIMIT_BYTES),
        name="ffn2_final",
    )(h, g2, wg, wu, wd, gf)


def kernel(x, ffn1_norm, ffn1_w_gate, ffn1_w_up, ffn1_w_down, mix_norm, w_in, pool_w_group, pool_scale,
           w_branch_attn, w_branch_pool, w_out, ffn2_norm, ffn2_w_gate, ffn2_w_up, ffn2_w_down,
           rpb_table, final_norm):
    B, S, D = x.shape
    assert (D, ffn1_w_gate.shape[0]) == (D_MODEL, 1)
    assert S % MOBA_BLOCK == 0 and S % (FFN2_SUBTILES * FFN_TOKENS) == 0 and S % (MIX_SUBTILES * MIX_TOKENS) == 0
    assert B % ATTN_ROWS == 0
    bf = lambda w: w.astype(BF16)
    row = lambda v: v.reshape(1, -1)
    w_in = w_in[0]

    diag, sub, (wg1_bf, wu1_bf, wd1_bf) = _prep(rpb_table, [ffn1_w_gate[0], ffn1_w_up[0], ffn1_w_down[0]])
    (h1, un, qt, k, vt), later = _ffn1_qkv(
        x, row(ffn1_norm[0]), wg1_bf, wu1_bf, wd1_bf, row(mix_norm[0]), w_in,
        [w_in, w_branch_attn[0], w_branch_pool[0], w_out[0], ffn2_w_gate[0], ffn2_w_up[0], ffn2_w_down[0]])
    win_bf, wa_bf, wp_bf, wout_bf, wg2_bf, wu2_bf, wd2_bf = later
    a = _moba_attn(qt, k, vt, diag, sub, rpb_table[RPB_BUCKETS - 1])
    h2 = _mix_merge(h1, un, a, win_bf, bf(pool_w_group[0]), row(pool_scale[0]), wa_bf, wp_bf, wout_bf)
    return _ffn2_final(h2, row(ffn2_norm[0]), wg2_bf, wu2_bf, wd2_bf, row(final_norm))
```

```python
import math

import jax
import jax.numpy as jnp
from jax import lax
from jax.experimental import pallas as pl
from jax.experimental.pallas import tpu as pltpu

D_MODEL = 1024
HEAD_DIM = 64
ATTN_WIDTH = 512
ATTN_HEADS = 8
MOBA_BLOCK = 256
MOBA_TOPK = 3
POOL_WINDOWS = (2, 4, 8, 16)
POOL_WIDTH = 512
POOL_GROUP_WIDTH = 128
POOL_HALO = 16
RPB_BUCKETS = 32
RPB_MAX_DISTANCE = 128
RMS_EPS = 1e-6
MASK_VALUE = -1e30
LOG2E = math.log2(math.e)
Q_SCALE = HEAD_DIM ** -0.5 * LOG2E
ATTN_ROWS = 2
V7X_VMEM_LIMIT_BYTES = 56 * 1024 * 1024

FFN_TOKENS = 512
FFN2_SUBTILES = 2
MIX_TOKENS = 512
MIX_SUBTILES = 2
N_LATER_WEIGHTS = 7
F32_SUBLANES = 8
BF16_SUBLANES = 16
PAIR_WIDTH = 2 * HEAD_DIM
PREP_HEADS_PER_STEP = 2

BF16 = jnp.bfloat16
F32 = jnp.float32


def _resident(shape):
    nd = len(shape)
    return pl.BlockSpec(shape, lambda *_: (0,) * nd, pipeline_mode=pl.Buffered(1))


def _dot(a, b):
    return jnp.dot(a, b, preferred_element_type=F32)


def _rms(x, g):
    return x * lax.rsqrt(jnp.mean(x * x, axis=-1, keepdims=True) + RMS_EPS) * g


def _swiglu_half_step(x, g_ref, wg_ref, wu_ref, wd_ref):
    xn = _rms(x, g_ref[...]).astype(BF16)
    gate = _dot(xn, wg_ref[...])
    up = _dot(xn, wu_ref[...])
    act = (gate * jax.nn.sigmoid(gate) * up).astype(BF16)
    return x + 0.5 * _dot(act, wd_ref[...])


def _rpb_bucket(dist):
    n = jnp.maximum(dist, 0)
    max_exact = RPB_BUCKETS // 2
    nf = jnp.maximum(n, 1).astype(F32)
    large = max_exact + jnp.floor(jnp.log(nf / max_exact) / math.log(RPB_MAX_DISTANCE / max_exact)
                                  * (RPB_BUCKETS - max_exact)).astype(jnp.int32)
    large = jnp.minimum(large, RPB_BUCKETS - 1)
    return jnp.where(n < max_exact, n, large)


def _rpb_bias_tiles(table_ref, h, slot, diag_ref, sub_ref):
    width = 4 * MOBA_BLOCK
    sublanes = F32_SUBLANES
    dist = lax.broadcasted_iota(jnp.int32, (sublanes, width), 1) - MOBA_BLOCK
    bucket = _rpb_bucket(dist)
    f = jnp.zeros((sublanes, width), F32)
    for b in range(RPB_BUCKETS):
        f = jnp.where(bucket == b, table_ref[b, h], f)
    f = f * LOG2E
    sub = lax.broadcasted_iota(jnp.int32, (sublanes, width), 0)
    group = f
    for r in range(1, sublanes):
        group = jnp.where(sub == r, pltpu.roll(f, r, 1), group)
    ql = lax.broadcasted_iota(jnp.int32, (sublanes, MOBA_BLOCK), 1)
    kl0 = lax.broadcasted_iota(jnp.int32, (sublanes, MOBA_BLOCK), 0)
    for g in range(MOBA_BLOCK // sublanes):
        rows = slice(g * sublanes, (g + 1) * sublanes)
        blk = pltpu.roll(group, g * sublanes, 1) if g else group
        diag_ref[slot, rows] = jnp.where(ql >= kl0 + g * sublanes, blk[:, MOBA_BLOCK:2 * MOBA_BLOCK], MASK_VALUE)
        sub_ref[slot, rows] = blk[:, 2 * MOBA_BLOCK:3 * MOBA_BLOCK]


def _prep_kernel(table_ref, *refs):
    n = (len(refs) - 2) // 2
    weights_f32, (diag_ref, sub_ref), weights_bf16 = refs[:n], refs[n:n + 2], refs[n + 2:]
    for slot in range(PREP_HEADS_PER_STEP):
        _rpb_bias_tiles(table_ref, pl.program_id(0) * PREP_HEADS_PER_STEP + slot, slot, diag_ref, sub_ref)
    for src_ref, dst_ref in zip(weights_f32, weights_bf16):
        dst_ref[...] = src_ref[...].astype(BF16)


def _prep(rpb_table, weights):
    steps = ATTN_HEADS // PREP_HEADS_PER_STEP
    tile = jax.ShapeDtypeStruct((ATTN_HEADS, MOBA_BLOCK, MOBA_BLOCK), F32)
    tile_spec = pl.BlockSpec((PREP_HEADS_PER_STEP, MOBA_BLOCK, MOBA_BLOCK), lambda s: (s, 0, 0))
    slabs = [w.reshape(steps, w.shape[0] // steps, w.shape[1]) for w in weights]
    assert all(w.shape[1] % BF16_SUBLANES == 0 for w in slabs)
    slab_specs = [pl.BlockSpec((1,) + w.shape[1:], lambda s: (s, 0, 0)) for w in slabs]
    outs = pl.pallas_call(
        _prep_kernel,
        grid=(steps,),
        in_specs=[pl.BlockSpec(memory_space=pltpu.SMEM), *slab_specs],
        out_specs=[tile_spec, tile_spec, *slab_specs],
        out_shape=[tile, tile] + [jax.ShapeDtypeStruct(w.shape, BF16) for w in slabs],
        compiler_params=pltpu.CompilerParams(
            dimension_semantics=("arbitrary",), vmem_limit_bytes=V7X_VMEM_LIMIT_BYTES),
        name="prep",
    )(rpb_table, *slabs)
    return outs[0], outs[1], [o.reshape(w.shape) for o, w in zip(outs[2:], weights)]


def _ffn1_qkv_kernel(x_ref, g1_ref, wg_ref, wu_ref, wd_ref, gm_ref, wqkv_f32_ref, *refs):
    later_f32, (h1_ref, un_ref, qt_ref, k_ref, vt_ref), later_bf16, wqkv_ref = (
        refs[:N_LATER_WEIGHTS], refs[N_LATER_WEIGHTS:N_LATER_WEIGHTS + 5], refs[N_LATER_WEIGHTS + 5:-1], refs[-1])

    @pl.when((pl.program_id(0) == 0) & (pl.program_id(1) == 0))
    def _():
        wqkv_ref[...] = wqkv_f32_ref[...].astype(BF16)

    for src_ref, dst_ref in zip(later_f32, later_bf16):
        dst_ref[...] = src_ref[...].astype(BF16)
    h1 = _swiglu_half_step(x_ref[0], g1_ref, wg_ref, wu_ref, wd_ref)
    h1_ref[0] = h1
    un = _rms(h1, gm_ref[...]).astype(BF16)
    un_ref[0] = un
    qkv = _dot(un, wqkv_ref[...])
    qt_ref[0] = (qkv[:, :ATTN_WIDTH] * Q_SCALE).T.astype(BF16)
    k_ref[0] = qkv[:, ATTN_WIDTH:2 * ATTN_WIDTH].astype(BF16)
    for c in range(FFN_TOKENS // MOBA_BLOCK):
        vt_ref[0, c] = qkv[c * MOBA_BLOCK:(c + 1) * MOBA_BLOCK, 2 * ATTN_WIDTH:].T.astype(BF16)


def _ffn1_qkv(x, g1, wg, wu, wd, gm, w_in, later_weights):
    B, S, D = x.shape
    tm = FFN_TOKENS
    nblk = S // MOBA_BLOCK
    tiles_per_seq = S // tm
    steps = B * tiles_per_seq
    assert len(later_weights) == N_LATER_WEIGHTS
    step_of = lambda b, i: b * tiles_per_seq + i

    def slab_view(w):
        n = max(n for n in range(1, steps + 1) if steps % n == 0 and w.shape[0] % (n * BF16_SUBLANES) == 0)
        view = w.reshape(n, w.shape[0] // n, w.shape[1])
        return view, pl.BlockSpec((1,) + view.shape[1:], lambda b, i: (step_of(b, i) // (steps // n), 0, 0))

    slabs, slab_specs = zip(*[slab_view(w) for w in later_weights])
    tok = lambda width, dt: (pl.BlockSpec((1, tm, width), lambda b, i: (b, i, 0)),
                             jax.ShapeDtypeStruct((B, S, width), dt))
    h1_spec, h1_shape = tok(D, F32)
    un_spec, un_shape = tok(D, BF16)
    k_spec, k_shape = tok(ATTN_WIDTH, BF16)
    outs = pl.pallas_call(
        _ffn1_qkv_kernel,
        grid=(B, tiles_per_seq),
        in_specs=[pl.BlockSpec((1, tm, D), lambda b, i: (b, i, 0)),
                  _resident(g1.shape), _resident(wg.shape), _resident(wu.shape), _resident(wd.shape),
                  _resident(gm.shape), _resident((D, 3 * ATTN_WIDTH)), *slab_specs],
        out_specs=[h1_spec, un_spec,
                   pl.BlockSpec((1, ATTN_WIDTH, tm), lambda b, i: (b, 0, i)),
                   k_spec,
                   pl.BlockSpec((1, tm // MOBA_BLOCK, ATTN_WIDTH, MOBA_BLOCK), lambda b, i: (b, i, 0, 0)),
                   *slab_specs],
        out_shape=[h1_shape, un_shape,
                   jax.ShapeDtypeStruct((B, ATTN_WIDTH, S), BF16),
                   k_shape,
                   jax.ShapeDtypeStruct((B, nblk, ATTN_WIDTH, MOBA_BLOCK), BF16)]
                  + [jax.ShapeDtypeStruct(w.shape, BF16) for w in slabs],
        scratch_shapes=[pltpu.VMEM((D, 3 * ATTN_WIDTH), BF16)],
        compiler_params=pltpu.CompilerParams(
            dimension_semantics=("arbitrary", "arbitrary"), vmem_limit_bytes=V7X_VMEM_LIMIT_BYTES),
        name="ffn1_qkv",
    )(x, g1, wg, wu, wd, gm, w_in, *slabs)
    return outs[:5], [o.reshape(w.shape) for o, w in zip(outs[5:], later_weights)]


def _moba_attn_kernel(t31_ref, qt_ref, k_ref, vt_ref, diag_ref, sub_ref, a_ref,
                      kmhi_ref, kmlo_ref, rb_far_ref, rb_sub_ref, at_ref, qz_ref,
                      m_ref, l_ref, mblk_ref, alpha_ref, s_ref, p_ref):
    qi = pl.program_id(1)
    nblk = k_ref.shape[2] // MOBA_BLOCK
    nq = MOBA_BLOCK
    units = [(r, h) for r in range(ATTN_ROWS) for h in range(ATTN_HEADS)]

    @pl.when(qi == 0)
    def _():
        for r in range(ATTN_ROWS):
            blk_row = lax.broadcasted_iota(jnp.int32, (nblk, ATTN_WIDTH), 0)
            km = jnp.zeros((nblk, ATTN_WIDTH), F32)
            for j in range(nblk):
                mean_j = jnp.mean(k_ref[r, 0, j * MOBA_BLOCK:(j + 1) * MOBA_BLOCK, :].astype(F32), axis=0,
                                  keepdims=True)
                km = jnp.where(blk_row == j, mean_j, km)
            km = jnp.concatenate([km] * ATTN_HEADS, axis=0)
            row_head = lax.broadcasted_iota(jnp.int32, km.shape, 0) // nblk
            col_head = lax.broadcasted_iota(jnp.int32, km.shape, 1) // HEAD_DIM
            km = jnp.where(row_head == col_head, km, 0.0)
            hi = km.astype(BF16)
            kmhi_ref[r] = hi
            kmlo_ref[r] = (km - hi.astype(F32)).astype(BF16)

    def select_blocks():
        jrow = lax.broadcasted_iota(jnp.int32, (nblk, nq), 0)
        for r in range(ATTN_ROWS):
            qt = qt_ref[r, 0]
            gates = _dot(kmhi_ref[r], qt) + _dot(kmlo_ref[r], qt)
            for h in range(ATTN_HEADS):
                u = r * ATTN_HEADS + h
                g = gates[h * nblk:(h + 1) * nblk]
                cnt = jnp.zeros((nblk, nq), jnp.int32)
                for jp in range(nblk):
                    gb = jnp.broadcast_to(g[jp:jp + 1], (nblk, nq))
                    beats = (gb > g) | ((gb == g) & (jp < jrow))
                    cnt = cnt + jnp.where(beats, jnp.where(jp < qi, 1, 0), 0)
                sel = (jrow < qi) & (cnt < MOBA_TOPK)
                far = jnp.where(sel, t31_ref[h] * LOG2E, MASK_VALUE)
                sub = jnp.where(sel, 0.0, MASK_VALUE)
                for j in range(nblk):
                    rb_far_ref[u * nblk + j] = far[j:j + 1]
                    rb_sub_ref[u * nblk + j] = sub[j:j + 1]

    zeros_half = jnp.zeros((HEAD_DIM, nq), BF16)
    for r, h in units:
        qh = qt_ref[r, 0, h * HEAD_DIM:(h + 1) * HEAD_DIM, :]
        qz_ref[r * ATTN_HEADS + h] = jnp.concatenate([qh, zeros_half] if h % 2 == 0 else [zeros_half, qh], axis=0)

    def scores_stage(j, kind):
        start = pl.multiple_of(j * MOBA_BLOCK, MOBA_BLOCK)
        for u, (r, h) in enumerate(units):
            pair = h // 2
            kblk = k_ref[r, 0, pl.ds(start, MOBA_BLOCK), pair * PAIR_WIDTH:(pair + 1) * PAIR_WIDTH]
            s = _dot(kblk, qz_ref[u])
            if kind == "diag":
                s = s + diag_ref[h]
            elif kind == "sub":
                s = s + sub_ref[h]
            s_ref[u] = s
            mblk_ref[u] = jnp.max(s, axis=0, keepdims=True)

    def softmax_stage(row_bias):
        for u in range(len(units)):
            rb = row_bias(u)
            m_old = m_ref[u]
            m_blk = mblk_ref[u] if rb is None else mblk_ref[u] + rb
            m_new = jnp.maximum(m_old, m_blk)
            offset = m_new if rb is None else m_new - rb
            p_ref[u] = jnp.exp2(s_ref[u] - offset).astype(BF16)
            alpha_ref[u] = jnp.exp2(m_old - m_new)
            m_ref[u] = m_new

    ones_rows = jnp.ones((BF16_SUBLANES, nq), BF16)

    def values_stage(j):
        for u, (r, h) in enumerate(units):
            rows = slice(u * HEAD_DIM, (u + 1) * HEAD_DIM)
            v_h = vt_ref[r, 0, j, h * HEAD_DIM:(h + 1) * HEAD_DIM, :]
            pv = _dot(jnp.concatenate([v_h, ones_rows], axis=0), p_ref[u])
            alpha = alpha_ref[u]
            at_ref[rows] = alpha * at_ref[rows] + pv[:HEAD_DIM]
            l_ref[u] = alpha * l_ref[u] + pv[HEAD_DIM:HEAD_DIM + 1]

    def finish():
        for u in range(len(units)):
            rows = slice(u * HEAD_DIM, (u + 1) * HEAD_DIM)
            at_ref[rows] = at_ref[rows] * (1.0 / l_ref[u])
        for r in range(ATTN_ROWS):
            a_ref[r, 0] = at_ref[r * ATTN_WIDTH:(r + 1) * ATTN_WIDTH, :].T.astype(BF16)

    m_ref[...] = jnp.full(m_ref.shape, MASK_VALUE, F32)
    l_ref[...] = jnp.zeros(l_ref.shape, F32)
    at_ref[...] = jnp.zeros(at_ref.shape, F32)

    @pl.when(qi == 0)
    def _():
        scores_stage(qi, "diag")
        softmax_stage(lambda u: None)
        values_stage(qi)
        finish()

    @pl.when(qi == 1)
    def _():
        scores_stage(qi, "diag")
        softmax_stage(lambda u: None)
        scores_stage(0, "sub")
        values_stage(qi)
        softmax_stage(lambda u: None)
        values_stage(0)
        finish()

    @pl.when(qi > 1)
    def _():
        select_blocks()
        j_sub = qi - 1
        n_far = j_sub

        scores_stage(qi, "diag")
        softmax_stage(lambda u: None)
        scores_stage(j_sub, "sub")
        values_stage(qi)
        softmax_stage(lambda u: rb_sub_ref[u * nblk + j_sub])
        scores_stage(0, "far")

        def far_body(t, carry):
            values_stage(jnp.where(t == 0, j_sub, t - 1))
            softmax_stage(lambda u: rb_far_ref[u * nblk + t])
            scores_stage(t + 1, "far")
            return carry

        lax.fori_loop(0, n_far - 1, far_body, 0)
        values_stage(jnp.where(n_far == 1, j_sub, n_far - 2))
        softmax_stage(lambda u: rb_far_ref[u * nblk + n_far - 1])
        values_stage(n_far - 1)
        finish()


def _moba_attn(qt, k, vt, diag, sub, t31):
    B, S, _ = k.shape
    nblk = S // MOBA_BLOCK
    groups = B // ATTN_ROWS
    n_units = ATTN_ROWS * ATTN_HEADS
    split = lambda x: x.reshape((ATTN_ROWS, groups) + x.shape[1:])
    tile = pl.BlockSpec((ATTN_HEADS, MOBA_BLOCK, MOBA_BLOCK), lambda g, i: (0, 0, 0), pipeline_mode=pl.Buffered(1))
    a = pl.pallas_call(
        _moba_attn_kernel,
        grid=(groups, nblk),
        in_specs=[pl.BlockSpec(memory_space=pltpu.SMEM),
                  pl.BlockSpec((ATTN_ROWS, 1, ATTN_WIDTH, MOBA_BLOCK), lambda g, i: (0, g, 0, i)),
                  pl.BlockSpec((ATTN_ROWS, 1, S, ATTN_WIDTH), lambda g, i: (0, g, 0, 0)),
                  pl.BlockSpec((ATTN_ROWS, 1, nblk, ATTN_WIDTH, MOBA_BLOCK), lambda g, i: (0, g, 0, 0, 0)),
                  tile, tile],
        out_specs=pl.BlockSpec((ATTN_ROWS, 1, MOBA_BLOCK, ATTN_WIDTH), lambda g, i: (0, g, i, 0)),
        out_shape=jax.ShapeDtypeStruct((ATTN_ROWS, groups, S, ATTN_WIDTH), BF16),
        scratch_shapes=[pltpu.VMEM((ATTN_ROWS, ATTN_HEADS * nblk, ATTN_WIDTH), BF16),
                        pltpu.VMEM((ATTN_ROWS, ATTN_HEADS * nblk, ATTN_WIDTH), BF16),
                        pltpu.VMEM((n_units * nblk, 1, MOBA_BLOCK), F32),
                        pltpu.VMEM((n_units * nblk, 1, MOBA_BLOCK), F32),
                        pltpu.VMEM((n_units * HEAD_DIM, MOBA_BLOCK), F32),
                        pltpu.VMEM((n_units, 2 * HEAD_DIM, MOBA_BLOCK), BF16),
                        *[pltpu.VMEM((n_units, 1, MOBA_BLOCK), F32)] * 4,
                        pltpu.VMEM((n_units, MOBA_BLOCK, MOBA_BLOCK), F32),
                        pltpu.VMEM((n_units, MOBA_BLOCK, MOBA_BLOCK), BF16)],
        compiler_params=pltpu.CompilerParams(
            dimension_semantics=("arbitrary", "arbitrary"), vmem_limit_bytes=V7X_VMEM_LIMIT_BYTES),
        name="moba_attn",
    )(t31, split(qt), split(k), split(vt), diag, sub)
    return a.reshape(B, S, ATTN_WIDTH)


def _mix_merge_kernel(h1_ref, un_ref, unprev_ref, a_ref, win_ref, wgrp_ref, scale_ref,
                      wa_ref, wp_ref, wout_ref, h2_ref, zext_ref):
    i = pl.program_id(1)
    tm = MIX_TOKENS
    z0 = 3 * ATTN_WIDTH
    for c in range(MIX_SUBTILES):
        rows = slice(c * tm, (c + 1) * tm)
        zx_ref = zext_ref.at[c]
        zg = _dot(un_ref[0, rows], win_ref[:, z0:])
        halo = unprev_ref[0] if c == 0 else un_ref[0, c * tm - POOL_HALO:c * tm]
        zprev = _dot(halo, win_ref[:, z0:z0 + POOL_WIDTH])
        zx_ref[:POOL_HALO] = jnp.where(i > 0, zprev, 0.0) if c == 0 else zprev
        zx_ref[POOL_HALO:] = zg[:, :POOL_WIDTH]

        tpos = (i * MIX_SUBTILES + c) * tm + lax.broadcasted_iota(jnp.int32, (tm, POOL_GROUP_WIDTH), 0)
        mixed = []
        for g, w in enumerate(POOL_WINDOWS):
            cols = slice(g * POOL_GROUP_WIDTH, (g + 1) * POOL_GROUP_WIDTH)
            wsum = zx_ref[POOL_HALO:, cols]
            for lag in range(1, w):
                wsum = wsum + zx_ref[POOL_HALO - lag:POOL_HALO - lag + tm, cols]
            mean = wsum / jnp.minimum(tpos + 1, w).astype(F32)
            pooled = (mean - zx_ref[POOL_HALO:, cols]).astype(BF16)
            mixed.append(_dot(pooled, wgrp_ref[g]))
        p = (jnp.concatenate(mixed, axis=1) * scale_ref[...]).astype(BF16)

        g_attn = zg[:, POOL_WIDTH:POOL_WIDTH + D_MODEL]
        g_pool = zg[:, POOL_WIDTH + D_MODEL:]
        merged = (jax.nn.sigmoid(g_attn) * _dot(a_ref[0, rows], wa_ref[...])
                  + jax.nn.sigmoid(g_pool) * _dot(p, wp_ref[...]))
        h2_ref[0, rows] = h1_ref[0, rows] + _dot(merged.astype(BF16), wout_ref[...])


def _mix_merge(h1, un, a, win, wgrp, scale, wa, wp, wout):
    B, S, D = h1.shape
    tm = MIX_SUBTILES * MIX_TOKENS
    halo_per_tile = tm // POOL_HALO
    return pl.pallas_call(
        _mix_merge_kernel,
        grid=(B, S // tm),
        in_specs=[pl.BlockSpec((1, tm, D), lambda b, i: (b, i, 0)),
                  pl.BlockSpec((1, tm, D), lambda b, i: (b, i, 0)),
                  pl.BlockSpec((1, POOL_HALO, D), lambda b, i: (b, jnp.maximum(i * halo_per_tile - 1, 0), 0)),
                  pl.BlockSpec((1, tm, ATTN_WIDTH), lambda b, i: (b, i, 0)),
                  _resident(win.shape), _resident(wgrp.shape), _resident(scale.shape),
                  _resident(wa.shape), _resident(wp.shape), _resident(wout.shape)],
        out_specs=pl.BlockSpec((1, tm, D), lambda b, i: (b, i, 0)),
        out_shape=jax.ShapeDtypeStruct((B, S, D), F32),
        scratch_shapes=[pltpu.VMEM((MIX_SUBTILES, POOL_HALO + MIX_TOKENS, POOL_WIDTH), F32)],
        compiler_params=pltpu.CompilerParams(
            dimension_semantics=("arbitrary", "arbitrary"), vmem_limit_bytes=V7X_VMEM_LIMIT_BYTES),
        name="mix_merge",
    )(h1, un, un, a, win, wgrp, scale, wa, wp, wout)


def _ffn2_final_kernel(h_ref, g2_ref, wg_ref, wu_ref, wd_ref, gf_ref, out_ref):
    for c in range(FFN2_SUBTILES):
        rows = slice(c * FFN_TOKENS, (c + 1) * FFN_TOKENS)
        h3 = _swiglu_half_step(h_ref[0, rows], g2_ref, wg_ref, wu_ref, wd_ref)
        out_ref[0, rows] = _rms(h3, gf_ref[...])


def _ffn2_final(h, g2, wg, wu, wd, gf):
    B, S, D = h.shape
    tm = FFN2_SUBTILES * FFN_TOKENS
    return pl.pallas_call(
        _ffn2_final_kernel,
        grid=(B, S // tm),
        in_specs=[pl.BlockSpec((1, tm, D), lambda b, i: (b, i, 0)),
                  _resident(g2.shape), _resident(wg.shape), _resident(wu.shape), _resident(wd.shape),
                  _resident(gf.shape)],
        out_specs=pl.BlockSpec((1, tm, D), lambda b, i: (b, i, 0)),
        out_shape=jax.ShapeDtypeStruct((B, S, D), F32),
        compiler_params=pltpu.CompilerParams(
            dimension_semantics=("arbitrary", "arbitrary"), vmem_limit_bytes=V7X_VMEM_LIMIT_BYTES),
        name="ffn2_final",
    )(h, g2, wg, wu, wd, gf)


def kernel(x, ffn1_norm, ffn1_w_gate, ffn1_w_up, ffn1_w_down, mix_norm, w_in, pool_w_group, pool_scale,
           w_branch_attn, w_branch_pool, w_out, ffn2_norm, ffn2_w_gate, ffn2_w_up, ffn2_w_down,
           rpb_table, final_norm):
    B, S, D = x.shape
    assert (D, ffn1_w_gate.shape[0]) == (D_MODEL, 1)
    assert S % MOBA_BLOCK == 0 and S % (FFN2_SUBTILES * FFN_TOKENS) == 0 and S % (MIX_SUBTILES * MIX_TOKENS) == 0
    assert B % ATTN_ROWS == 0
    bf = lambda w: w.astype(BF16)
    row = lambda v: v.reshape(1, -1)
    w_in = w_in[0]

    diag, sub, (wg1_bf, wu1_bf, wd1_bf) = _prep(rpb_table, [ffn1_w_gate[0], ffn1_w_up[0], ffn1_w_down[0]])
    (h1, un, qt, k, vt), later = _ffn1_qkv(
        x, row(ffn1_norm[0]), wg1_bf, wu1_bf, wd1_bf, row(mix_norm[0]), w_in,
        [w_in, w_branch_attn[0], w_branch_pool[0], w_out[0], ffn2_w_gate[0], ffn2_w_up[0], ffn2_w_down[0]])
    win_bf, wa_bf, wp_bf, wout_bf, wg2_bf, wu2_bf, wd2_bf = later
    a = _moba_attn(qt, k, vt, diag, sub, rpb_table[RPB_BUCKETS - 1])
    h2 = _mix_merge(h1, un, a, win_bf, bf(pool_w_group[0]), row(pool_scale[0]), wa_bf, wp_bf, wout_bf)
    return _ffn2_final(h2, row(ffn2_norm[0]), wg2_bf, wu2_bf, wd2_bf, row(final_norm))
```

```python
import math

import jax
import jax.numpy as jnp
from jax import lax
from jax.experimental import pallas as pl
from jax.experimental.pallas import tpu as pltpu

D_MODEL = 1024
HEAD_DIM = 64
ATTN_WIDTH = 512
ATTN_HEADS = 8
MOBA_BLOCK = 256
MOBA_TOPK = 3
POOL_WINDOWS = (2, 4, 8, 16)
POOL_WIDTH = 512
POOL_GROUP_WIDTH = 128
POOL_HALO = 16
RPB_BUCKETS = 32
RPB_MAX_DISTANCE = 128
RMS_EPS = 1e-6
MASK_VALUE = -1e30
LOG2E = math.log2(math.e)
Q_SCALE = HEAD_DIM ** -0.5 * LOG2E
ATTN_ROWS = 2
V7X_VMEM_LIMIT_BYTES = 56 * 1024 * 1024

FFN_TOKENS = 512
FFN2_SUBTILES = 2
MIX_TOKENS = 512
MIX_SUBTILES = 2
N_LATER_WEIGHTS = 7
F32_SUBLANES = 8
BF16_SUBLANES = 16
PAIR_WIDTH = 2 * HEAD_DIM
PREP_HEADS_PER_STEP = 2

BF16 = jnp.bfloat16
F32 = jnp.float32


def _resident(shape):
    nd = len(shape)
    return pl.BlockSpec(shape, lambda *_: (0,) * nd, pipeline_mode=pl.Buffered(1))


def _dot(a, b):
    return jnp.dot(a, b, preferred_element_type=F32)


def _rms(x, g):
    return x * lax.rsqrt(jnp.mean(x * x, axis=-1, keepdims=True) + RMS_EPS) * g


def _swiglu_half_step(x, g_ref, wg_ref, wu_ref, wd_ref):
    xn = _rms(x, g_ref[...]).astype(BF16)
    gate = _dot(xn, wg_ref[...])
    up = _dot(xn, wu_ref[...])
    act = (gate * jax.nn.sigmoid(gate) * up).astype(BF16)
    return x + 0.5 * _dot(act, wd_ref[...])


def _rpb_bucket(dist):
    n = jnp.maximum(dist, 0)
    max_exact = RPB_BUCKETS // 2
    nf = jnp.maximum(n, 1).astype(F32)
    large = max_exact + jnp.floor(jnp.log(nf / max_exact) / math.log(RPB_MAX_DISTANCE / max_exact)
                                  * (RPB_BUCKETS - max_exact)).astype(jnp.int32)
    large = jnp.minimum(large, RPB_BUCKETS - 1)
    return jnp.where(n < max_exact, n, large)


def _rpb_bias_tiles(table_ref, h, slot, diag_ref, sub_ref):
    width = 4 * MOBA_BLOCK
    sublanes = F32_SUBLANES
    dist = lax.broadcasted_iota(jnp.int32, (sublanes, width), 1) - MOBA_BLOCK
    bucket = _rpb_bucket(dist)
    f = jnp.zeros((sublanes, width), F32)
    for b in range(RPB_BUCKETS):
        f = jnp.where(bucket == b, table_ref[b, h], f)
    f = f * LOG2E
    sub = lax.broadcasted_iota(jnp.int32, (sublanes, width), 0)
    group = f
    for r in range(1, sublanes):
        group = jnp.where(sub == r, pltpu.roll(f, r, 1), group)
    ql = lax.broadcasted_iota(jnp.int32, (sublanes, MOBA_BLOCK), 1)
    kl0 = lax.broadcasted_iota(jnp.int32, (sublanes, MOBA_BLOCK), 0)
    for g in range(MOBA_BLOCK // sublanes):
        rows = slice(g * sublanes, (g + 1) * sublanes)
        blk = pltpu.roll(group, g * sublanes, 1) if g else group
        diag_ref[slot, rows] = jnp.where(ql >= kl0 + g * sublanes, blk[:, MOBA_BLOCK:2 * MOBA_BLOCK], MASK_VALUE)
        sub_ref[slot, rows] = blk[:, 2 * MOBA_BLOCK:3 * MOBA_BLOCK]


def _prep_kernel(table_ref, *refs):
    n = (len(refs) - 2) // 2
    weights_f32, (diag_ref, sub_ref), weights_bf16 = refs[:n], refs[n:n + 2], refs[n + 2:]
    for slot in range(PREP_HEADS_PER_STEP):
        _rpb_bias_tiles(table_ref, pl.program_id(0) * PREP_HEADS_PER_STEP + slot, slot, diag_ref, sub_ref)
    for src_ref, dst_ref in zip(weights_f32, weights_bf16):
        dst_ref[...] = src_ref[...].astype(BF16)


def _prep(rpb_table, weights):
    steps = ATTN_HEADS // PREP_HEADS_PER_STEP
    tile = jax.ShapeDtypeStruct((ATTN_HEADS, MOBA_BLOCK, MOBA_BLOCK), F32)
    tile_spec = pl.BlockSpec((PREP_HEADS_PER_STEP, MOBA_BLOCK, MOBA_BLOCK), lambda s: (s, 0, 0))
    slabs = [w.reshape(steps, w.shape[0] // steps, w.shape[1]) for w in weights]
    assert all(w.shape[1] % BF16_SUBLANES == 0 for w in slabs)
    slab_specs = [pl.BlockSpec((1,) + w.shape[1:], lambda s: (s, 0, 0)) for w in slabs]
    outs = pl.pallas_call(
        _prep_kernel,
        grid=(steps,),
        in_specs=[pl.BlockSpec(memory_space=pltpu.SMEM), *slab_specs],
        out_specs=[tile_spec, tile_spec, *slab_specs],
        out_shape=[tile, tile] + [jax.ShapeDtypeStruct(w.shape, BF16) for w in slabs],
        compiler_params=pltpu.CompilerParams(
            dimension_semantics=("arbitrary",), vmem_limit_bytes=V7X_VMEM_LIMIT_BYTES),
        name="prep",
    )(rpb_table, *slabs)
    return outs[0], outs[1], [o.reshape(w.shape) for o, w in zip(outs[2:], weights)]


def _ffn1_qkv_kernel(x_ref, g1_ref, wg_ref, wu_ref, wd_ref, gm_ref, wqkv_f32_ref, *refs):
    later_f32, (h1_ref, un_ref, qt_ref, k_ref, vt_ref), later_bf16, wqkv_ref = (
        refs[:N_LATER_WEIGHTS], refs[N_LATER_WEIGHTS:N_LATER_WEIGHTS + 5], refs[N_LATER_WEIGHTS + 5:-1], refs[-1])

    @pl.when((pl.program_id(0) == 0) & (pl.program_id(1) == 0))
    def _():
        wqkv_ref[...] = wqkv_f32_ref[...].astype(BF16)

    for src_ref, dst_ref in zip(later_f32, later_bf16):
        dst_ref[...] = src_ref[...].astype(BF16)
    h1 = _swiglu_half_step(x_ref[0], g1_ref, wg_ref, wu_ref, wd_ref)
    h1_ref[0] = h1
    un = _rms(h1, gm_ref[...]).astype(BF16)
    un_ref[0] = un
    qkv = _dot(un, wqkv_ref[...])
    qt_ref[0] = (qkv[:, :ATTN_WIDTH] * Q_SCALE).T.astype(BF16)
    k_ref[0] = qkv[:, ATTN_WIDTH:2 * ATTN_WIDTH].astype(BF16)
    for c in range(FFN_TOKENS // MOBA_BLOCK):
        vt_ref[0, c] = qkv[c * MOBA_BLOCK:(c + 1) * MOBA_BLOCK, 2 * ATTN_WIDTH:].T.astype(BF16)


def _ffn1_qkv(x, g1, wg, wu, wd, gm, w_in, later_weights):
    B, S, D = x.shape
    tm = FFN_TOKENS
    nblk = S // MOBA_BLOCK
    tiles_per_seq = S // tm
    steps = B * tiles_per_seq
    assert len(later_weights) == N_LATER_WEIGHTS
    step_of = lambda b, i: b * tiles_per_seq + i

    def slab_view(w):
        n = max(n for n in range(1, steps + 1) if steps % n == 0 and w.shape[0] % (n * BF16_SUBLANES) == 0)
        view = w.reshape(n, w.shape[0] // n, w.shape[1])
        return view, pl.BlockSpec((1,) + view.shape[1:], lambda b, i: (step_of(b, i) // (steps // n), 0, 0))

    slabs, slab_specs = zip(*[slab_view(w) for w in later_weights])
    tok = lambda width, dt: (pl.BlockSpec((1, tm, width), lambda b, i: (b, i, 0)),
                             jax.ShapeDtypeStruct((B, S, width), dt))
    h1_spec, h1_shape = tok(D, F32)
    un_spec, un_shape = tok(D, BF16)
    k_spec, k_shape = tok(ATTN_WIDTH, BF16)
    outs = pl.pallas_call(
        _ffn1_qkv_kernel,
        grid=(B, tiles_per_seq),
        in_specs=[pl.BlockSpec((1, tm, D), lambda b, i: (b, i, 0)),
                  _resident(g1.shape), _resident(wg.shape), _resident(wu.shape), _resident(wd.shape),
                  _resident(gm.shape), _resident((D, 3 * ATTN_WIDTH)), *slab_specs],
        out_specs=[h1_spec, un_spec,
                   pl.BlockSpec((1, ATTN_WIDTH, tm), lambda b, i: (b, 0, i)),
                   k_spec,
                   pl.BlockSpec((1, tm // MOBA_BLOCK, ATTN_WIDTH, MOBA_BLOCK), lambda b, i: (b, i, 0, 0)),
                   *slab_specs],
        out_shape=[h1_shape, un_shape,
                   jax.ShapeDtypeStruct((B, ATTN_WIDTH, S), BF16),
                   k_shape,
                   jax.ShapeDtypeStruct((B, nblk, ATTN_WIDTH, MOBA_BLOCK), BF16)]
                  + [jax.ShapeDtypeStruct(w.shape, BF16) for w in slabs],
        scratch_shapes=[pltpu.VMEM((D, 3 * ATTN_WIDTH), BF16)],
        compiler_params=pltpu.CompilerParams(
            dimension_semantics=("arbitrary", "arbitrary"), vmem_limit_bytes=V7X_VMEM_LIMIT_BYTES),
        name="ffn1_qkv",
    )(x, g1, wg, wu, wd, gm, w_in, *slabs)
    return outs[:5], [o.reshape(w.shape) for o, w in zip(outs[5:], later_weights)]


def _moba_attn_kernel(table_ref, qt_ref, k_ref, vt_ref, diag_ref, sub_ref, a_ref,
                      kmhi_ref, kmlo_ref, rb_far_ref, rb_sub_ref, at_ref, qz_ref,
                      m_ref, l_ref, mblk_ref, alpha_ref, s_ref, p_ref):
    qi = pl.program_id(1)
    nblk = k_ref.shape[2] // MOBA_BLOCK
    nq = MOBA_BLOCK
    units = [(r, h) for r in range(ATTN_ROWS) for h in range(ATTN_HEADS)]

    @pl.when(qi == 0)
    def _():
        for r in range(ATTN_ROWS):
            blk_row = lax.broadcasted_iota(jnp.int32, (nblk, ATTN_WIDTH), 0)
            km = jnp.zeros((nblk, ATTN_WIDTH), F32)
            for j in range(nblk):
                mean_j = jnp.mean(k_ref[r, 0, j * MOBA_BLOCK:(j + 1) * MOBA_BLOCK, :].astype(F32), axis=0,
                                  keepdims=True)
                km = jnp.where(blk_row == j, mean_j, km)
            km = jnp.concatenate([km] * ATTN_HEADS, axis=0)
            row_head = lax.broadcasted_iota(jnp.int32, km.shape, 0) // nblk
            col_head = lax.broadcasted_iota(jnp.int32, km.shape, 1) // HEAD_DIM
            km = jnp.where(row_head == col_head, km, 0.0)
            hi = km.astype(BF16)
            kmhi_ref[r] = hi
            kmlo_ref[r] = (km - hi.astype(F32)).astype(BF16)

    def select_blocks():
        jrow = lax.broadcasted_iota(jnp.int32, (nblk, nq), 0)
        for r in range(ATTN_ROWS):
            qt = qt_ref[r, 0]
            gates = _dot(kmhi_ref[r], qt) + _dot(kmlo_ref[r], qt)
            for h in range(ATTN_HEADS):
                u = r * ATTN_HEADS + h
                g = gates[h * nblk:(h + 1) * nblk]
                cnt = jnp.zeros((nblk, nq), jnp.int32)
                for jp in range(nblk):
                    gb = jnp.broadcast_to(g[jp:jp + 1], (nblk, nq))
                    beats = (gb > g) | ((gb == g) & (jp < jrow))
                    cnt = cnt + jnp.where(beats, jnp.where(jp < qi, 1, 0), 0)
                sel = (jrow < qi) & (cnt < MOBA_TOPK)
                far = jnp.where(sel, table_ref[RPB_BUCKETS - 1, h] * LOG2E, MASK_VALUE)
                sub = jnp.where(sel, 0.0, MASK_VALUE)
                for j in range(nblk):
                    rb_far_ref[u * nblk + j] = far[j:j + 1]
                    rb_sub_ref[u * nblk + j] = sub[j:j + 1]

    zeros_half = jnp.zeros((HEAD_DIM, nq), BF16)
    for r, h in units:
        qh = qt_ref[r, 0, h * HEAD_DIM:(h + 1) * HEAD_DIM, :]
        qz_ref[r * ATTN_HEADS + h] = jnp.concatenate([qh, zeros_half] if h % 2 == 0 else [zeros_half, qh], axis=0)

    def scores_stage(j, kind):
        start = pl.multiple_of(j * MOBA_BLOCK, MOBA_BLOCK)
        for u, (r, h) in enumerate(units):
            pair = h // 2
            kblk = k_ref[r, 0, pl.ds(start, MOBA_BLOCK), pair * PAIR_WIDTH:(pair + 1) * PAIR_WIDTH]
            s = _dot(kblk, qz_ref[u])
            if kind == "diag":
                s = s + diag_ref[h]
            elif kind == "sub":
                s = s + sub_ref[h]
            s_ref[u] = s
            mblk_ref[u] = jnp.max(s, axis=0, keepdims=True)

    def softmax_stage(row_bias):
        for u in range(len(units)):
            rb = row_bias(u)
            m_old = m_ref[u]
            m_blk = mblk_ref[u] if rb is None else mblk_ref[u] + rb
            m_new = jnp.maximum(m_old, m_blk)
            offset = m_new if rb is None else m_new - rb
            p_ref[u] = jnp.exp2(s_ref[u] - offset).astype(BF16)
            alpha_ref[u] = jnp.exp2(m_old - m_new)
            m_ref[u] = m_new

    ones_rows = jnp.ones((BF16_SUBLANES, nq), BF16)

    def values_stage(j):
        for u, (r, h) in enumerate(units):
            rows = slice(u * HEAD_DIM, (u + 1) * HEAD_DIM)
            v_h = vt_ref[r, 0, j, h * HEAD_DIM:(h + 1) * HEAD_DIM, :]
            pv = _dot(jnp.concatenate([v_h, ones_rows], axis=0), p_ref[u])
            alpha = alpha_ref[u]
            at_ref[rows] = alpha * at_ref[rows] + pv[:HEAD_DIM]
            l_ref[u] = alpha * l_ref[u] + pv[HEAD_DIM:HEAD_DIM + 1]

    def finish():
        for u in range(len(units)):
            rows = slice(u * HEAD_DIM, (u + 1) * HEAD_DIM)
            at_ref[rows] = at_ref[rows] * (1.0 / l_ref[u])
        for r in range(ATTN_ROWS):
            a_ref[r, 0] = at_ref[r * ATTN_WIDTH:(r + 1) * ATTN_WIDTH, :].T.astype(BF16)

    m_ref[...] = jnp.full(m_ref.shape, MASK_VALUE, F32)
    l_ref[...] = jnp.zeros(l_ref.shape, F32)
    at_ref[...] = jnp.zeros(at_ref.shape, F32)

    @pl.when(qi == 0)
    def _():
        scores_stage(qi, "diag")
        softmax_stage(lambda u: None)
        values_stage(qi)
        finish()

    @pl.when(qi == 1)
    def _():
        scores_stage(qi, "diag")
        softmax_stage(lambda u: None)
        scores_stage(0, "sub")
        values_stage(qi)
        softmax_stage(lambda u: None)
        values_stage(0)
        finish()

    @pl.when(qi > 1)
    def _():
        select_blocks()
        j_sub = qi - 1
        n_far = j_sub

        scores_stage(qi, "diag")
        softmax_stage(lambda u: None)
        scores_stage(j_sub, "sub")
        values_stage(qi)
        softmax_stage(lambda u: rb_sub_ref[u * nblk + j_sub])
        scores_stage(0, "far")

        def far_body(t, carry):
            values_stage(jnp.where(t == 0, j_sub, t - 1))
            softmax_stage(lambda u: rb_far_ref[u * nblk + t])
            scores_stage(t + 1, "far")
            return carry

        lax.fori_loop(0, n_far - 1, far_body, 0)
        values_stage(jnp.where(n_far == 1, j_sub, n_far - 2))
        softmax_stage(lambda u: rb_far_ref[u * nblk + n_far - 1])
        values_stage(n_far - 1)
        finish()


def _moba_attn(qt, k, vt, diag, sub, rpb_table):
    B, S, _ = k.shape
    nblk = S // MOBA_BLOCK
    groups = B // ATTN_ROWS
    n_units = ATTN_ROWS * ATTN_HEADS
    split = lambda x: x.reshape((ATTN_ROWS, groups) + x.shape[1:])
    tile = pl.BlockSpec((ATTN_HEADS, MOBA_BLOCK, MOBA_BLOCK), lambda g, i: (0, 0, 0), pipeline_mode=pl.Buffered(1))
    a = pl.pallas_call(
        _moba_attn_kernel,
        grid=(groups, nblk),
        in_specs=[pl.BlockSpec(memory_space=pltpu.SMEM),
                  pl.BlockSpec((ATTN_ROWS, 1, ATTN_WIDTH, MOBA_BLOCK), lambda g, i: (0, g, 0, i)),
                  pl.BlockSpec((ATTN_ROWS, 1, S, ATTN_WIDTH), lambda g, i: (0, g, 0, 0)),
                  pl.BlockSpec((ATTN_ROWS, 1, nblk, ATTN_WIDTH, MOBA_BLOCK), lambda g, i: (0, g, 0, 0, 0)),
                  tile, tile],
        out_specs=pl.BlockSpec((ATTN_ROWS, 1, MOBA_BLOCK, ATTN_WIDTH), lambda g, i: (0, g, i, 0)),
        out_shape=jax.ShapeDtypeStruct((ATTN_ROWS, groups, S, ATTN_WIDTH), BF16),
        scratch_shapes=[pltpu.VMEM((ATTN_ROWS, ATTN_HEADS * nblk, ATTN_WIDTH), BF16),
                        pltpu.VMEM((ATTN_ROWS, ATTN_HEADS * nblk, ATTN_WIDTH), BF16),
                        pltpu.VMEM((n_units * nblk, 1, MOBA_BLOCK), F32),
                        pltpu.VMEM((n_units * nblk, 1, MOBA_BLOCK), F32),
                        pltpu.VMEM((n_units * HEAD_DIM, MOBA_BLOCK), F32),
                        pltpu.VMEM((n_units, 2 * HEAD_DIM, MOBA_BLOCK), BF16),
                        *[pltpu.VMEM((n_units, 1, MOBA_BLOCK), F32)] * 4,
                        pltpu.VMEM((n_units, MOBA_BLOCK, MOBA_BLOCK), F32),
                        pltpu.VMEM((n_units, MOBA_BLOCK, MOBA_BLOCK), BF16)],
        compiler_params=pltpu.CompilerParams(
            dimension_semantics=("arbitrary", "arbitrary"), vmem_limit_bytes=V7X_VMEM_LIMIT_BYTES),
        name="moba_attn",
    )(rpb_table, split(qt), split(k), split(vt), diag, sub)
    return a.reshape(B, S, ATTN_WIDTH)


def _mix_merge_kernel(h1_ref, un_ref, unprev_ref, a_ref, win_ref, wgrp_ref, scale_ref,
                      wa_ref, wp_ref, wout_ref, h2_ref, zext_ref):
    i = pl.program_id(1)
    tm = MIX_TOKENS
    z0 = 3 * ATTN_WIDTH
    for c in range(MIX_SUBTILES):
        rows = slice(c * tm, (c + 1) * tm)
        zx_ref = zext_ref.at[c]
        zg = _dot(un_ref[0, rows], win_ref[:, z0:])
        halo = unprev_ref[0] if c == 0 else un_ref[0, c * tm - POOL_HALO:c * tm]
        zprev = _dot(halo, win_ref[:, z0:z0 + POOL_WIDTH])
        zx_ref[:POOL_HALO] = jnp.where(i > 0, zprev, 0.0) if c == 0 else zprev
        zx_ref[POOL_HALO:] = zg[:, :POOL_WIDTH]

        tpos = (i * MIX_SUBTILES + c) * tm + lax.broadcasted_iota(jnp.int32, (tm, POOL_GROUP_WIDTH), 0)
        mixed = []
        for g, w in enumerate(POOL_WINDOWS):
            cols = slice(g * POOL_GROUP_WIDTH, (g + 1) * POOL_GROUP_WIDTH)
            wsum = zx_ref[POOL_HALO:, cols]
            for lag in range(1, w):
                wsum = wsum + zx_ref[POOL_HALO - lag:POOL_HALO - lag + tm, cols]
            mean = wsum / jnp.minimum(tpos + 1, w).astype(F32)
            pooled = (mean - zx_ref[POOL_HALO:, cols]).astype(BF16)
            mixed.append(_dot(pooled, wgrp_ref[g].astype(BF16)))
        p = (jnp.concatenate(mixed, axis=1) * scale_ref[...]).astype(BF16)

        g_attn = zg[:, POOL_WIDTH:POOL_WIDTH + D_MODEL]
        g_pool = zg[:, POOL_WIDTH + D_MODEL:]
        merged = (jax.nn.sigmoid(g_attn) * _dot(a_ref[0, rows], wa_ref[...])
                  + jax.nn.sigmoid(g_pool) * _dot(p, wp_ref[...]))
        h2_ref[0, rows] = h1_ref[0, rows] + _dot(merged.astype(BF16), wout_ref[...])


def _mix_merge(h1, un, a, win, wgrp, scale, wa, wp, wout):
    B, S, D = h1.shape
    tm = MIX_SUBTILES * MIX_TOKENS
    halo_per_tile = tm // POOL_HALO
    return pl.pallas_call(
        _mix_merge_kernel,
        grid=(B, S // tm),
        in_specs=[pl.BlockSpec((1, tm, D), lambda b, i: (b, i, 0)),
                  pl.BlockSpec((1, tm, D), lambda b, i: (b, i, 0)),
                  pl.BlockSpec((1, POOL_HALO, D), lambda b, i: (b, jnp.maximum(i * halo_per_tile - 1, 0), 0)),
                  pl.BlockSpec((1, tm, ATTN_WIDTH), lambda b, i: (b, i, 0)),
                  _resident(win.shape), _resident(wgrp.shape), _resident(scale.shape),
                  _resident(wa.shape), _resident(wp.shape), _resident(wout.shape)],
        out_specs=pl.BlockSpec((1, tm, D), lambda b, i: (b, i, 0)),
        out_shape=jax.ShapeDtypeStruct((B, S, D), F32),
        scratch_shapes=[pltpu.VMEM((MIX_SUBTILES, POOL_HALO + MIX_TOKENS, POOL_WIDTH), F32)],
        compiler_params=pltpu.CompilerParams(
            dimension_semantics=("arbitrary", "arbitrary"), vmem_limit_bytes=V7X_VMEM_LIMIT_BYTES),
        name="mix_merge",
    )(h1, un, un, a, win, wgrp, scale, wa, wp, wout)


def _ffn2_final_kernel(h_ref, g2_ref, wg_ref, wu_ref, wd_ref, gf_ref, out_ref):
    for c in range(FFN2_SUBTILES):
        rows = slice(c * FFN_TOKENS, (c + 1) * FFN_TOKENS)
        h3 = _swiglu_half_step(h_ref[0, rows], g2_ref, wg_ref, wu_ref, wd_ref)
        out_ref[0, rows] = _rms(h3, gf_ref[...])


def _ffn2_final(h, g2, wg, wu, wd, gf):
    B, S, D = h.shape
    tm = FFN2_SUBTILES * FFN_TOKENS
    return pl.pallas_call(
        _ffn2_final_kernel,
        grid=(B, S // tm),
        in_specs=[pl.BlockSpec((1, tm, D), lambda b, i: (b, i, 0)),
                  _resident(g2.shape), _resident(wg.shape), _resident(wu.shape), _resident(wd.shape),
                  _resident(gf.shape)],
        out_specs=pl.BlockSpec((1, tm, D), lambda b, i: (b, i, 0)),
        out_shape=jax.ShapeDtypeStruct((B, S, D), F32),
        compiler_params=pltpu.CompilerParams(
            dimension_semantics=("arbitrary", "arbitrary"), vmem_limit_bytes=V7X_VMEM_LIMIT_BYTES),
        name="ffn2_final",
    )(h, g2, wg, wu, wd, gf)


def kernel(x, ffn1_norm, ffn1_w_gate, ffn1_w_up, ffn1_w_down, mix_norm, w_in, pool_w_group, pool_scale,
           w_branch_attn, w_branch_pool, w_out, ffn2_norm, ffn2_w_gate, ffn2_w_up, ffn2_w_down,
           rpb_table, final_norm):
    B, S, D = x.shape
    assert (D, ffn1_w_gate.shape[0]) == (D_MODEL, 1)
    assert S % MOBA_BLOCK == 0 and S % (FFN2_SUBTILES * FFN_TOKENS) == 0 and S % (MIX_SUBTILES * MIX_TOKENS) == 0
    assert B % ATTN_ROWS == 0
    bf = lambda w: w.astype(BF16)
    row = lambda v: v.reshape(1, -1)
    w_in = w_in[0]

    diag, sub, (wg1_bf, wu1_bf, wd1_bf) = _prep(rpb_table, [ffn1_w_gate[0], ffn1_w_up[0], ffn1_w_down[0]])
    (h1, un, qt, k, vt), later = _ffn1_qkv(
        x, row(ffn1_norm[0]), wg1_bf, wu1_bf, wd1_bf, row(mix_norm[0]), w_in,
        [w_in, w_branch_attn[0], w_branch_pool[0], w_out[0], ffn2_w_gate[0], ffn2_w_up[0], ffn2_w_down[0]])
    win_bf, wa_bf, wp_bf, wout_bf, wg2_bf, wu2_bf, wd2_bf = later
    a = _moba_attn(qt, k, vt, diag, sub, rpb_table)
    h2 = _mix_merge(h1, un, a, win_bf, pool_w_group[0], row(pool_scale[0]), wa_bf, wp_bf, wout_bf)
    return _ffn2_final(h2, row(ffn2_norm[0]), wg2_bf, wu2_bf, wd2_bf, row(final_norm))
```

```python
import math

import jax
import jax.numpy as jnp
from jax import lax
from jax.experimental import pallas as pl
from jax.experimental.pallas import tpu as pltpu

D_MODEL = 1024
HEAD_DIM = 64
ATTN_WIDTH = 512
ATTN_HEADS = 8
MOBA_BLOCK = 256
MOBA_TOPK = 3
POOL_WINDOWS = (2, 4, 8, 16)
POOL_WIDTH = 512
POOL_GROUP_WIDTH = 128
POOL_HALO = 16
RPB_BUCKETS = 32
RPB_MAX_DISTANCE = 128
RMS_EPS = 1e-6
MASK_VALUE = -1e30
LOG2E = math.log2(math.e)
Q_SCALE = HEAD_DIM ** -0.5 * LOG2E
ATTN_ROWS = 2
V7X_VMEM_LIMIT_BYTES = 56 * 1024 * 1024

FFN_TOKENS = 512
FFN2_SUBTILES = 2
MIX_TOKENS = 512
MIX_SUBTILES = 2
N_LATER_WEIGHTS = 7
F32_SUBLANES = 8
BF16_SUBLANES = 16
PAIR_WIDTH = 2 * HEAD_DIM
PREP_HEADS_PER_STEP = 2

BF16 = jnp.bfloat16
F32 = jnp.float32


def _resident(shape):
    nd = len(shape)
    return pl.BlockSpec(shape, lambda *_: (0,) * nd, pipeline_mode=pl.Buffered(1))


def _dot(a, b):
    return jnp.dot(a, b, preferred_element_type=F32)


def _rms(x, g):
    return x * lax.rsqrt(jnp.mean(x * x, axis=-1, keepdims=True) + RMS_EPS) * g


def _swiglu_half_step(x, g_ref, wg_ref, wu_ref, wd_ref):
    xn = _rms(x, g_ref[...]).astype(BF16)
    gate = _dot(xn, wg_ref[...])
    up = _dot(xn, wu_ref[...])
    act = (gate * jax.nn.sigmoid(gate) * up).astype(BF16)
    return x + 0.5 * _dot(act, wd_ref[...])


def _rpb_bucket(dist):
    n = jnp.maximum(dist, 0)
    max_exact = RPB_BUCKETS // 2
    nf = jnp.maximum(n, 1).astype(F32)
    large = max_exact + jnp.floor(jnp.log(nf / max_exact) / math.log(RPB_MAX_DISTANCE / max_exact)
                                  * (RPB_BUCKETS - max_exact)).astype(jnp.int32)
    large = jnp.minimum(large, RPB_BUCKETS - 1)
    return jnp.where(n < max_exact, n, large)


def _rpb_bias_tiles(table_ref, h, slot, diag_ref, sub_ref):
    width = 4 * MOBA_BLOCK
    sublanes = F32_SUBLANES
    dist = lax.broadcasted_iota(jnp.int32, (sublanes, width), 1) - MOBA_BLOCK
    bucket = _rpb_bucket(dist)
    f = jnp.zeros((sublanes, width), F32)
    for b in range(RPB_BUCKETS):
        f = jnp.where(bucket == b, table_ref[b, h], f)
    f = f * LOG2E
    sub = lax.broadcasted_iota(jnp.int32, (sublanes, width), 0)
    group = f
    for r in range(1, sublanes):
        group = jnp.where(sub == r, pltpu.roll(f, r, 1), group)
    ql = lax.broadcasted_iota(jnp.int32, (sublanes, MOBA_BLOCK), 1)
    kl0 = lax.broadcasted_iota(jnp.int32, (sublanes, MOBA_BLOCK), 0)
    for g in range(MOBA_BLOCK // sublanes):
        rows = slice(g * sublanes, (g + 1) * sublanes)
        blk = pltpu.roll(group, g * sublanes, 1) if g else group
        diag_ref[slot, rows] = jnp.where(ql >= kl0 + g * sublanes, blk[:, MOBA_BLOCK:2 * MOBA_BLOCK], MASK_VALUE)
        sub_ref[slot, rows] = blk[:, 2 * MOBA_BLOCK:3 * MOBA_BLOCK]


def _prep_kernel(table_ref, *refs):
    n = (len(refs) - 2) // 2
    weights_f32, (diag_ref, sub_ref), weights_bf16 = refs[:n], refs[n:n + 2], refs[n + 2:]
    for slot in range(PREP_HEADS_PER_STEP):
        _rpb_bias_tiles(table_ref, pl.program_id(0) * PREP_HEADS_PER_STEP + slot, slot, diag_ref, sub_ref)
    for src_ref, dst_ref in zip(weights_f32, weights_bf16):
        dst_ref[...] = src_ref[...].astype(BF16)


def _prep(rpb_table, weights):
    steps = ATTN_HEADS // PREP_HEADS_PER_STEP
    tile = jax.ShapeDtypeStruct((ATTN_HEADS, MOBA_BLOCK, MOBA_BLOCK), F32)
    tile_spec = pl.BlockSpec((PREP_HEADS_PER_STEP, MOBA_BLOCK, MOBA_BLOCK), lambda s: (s, 0, 0))
    slabs = [w.reshape(steps, w.shape[0] // steps, w.shape[1]) for w in weights]
    assert all(w.shape[1] % BF16_SUBLANES == 0 for w in slabs)
    slab_specs = [pl.BlockSpec((1,) + w.shape[1:], lambda s: (s, 0, 0)) for w in slabs]
    outs = pl.pallas_call(
        _prep_kernel,
        grid=(steps,),
        in_specs=[pl.BlockSpec(memory_space=pltpu.SMEM), *slab_specs],
        out_specs=[tile_spec, tile_spec, *slab_specs],
        out_shape=[tile, tile] + [jax.ShapeDtypeStruct(w.shape, BF16) for w in slabs],
        compiler_params=pltpu.CompilerParams(
            dimension_semantics=("arbitrary",), vmem_limit_bytes=V7X_VMEM_LIMIT_BYTES),
        name="prep",
    )(rpb_table, *slabs)
    return outs[0], outs[1], [o.reshape(w.shape) for o, w in zip(outs[2:], weights)]


def _ffn1_qkv_kernel(x_ref, g1_ref, wg_ref, wu_ref, wd_ref, gm_ref, wqkv_f32_ref, *refs):
    later_f32, (h1_ref, un_ref, qt_ref, k_ref, vt_ref), later_bf16, wqkv_ref = (
        refs[:N_LATER_WEIGHTS], refs[N_LATER_WEIGHTS:N_LATER_WEIGHTS + 5], refs[N_LATER_WEIGHTS + 5:-1], refs[-1])

    @pl.when((pl.program_id(0) == 0) & (pl.program_id(1) == 0))
    def _():
        wqkv_ref[...] = wqkv_f32_ref[...].astype(BF16)

    for src_ref, dst_ref in zip(later_f32, later_bf16):
        dst_ref[...] = src_ref[...].astype(BF16)
    h1 = _swiglu_half_step(x_ref[0], g1_ref, wg_ref, wu_ref, wd_ref)
    h1_ref[0] = h1
    un = _rms(h1, gm_ref[...]).astype(BF16)
    un_ref[0] = un
    qkv = _dot(un, wqkv_ref[...])
    qt_ref[0] = (qkv[:, :ATTN_WIDTH] * Q_SCALE).T.astype(BF16)
    k_ref[0] = qkv[:, ATTN_WIDTH:2 * ATTN_WIDTH].astype(BF16)
    for c in range(FFN_TOKENS // MOBA_BLOCK):
        vt_ref[0, c] = qkv[c * MOBA_BLOCK:(c + 1) * MOBA_BLOCK, 2 * ATTN_WIDTH:].T.astype(BF16)


def _ffn1_qkv(x, g1, wg, wu, wd, gm, w_in, later_weights):
    B, S, D = x.shape
    tm = FFN_TOKENS
    nblk = S // MOBA_BLOCK
    tiles_per_seq = S // tm
    steps = B * tiles_per_seq
    assert len(later_weights) == N_LATER_WEIGHTS
    step_of = lambda b, i: b * tiles_per_seq + i

    def slab_view(w):
        n = max(n for n in range(1, steps + 1) if steps % n == 0 and w.shape[0] % (n * BF16_SUBLANES) == 0)
        view = w.reshape(n, w.shape[0] // n, w.shape[1])
        return view, pl.BlockSpec((1,) + view.shape[1:], lambda b, i: (step_of(b, i) // (steps // n), 0, 0))

    slabs, slab_specs = zip(*[slab_view(w) for w in later_weights])
    tok = lambda width, dt: (pl.BlockSpec((1, tm, width), lambda b, i: (b, i, 0)),
                             jax.ShapeDtypeStruct((B, S, width), dt))
    h1_spec, h1_shape = tok(D, F32)
    un_spec, un_shape = tok(D, BF16)
    k_spec, k_shape = tok(ATTN_WIDTH, BF16)
    outs = pl.pallas_call(
        _ffn1_qkv_kernel,
        grid=(B, tiles_per_seq),
        in_specs=[pl.BlockSpec((1, tm, D), lambda b, i: (b, i, 0)),
                  _resident(g1.shape), _resident(wg.shape), _resident(wu.shape), _resident(wd.shape),
                  _resident(gm.shape), _resident((D, 3 * ATTN_WIDTH)), *slab_specs],
        out_specs=[h1_spec, un_spec,
                   pl.BlockSpec((1, ATTN_WIDTH, tm), lambda b, i: (b, 0, i)),
                   k_spec,
                   pl.BlockSpec((1, tm // MOBA_BLOCK, ATTN_WIDTH, MOBA_BLOCK), lambda b, i: (b, i, 0, 0)),
                   *slab_specs],
        out_shape=[h1_shape, un_shape,
                   jax.ShapeDtypeStruct((B, ATTN_WIDTH, S), BF16),
                   k_shape,
                   jax.ShapeDtypeStruct((B, nblk, ATTN_WIDTH, MOBA_BLOCK), BF16)]
                  + [jax.ShapeDtypeStruct(w.shape, BF16) for w in slabs],
        scratch_shapes=[pltpu.VMEM((D, 3 * ATTN_WIDTH), BF16)],
        compiler_params=pltpu.CompilerParams(
            dimension_semantics=("arbitrary", "arbitrary"), vmem_limit_bytes=V7X_VMEM_LIMIT_BYTES),
        name="ffn1_qkv",
    )(x, g1, wg, wu, wd, gm, w_in, *slabs)
    return outs[:5], [o.reshape(w.shape) for o, w in zip(outs[5:], later_weights)]


def _moba_attn_kernel(table_ref, qt_ref, k_ref, vt_ref, diag_ref, sub_ref, a_ref,
                      kmhi_ref, kmlo_ref, rb_far_ref, rb_sub_ref, at_ref, qz_ref,
                      m_ref, l_ref, mblk_ref, alpha_ref, s_ref, p_ref):
    qi = pl.program_id(1)
    nblk = k_ref.shape[2] // MOBA_BLOCK
    nq = MOBA_BLOCK
    units = [(r, h) for r in range(ATTN_ROWS) for h in range(ATTN_HEADS)]

    @pl.when(qi == 0)
    def _():
        for r in range(ATTN_ROWS):
            blk_row = lax.broadcasted_iota(jnp.int32, (nblk, ATTN_WIDTH), 0)
            km = jnp.zeros((nblk, ATTN_WIDTH), F32)
            for j in range(nblk):
                mean_j = jnp.mean(k_ref[r, 0, j * MOBA_BLOCK:(j + 1) * MOBA_BLOCK, :].astype(F32), axis=0,
                                  keepdims=True)
                km = jnp.where(blk_row == j, mean_j, km)
            km = jnp.concatenate([km] * ATTN_HEADS, axis=0)
            row_head = lax.broadcasted_iota(jnp.int32, km.shape, 0) // nblk
            col_head = lax.broadcasted_iota(jnp.int32, km.shape, 1) // HEAD_DIM
            km = jnp.where(row_head == col_head, km, 0.0)
            hi = km.astype(BF16)
            kmhi_ref[r] = hi
            kmlo_ref[r] = (km - hi.astype(F32)).astype(BF16)

    def select_blocks():
        jrow = lax.broadcasted_iota(jnp.int32, (nblk, nq), 0)
        for r in range(ATTN_ROWS):
            qt = qt_ref[r, 0]
            gates = _dot(kmhi_ref[r], qt) + _dot(kmlo_ref[r], qt)
            for h in range(ATTN_HEADS):
                u = r * ATTN_HEADS + h
                g = gates[h * nblk:(h + 1) * nblk]
                cnt = jnp.zeros((nblk, nq), jnp.int32)
                for jp in range(nblk):
                    gb = jnp.broadcast_to(g[jp:jp + 1], (nblk, nq))
                    beats = (gb > g) | ((gb == g) & (jp < jrow))
                    cnt = cnt + jnp.where(beats, jnp.where(jp < qi, 1, 0), 0)
                sel = (jrow < qi) & (cnt < MOBA_TOPK)
                far = jnp.where(sel, table_ref[RPB_BUCKETS - 1, h] * LOG2E, MASK_VALUE)
                sub = jnp.where(sel, 0.0, MASK_VALUE)
                for j in range(nblk):
                    rb_far_ref[u * nblk + j] = far[j:j + 1]
                    rb_sub_ref[u * nblk + j] = sub[j:j + 1]

    def begin():
        m_ref[...] = jnp.full(m_ref.shape, MASK_VALUE, F32)
        l_ref[...] = jnp.zeros(l_ref.shape, F32)
        at_ref[...] = jnp.zeros(at_ref.shape, F32)
        zeros_half = jnp.zeros((HEAD_DIM, nq), BF16)
        for r, h in units:
            qh = qt_ref[r, 0, h * HEAD_DIM:(h + 1) * HEAD_DIM, :]
            qz_ref[r * ATTN_HEADS + h] = jnp.concatenate([qh, zeros_half] if h % 2 == 0 else [zeros_half, qh], axis=0)

    def scores_stage(j, kind):
        start = pl.multiple_of(j * MOBA_BLOCK, MOBA_BLOCK)
        for u, (r, h) in enumerate(units):
            pair = h // 2
            kblk = k_ref[r, 0, pl.ds(start, MOBA_BLOCK), pair * PAIR_WIDTH:(pair + 1) * PAIR_WIDTH]
            s = _dot(kblk, qz_ref[u])
            if kind == "diag":
                s = s + diag_ref[h]
            elif kind == "sub":
                s = s + sub_ref[h]
            s_ref[u] = s
            mblk_ref[u] = jnp.max(s, axis=0, keepdims=True)

    def softmax_stage(row_bias):
        for u in range(len(units)):
            rb = row_bias(u)
            m_old = m_ref[u]
            m_blk = mblk_ref[u] if rb is None else mblk_ref[u] + rb
            m_new = jnp.maximum(m_old, m_blk)
            offset = m_new if rb is None else m_new - rb
            p_ref[u] = jnp.exp2(s_ref[u] - offset).astype(BF16)
            alpha_ref[u] = jnp.exp2(m_old - m_new)
            m_ref[u] = m_new

    ones_rows = jnp.ones((BF16_SUBLANES, nq), BF16)

    def values_stage(j):
        for u, (r, h) in enumerate(units):
            rows = slice(u * HEAD_DIM, (u + 1) * HEAD_DIM)
            v_h = vt_ref[r, 0, j, h * HEAD_DIM:(h + 1) * HEAD_DIM, :]
            pv = _dot(jnp.concatenate([v_h, ones_rows], axis=0), p_ref[u])
            alpha = alpha_ref[u]
            at_ref[rows] = alpha * at_ref[rows] + pv[:HEAD_DIM]
            l_ref[u] = alpha * l_ref[u] + pv[HEAD_DIM:HEAD_DIM + 1]

    def finish():
        for u in range(len(units)):
            rows = slice(u * HEAD_DIM, (u + 1) * HEAD_DIM)
            at_ref[rows] = at_ref[rows] * (1.0 / l_ref[u])
        for r in range(ATTN_ROWS):
            a_ref[r, 0] = at_ref[r * ATTN_WIDTH:(r + 1) * ATTN_WIDTH, :].T.astype(BF16)

    @pl.when(qi == 0)
    def _():
        begin()
        scores_stage(qi, "diag")
        softmax_stage(lambda u: None)
        values_stage(qi)
        finish()

    @pl.when(qi == 1)
    def _():
        begin()
        scores_stage(qi, "diag")
        softmax_stage(lambda u: None)
        scores_stage(0, "sub")
        values_stage(qi)
        softmax_stage(lambda u: None)
        values_stage(0)
        finish()

    @pl.when(qi > 1)
    def _():
        begin()
        select_blocks()
        j_sub = qi - 1
        n_far = j_sub

        scores_stage(qi, "diag")
        softmax_stage(lambda u: None)
        scores_stage(j_sub, "sub")
        values_stage(qi)
        softmax_stage(lambda u: rb_sub_ref[u * nblk + j_sub])
        scores_stage(0, "far")

        def far_body(t, carry):
            values_stage(jnp.where(t == 0, j_sub, t - 1))
            softmax_stage(lambda u: rb_far_ref[u * nblk + t])
            scores_stage(t + 1, "far")
            return carry

        lax.fori_loop(0, n_far - 1, far_body, 0)
        values_stage(jnp.where(n_far == 1, j_sub, n_far - 2))
        softmax_stage(lambda u: rb_far_ref[u * nblk + n_far - 1])
        values_stage(n_far - 1)
        finish()


def _moba_attn(qt, k, vt, diag, sub, rpb_table):
    B, S, _ = k.shape
    nblk = S // MOBA_BLOCK
    groups = B // ATTN_ROWS
    n_units = ATTN_ROWS * ATTN_HEADS
    split = lambda x: x.reshape((ATTN_ROWS, groups) + x.shape[1:])
    tile = pl.BlockSpec((ATTN_HEADS, MOBA_BLOCK, MOBA_BLOCK), lambda g, i: (0, 0, 0), pipeline_mode=pl.Buffered(1))
    a = pl.pallas_call(
        _moba_attn_kernel,
        grid=(groups, nblk),
        in_specs=[pl.BlockSpec(memory_space=pltpu.SMEM),
                  pl.BlockSpec((ATTN_ROWS, 1, ATTN_WIDTH, MOBA_BLOCK), lambda g, i: (0, g, 0, i)),
                  pl.BlockSpec((ATTN_ROWS, 1, S, ATTN_WIDTH), lambda g, i: (0, g, 0, 0)),
                  pl.BlockSpec((ATTN_ROWS, 1, nblk, ATTN_WIDTH, MOBA_BLOCK), lambda g, i: (0, g, 0, 0, 0)),
                  tile, tile],
        out_specs=pl.BlockSpec((ATTN_ROWS, 1, MOBA_BLOCK, ATTN_WIDTH), lambda g, i: (0, g, i, 0)),
        out_shape=jax.ShapeDtypeStruct((ATTN_ROWS, groups, S, ATTN_WIDTH), BF16),
        scratch_shapes=[pltpu.VMEM((ATTN_ROWS, ATTN_HEADS * nblk, ATTN_WIDTH), BF16),
                        pltpu.VMEM((ATTN_ROWS, ATTN_HEADS * nblk, ATTN_WIDTH), BF16),
                        pltpu.VMEM((n_units * nblk, 1, MOBA_BLOCK), F32),
                        pltpu.VMEM((n_units * nblk, 1, MOBA_BLOCK), F32),
                        pltpu.VMEM((n_units * HEAD_DIM, MOBA_BLOCK), F32),
                        pltpu.VMEM((n_units, 2 * HEAD_DIM, MOBA_BLOCK), BF16),
                        *[pltpu.VMEM((n_units, 1, MOBA_BLOCK), F32)] * 4,
                        pltpu.VMEM((n_units, MOBA_BLOCK, MOBA_BLOCK), F32),
                        pltpu.VMEM((n_units, MOBA_BLOCK, MOBA_BLOCK), BF16)],
        compiler_params=pltpu.CompilerParams(
            dimension_semantics=("arbitrary", "arbitrary"), vmem_limit_bytes=V7X_VMEM_LIMIT_BYTES),
        name="moba_attn",
    )(rpb_table, split(qt), split(k), split(vt), diag, sub)
    return a.reshape(B, S, ATTN_WIDTH)


def _mix_merge_kernel(h1_ref, un_ref, unprev_ref, a_ref, win_ref, wgrp_ref, scale_ref,
                      wa_ref, wp_ref, wout_ref, h2_ref, zext_ref):
    i = pl.program_id(1)
    tm = MIX_TOKENS
    z0 = 3 * ATTN_WIDTH
    for c in range(MIX_SUBTILES):
        rows = slice(c * tm, (c + 1) * tm)
        zx_ref = zext_ref.at[c]
        zg = _dot(un_ref[0, rows], win_ref[:, z0:])
        halo = unprev_ref[0] if c == 0 else un_ref[0, c * tm - POOL_HALO:c * tm]
        zprev = _dot(halo, win_ref[:, z0:z0 + POOL_WIDTH])
        zx_ref[:POOL_HALO] = jnp.where(i > 0, zprev, 0.0) if c == 0 else zprev
        zx_ref[POOL_HALO:] = zg[:, :POOL_WIDTH]

        tpos = (i * MIX_SUBTILES + c) * tm + lax.broadcasted_iota(jnp.int32, (tm, POOL_GROUP_WIDTH), 0)
        mixed = []
        for g, w in enumerate(POOL_WINDOWS):
            cols = slice(g * POOL_GROUP_WIDTH, (g + 1) * POOL_GROUP_WIDTH)
            wsum = zx_ref[POOL_HALO:, cols]
            for lag in range(1, w):
                wsum = wsum + zx_ref[POOL_HALO - lag:POOL_HALO - lag + tm, cols]
            mean = wsum / jnp.minimum(tpos + 1, w).astype(F32)
            pooled = (mean - zx_ref[POOL_HALO:, cols]).astype(BF16)
            mixed.append(_dot(pooled, wgrp_ref[g].astype(BF16)))
        p = (jnp.concatenate(mixed, axis=1) * scale_ref[...]).astype(BF16)

        g_attn = zg[:, POOL_WIDTH:POOL_WIDTH + D_MODEL]
        g_pool = zg[:, POOL_WIDTH + D_MODEL:]
        merged = (jax.nn.sigmoid(g_attn) * _dot(a_ref[0, rows], wa_ref[...])
                  + jax.nn.sigmoid(g_pool) * _dot(p, wp_ref[...]))
        h2_ref[0, rows] = h1_ref[0, rows] + _dot(merged.astype(BF16), wout_ref[...])


def _mix_merge(h1, un, a, win, wgrp, scale, wa, wp, wout):
    B, S, D = h1.shape
    tm = MIX_SUBTILES * MIX_TOKENS
    halo_per_tile = tm // POOL_HALO
    return pl.pallas_call(
        _mix_merge_kernel,
        grid=(B, S // tm),
        in_specs=[pl.BlockSpec((1, tm, D), lambda b, i: (b, i, 0)),
                  pl.BlockSpec((1, tm, D), lambda b, i: (b, i, 0)),
                  pl.BlockSpec((1, POOL_HALO, D), lambda b, i: (b, jnp.maximum(i * halo_per_tile - 1, 0), 0)),
                  pl.BlockSpec((1, tm, ATTN_WIDTH), lambda b, i: (b, i, 0)),
                  _resident(win.shape), _resident(wgrp.shape), _resident(scale.shape),
                  _resident(wa.shape), _resident(wp.shape), _resident(wout.shape)],
        out_specs=pl.BlockSpec((1, tm, D), lambda b, i: (b, i, 0)),
        out_shape=jax.ShapeDtypeStruct((B, S, D), F32),
        scratch_shapes=[pltpu.VMEM((MIX_SUBTILES, POOL_HALO + MIX_TOKENS, POOL_WIDTH), F32)],
        compiler_params=pltpu.CompilerParams(
            dimension_semantics=("arbitrary", "arbitrary"), vmem_limit_bytes=V7X_VMEM_LIMIT_BYTES),
        name="mix_merge",
    )(h1, un, un, a, win, wgrp, scale, wa, wp, wout)


def _ffn2_final_kernel(h_ref, g2_ref, wg_ref, wu_ref, wd_ref, gf_ref, out_ref):
    for c in range(FFN2_SUBTILES):
        rows = slice(c * FFN_TOKENS, (c + 1) * FFN_TOKENS)
        h3 = _swiglu_half_step(h_ref[0, rows], g2_ref, wg_ref, wu_ref, wd_ref)
        out_ref[0, rows] = _rms(h3, gf_ref[...])


def _ffn2_final(h, g2, wg, wu, wd, gf):
    B, S, D = h.shape
    tm = FFN2_SUBTILES * FFN_TOKENS
    return pl.pallas_call(
        _ffn2_final_kernel,
        grid=(B, S // tm),
        in_specs=[pl.BlockSpec((1, tm, D), lambda b, i: (b, i, 0)),
                  _resident(g2.shape), _resident(wg.shape), _resident(wu.shape), _resident(wd.shape),
                  _resident(gf.shape)],
        out_specs=pl.BlockSpec((1, tm, D), lambda b, i: (b, i, 0)),
        out_shape=jax.ShapeDtypeStruct((B, S, D), F32),
        compiler_params=pltpu.CompilerParams(
            dimension_semantics=("arbitrary", "arbitrary"), vmem_limit_bytes=V7X_VMEM_LIMIT_BYTES),
        name="ffn2_final",
    )(h, g2, wg, wu, wd, gf)


def kernel(x, ffn1_norm, ffn1_w_gate, ffn1_w_up, ffn1_w_down, mix_norm, w_in, pool_w_group, pool_scale,
           w_branch_attn, w_branch_pool, w_out, ffn2_norm, ffn2_w_gate, ffn2_w_up, ffn2_w_down,
           rpb_table, final_norm):
    B, S, D = x.shape
    assert (D, ffn1_w_gate.shape[0]) == (D_MODEL, 1)
    assert S % MOBA_BLOCK == 0 and S % (FFN2_SUBTILES * FFN_TOKENS) == 0 and S % (MIX_SUBTILES * MIX_TOKENS) == 0
    assert B % ATTN_ROWS == 0
    bf = lambda w: w.astype(BF16)
    row = lambda v: v.reshape(1, -1)
    w_in = w_in[0]

    diag, sub, (wg1_bf, wu1_bf, wd1_bf) = _prep(rpb_table, [ffn1_w_gate[0], ffn1_w_up[0], ffn1_w_down[0]])
    (h1, un, qt, k, vt), later = _ffn1_qkv(
        x, row(ffn1_norm[0]), wg1_bf, wu1_bf, wd1_bf, row(mix_norm[0]), w_in,
        [w_in, w_branch_attn[0], w_branch_pool[0], w_out[0], ffn2_w_gate[0], ffn2_w_up[0], ffn2_w_down[0]])
    win_bf, wa_bf, wp_bf, wout_bf, wg2_bf, wu2_bf, wd2_bf = later
    a = _moba_attn(qt, k, vt, diag, sub, rpb_table)
    h2 = _mix_merge(h1, un, a, win_bf, pool_w_group[0], row(pool_scale[0]), wa_bf, wp_bf, wout_bf)
    return _ffn2_final(h2, row(ffn2_norm[0]), wg2_bf, wu2_bf, wd2_bf, row(final_norm))
```

```python
import math

import jax
import jax.numpy as jnp
from jax import lax
from jax.experimental import pallas as pl
from jax.experimental.pallas import tpu as pltpu

D_MODEL = 1024
HEAD_DIM = 64
ATTN_WIDTH = 512
ATTN_HEADS = 8
MOBA_BLOCK = 256
MOBA_TOPK = 3
POOL_WINDOWS = (2, 4, 8, 16)
POOL_WIDTH = 512
POOL_GROUP_WIDTH = 128
POOL_HALO = 16
RPB_BUCKETS = 32
RPB_MAX_DISTANCE = 128
RMS_EPS = 1e-6
MASK_VALUE = -1e30
LOG2E = math.log2(math.e)
Q_SCALE = HEAD_DIM ** -0.5 * LOG2E
ATTN_ROWS = 2
ATTN_QBLOCKS = 2
V7X_VMEM_LIMIT_BYTES = 56 * 1024 * 1024

FFN_TOKENS = 512
FFN2_SUBTILES = 2
MIX_TOKENS = 512
MIX_SUBTILES = 2
N_LATER_WEIGHTS = 7
F32_SUBLANES = 8
BF16_SUBLANES = 16
PAIR_WIDTH = 2 * HEAD_DIM
PREP_HEADS_PER_STEP = 2

BF16 = jnp.bfloat16
F32 = jnp.float32


def _resident(shape):
    nd = len(shape)
    return pl.BlockSpec(shape, lambda *_: (0,) * nd, pipeline_mode=pl.Buffered(1))


def _dot(a, b):
    return jnp.dot(a, b, preferred_element_type=F32)


def _rms(x, g):
    return x * lax.rsqrt(jnp.mean(x * x, axis=-1, keepdims=True) + RMS_EPS) * g


def _swiglu_half_step(x, g_ref, wg_ref, wu_ref, wd_ref):
    xn = _rms(x, g_ref[...]).astype(BF16)
    gate = _dot(xn, wg_ref[...])
    up = _dot(xn, wu_ref[...])
    act = (gate * jax.nn.sigmoid(gate) * up).astype(BF16)
    return x + 0.5 * _dot(act, wd_ref[...])


def _rpb_bucket(dist):
    n = jnp.maximum(dist, 0)
    max_exact = RPB_BUCKETS // 2
    nf = jnp.maximum(n, 1).astype(F32)
    large = max_exact + jnp.floor(jnp.log(nf / max_exact) / math.log(RPB_MAX_DISTANCE / max_exact)
                                  * (RPB_BUCKETS - max_exact)).astype(jnp.int32)
    large = jnp.minimum(large, RPB_BUCKETS - 1)
    return jnp.where(n < max_exact, n, large)


def _rpb_bias_tiles(table_ref, h, slot, diag_ref, sub_ref):
    width = 4 * MOBA_BLOCK
    sublanes = F32_SUBLANES
    dist = lax.broadcasted_iota(jnp.int32, (sublanes, width), 1) - MOBA_BLOCK
    bucket = _rpb_bucket(dist)
    f = jnp.zeros((sublanes, width), F32)
    for b in range(RPB_BUCKETS):
        f = jnp.where(bucket == b, table_ref[b, h], f)
    f = f * LOG2E
    sub = lax.broadcasted_iota(jnp.int32, (sublanes, width), 0)
    group = f
    for r in range(1, sublanes):
        group = jnp.where(sub == r, pltpu.roll(f, r, 1), group)
    ql = lax.broadcasted_iota(jnp.int32, (sublanes, MOBA_BLOCK), 1)
    kl0 = lax.broadcasted_iota(jnp.int32, (sublanes, MOBA_BLOCK), 0)
    for g in range(MOBA_BLOCK // sublanes):
        rows = slice(g * sublanes, (g + 1) * sublanes)
        blk = pltpu.roll(group, g * sublanes, 1) if g else group
        diag_ref[slot, rows] = jnp.where(ql >= kl0 + g * sublanes, blk[:, MOBA_BLOCK:2 * MOBA_BLOCK], MASK_VALUE)
        sub_ref[slot, rows] = blk[:, 2 * MOBA_BLOCK:3 * MOBA_BLOCK]


def _prep_kernel(table_ref, *refs):
    n = (len(refs) - 2) // 2
    weights_f32, (diag_ref, sub_ref), weights_bf16 = refs[:n], refs[n:n + 2], refs[n + 2:]
    for slot in range(PREP_HEADS_PER_STEP):
        _rpb_bias_tiles(table_ref, pl.program_id(0) * PREP_HEADS_PER_STEP + slot, slot, diag_ref, sub_ref)
    for src_ref, dst_ref in zip(weights_f32, weights_bf16):
        dst_ref[...] = src_ref[...].astype(BF16)


def _prep(rpb_table, weights):
    steps = ATTN_HEADS // PREP_HEADS_PER_STEP
    tile = jax.ShapeDtypeStruct((ATTN_HEADS, MOBA_BLOCK, MOBA_BLOCK), F32)
    tile_spec = pl.BlockSpec((PREP_HEADS_PER_STEP, MOBA_BLOCK, MOBA_BLOCK), lambda s: (s, 0, 0))
    slabs = [w.reshape(steps, w.shape[0] // steps, w.shape[1]) for w in weights]
    assert all(w.shape[1] % BF16_SUBLANES == 0 for w in slabs)
    slab_specs = [pl.BlockSpec((1,) + w.shape[1:], lambda s: (s, 0, 0)) for w in slabs]
    outs = pl.pallas_call(
        _prep_kernel,
        grid=(steps,),
        in_specs=[pl.BlockSpec(memory_space=pltpu.SMEM), *slab_specs],
        out_specs=[tile_spec, tile_spec, *slab_specs],
        out_shape=[tile, tile] + [jax.ShapeDtypeStruct(w.shape, BF16) for w in slabs],
        compiler_params=pltpu.CompilerParams(
            dimension_semantics=("arbitrary",), vmem_limit_bytes=V7X_VMEM_LIMIT_BYTES),
        name="prep",
    )(rpb_table, *slabs)
    return outs[0], outs[1], [o.reshape(w.shape) for o, w in zip(outs[2:], weights)]


def _ffn1_qkv_kernel(x_ref, g1_ref, wg_ref, wu_ref, wd_ref, gm_ref, wqkv_f32_ref, *refs):
    later_f32, (h1_ref, un_ref, qt_ref, k_ref, vt_ref), later_bf16, wqkv_ref = (
        refs[:N_LATER_WEIGHTS], refs[N_LATER_WEIGHTS:N_LATER_WEIGHTS + 5], refs[N_LATER_WEIGHTS + 5:-1], refs[-1])

    @pl.when((pl.program_id(0) == 0) & (pl.program_id(1) == 0))
    def _():
        wqkv_ref[...] = wqkv_f32_ref[...].astype(BF16)

    for src_ref, dst_ref in zip(later_f32, later_bf16):
        dst_ref[...] = src_ref[...].astype(BF16)
    h1 = _swiglu_half_step(x_ref[0], g1_ref, wg_ref, wu_ref, wd_ref)
    h1_ref[0] = h1
    un = _rms(h1, gm_ref[...]).astype(BF16)
    un_ref[0] = un
    qkv = _dot(un, wqkv_ref[...])
    qt_ref[0] = (qkv[:, :ATTN_WIDTH] * Q_SCALE).T.astype(BF16)
    k_ref[0] = qkv[:, ATTN_WIDTH:2 * ATTN_WIDTH].astype(BF16)
    for c in range(FFN_TOKENS // MOBA_BLOCK):
        vt_ref[0, c] = qkv[c * MOBA_BLOCK:(c + 1) * MOBA_BLOCK, 2 * ATTN_WIDTH:].T.astype(BF16)


def _ffn1_qkv(x, g1, wg, wu, wd, gm, w_in, later_weights):
    B, S, D = x.shape
    tm = FFN_TOKENS
    nblk = S // MOBA_BLOCK
    tiles_per_seq = S // tm
    steps = B * tiles_per_seq
    assert len(later_weights) == N_LATER_WEIGHTS
    step_of = lambda b, i: b * tiles_per_seq + i

    def slab_view(w):
        n = max(n for n in range(1, steps + 1) if steps % n == 0 and w.shape[0] % (n * BF16_SUBLANES) == 0)
        view = w.reshape(n, w.shape[0] // n, w.shape[1])
        return view, pl.BlockSpec((1,) + view.shape[1:], lambda b, i: (step_of(b, i) // (steps // n), 0, 0))

    slabs, slab_specs = zip(*[slab_view(w) for w in later_weights])
    tok = lambda width, dt: (pl.BlockSpec((1, tm, width), lambda b, i: (b, i, 0)),
                             jax.ShapeDtypeStruct((B, S, width), dt))
    h1_spec, h1_shape = tok(D, F32)
    un_spec, un_shape = tok(D, BF16)
    k_spec, k_shape = tok(ATTN_WIDTH, BF16)
    outs = pl.pallas_call(
        _ffn1_qkv_kernel,
        grid=(B, tiles_per_seq),
        in_specs=[pl.BlockSpec((1, tm, D), lambda b, i: (b, i, 0)),
                  _resident(g1.shape), _resident(wg.shape), _resident(wu.shape), _resident(wd.shape),
                  _resident(gm.shape), _resident((D, 3 * ATTN_WIDTH)), *slab_specs],
        out_specs=[h1_spec, un_spec,
                   pl.BlockSpec((1, ATTN_WIDTH, tm), lambda b, i: (b, 0, i)),
                   k_spec,
                   pl.BlockSpec((1, tm // MOBA_BLOCK, ATTN_WIDTH, MOBA_BLOCK), lambda b, i: (b, i, 0, 0)),
                   *slab_specs],
        out_shape=[h1_shape, un_shape,
                   jax.ShapeDtypeStruct((B, ATTN_WIDTH, S), BF16),
                   k_shape,
                   jax.ShapeDtypeStruct((B, nblk, ATTN_WIDTH, MOBA_BLOCK), BF16)]
                  + [jax.ShapeDtypeStruct(w.shape, BF16) for w in slabs],
        scratch_shapes=[pltpu.VMEM((D, 3 * ATTN_WIDTH), BF16)],
        compiler_params=pltpu.CompilerParams(
            dimension_semantics=("arbitrary", "arbitrary"), vmem_limit_bytes=V7X_VMEM_LIMIT_BYTES),
        name="ffn1_qkv",
    )(x, g1, wg, wu, wd, gm, w_in, *slabs)
    return outs[:5], [o.reshape(w.shape) for o, w in zip(outs[5:], later_weights)]


def _moba_attn_kernel(*refs):
    for c in range(ATTN_QBLOCKS):
        _attn_query_block(c, *refs)


def _attn_query_block(c, table_ref, qt_ref, k_ref, vt_ref, diag_ref, sub_ref, a_ref,
                      kmhi_ref, kmlo_ref, rb_far_ref, rb_sub_ref, at_ref, qz_ref,
                      m_ref, l_ref, mblk_ref, alpha_ref, s_ref, p_ref):
    qi = pl.program_id(1) * ATTN_QBLOCKS + c
    qcols = slice(c * MOBA_BLOCK, (c + 1) * MOBA_BLOCK)
    nblk = k_ref.shape[2] // MOBA_BLOCK
    nq = MOBA_BLOCK
    units = [(r, h) for r in range(ATTN_ROWS) for h in range(ATTN_HEADS)]

    def when_qi(value):
        return pl.when(qi == value) if value % ATTN_QBLOCKS == c else (lambda body: None)

    @when_qi(0)
    def _():
        for r in range(ATTN_ROWS):
            blk_row = lax.broadcasted_iota(jnp.int32, (nblk, ATTN_WIDTH), 0)
            km = jnp.zeros((nblk, ATTN_WIDTH), F32)
            for j in range(nblk):
                mean_j = jnp.mean(k_ref[r, 0, j * MOBA_BLOCK:(j + 1) * MOBA_BLOCK, :].astype(F32), axis=0,
                                  keepdims=True)
                km = jnp.where(blk_row == j, mean_j, km)
            km = jnp.concatenate([km] * ATTN_HEADS, axis=0)
            row_head = lax.broadcasted_iota(jnp.int32, km.shape, 0) // nblk
            col_head = lax.broadcasted_iota(jnp.int32, km.shape, 1) // HEAD_DIM
            km = jnp.where(row_head == col_head, km, 0.0)
            hi = km.astype(BF16)
            kmhi_ref[r] = hi
            kmlo_ref[r] = (km - hi.astype(F32)).astype(BF16)

    def select_blocks():
        jrow = lax.broadcasted_iota(jnp.int32, (nblk, nq), 0)
        for r in range(ATTN_ROWS):
            qt = qt_ref[r, 0, :, qcols]
            gates = _dot(kmhi_ref[r], qt) + _dot(kmlo_ref[r], qt)
            for h in range(ATTN_HEADS):
                u = r * ATTN_HEADS + h
                g = gates[h * nblk:(h + 1) * nblk]
                cnt = jnp.zeros((nblk, nq), jnp.int32)
                for jp in range(nblk):
                    gb = jnp.broadcast_to(g[jp:jp + 1], (nblk, nq))
                    beats = (gb > g) | ((gb == g) & (jp < jrow))
                    cnt = cnt + jnp.where(beats, jnp.where(jp < qi, 1, 0), 0)
                sel = (jrow < qi) & (cnt < MOBA_TOPK)
                far = jnp.where(sel, table_ref[RPB_BUCKETS - 1, h] * LOG2E, MASK_VALUE)
                sub = jnp.where(sel, 0.0, MASK_VALUE)
                for j in range(nblk):
                    rb_far_ref[u * nblk + j] = far[j:j + 1]
                    rb_sub_ref[u * nblk + j] = sub[j:j + 1]

    def begin():
        m_ref[...] = jnp.full(m_ref.shape, MASK_VALUE, F32)
        l_ref[...] = jnp.zeros(l_ref.shape, F32)
        at_ref[...] = jnp.zeros(at_ref.shape, F32)
        zeros_half = jnp.zeros((HEAD_DIM, nq), BF16)
        for r, h in units:
            qh = qt_ref[r, 0, h * HEAD_DIM:(h + 1) * HEAD_DIM, qcols]
            qz_ref[r * ATTN_HEADS + h] = jnp.concatenate([qh, zeros_half] if h % 2 == 0 else [zeros_half, qh], axis=0)

    def scores_stage(j, kind):
        start = pl.multiple_of(j * MOBA_BLOCK, MOBA_BLOCK)
        for u, (r, h) in enumerate(units):
            pair = h // 2
            kblk = k_ref[r, 0, pl.ds(start, MOBA_BLOCK), pair * PAIR_WIDTH:(pair + 1) * PAIR_WIDTH]
            s = _dot(kblk, qz_ref[u])
            if kind == "diag":
                s = s + diag_ref[h]
            elif kind == "sub":
                s = s + sub_ref[h]
            s_ref[u] = s
            mblk_ref[u] = jnp.max(s, axis=0, keepdims=True)

    def softmax_stage(row_bias):
        for u in range(len(units)):
            rb = row_bias(u)
            m_old = m_ref[u]
            m_blk = mblk_ref[u] if rb is None else mblk_ref[u] + rb
            m_new = jnp.maximum(m_old, m_blk)
            offset = m_new if rb is None else m_new - rb
            p_ref[u] = jnp.exp2(s_ref[u] - offset).astype(BF16)
            alpha_ref[u] = jnp.exp2(m_old - m_new)
            m_ref[u] = m_new

    ones_rows = jnp.ones((BF16_SUBLANES, nq), BF16)

    def values_stage(j):
        for u, (r, h) in enumerate(units):
            rows = slice(u * HEAD_DIM, (u + 1) * HEAD_DIM)
            v_h = vt_ref[r, 0, j, h * HEAD_DIM:(h + 1) * HEAD_DIM, :]
            pv = _dot(jnp.concatenate([v_h, ones_rows], axis=0), p_ref[u])
            alpha = alpha_ref[u]
            at_ref[rows] = alpha * at_ref[rows] + pv[:HEAD_DIM]
            l_ref[u] = alpha * l_ref[u] + pv[HEAD_DIM:HEAD_DIM + 1]

    def finish():
        for u in range(len(units)):
            rows = slice(u * HEAD_DIM, (u + 1) * HEAD_DIM)
            at_ref[rows] = at_ref[rows] * (1.0 / l_ref[u])
        for r in range(ATTN_ROWS):
            a_ref[r, 0, qcols] = at_ref[r * ATTN_WIDTH:(r + 1) * ATTN_WIDTH, :].T.astype(BF16)

    @when_qi(0)
    def _():
        begin()
        scores_stage(qi, "diag")
        softmax_stage(lambda u: None)
        values_stage(qi)
        finish()

    @when_qi(1)
    def _():
        begin()
        scores_stage(qi, "diag")
        softmax_stage(lambda u: None)
        scores_stage(0, "sub")
        values_stage(qi)
        softmax_stage(lambda u: None)
        values_stage(0)
        finish()

    @pl.when(qi > 1)
    def _():
        begin()
        select_blocks()
        j_sub = qi - 1
        n_far = j_sub

        scores_stage(qi, "diag")
        softmax_stage(lambda u: None)
        scores_stage(j_sub, "sub")
        values_stage(qi)
        softmax_stage(lambda u: rb_sub_ref[u * nblk + j_sub])
        scores_stage(0, "far")

        def far_body(t, carry):
            values_stage(jnp.where(t == 0, j_sub, t - 1))
            softmax_stage(lambda u: rb_far_ref[u * nblk + t])
            scores_stage(t + 1, "far")
            return carry

        lax.fori_loop(0, n_far - 1, far_body, 0)
        values_stage(jnp.where(n_far == 1, j_sub, n_far - 2))
        softmax_stage(lambda u: rb_far_ref[u * nblk + n_far - 1])
        values_stage(n_far - 1)
        finish()


def _moba_attn(qt, k, vt, diag, sub, rpb_table):
    B, S, _ = k.shape
    nblk = S // MOBA_BLOCK
    groups = B // ATTN_ROWS
    n_units = ATTN_ROWS * ATTN_HEADS
    step_queries = ATTN_QBLOCKS * MOBA_BLOCK
    split = lambda x: x.reshape((ATTN_ROWS, groups) + x.shape[1:])
    tile = pl.BlockSpec((ATTN_HEADS, MOBA_BLOCK, MOBA_BLOCK), lambda g, i: (0, 0, 0), pipeline_mode=pl.Buffered(1))
    a = pl.pallas_call(
        _moba_attn_kernel,
        grid=(groups, nblk // ATTN_QBLOCKS),
        in_specs=[pl.BlockSpec(memory_space=pltpu.SMEM),
                  pl.BlockSpec((ATTN_ROWS, 1, ATTN_WIDTH, step_queries), lambda g, i: (0, g, 0, i)),
                  pl.BlockSpec((ATTN_ROWS, 1, S, ATTN_WIDTH), lambda g, i: (0, g, 0, 0)),
                  pl.BlockSpec((ATTN_ROWS, 1, nblk, ATTN_WIDTH, MOBA_BLOCK), lambda g, i: (0, g, 0, 0, 0)),
                  tile, tile],
        out_specs=pl.BlockSpec((ATTN_ROWS, 1, step_queries, ATTN_WIDTH), lambda g, i: (0, g, i, 0)),
        out_shape=jax.ShapeDtypeStruct((ATTN_ROWS, groups, S, ATTN_WIDTH), BF16),
        scratch_shapes=[pltpu.VMEM((ATTN_ROWS, ATTN_HEADS * nblk, ATTN_WIDTH), BF16),
                        pltpu.VMEM((ATTN_ROWS, ATTN_HEADS * nblk, ATTN_WIDTH), BF16),
                        pltpu.VMEM((n_units * nblk, 1, MOBA_BLOCK), F32),
                        pltpu.VMEM((n_units * nblk, 1, MOBA_BLOCK), F32),
                        pltpu.VMEM((n_units * HEAD_DIM, MOBA_BLOCK), F32),
                        pltpu.VMEM((n_units, 2 * HEAD_DIM, MOBA_BLOCK), BF16),
                        *[pltpu.VMEM((n_units, 1, MOBA_BLOCK), F32)] * 4,
                        pltpu.VMEM((n_units, MOBA_BLOCK, MOBA_BLOCK), F32),
                        pltpu.VMEM((n_units, MOBA_BLOCK, MOBA_BLOCK), BF16)],
        compiler_params=pltpu.CompilerParams(
            dimension_semantics=("arbitrary", "arbitrary"), vmem_limit_bytes=V7X_VMEM_LIMIT_BYTES),
        name="moba_attn",
    )(rpb_table, split(qt), split(k), split(vt), diag, sub)
    return a.reshape(B, S, ATTN_WIDTH)


def _mix_merge_kernel(h1_ref, un_ref, unprev_ref, a_ref, win_ref, wgrp_ref, scale_ref,
                      wa_ref, wp_ref, wout_ref, h2_ref, zext_ref):
    i = pl.program_id(1)
    tm = MIX_TOKENS
    z0 = 3 * ATTN_WIDTH
    for c in range(MIX_SUBTILES):
        rows = slice(c * tm, (c + 1) * tm)
        zx_ref = zext_ref.at[c]
        zg = _dot(un_ref[0, rows], win_ref[:, z0:])
        halo = unprev_ref[0] if c == 0 else un_ref[0, c * tm - POOL_HALO:c * tm]
        zprev = _dot(halo, win_ref[:, z0:z0 + POOL_WIDTH])
        zx_ref[:POOL_HALO] = jnp.where(i > 0, zprev, 0.0) if c == 0 else zprev
        zx_ref[POOL_HALO:] = zg[:, :POOL_WIDTH]

        tpos = (i * MIX_SUBTILES + c) * tm + lax.broadcasted_iota(jnp.int32, (tm, POOL_GROUP_WIDTH), 0)
        mixed = []
        for g, w in enumerate(POOL_WINDOWS):
            cols = slice(g * POOL_GROUP_WIDTH, (g + 1) * POOL_GROUP_WIDTH)
            wsum = zx_ref[POOL_HALO:, cols]
            for lag in range(1, w):
                wsum = wsum + zx_ref[POOL_HALO - lag:POOL_HALO - lag + tm, cols]
            mean = wsum / jnp.minimum(tpos + 1, w).astype(F32)
            pooled = (mean - zx_ref[POOL_HALO:, cols]).astype(BF16)
            mixed.append(_dot(pooled, wgrp_ref[g].astype(BF16)))
        p = (jnp.concatenate(mixed, axis=1) * scale_ref[...]).astype(BF16)

        g_attn = zg[:, POOL_WIDTH:POOL_WIDTH + D_MODEL]
        g_pool = zg[:, POOL_WIDTH + D_MODEL:]
        merged = (jax.nn.sigmoid(g_attn) * _dot(a_ref[0, rows], wa_ref[...])
                  + jax.nn.sigmoid(g_pool) * _dot(p, wp_ref[...]))
        h2_ref[0, rows] = h1_ref[0, rows] + _dot(merged.astype(BF16), wout_ref[...])


def _mix_merge(h1, un, a, win, wgrp, scale, wa, wp, wout):
    B, S, D = h1.shape
    tm = MIX_SUBTILES * MIX_TOKENS
    halo_per_tile = tm // POOL_HALO
    return pl.pallas_call(
        _mix_merge_kernel,
        grid=(B, S // tm),
        in_specs=[pl.BlockSpec((1, tm, D), lambda b, i: (b, i, 0)),
                  pl.BlockSpec((1, tm, D), lambda b, i: (b, i, 0)),
                  pl.BlockSpec((1, POOL_HALO, D), lambda b, i: (b, jnp.maximum(i * halo_per_tile - 1, 0), 0)),
                  pl.BlockSpec((1, tm, ATTN_WIDTH), lambda b, i: (b, i, 0)),
                  _resident(win.shape), _resident(wgrp.shape), _resident(scale.shape),
                  _resident(wa.shape), _resident(wp.shape), _resident(wout.shape)],
        out_specs=pl.BlockSpec((1, tm, D), lambda b, i: (b, i, 0)),
        out_shape=jax.ShapeDtypeStruct((B, S, D), F32),
        scratch_shapes=[pltpu.VMEM((MIX_SUBTILES, POOL_HALO + MIX_TOKENS, POOL_WIDTH), F32)],
        compiler_params=pltpu.CompilerParams(
            dimension_semantics=("arbitrary", "arbitrary"), vmem_limit_bytes=V7X_VMEM_LIMIT_BYTES),
        name="mix_merge",
    )(h1, un, un, a, win, wgrp, scale, wa, wp, wout)


def _ffn2_final_kernel(h_ref, g2_ref, wg_ref, wu_ref, wd_ref, gf_ref, out_ref):
    for c in range(FFN2_SUBTILES):
        rows = slice(c * FFN_TOKENS, (c + 1) * FFN_TOKENS)
        h3 = _swiglu_half_step(h_ref[0, rows], g2_ref, wg_ref, wu_ref, wd_ref)
        out_ref[0, rows] = _rms(h3, gf_ref[...])


def _ffn2_final(h, g2, wg, wu, wd, gf):
    B, S, D = h.shape
    tm = FFN2_SUBTILES * FFN_TOKENS
    return pl.pallas_call(
        _ffn2_final_kernel,
        grid=(B, S // tm),
        in_specs=[pl.BlockSpec((1, tm, D), lambda b, i: (b, i, 0)),
                  _resident(g2.shape), _resident(wg.shape), _resident(wu.shape), _resident(wd.shape),
                  _resident(gf.shape)],
        out_specs=pl.BlockSpec((1, tm, D), lambda b, i: (b, i, 0)),
        out_shape=jax.ShapeDtypeStruct((B, S, D), F32),
        compiler_params=pltpu.CompilerParams(
            dimension_semantics=("arbitrary", "arbitrary"), vmem_limit_bytes=V7X_VMEM_LIMIT_BYTES),
        name="ffn2_final",
    )(h, g2, wg, wu, wd, gf)


def kernel(x, ffn1_norm, ffn1_w_gate, ffn1_w_up, ffn1_w_down, mix_norm, w_in, pool_w_group, pool_scale,
           w_branch_attn, w_branch_pool, w_out, ffn2_norm, ffn2_w_gate, ffn2_w_up, ffn2_w_down,
           rpb_table, final_norm):
    B, S, D = x.shape
    assert (D, ffn1_w_gate.shape[0]) == (D_MODEL, 1)
    assert S % MOBA_BLOCK == 0 and S % (FFN2_SUBTILES * FFN_TOKENS) == 0 and S % (MIX_SUBTILES * MIX_TOKENS) == 0
    assert B % ATTN_ROWS == 0 and S % (ATTN_QBLOCKS * MOBA_BLOCK) == 0
    bf = lambda w: w.astype(BF16)
    row = lambda v: v.reshape(1, -1)
    w_in = w_in[0]

    diag, sub, (wg1_bf, wu1_bf, wd1_bf) = _prep(rpb_table, [ffn1_w_gate[0], ffn1_w_up[0], ffn1_w_down[0]])
    (h1, un, qt, k, vt), later = _ffn1_qkv(
        x, row(ffn1_norm[0]), wg1_bf, wu1_bf, wd1_bf, row(mix_norm[0]), w_in,
        [w_in, w_branch_attn[0], w_branch_pool[0], w_out[0], ffn2_w_gate[0], ffn2_w_up[0], ffn2_w_down[0]])
    win_bf, wa_bf, wp_bf, wout_bf, wg2_bf, wu2_bf, wd2_bf = later
    a = _moba_attn(qt, k, vt, diag, sub, rpb_table)
    h2 = _mix_merge(h1, un, a, win_bf, pool_w_group[0], row(pool_scale[0]), wa_bf, wp_bf, wout_bf)
    return _ffn2_final(h2, row(ffn2_norm[0]), wg2_bf, wu2_bf, wd2_bf, row(final_norm))
```

```python
import math

import jax
import jax.numpy as jnp
from jax import lax
from jax.experimental import pallas as pl
from jax.experimental.pallas import tpu as pltpu

D_MODEL = 1024
HEAD_DIM = 64
ATTN_WIDTH = 512
ATTN_HEADS = 8
MOBA_BLOCK = 256
MOBA_TOPK = 3
POOL_WINDOWS = (2, 4, 8, 16)
POOL_WIDTH = 512
POOL_GROUP_WIDTH = 128
POOL_HALO = 16
RPB_BUCKETS = 32
RPB_MAX_DISTANCE = 128
RMS_EPS = 1e-6
MASK_VALUE = -1e30
LOG2E = math.log2(math.e)
Q_SCALE = HEAD_DIM ** -0.5 * LOG2E
ATTN_ROWS = 2
V7X_VMEM_LIMIT_BYTES = 56 * 1024 * 1024

FFN_TOKENS = 512
FFN2_SUBTILES = 2
MIX_TOKENS = 512
MIX_SUBTILES = 2
N_LATER_WEIGHTS = 7
F32_SUBLANES = 8
BF16_SUBLANES = 16
PAIR_WIDTH = 2 * HEAD_DIM
PREP_HEADS_PER_STEP = 2

BF16 = jnp.bfloat16
F32 = jnp.float32


def _resident(shape):
    nd = len(shape)
    return pl.BlockSpec(shape, lambda *_: (0,) * nd, pipeline_mode=pl.Buffered(1))


def _dot(a, b):
    return jnp.dot(a, b, preferred_element_type=F32)


def _rms(x, g):
    return x * lax.rsqrt(jnp.mean(x * x, axis=-1, keepdims=True) + RMS_EPS) * g


def _swiglu_half_step(x, g_ref, wg_ref, wu_ref, wd_ref):
    xn = _rms(x, g_ref[...]).astype(BF16)
    gate = _dot(xn, wg_ref[...])
    up = _dot(xn, wu_ref[...])
    act = (gate * jax.nn.sigmoid(gate) * up).astype(BF16)
    return x + 0.5 * _dot(act, wd_ref[...])


def _rpb_bucket(dist):
    n = jnp.maximum(dist, 0)
    max_exact = RPB_BUCKETS // 2
    nf = jnp.maximum(n, 1).astype(F32)
    large = max_exact + jnp.floor(jnp.log(nf / max_exact) / math.log(RPB_MAX_DISTANCE / max_exact)
                                  * (RPB_BUCKETS - max_exact)).astype(jnp.int32)
    large = jnp.minimum(large, RPB_BUCKETS - 1)
    return jnp.where(n < max_exact, n, large)


def _rpb_bias_tiles(table_ref, h, slot, diag_ref, sub_ref):
    width = 4 * MOBA_BLOCK
    sublanes = F32_SUBLANES
    dist = lax.broadcasted_iota(jnp.int32, (sublanes, width), 1) - MOBA_BLOCK
    bucket = _rpb_bucket(dist)
    f = jnp.zeros((sublanes, width), F32)
    for b in range(RPB_BUCKETS):
        f = jnp.where(bucket == b, table_ref[b, h], f)
    f = f * LOG2E
    sub = lax.broadcasted_iota(jnp.int32, (sublanes, width), 0)
    group = f
    for r in range(1, sublanes):
        group = jnp.where(sub == r, pltpu.roll(f, r, 1), group)
    ql = lax.broadcasted_iota(jnp.int32, (sublanes, MOBA_BLOCK), 1)
    kl0 = lax.broadcasted_iota(jnp.int32, (sublanes, MOBA_BLOCK), 0)
    for g in range(MOBA_BLOCK // sublanes):
        rows = slice(g * sublanes, (g + 1) * sublanes)
        blk = pltpu.roll(group, g * sublanes, 1) if g else group
        diag_ref[slot, rows] = jnp.where(ql >= kl0 + g * sublanes, blk[:, MOBA_BLOCK:2 * MOBA_BLOCK], MASK_VALUE)
        sub_ref[slot, rows] = blk[:, 2 * MOBA_BLOCK:3 * MOBA_BLOCK]


def _prep_kernel(table_ref, *refs):
    n = (len(refs) - 2) // 2
    weights_f32, (diag_ref, sub_ref), weights_bf16 = refs[:n], refs[n:n + 2], refs[n + 2:]
    for slot in range(PREP_HEADS_PER_STEP):
        _rpb_bias_tiles(table_ref, pl.program_id(0) * PREP_HEADS_PER_STEP + slot, slot, diag_ref, sub_ref)
    for src_ref, dst_ref in zip(weights_f32, weights_bf16):
        dst_ref[...] = src_ref[...].astype(BF16)


def _prep(rpb_table, weights):
    steps = ATTN_HEADS // PREP_HEADS_PER_STEP
    tile = jax.ShapeDtypeStruct((ATTN_HEADS, MOBA_BLOCK, MOBA_BLOCK), F32)
    tile_spec = pl.BlockSpec((PREP_HEADS_PER_STEP, MOBA_BLOCK, MOBA_BLOCK), lambda s: (s, 0, 0))
    slabs = [w.reshape(steps, w.shape[0] // steps, w.shape[1]) for w in weights]
    assert all(w.shape[1] % BF16_SUBLANES == 0 for w in slabs)
    slab_specs = [pl.BlockSpec((1,) + w.shape[1:], lambda s: (s, 0, 0)) for w in slabs]
    outs = pl.pallas_call(
        _prep_kernel,
        grid=(steps,),
        in_specs=[pl.BlockSpec(memory_space=pltpu.SMEM), *slab_specs],
        out_specs=[tile_spec, tile_spec, *slab_specs],
        out_shape=[tile, tile] + [jax.ShapeDtypeStruct(w.shape, BF16) for w in slabs],
        compiler_params=pltpu.CompilerParams(
            dimension_semantics=("arbitrary",), vmem_limit_bytes=V7X_VMEM_LIMIT_BYTES),
        name="prep",
    )(rpb_table, *slabs)
    return outs[0], outs[1], [o.reshape(w.shape) for o, w in zip(outs[2:], weights)]


def _ffn1_qkv_kernel(x_ref, g1_ref, wg_ref, wu_ref, wd_ref, gm_ref, wqkv_f32_ref, *refs):
    later_f32, (h1_ref, un_ref, qt_ref, k_ref, vt_ref), later_bf16, wqkv_ref = (
        refs[:N_LATER_WEIGHTS], refs[N_LATER_WEIGHTS:N_LATER_WEIGHTS + 5], refs[N_LATER_WEIGHTS + 5:-1], refs[-1])

    @pl.when((pl.program_id(0) == 0) & (pl.program_id(1) == 0))
    def _():
        wqkv_ref[...] = wqkv_f32_ref[...].astype(BF16)

    for src_ref, dst_ref in zip(later_f32, later_bf16):
        dst_ref[...] = src_ref[...].astype(BF16)
    h1 = _swiglu_half_step(x_ref[0], g1_ref, wg_ref, wu_ref, wd_ref)
    h1_ref[0] = h1
    un = _rms(h1, gm_ref[...]).astype(BF16)
    un_ref[0] = un
    qkv = _dot(un, wqkv_ref[...])
    qt_ref[0] = (qkv[:, :ATTN_WIDTH] * Q_SCALE).T.astype(BF16)
    k_ref[0] = qkv[:, ATTN_WIDTH:2 * ATTN_WIDTH].astype(BF16)
    for c in range(FFN_TOKENS // MOBA_BLOCK):
        vt_ref[0, c] = qkv[c * MOBA_BLOCK:(c + 1) * MOBA_BLOCK, 2 * ATTN_WIDTH:].T.astype(BF16)


def _ffn1_qkv(x, g1, wg, wu, wd, gm, w_in, later_weights):
    B, S, D = x.shape
    tm = FFN_TOKENS
    nblk = S // MOBA_BLOCK
    tiles_per_seq = S // tm
    steps = B * tiles_per_seq
    assert len(later_weights) == N_LATER_WEIGHTS
    step_of = lambda b, i: b * tiles_per_seq + i

    def slab_view(w):
        n = max(n for n in range(1, steps + 1) if steps % n == 0 and w.shape[0] % (n * BF16_SUBLANES) == 0)
        view = w.reshape(n, w.shape[0] // n, w.shape[1])
        return view, pl.BlockSpec((1,) + view.shape[1:], lambda b, i: (step_of(b, i) // (steps // n), 0, 0))

    slabs, slab_specs = zip(*[slab_view(w) for w in later_weights])
    tok = lambda width, dt: (pl.BlockSpec((1, tm, width), lambda b, i: (b, i, 0)),
                             jax.ShapeDtypeStruct((B, S, width), dt))
    h1_spec, h1_shape = tok(D, F32)
    un_spec, un_shape = tok(D, BF16)
    k_spec, k_shape = tok(ATTN_WIDTH, BF16)
    outs = pl.pallas_call(
        _ffn1_qkv_kernel,
        grid=(B, tiles_per_seq),
        in_specs=[pl.BlockSpec((1, tm, D), lambda b, i: (b, i, 0)),
                  _resident(g1.shape), _resident(wg.shape), _resident(wu.shape), _resident(wd.shape),
                  _resident(gm.shape), _resident((D, 3 * ATTN_WIDTH)), *slab_specs],
        out_specs=[h1_spec, un_spec,
                   pl.BlockSpec((1, ATTN_WIDTH, tm), lambda b, i: (b, 0, i)),
                   k_spec,
                   pl.BlockSpec((1, tm // MOBA_BLOCK, ATTN_WIDTH, MOBA_BLOCK), lambda b, i: (b, i, 0, 0)),
                   *slab_specs],
        out_shape=[h1_shape, un_shape,
                   jax.ShapeDtypeStruct((B, ATTN_WIDTH, S), BF16),
                   k_shape,
                   jax.ShapeDtypeStruct((B, nblk, ATTN_WIDTH, MOBA_BLOCK), BF16)]
                  + [jax.ShapeDtypeStruct(w.shape, BF16) for w in slabs],
        scratch_shapes=[pltpu.VMEM((D, 3 * ATTN_WIDTH), BF16)],
        compiler_params=pltpu.CompilerParams(
            dimension_semantics=("arbitrary", "arbitrary"), vmem_limit_bytes=V7X_VMEM_LIMIT_BYTES),
        name="ffn1_qkv",
    )(x, g1, wg, wu, wd, gm, w_in, *slabs)
    return outs[:5], [o.reshape(w.shape) for o, w in zip(outs[5:], later_weights)]


def _moba_attn_kernel(table_ref, qt_ref, k_ref, vt_ref, diag_ref, sub_ref, a_ref,
                      kmhi_ref, kmlo_ref, rb_far_ref, rb_sub_ref, at_ref, qz_ref,
                      m_ref, l_ref, mblk_ref, alpha_ref, s_ref, p_ref):
    qi = pl.program_id(1)
    nblk = k_ref.shape[2] // MOBA_BLOCK
    nq = MOBA_BLOCK
    units = [(r, h) for r in range(ATTN_ROWS) for h in range(ATTN_HEADS)]

    @pl.when(qi == 0)
    def _():
        for r in range(ATTN_ROWS):
            blk_row = lax.broadcasted_iota(jnp.int32, (nblk, ATTN_WIDTH), 0)
            km = jnp.zeros((nblk, ATTN_WIDTH), F32)
            for j in range(nblk):
                mean_j = jnp.mean(k_ref[r, 0, j * MOBA_BLOCK:(j + 1) * MOBA_BLOCK, :].astype(F32), axis=0,
                                  keepdims=True)
                km = jnp.where(blk_row == j, mean_j, km)
            km = jnp.concatenate([km] * ATTN_HEADS, axis=0)
            row_head = lax.broadcasted_iota(jnp.int32, km.shape, 0) // nblk
            col_head = lax.broadcasted_iota(jnp.int32, km.shape, 1) // HEAD_DIM
            km = jnp.where(row_head == col_head, km, 0.0)
            hi = km.astype(BF16)
            kmhi_ref[r] = hi
            kmlo_ref[r] = (km - hi.astype(F32)).astype(BF16)

    def select_blocks():
        jrow = lax.broadcasted_iota(jnp.int32, (nblk, nq), 0)
        for r in range(ATTN_ROWS):
            qt = qt_ref[r, 0]
            gates = _dot(kmhi_ref[r], qt) + _dot(kmlo_ref[r], qt)
            for h in range(ATTN_HEADS):
                u = r * ATTN_HEADS + h
                g = gates[h * nblk:(h + 1) * nblk]
                cnt = jnp.zeros((nblk, nq), jnp.int32)
                for jp in range(nblk):
                    gb = jnp.broadcast_to(g[jp:jp + 1], (nblk, nq))
                    beats = (gb > g) | ((gb == g) & (jp < jrow))
                    cnt = cnt + jnp.where(beats, jnp.where(jp < qi, 1, 0), 0)
                sel = (jrow < qi) & (cnt < MOBA_TOPK)
                far = jnp.where(sel, table_ref[RPB_BUCKETS - 1, h] * LOG2E, MASK_VALUE)
                sub = jnp.where(sel, 0.0, MASK_VALUE)
                for j in range(nblk):
                    rb_far_ref[u * nblk + j] = far[j:j + 1]
                    rb_sub_ref[u * nblk + j] = sub[j:j + 1]

    def begin():
        m_ref[...] = jnp.full(m_ref.shape, MASK_VALUE, F32)
        l_ref[...] = jnp.zeros(l_ref.shape, F32)
        at_ref[...] = jnp.zeros(at_ref.shape, F32)
        zeros_half = jnp.zeros((HEAD_DIM, nq), BF16)
        for r, h in units:
            qh = qt_ref[r, 0, h * HEAD_DIM:(h + 1) * HEAD_DIM, :]
            qz_ref[r * ATTN_HEADS + h] = jnp.concatenate([qh, zeros_half] if h % 2 == 0 else [zeros_half, qh], axis=0)

    def scores_stage(j, kind):
        start = pl.multiple_of(j * MOBA_BLOCK, MOBA_BLOCK)
        for u, (r, h) in enumerate(units):
            pair = h // 2
            kblk = k_ref[r, 0, pl.ds(start, MOBA_BLOCK), pair * PAIR_WIDTH:(pair + 1) * PAIR_WIDTH]
            s = _dot(kblk, qz_ref[u])
            if kind == "diag":
                s = s + diag_ref[h]
            elif kind == "sub":
                s = s + sub_ref[h]
            s_ref[u] = s
            mblk_ref[u] = jnp.max(s, axis=0, keepdims=True)

    def softmax_stage(row_bias):
        for u in range(len(units)):
            rb = row_bias(u)
            m_old = m_ref[u]
            m_blk = mblk_ref[u] if rb is None else mblk_ref[u] + rb
            m_new = jnp.maximum(m_old, m_blk)
            offset = m_new if rb is None else m_new - rb
            p_ref[u] = jnp.exp2(s_ref[u] - offset).astype(BF16)
            alpha_ref[u] = jnp.exp2(m_old - m_new)
            m_ref[u] = m_new

    ones_rows = jnp.ones((BF16_SUBLANES, nq), BF16)

    def values_stage(j):
        for u, (r, h) in enumerate(units):
            rows = slice(u * HEAD_DIM, (u + 1) * HEAD_DIM)
            v_h = vt_ref[r, 0, j, h * HEAD_DIM:(h + 1) * HEAD_DIM, :]
            pv = _dot(jnp.concatenate([v_h, ones_rows], axis=0), p_ref[u])
            alpha = alpha_ref[u]
            at_ref[rows] = alpha * at_ref[rows] + pv[:HEAD_DIM]
            l_ref[u] = alpha * l_ref[u] + pv[HEAD_DIM:HEAD_DIM + 1]

    def finish():
        for u in range(len(units)):
            rows = slice(u * HEAD_DIM, (u + 1) * HEAD_DIM)
            at_ref[rows] = at_ref[rows] * (1.0 / l_ref[u])
        for r in range(ATTN_ROWS):
            a_ref[r, 0] = at_ref[r * ATTN_WIDTH:(r + 1) * ATTN_WIDTH, :].T.astype(BF16)

    @pl.when(qi == 0)
    def _():
        begin()
        scores_stage(qi, "diag")
        softmax_stage(lambda u: None)
        values_stage(qi)
        finish()

    @pl.when(qi == 1)
    def _():
        begin()
        scores_stage(qi, "diag")
        softmax_stage(lambda u: None)
        scores_stage(0, "sub")
        values_stage(qi)
        softmax_stage(lambda u: None)
        values_stage(0)
        finish()

    @pl.when(qi > 1)
    def _():
        begin()
        select_blocks()
        j_sub = qi - 1
        n_far = j_sub

        scores_stage(qi, "diag")
        softmax_stage(lambda u: None)
        scores_stage(j_sub, "sub")
        values_stage(qi)
        softmax_stage(lambda u: rb_sub_ref[u * nblk + j_sub])
        scores_stage(0, "far")

        def far_body(t, carry):
            values_stage(jnp.where(t == 0, j_sub, t - 1))
            softmax_stage(lambda u: rb_far_ref[u * nblk + t])
            scores_stage(t + 1, "far")
            return carry

        lax.fori_loop(0, n_far - 1, far_body, 0)
        values_stage(jnp.where(n_far == 1, j_sub, n_far - 2))
        softmax_stage(lambda u: rb_far_ref[u * nblk + n_far - 1])
        values_stage(n_far - 1)
        finish()


def _moba_attn(qt, k, vt, diag, sub, rpb_table):
    B, S, _ = k.shape
    nblk = S // MOBA_BLOCK
    groups = B // ATTN_ROWS
    n_units = ATTN_ROWS * ATTN_HEADS
    split = lambda x: x.reshape((ATTN_ROWS, groups) + x.shape[1:])
    tile = pl.BlockSpec((ATTN_HEADS, MOBA_BLOCK, MOBA_BLOCK), lambda g, i: (0, 0, 0), pipeline_mode=pl.Buffered(1))
    a = pl.pallas_call(
        _moba_attn_kernel,
        grid=(groups, nblk),
        in_specs=[pl.BlockSpec(memory_space=pltpu.SMEM),
                  pl.BlockSpec((ATTN_ROWS, 1, ATTN_WIDTH, MOBA_BLOCK), lambda g, i: (0, g, 0, i)),
                  pl.BlockSpec((ATTN_ROWS, 1, S, ATTN_WIDTH), lambda g, i: (0, g, 0, 0)),
                  pl.BlockSpec((ATTN_ROWS, 1, nblk, ATTN_WIDTH, MOBA_BLOCK), lambda g, i: (0, g, 0, 0, 0)),
                  tile, tile],
        out_specs=pl.BlockSpec((ATTN_ROWS, 1, MOBA_BLOCK, ATTN_WIDTH), lambda g, i: (0, g, i, 0)),
        out_shape=jax.ShapeDtypeStruct((ATTN_ROWS, groups, S, ATTN_WIDTH), BF16),
        scratch_shapes=[pltpu.VMEM((ATTN_ROWS, ATTN_HEADS * nblk, ATTN_WIDTH), BF16),
                        pltpu.VMEM((ATTN_ROWS, ATTN_HEADS * nblk, ATTN_WIDTH), BF16),
                        pltpu.VMEM((n_units * nblk, 1, MOBA_BLOCK), F32),
                        pltpu.VMEM((n_units * nblk, 1, MOBA_BLOCK), F32),
                        pltpu.VMEM((n_units * HEAD_DIM, MOBA_BLOCK), F32),
                        pltpu.VMEM((n_units, 2 * HEAD_DIM, MOBA_BLOCK), BF16),
                        *[pltpu.VMEM((n_units, 1, MOBA_BLOCK), F32)] * 4,
                        pltpu.VMEM((n_units, MOBA_BLOCK, MOBA_BLOCK), F32),
                        pltpu.VMEM((n_units, MOBA_BLOCK, MOBA_BLOCK), BF16)],
        compiler_params=pltpu.CompilerParams(
            dimension_semantics=("arbitrary", "arbitrary"), vmem_limit_bytes=V7X_VMEM_LIMIT_BYTES),
        name="moba_attn",
    )(rpb_table, split(qt), split(k), split(vt), diag, sub)
    return a.reshape(B, S, ATTN_WIDTH)


def _mix_merge_kernel(h1_ref, un_ref, a_ref, win_ref, wgrp_ref, scale_ref,
                      wa_ref, wp_ref, wout_ref, h2_ref, zext_ref):
    i = pl.program_id(1)
    tm = MIX_TOKENS
    z0 = 3 * ATTN_WIDTH
    @pl.when(i == 0)
    def _():
        zext_ref[0, :POOL_HALO] = jnp.zeros((POOL_HALO, POOL_WIDTH), F32)

    @pl.when(i > 0)
    def _():
        zext_ref[0, :POOL_HALO] = zext_ref[MIX_SUBTILES - 1, tm:]

    for c in range(MIX_SUBTILES):
        rows = slice(c * tm, (c + 1) * tm)
        zx_ref = zext_ref.at[c]
        zg = _dot(un_ref[0, rows], win_ref[:, z0:])
        if c > 0:
            zx_ref[:POOL_HALO] = zext_ref[c - 1, tm:]
        zx_ref[POOL_HALO:] = zg[:, :POOL_WIDTH]

        tpos = (i * MIX_SUBTILES + c) * tm + lax.broadcasted_iota(jnp.int32, (tm, POOL_GROUP_WIDTH), 0)
        mixed = []
        for g, w in enumerate(POOL_WINDOWS):
            cols = slice(g * POOL_GROUP_WIDTH, (g + 1) * POOL_GROUP_WIDTH)
            wsum = zx_ref[POOL_HALO:, cols]
            for lag in range(1, w):
                wsum = wsum + zx_ref[POOL_HALO - lag:POOL_HALO - lag + tm, cols]
            mean = wsum / jnp.minimum(tpos + 1, w).astype(F32)
            pooled = (mean - zx_ref[POOL_HALO:, cols]).astype(BF16)
            mixed.append(_dot(pooled, wgrp_ref[g].astype(BF16)))
        p = (jnp.concatenate(mixed, axis=1) * scale_ref[...]).astype(BF16)

        g_attn = zg[:, POOL_WIDTH:POOL_WIDTH + D_MODEL]
        g_pool = zg[:, POOL_WIDTH + D_MODEL:]
        merged = (jax.nn.sigmoid(g_attn) * _dot(a_ref[0, rows], wa_ref[...])
                  + jax.nn.sigmoid(g_pool) * _dot(p, wp_ref[...]))
        h2_ref[0, rows] = h1_ref[0, rows] + _dot(merged.astype(BF16), wout_ref[...])


def _mix_merge(h1, un, a, win, wgrp, scale, wa, wp, wout):
    B, S, D = h1.shape
    tm = MIX_SUBTILES * MIX_TOKENS
    return pl.pallas_call(
        _mix_merge_kernel,
        grid=(B, S // tm),
        in_specs=[pl.BlockSpec((1, tm, D), lambda b, i: (b, i, 0)),
                  pl.BlockSpec((1, tm, D), lambda b, i: (b, i, 0)),
                  pl.BlockSpec((1, tm, ATTN_WIDTH), lambda b, i: (b, i, 0)),
                  _resident(win.shape), _resident(wgrp.shape), _resident(scale.shape),
                  _resident(wa.shape), _resident(wp.shape), _resident(wout.shape)],
        out_specs=pl.BlockSpec((1, tm, D), lambda b, i: (b, i, 0)),
        out_shape=jax.ShapeDtypeStruct((B, S, D), F32),
        scratch_shapes=[pltpu.VMEM((MIX_SUBTILES, POOL_HALO + MIX_TOKENS, POOL_WIDTH), F32)],
        compiler_params=pltpu.CompilerParams(
            dimension_semantics=("arbitrary", "arbitrary"), vmem_limit_bytes=V7X_VMEM_LIMIT_BYTES),
        name="mix_merge",
    )(h1, un, a, win, wgrp, scale, wa, wp, wout)


def _ffn2_final_kernel(h_ref, g2_ref, wg_ref, wu_ref, wd_ref, gf_ref, out_ref):
    for c in range(FFN2_SUBTILES):
        rows = slice(c * FFN_TOKENS, (c + 1) * FFN_TOKENS)
        h3 = _swiglu_half_step(h_ref[0, rows], g2_ref, wg_ref, wu_ref, wd_ref)
        out_ref[0, rows] = _rms(h3, gf_ref[...])


def _ffn2_final(h, g2, wg, wu, wd, gf):
    B, S, D = h.shape
    tm = FFN2_SUBTILES * FFN_TOKENS
    return pl.pallas_call(
        _ffn2_final_kernel,
        grid=(B, S // tm),
        in_specs=[pl.BlockSpec((1, tm, D), lambda b, i: (b, i, 0)),
                  _resident(g2.shape), _resident(wg.shape), _resident(wu.shape), _resident(wd.shape),
                  _resident(gf.shape)],
        out_specs=pl.BlockSpec((1, tm, D), lambda b, i: (b, i, 0)),
        out_shape=jax.ShapeDtypeStruct((B, S, D), F32),
        compiler_params=pltpu.CompilerParams(
            dimension_semantics=("arbitrary", "arbitrary"), vmem_limit_bytes=V7X_VMEM_LIMIT_BYTES),
        name="ffn2_final",
    )(h, g2, wg, wu, wd, gf)


def kernel(x, ffn1_norm, ffn1_w_gate, ffn1_w_up, ffn1_w_down, mix_norm, w_in, pool_w_group, pool_scale,
           w_branch_attn, w_branch_pool, w_out, ffn2_norm, ffn2_w_gate, ffn2_w_up, ffn2_w_down,
           rpb_table, final_norm):
    B, S, D = x.shape
    assert (D, ffn1_w_gate.shape[0]) == (D_MODEL, 1)
    assert S % MOBA_BLOCK == 0 and S % (FFN2_SUBTILES * FFN_TOKENS) == 0 and S % (MIX_SUBTILES * MIX_TOKENS) == 0
    assert B % ATTN_ROWS == 0
    bf = lambda w: w.astype(BF16)
    row = lambda v: v.reshape(1, -1)
    w_in = w_in[0]

    diag, sub, (wg1_bf, wu1_bf, wd1_bf) = _prep(rpb_table, [ffn1_w_gate[0], ffn1_w_up[0], ffn1_w_down[0]])
    (h1, un, qt, k, vt), later = _ffn1_qkv(
        x, row(ffn1_norm[0]), wg1_bf, wu1_bf, wd1_bf, row(mix_norm[0]), w_in,
        [w_in, w_branch_attn[0], w_branch_pool[0], w_out[0], ffn2_w_gate[0], ffn2_w_up[0], ffn2_w_down[0]])
    win_bf, wa_bf, wp_bf, wout_bf, wg2_bf, wu2_bf, wd2_bf = later
    a = _moba_attn(qt, k, vt, diag, sub, rpb_table)
    h2 = _mix_merge(h1, un, a, win_bf, pool_w_group[0], row(pool_scale[0]), wa_bf, wp_bf, wout_bf)
    return _ffn2_final(h2, row(ffn2_norm[0]), wg2_bf, wu2_bf, wd2_bf, row(final_norm))
```

```python
import math

import jax
import jax.numpy as jnp
from jax import lax
from jax.experimental import pallas as pl
from jax.experimental.pallas import tpu as pltpu

D_MODEL = 1024
HEAD_DIM = 64
ATTN_WIDTH = 512
ATTN_HEADS = 8
MOBA_BLOCK = 256
MOBA_TOPK = 3
POOL_WINDOWS = (2, 4, 8, 16)
POOL_WIDTH = 512
POOL_GROUP_WIDTH = 128
POOL_HALO = 16
RPB_BUCKETS = 32
RPB_MAX_DISTANCE = 128
RMS_EPS = 1e-6
MASK_VALUE = -1e30
LOG2E = math.log2(math.e)
Q_SCALE = HEAD_DIM ** -0.5 * LOG2E
ATTN_ROWS = 2
V7X_VMEM_LIMIT_BYTES = 56 * 1024 * 1024

FFN_TOKENS = 512
FFN_HIDDEN_CHUNK = 1024
FFN2_SUBTILES = 4
MIX_TOKENS = 512
MIX_SUBTILES = 2
N_LATER_WEIGHTS = 7
F32_SUBLANES = 8
BF16_SUBLANES = 16
PAIR_WIDTH = 2 * HEAD_DIM
PREP_HEADS_PER_STEP = 2

BF16 = jnp.bfloat16
F32 = jnp.float32


def _resident(shape):
    nd = len(shape)
    return pl.BlockSpec(shape, lambda *_: (0,) * nd, pipeline_mode=pl.Buffered(1))


def _dot(a, b):
    return jnp.dot(a, b, preferred_element_type=F32)


def _rms(x, g):
    return x * lax.rsqrt(jnp.mean(x * x, axis=-1, keepdims=True) + RMS_EPS) * g


def _swiglu_half_step(x, g_ref, wg_ref, wu_ref, wd_ref):
    xn = _rms(x, g_ref[...]).astype(BF16)
    hidden = wg_ref.shape[1]
    acc = None
    for lo in range(0, hidden, FFN_HIDDEN_CHUNK):
        hi = min(lo + FFN_HIDDEN_CHUNK, hidden)
        gate = _dot(xn, wg_ref[:, lo:hi])
        up = _dot(xn, wu_ref[:, lo:hi])
        act = (gate * jax.nn.sigmoid(gate) * up).astype(BF16)
        part = _dot(act, wd_ref[lo:hi, :])
        acc = part if acc is None else acc + part
    return x + 0.5 * acc


def _rpb_bucket(dist):
    n = jnp.maximum(dist, 0)
    max_exact = RPB_BUCKETS // 2
    nf = jnp.maximum(n, 1).astype(F32)
    large = max_exact + jnp.floor(jnp.log(nf / max_exact) / math.log(RPB_MAX_DISTANCE / max_exact)
                                  * (RPB_BUCKETS - max_exact)).astype(jnp.int32)
    large = jnp.minimum(large, RPB_BUCKETS - 1)
    return jnp.where(n < max_exact, n, large)


def _rpb_bias_tiles(table_ref, h, slot, diag_ref, sub_ref):
    width = 4 * MOBA_BLOCK
    sublanes = F32_SUBLANES
    dist = lax.broadcasted_iota(jnp.int32, (sublanes, width), 1) - MOBA_BLOCK
    bucket = _rpb_bucket(dist)
    f = jnp.zeros((sublanes, width), F32)
    for b in range(RPB_BUCKETS):
        f = jnp.where(bucket == b, table_ref[b, h], f)
    f = f * LOG2E
    sub = lax.broadcasted_iota(jnp.int32, (sublanes, width), 0)
    group = f
    for r in range(1, sublanes):
        group = jnp.where(sub == r, pltpu.roll(f, r, 1), group)
    ql = lax.broadcasted_iota(jnp.int32, (sublanes, MOBA_BLOCK), 1)
    kl0 = lax.broadcasted_iota(jnp.int32, (sublanes, MOBA_BLOCK), 0)
    for g in range(MOBA_BLOCK // sublanes):
        rows = slice(g * sublanes, (g + 1) * sublanes)
        blk = pltpu.roll(group, g * sublanes, 1) if g else group
        diag_ref[slot, rows] = jnp.where(ql >= kl0 + g * sublanes, blk[:, MOBA_BLOCK:2 * MOBA_BLOCK], MASK_VALUE)
        sub_ref[slot, rows] = blk[:, 2 * MOBA_BLOCK:3 * MOBA_BLOCK]


def _prep_kernel(table_ref, *refs):
    n = (len(refs) - 2) // 2
    weights_f32, (diag_ref, sub_ref), weights_bf16 = refs[:n], refs[n:n + 2], refs[n + 2:]
    for slot in range(PREP_HEADS_PER_STEP):
        _rpb_bias_tiles(table_ref, pl.program_id(0) * PREP_HEADS_PER_STEP + slot, slot, diag_ref, sub_ref)
    for src_ref, dst_ref in zip(weights_f32, weights_bf16):
        dst_ref[...] = src_ref[...].astype(BF16)


def _prep(rpb_table, weights):
    steps = ATTN_HEADS // PREP_HEADS_PER_STEP
    tile = jax.ShapeDtypeStruct((ATTN_HEADS, MOBA_BLOCK, MOBA_BLOCK), F32)
    tile_spec = pl.BlockSpec((PREP_HEADS_PER_STEP, MOBA_BLOCK, MOBA_BLOCK), lambda s: (s, 0, 0))
    slabs = [w.reshape(steps, w.shape[0] // steps, w.shape[1]) for w in weights]
    assert all(w.shape[1] % BF16_SUBLANES == 0 for w in slabs)
    slab_specs = [pl.BlockSpec((1,) + w.shape[1:], lambda s: (s, 0, 0)) for w in slabs]
    outs = pl.pallas_call(
        _prep_kernel,
        grid=(steps,),
        in_specs=[pl.BlockSpec(memory_space=pltpu.SMEM), *slab_specs],
        out_specs=[tile_spec, tile_spec, *slab_specs],
        out_shape=[tile, tile] + [jax.ShapeDtypeStruct(w.shape, BF16) for w in slabs],
        compiler_params=pltpu.CompilerParams(
            dimension_semantics=("arbitrary",), vmem_limit_bytes=V7X_VMEM_LIMIT_BYTES),
        name="prep",
    )(rpb_table, *slabs)
    return outs[0], outs[1], [o.reshape(w.shape) for o, w in zip(outs[2:], weights)]


def _ffn1_qkv_kernel(x_ref, g1_ref, wg_ref, wu_ref, wd_ref, gm_ref, wqkv_f32_ref, *refs):
    later_f32, (h1_ref, un_ref, qt_ref, k_ref, vt_ref), later_bf16, wqkv_ref = (
        refs[:N_LATER_WEIGHTS], refs[N_LATER_WEIGHTS:N_LATER_WEIGHTS + 5], refs[N_LATER_WEIGHTS + 5:-1], refs[-1])

    @pl.when((pl.program_id(0) == 0) & (pl.program_id(1) == 0))
    def _():
        wqkv_ref[...] = wqkv_f32_ref[...].astype(BF16)

    for src_ref, dst_ref in zip(later_f32, later_bf16):
        dst_ref[...] = src_ref[...].astype(BF16)
    h1 = _swiglu_half_step(x_ref[0], g1_ref, wg_ref, wu_ref, wd_ref)
    h1_ref[0] = h1
    un = _rms(h1, gm_ref[...]).astype(BF16)
    un_ref[0] = un
    qkv = _dot(un, wqkv_ref[...])
    qt_ref[0] = (qkv[:, :ATTN_WIDTH] * Q_SCALE).T.astype(BF16)
    k_ref[0] = qkv[:, ATTN_WIDTH:2 * ATTN_WIDTH].astype(BF16)
    for c in range(FFN_TOKENS // MOBA_BLOCK):
        vt_ref[0, c] = qkv[c * MOBA_BLOCK:(c + 1) * MOBA_BLOCK, 2 * ATTN_WIDTH:].T.astype(BF16)


def _ffn1_qkv(x, g1, wg, wu, wd, gm, w_in, later_weights):
    B, S, D = x.shape
    tm = FFN_TOKENS
    nblk = S // MOBA_BLOCK
    tiles_per_seq = S // tm
    steps = B * tiles_per_seq
    assert len(later_weights) == N_LATER_WEIGHTS
    step_of = lambda b, i: b * tiles_per_seq + i

    def slab_view(w):
        n = max(n for n in range(1, steps + 1) if steps % n == 0 and w.shape[0] % (n * BF16_SUBLANES) == 0)
        view = w.reshape(n, w.shape[0] // n, w.shape[1])
        return view, pl.BlockSpec((1,) + view.shape[1:], lambda b, i: (step_of(b, i) // (steps // n), 0, 0))

    slabs, slab_specs = zip(*[slab_view(w) for w in later_weights])
    tok = lambda width, dt: (pl.BlockSpec((1, tm, width), lambda b, i: (b, i, 0)),
                             jax.ShapeDtypeStruct((B, S, width), dt))
    h1_spec, h1_shape = tok(D, F32)
    un_spec, un_shape = tok(D, BF16)
    k_spec, k_shape = tok(ATTN_WIDTH, BF16)
    outs = pl.pallas_call(
        _ffn1_qkv_kernel,
        grid=(B, tiles_per_seq),
        in_specs=[pl.BlockSpec((1, tm, D), lambda b, i: (b, i, 0)),
                  _resident(g1.shape), _resident(wg.shape), _resident(wu.shape), _resident(wd.shape),
                  _resident(gm.shape), _resident((D, 3 * ATTN_WIDTH)), *slab_specs],
        out_specs=[h1_spec, un_spec,
                   pl.BlockSpec((1, ATTN_WIDTH, tm), lambda b, i: (b, 0, i)),
                   k_spec,
                   pl.BlockSpec((1, tm // MOBA_BLOCK, ATTN_WIDTH, MOBA_BLOCK), lambda b, i: (b, i, 0, 0)),
                   *slab_specs],
        out_shape=[h1_shape, un_shape,
                   jax.ShapeDtypeStruct((B, ATTN_WIDTH, S), BF16),
                   k_shape,
                   jax.ShapeDtypeStruct((B, nblk, ATTN_WIDTH, MOBA_BLOCK), BF16)]
                  + [jax.ShapeDtypeStruct(w.shape, BF16) for w in slabs],
        scratch_shapes=[pltpu.VMEM((D, 3 * ATTN_WIDTH), BF16)],
        compiler_params=pltpu.CompilerParams(
            dimension_semantics=("arbitrary", "arbitrary"), vmem_limit_bytes=V7X_VMEM_LIMIT_BYTES),
        name="ffn1_qkv",
    )(x, g1, wg, wu, wd, gm, w_in, *slabs)
    return outs[:5], [o.reshape(w.shape) for o, w in zip(outs[5:], later_weights)]


def _moba_attn_kernel(table_ref, qt_ref, k_ref, vt_ref, diag_ref, sub_ref, a_ref,
                      kmhi_ref, kmlo_ref, rb_far_ref, rb_sub_ref, at_ref, qz_ref,
                      m_ref, l_ref, mblk_ref, alpha_ref, s_ref, p_ref):
    qi = pl.program_id(1)
    nblk = k_ref.shape[2] // MOBA_BLOCK
    nq = MOBA_BLOCK
    units = [(r, h) for r in range(ATTN_ROWS) for h in range(ATTN_HEADS)]

    @pl.when(qi == 0)
    def _():
        for r in range(ATTN_ROWS):
            blk_row = lax.broadcasted_iota(jnp.int32, (nblk, ATTN_WIDTH), 0)
            km = jnp.zeros((nblk, ATTN_WIDTH), F32)
            for j in range(nblk):
                mean_j = jnp.mean(k_ref[r, 0, j * MOBA_BLOCK:(j + 1) * MOBA_BLOCK, :].astype(F32), axis=0,
                                  keepdims=True)
                km = jnp.where(blk_row == j, mean_j, km)
            km = jnp.concatenate([km] * ATTN_HEADS, axis=0)
            row_head = lax.broadcasted_iota(jnp.int32, km.shape, 0) // nblk
            col_head = lax.broadcasted_iota(jnp.int32, km.shape, 1) // HEAD_DIM
            km = jnp.where(row_head == col_head, km, 0.0)
            hi = km.astype(BF16)
            kmhi_ref[r] = hi
            kmlo_ref[r] = (km - hi.astype(F32)).astype(BF16)

    def select_blocks():
        jrow = lax.broadcasted_iota(jnp.int32, (nblk, nq), 0)
        for r in range(ATTN_ROWS):
            qt = qt_ref[r, 0]
            gates = _dot(kmhi_ref[r], qt) + _dot(kmlo_ref[r], qt)
            for h in range(ATTN_HEADS):
                u = r * ATTN_HEADS + h
                g = gates[h * nblk:(h + 1) * nblk]
                cnt = jnp.zeros((nblk, nq), jnp.int32)
                for jp in range(nblk):
                    gb = jnp.broadcast_to(g[jp:jp + 1], (nblk, nq))
                    beats = (gb > g) | ((gb == g) & (jp < jrow))
                    cnt = cnt + jnp.where(beats, jnp.where(jp < qi, 1, 0), 0)
                sel = (jrow < qi) & (cnt < MOBA_TOPK)
                far = jnp.where(sel, table_ref[RPB_BUCKETS - 1, h] * LOG2E, MASK_VALUE)
                sub = jnp.where(sel, 0.0, MASK_VALUE)
                for j in range(nblk):
                    rb_far_ref[u * nblk + j] = far[j:j + 1]
                    rb_sub_ref[u * nblk + j] = sub[j:j + 1]

    def begin():
        m_ref[...] = jnp.full(m_ref.shape, MASK_VALUE, F32)
        l_ref[...] = jnp.zeros(l_ref.shape, F32)
        at_ref[...] = jnp.zeros(at_ref.shape, F32)
        zeros_half = jnp.zeros((HEAD_DIM, nq), BF16)
        for r, h in units:
            qh = qt_ref[r, 0, h * HEAD_DIM:(h + 1) * HEAD_DIM, :]
            qz_ref[r * ATTN_HEADS + h] = jnp.concatenate([qh, zeros_half] if h % 2 == 0 else [zeros_half, qh], axis=0)

    def scores_stage(j, kind):
        start = pl.multiple_of(j * MOBA_BLOCK, MOBA_BLOCK)
        for u, (r, h) in enumerate(units):
            pair = h // 2
            kblk = k_ref[r, 0, pl.ds(start, MOBA_BLOCK), pair * PAIR_WIDTH:(pair + 1) * PAIR_WIDTH]
            s = _dot(kblk, qz_ref[u])
            if kind == "diag":
                s = s + diag_ref[h]
            elif kind == "sub":
                s = s + sub_ref[h]
            s_ref[u] = s
            mblk_ref[u] = jnp.max(s, axis=0, keepdims=True)

    def softmax_stage(row_bias):
        for u in range(len(units)):
            rb = row_bias(u)
            m_old = m_ref[u]
            m_blk = mblk_ref[u] if rb is None else mblk_ref[u] + rb
            m_new = jnp.maximum(m_old, m_blk)
            offset = m_new if rb is None else m_new - rb
            p_ref[u] = jnp.exp2(s_ref[u] - offset).astype(BF16)
            alpha_ref[u] = jnp.exp2(m_old - m_new)
            m_ref[u] = m_new

    ones_rows = jnp.ones((BF16_SUBLANES, nq), BF16)

    def values_stage(j):
        for u, (r, h) in enumerate(units):
            rows = slice(u * HEAD_DIM, (u + 1) * HEAD_DIM)
            v_h = vt_ref[r, 0, j, h * HEAD_DIM:(h + 1) * HEAD_DIM, :]
            pv = _dot(jnp.concatenate([v_h, ones_rows], axis=0), p_ref[u])
            alpha = alpha_ref[u]
            at_ref[rows] = alpha * at_ref[rows] + pv[:HEAD_DIM]
            l_ref[u] = alpha * l_ref[u] + pv[HEAD_DIM:HEAD_DIM + 1]

    def finish():
        for u in range(len(units)):
            rows = slice(u * HEAD_DIM, (u + 1) * HEAD_DIM)
            at_ref[rows] = at_ref[rows] * (1.0 / l_ref[u])
        for r in range(ATTN_ROWS):
            a_ref[r, 0] = at_ref[r * ATTN_WIDTH:(r + 1) * ATTN_WIDTH, :].T.astype(BF16)

    @pl.when(qi == 0)
    def _():
        begin()
        scores_stage(qi, "diag")
        softmax_stage(lambda u: None)
        values_stage(qi)
        finish()

    @pl.when(qi == 1)
    def _():
        begin()
        scores_stage(qi, "diag")
        softmax_stage(lambda u: None)
        scores_stage(0, "sub")
        values_stage(qi)
        softmax_stage(lambda u: None)
        values_stage(0)
        finish()

    @pl.when(qi > 1)
    def _():
        begin()
        select_blocks()
        j_sub = qi - 1
        n_far = j_sub

        scores_stage(qi, "diag")
        softmax_stage(lambda u: None)
        scores_stage(j_sub, "sub")
        values_stage(qi)
        softmax_stage(lambda u: rb_sub_ref[u * nblk + j_sub])
        scores_stage(0, "far")

        def far_body(t, carry):
            values_stage(jnp.where(t == 0, j_sub, t - 1))
            softmax_stage(lambda u: rb_far_ref[u * nblk + t])
            scores_stage(t + 1, "far")
            return carry

        lax.fori_loop(0, n_far - 1, far_body, 0)
        values_stage(jnp.where(n_far == 1, j_sub, n_far - 2))
        softmax_stage(lambda u: rb_far_ref[u * nblk + n_far - 1])
        values_stage(n_far - 1)
        finish()


def _moba_attn(qt, k, vt, diag, sub, rpb_table):
    B, S, _ = k.shape
    nblk = S // MOBA_BLOCK
    groups = B // ATTN_ROWS
    n_units = ATTN_ROWS * ATTN_HEADS
    split = lambda x: x.reshape((ATTN_ROWS, groups) + x.shape[1:])
    tile = pl.BlockSpec((ATTN_HEADS, MOBA_BLOCK, MOBA_BLOCK), lambda g, i: (0, 0, 0), pipeline_mode=pl.Buffered(1))
    a = pl.pallas_call(
        _moba_attn_kernel,
        grid=(groups, nblk),
        in_specs=[pl.BlockSpec(memory_space=pltpu.SMEM),
                  pl.BlockSpec((ATTN_ROWS, 1, ATTN_WIDTH, MOBA_BLOCK), lambda g, i: (0, g, 0, i)),
                  pl.BlockSpec((ATTN_ROWS, 1, S, ATTN_WIDTH), lambda g, i: (0, g, 0, 0)),
                  pl.BlockSpec((ATTN_ROWS, 1, nblk, ATTN_WIDTH, MOBA_BLOCK), lambda g, i: (0, g, 0, 0, 0)),
                  tile, tile],
        out_specs=pl.BlockSpec((ATTN_ROWS, 1, MOBA_BLOCK, ATTN_WIDTH), lambda g, i: (0, g, i, 0)),
        out_shape=jax.ShapeDtypeStruct((ATTN_ROWS, groups, S, ATTN_WIDTH), BF16),
        scratch_shapes=[pltpu.VMEM((ATTN_ROWS, ATTN_HEADS * nblk, ATTN_WIDTH), BF16),
                        pltpu.VMEM((ATTN_ROWS, ATTN_HEADS * nblk, ATTN_WIDTH), BF16),
                        pltpu.VMEM((n_units * nblk, 1, MOBA_BLOCK), F32),
                        pltpu.VMEM((n_units * nblk, 1, MOBA_BLOCK), F32),
                        pltpu.VMEM((n_units * HEAD_DIM, MOBA_BLOCK), F32),
                        pltpu.VMEM((n_units, 2 * HEAD_DIM, MOBA_BLOCK), BF16),
                        *[pltpu.VMEM((n_units, 1, MOBA_BLOCK), F32)] * 4,
                        pltpu.VMEM((n_units, MOBA_BLOCK, MOBA_BLOCK), F32),
                        pltpu.VMEM((n_units, MOBA_BLOCK, MOBA_BLOCK), BF16)],
        compiler_params=pltpu.CompilerParams(
            dimension_semantics=("arbitrary", "arbitrary"), vmem_limit_bytes=V7X_VMEM_LIMIT_BYTES),
        name="moba_attn",
    )(rpb_table, split(qt), split(k), split(vt), diag, sub)
    return a.reshape(B, S, ATTN_WIDTH)


def _mix_merge_kernel(h1_ref, un_ref, a_ref, win_ref, wgrp_ref, scale_ref,
                      wa_ref, wp_ref, wout_ref, h2_ref, zext_ref):
    i = pl.program_id(1)
    tm = MIX_TOKENS
    z0 = 3 * ATTN_WIDTH
    @pl.when(i == 0)
    def _():
        zext_ref[0, :POOL_HALO] = jnp.zeros((POOL_HALO, POOL_WIDTH), F32)

    @pl.when(i > 0)
    def _():
        zext_ref[0, :POOL_HALO] = zext_ref[MIX_SUBTILES - 1, tm:]

    for c in range(MIX_SUBTILES):
        rows = slice(c * tm, (c + 1) * tm)
        zx_ref = zext_ref.at[c]
        zg = _dot(un_ref[0, rows], win_ref[:, z0:])
        if c > 0:
            zx_ref[:POOL_HALO] = zext_ref[c - 1, tm:]
        zx_ref[POOL_HALO:] = zg[:, :POOL_WIDTH]

        tpos = (i * MIX_SUBTILES + c) * tm + lax.broadcasted_iota(jnp.int32, (tm, POOL_GROUP_WIDTH), 0)
        mixed = []
        for g, w in enumerate(POOL_WINDOWS):
            cols = slice(g * POOL_GROUP_WIDTH, (g + 1) * POOL_GROUP_WIDTH)
            wsum = zx_ref[POOL_HALO:, cols]
            for lag in range(1, w):
                wsum = wsum + zx_ref[POOL_HALO - lag:POOL_HALO - lag + tm, cols]
            mean = wsum / jnp.minimum(tpos + 1, w).astype(F32)
            pooled = (mean - zx_ref[POOL_HALO:, cols]).astype(BF16)
            mixed.append(_dot(pooled, wgrp_ref[g].astype(BF16)))
        p = (jnp.concatenate(mixed, axis=1) * scale_ref[...]).astype(BF16)

        g_attn = zg[:, POOL_WIDTH:POOL_WIDTH + D_MODEL]
        g_pool = zg[:, POOL_WIDTH + D_MODEL:]
        merged = (jax.nn.sigmoid(g_attn) * _dot(a_ref[0, rows], wa_ref[...])
                  + jax.nn.sigmoid(g_pool) * _dot(p, wp_ref[...]))
        h2_ref[0, rows] = h1_ref[0, rows] + _dot(merged.astype(BF16), wout_ref[...])


def _mix_merge(h1, un, a, win, wgrp, scale, wa, wp, wout):
    B, S, D = h1.shape
    tm = MIX_SUBTILES * MIX_TOKENS
    return pl.pallas_call(
        _mix_merge_kernel,
        grid=(B, S // tm),
        in_specs=[pl.BlockSpec((1, tm, D), lambda b, i: (b, i, 0)),
                  pl.BlockSpec((1, tm, D), lambda b, i: (b, i, 0)),
                  pl.BlockSpec((1, tm, ATTN_WIDTH), lambda b, i: (b, i, 0)),
                  _resident(win.shape), _resident(wgrp.shape), _resident(scale.shape),
                  _resident(wa.shape), _resident(wp.shape), _resident(wout.shape)],
        out_specs=pl.BlockSpec((1, tm, D), lambda b, i: (b, i, 0)),
        out_shape=jax.ShapeDtypeStruct((B, S, D), F32),
        scratch_shapes=[pltpu.VMEM((MIX_SUBTILES, POOL_HALO + MIX_TOKENS, POOL_WIDTH), F32)],
        compiler_params=pltpu.CompilerParams(
            dimension_semantics=("arbitrary", "arbitrary"), vmem_limit_bytes=V7X_VMEM_LIMIT_BYTES),
        name="mix_merge",
    )(h1, un, a, win, wgrp, scale, wa, wp, wout)


def _ffn2_final_kernel(h_ref, g2_ref, wg_ref, wu_ref, wd_ref, gf_ref, out_ref):
    for c in range(FFN2_SUBTILES):
        rows = slice(c * FFN_TOKENS, (c + 1) * FFN_TOKENS)
        h3 = _swiglu_half_step(h_ref[0, rows], g2_ref, wg_ref, wu_ref, wd_ref)
        out_ref[0, rows] = _rms(h3, gf_ref[...])


def _ffn2_final(h, g2, wg, wu, wd, gf):
    B, S, D = h.shape
    tm = FFN2_SUBTILES * FFN_TOKENS
    return pl.pallas_call(
        _ffn2_final_kernel,
        grid=(B, S // tm),
        in_specs=[pl.BlockSpec((1, tm, D), lambda b, i: (b, i, 0)),
                  _resident(g2.shape), _resident(wg.shape), _resident(wu.shape), _resident(wd.shape),
                  _resident(gf.shape)],
        out_specs=pl.BlockSpec((1, tm, D), lambda b, i: (b, i, 0)),
        out_shape=jax.ShapeDtypeStruct((B, S, D), F32),
        compiler_params=pltpu.CompilerParams(
            dimension_semantics=("arbitrary", "arbitrary"), vmem_limit_bytes=V7X_VMEM_LIMIT_BYTES),
        name="ffn2_final",
    )(h, g2, wg, wu, wd, gf)


def kernel(x, ffn1_norm, ffn1_w_gate, ffn1_w_up, ffn1_w_down, mix_norm, w_in, pool_w_group, pool_scale,
           w_branch_attn, w_branch_pool, w_out, ffn2_norm, ffn2_w_gate, ffn2_w_up, ffn2_w_down,
           rpb_table, final_norm):
    B, S, D = x.shape
    assert (D, ffn1_w_gate.shape[0]) == (D_MODEL, 1)
    assert S % MOBA_BLOCK == 0 and S % (FFN2_SUBTILES * FFN_TOKENS) == 0 and S % (MIX_SUBTILES * MIX_TOKENS) == 0
    assert B % ATTN_ROWS == 0
    bf = lambda w: w.astype(BF16)
    row = lambda v: v.reshape(1, -1)
    w_in = w_in[0]

    diag, sub, (wg1_bf, wu1_bf, wd1_bf) = _prep(rpb_table, [ffn1_w_gate[0], ffn1_w_up[0], ffn1_w_down[0]])
    (h1, un, qt, k, vt), later = _ffn1_qkv(
        x, row(ffn1_norm[0]), wg1_bf, wu1_bf, wd1_bf, row(mix_norm[0]), w_in,
        [w_in, w_branch_attn[0], w_branch_pool[0], w_out[0], ffn2_w_gate[0], ffn2_w_up[0], ffn2_w_down[0]])
    win_bf, wa_bf, wp_bf, wout_bf, wg2_bf, wu2_bf, wd2_bf = later
    a = _moba_attn(qt, k, vt, diag, sub, rpb_table)
    h2 = _mix_merge(h1, un, a, win_bf, pool_w_group[0], row(pool_scale[0]), wa_bf, wp_bf, wout_bf)
    return _ffn2_final(h2, row(ffn2_norm[0]), wg2_bf, wu2_bf, wd2_bf, row(final_norm))
```
